```python
import jax
import jax.numpy as jnp
from jax import lax
import numpy as np

D_MODEL = 1024
BATCH = 8
SEQ = 2048
DEPTH = 1

GRID_W = 64
CTX_LEN = 256

LRU_WIDTH = 1280
LRU_BLOCKS = 16
LRU_BLOCK = LRU_WIDTH // LRU_BLOCKS
LRU_CONV = 4
LRU_C = 8.0

RWKV_HEAD = 64
RWKV_WIDTH = 1024
RWKV_HEADS = RWKV_WIDTH // RWKV_HEAD
LORA_W = 64
LORA_A = 64
LORA_G = 160
RWKV_IN = 3 * RWKV_WIDTH + 2 * LORA_W + 2 * LORA_A + LORA_G

N_IN = 2 * LRU_WIDTH + RWKV_IN + 2 * D_MODEL
D_FF = ((8 * D_MODEL // 3 + 255) // 256) * 256

RMS_EPS = 1e-6
GN_EPS = 64e-5
L2_EPS = 1e-12

kernel_name = 'hybrid_rglru_rwkv7_prefix_block'


def _split_points(sizes):
    pts, acc = [], 0
    for s in sizes[:-1]:
        acc += s
        pts.append(acc)
    return pts


def rms_norm(x, g):
    x32 = x.astype(jnp.float32)
    y = x32 * lax.rsqrt(jnp.mean(x32 * x32, axis=-1, keepdims=True) + RMS_EPS)
    return (y * g.astype(jnp.float32)).astype(x.dtype)


def adaln(cvec, w_mod, b_mod):
    m = jax.nn.silu(cvec) @ w_mod + b_mod
    return [t[:, None, :] for t in jnp.split(m, 6, axis=-1)]


def modulate(x, g, shift, scale):
    return rms_norm(x, g) * (1 + scale) + shift


def swiglu(h, w_in, w_out):
    gate, up = jnp.split(h @ w_in, 2, axis=-1)
    return (jax.nn.silu(gate) * up) @ w_out


def to_colmajor(z, rows):
    b, l, ch = z.shape
    return z.reshape(b, rows, GRID_W, ch).transpose(0, 2, 1, 3).reshape(b, l, ch)


def from_colmajor(z, rows):
    b, l, ch = z.shape
    return z.reshape(b, GRID_W, rows, ch).transpose(0, 2, 1, 3).reshape(b, l, ch)


def directional_conv(u, w, bias, reverse):
    L = u.shape[1]
    pad = (0, LRU_CONV - 1) if reverse else (LRU_CONV - 1, 0)
    up = jnp.pad(u, ((0, 0), pad, (0, 0)))
    out = bias
    for j in range(LRU_CONV):
        off = LRU_CONV - 1 - j if reverse else j
        out = out + w[j] * up[:, off:off + L]
    return out


def _lin_combine(e1, e2):
    a1, b1 = e1
    a2, b2 = e2
    return a1 * a2, a2 * b1 + b2


def rglru_scan(u, h0, conv_w, conv_b, wa, ba, wx, bx, lam, reverse):
    B, L, W = u.shape
    xc = directional_conv(u.astype(jnp.float32), conv_w, conv_b, reverse)
    if reverse:
        xc = jnp.flip(xc, axis=1)
    xb = xc.reshape(B, L, LRU_BLOCKS, LRU_BLOCK)
    gate_r = jax.nn.sigmoid(jnp.einsum('blnc,ncd->blnd', xb, wa).reshape(B, L, W) + ba)
    gate_i = jax.nn.sigmoid(jnp.einsum('blnc,ncd->blnd', xb, wx).reshape(B, L, W) + bx)
    log_a = -LRU_C * gate_r * jax.nn.softplus(-lam)
    a = jnp.exp(log_a)
    b = jnp.sqrt(-jnp.expm1(2.0 * log_a)) * (gate_i * xc)
    b = b.at[:, 0].add(a[:, 0] * h0)
    _, h = lax.associative_scan(_lin_combine, (a, b), axis=1)
    h_last = h[:, -1]
    if reverse:
        h = jnp.flip(h, axis=1)
    return h, h_last


def lru_dir(u, h0, p, d, reverse):
    return rglru_scan(u, h0, p['lru_conv_w'][d], p['lru_conv_b'][d], p['lru_wa'][d], p['lru_ba'][d],
                      p['lru_wx'][d], p['lru_bx'][d], p['lru_lambda'][d], reverse)


def centred_shift(z, mu):
    zp = jnp.pad(z, ((0, 0), (1, 1), (0, 0)))
    return z + mu[0] * (zp[:, :-2] - z) + mu[1] * (zp[:, 2:] - z)


def orient(t):
    return jnp.stack([t[0], jnp.flip(t[1], axis=1)])


def heads(t):
    return t.reshape(t.shape[:-1] + (RWKV_HEADS, RWKV_HEAD))


def wkv7_scan(r, w, k, v, a, b, S0):
    def step(S, inp):
        r_t, w_t, k_t, v_t, a_t, b_t = inp
        sa = jnp.einsum('dbhij,dbhj->dbhi', S, a_t)
        S = S * w_t[..., None, :] + sa[..., :, None] * b_t[..., None, :] + v_t[..., :, None] * k_t[..., None, :]
        y = jnp.einsum('dbhij,dbhj->dbhi', S, r_t)
        return S, y
    tm = lambda t: jnp.moveaxis(t, 2, 0)
    S, ys = lax.scan(step, S0, (tm(r), tm(w), tm(k), tm(v), tm(a), tm(b)))
    return jnp.moveaxis(ys, 0, 2), S


def rwkv_branch(zs, S0, p, need_out):
    B, L, _ = zs.shape
    z32 = zs.astype(jnp.float32)
    pts = _split_points([RWKV_WIDTH, RWKV_WIDTH, RWKV_WIDTH, LORA_W, LORA_W, LORA_A, LORA_A, LORA_G])
    r, k, v, wd_f, wd_b, ad_f, ad_b, gd = jnp.split(z32, pts, axis=-1)
    wd = jnp.stack([wd_f, wd_b])
    ad = jnp.stack([ad_f, ad_b])
    w_log = -jax.nn.softplus(-(p['rwkv_w0'][:, None, None, :]
                               + jnp.einsum('dblr,drc->dblc', jnp.tanh(wd), p['rwkv_w2']))) - 0.5
    decay = jnp.exp(-jnp.exp(w_log))
    a = jax.nn.sigmoid(p['rwkv_a0'][:, None, None, :] + jnp.einsum('dblr,drc->dblc', ad, p['rwkv_a2']))
    kk = heads(k * p['rwkv_k_k'])
    kk = kk / jnp.maximum(jnp.sqrt(jnp.sum(kk * kk, axis=-1, keepdims=True)), L2_EPS)
    kd = heads(k[None] * (1 + (a - 1) * p['rwkv_k_a']))
    rr, vv = heads(r), heads(v)
    two = lambda t: jnp.broadcast_to(t[None], (2,) + t.shape)
    ys, S = wkv7_scan(orient(two(rr)), orient(heads(decay)), orient(kd), orient(two(vv)),
                      orient(two(-kk)), orient(kk[None] * heads(a)), S0)
    if not need_out:
        return None, S
    y = orient(ys)
    y = y[0] + y[1]
    mu = jnp.mean(y, axis=-1, keepdims=True)
    var = jnp.mean(jnp.square(y - mu), axis=-1, keepdims=True)
    yn = ((y - mu) * lax.rsqrt(var + GN_EPS)).reshape(B, L, RWKV_WIDTH) * p['rwkv_ln_g'] + p['rwkv_ln_b']
    bonus = jnp.sum(rr[None] * kd * p['rwkv_r_k'], axis=-1, keepdims=True)
    bonus = jnp.sum(bonus * vv[None], axis=0).reshape(B, L, RWKV_WIDTH)
    g = jax.nn.sigmoid(gd) @ p['rwkv_g2']
    return ((yn + bonus) * g).astype(zs.dtype), S


def merge(lru, rwkv, g_lru, g_rwkv, p):
    m = jax.nn.sigmoid(g_lru) * (lru @ p['w_o_lru']) + jax.nn.sigmoid(g_rwkv) * (rwkv @ p['w_o_rwkv'])
    return m @ p['w_out']


def mixer(h_ctx, h_lat, p, rows, need_ctx_out):
    B = h_lat.shape[0]
    pts = _split_points([LRU_WIDTH, LRU_WIDTH, RWKV_IN, D_MODEL, D_MODEL])
    ux_c, uy_c, zr_c, gl_c, gr_c = jnp.split(h_ctx @ p['w_in'], pts, axis=-1)
    ux_l, uy_l, zr_l, gl_l, gr_l = jnp.split(h_lat @ p['w_in'], pts, axis=-1)
    h_zero = jnp.zeros((B, LRU_WIDTH), jnp.float32)
    hf_c, sf = lru_dir(ux_c, h_zero, p, 0, False)
    hb_c, sb = lru_dir(ux_c, h_zero, p, 1, True)
    hf_l, _ = lru_dir(ux_l, sf, p, 0, False)
    hb_l, _ = lru_dir(ux_l, sb, p, 1, True)
    lru_l = ((hf_l + hb_l) * jax.nn.gelu(uy_l.astype(jnp.float32))).astype(h_lat.dtype)
    S_zero = jnp.zeros((2, B, RWKV_HEADS, RWKV_HEAD, RWKV_HEAD), jnp.float32)
    rw_c, S_c = rwkv_branch(centred_shift(zr_c, p['rwkv_mu']), S_zero, p, need_ctx_out)
    rw_l, _ = rwkv_branch(centred_shift(to_colmajor(zr_l, rows), p['rwkv_mu']), S_c, p, True)
    rw_l = from_colmajor(rw_l, rows)
    out_l = merge(lru_l, rw_l, gl_l, gr_l, p)
    out_c = None
    if need_ctx_out:
        lru_c = ((hf_c + hb_c) * jax.nn.gelu(uy_c.astype(jnp.float32))).astype(h_ctx.dtype)
        out_c = merge(lru_c, rw_c, gl_c, gr_c, p)
    return out_c, out_l


def setup_inputs(seed: int = 0) -> dict:
    key = jax.random.key(seed)
    ks = jax.random.split(key, 40)
    nrm = lambda k, shape, s: jax.random.normal(k, shape, jnp.float32) * s
    D, Ld = D_MODEL, DEPTH
    u = jax.random.uniform(ks[15], (Ld, 2, LRU_WIDTH), jnp.float32, 0.9, 0.999)
    sig = u ** (1.0 / LRU_C)
    return {
        'x': nrm(ks[0], (BATCH, SEQ, D), 1.0),
        'c': nrm(ks[1], (BATCH, D), 1.0),
        'ctx': nrm(ks[2], (BATCH, CTX_LEN, D), 1.0),
        'c_ctx': nrm(ks[3], (D,), 1.0),
        'norm_mix_g': 1.0 + nrm(ks[4], (Ld, D), 0.1),
        'norm_ffn_g': 1.0 + nrm(ks[5], (Ld, D), 0.1),
        'w_mod': nrm(ks[6], (Ld, D, 6 * D), 0.5 * D ** -0.5),
        'b_mod': nrm(ks[7], (Ld, 6 * D), 0.02),
        'w_in': nrm(ks[8], (Ld, D, N_IN), D ** -0.5),
        'lru_conv_w': nrm(ks[9], (Ld, 2, LRU_CONV, LRU_WIDTH), 0.5),
        'lru_conv_b': nrm(ks[10], (Ld, 2, LRU_WIDTH), 0.02),
        'lru_wa': nrm(ks[11], (Ld, 2, LRU_BLOCKS, LRU_BLOCK, LRU_BLOCK), LRU_BLOCK ** -0.5),
        'lru_ba': nrm(ks[12], (Ld, 2, LRU_WIDTH), 0.02),
        'lru_wx': nrm(ks[13], (Ld, 2, LRU_BLOCKS, LRU_BLOCK, LRU_BLOCK), LRU_BLOCK ** -0.5),
        'lru_bx': nrm(ks[14], (Ld, 2, LRU_WIDTH), 0.02),
        'lru_lambda': jnp.log(sig) - jnp.log1p(-sig),
        'w_o_lru': nrm(ks[16], (Ld, LRU_WIDTH, D), LRU_WIDTH ** -0.5),
        'rwkv_mu': jax.random.uniform(ks[17], (Ld, 2, RWKV_IN), jnp.float32, 0.1, 0.5),
        'rwkv_w0': jax.random.uniform(ks[18], (Ld, 2, RWKV_WIDTH), jnp.float32, -6.0, 0.0),
        'rwkv_w2': nrm(ks[19], (Ld, 2, LORA_W, RWKV_WIDTH), 0.1),
        'rwkv_a0': nrm(ks[20], (Ld, 2, RWKV_WIDTH), 0.5),
        'rwkv_a2': nrm(ks[21], (Ld, 2, LORA_A, RWKV_WIDTH), 0.1),
        'rwkv_g2': nrm(ks[22], (Ld, LORA_G, RWKV_WIDTH), LORA_G ** -0.5),
        'rwkv_k_k': 0.85 + nrm(ks[23], (Ld, RWKV_WIDTH), 0.1),
        'rwkv_k_a': 1.0 + nrm(ks[24], (Ld, RWKV_WIDTH), 0.1),
        'rwkv_r_k': nrm(ks[25], (Ld, RWKV_HEADS, RWKV_HEAD), 0.1),
        'rwkv_ln_g': 1.0 + nrm(ks[26], (Ld, RWKV_WIDTH), 0.1),
        'rwkv_ln_b': nrm(ks[27], (Ld, RWKV_WIDTH), 0.02),
        'w_o_rwkv': nrm(ks[28], (Ld, RWKV_WIDTH, D), RWKV_WIDTH ** -0.5),
        'w_out': nrm(ks[29], (Ld, D, D), D ** -0.5),
        'w_ffn_in': nrm(ks[30], (Ld, D, 2 * D_FF), D ** -0.5),
        'w_ffn_out': nrm(ks[31], (Ld, D_FF, D), D_FF ** -0.5),
        'norm_final_g': 1.0 + nrm(ks[32], (D,), 0.1),
    }


def reference(x, c, ctx, c_ctx, norm_mix_g, norm_ffn_g, w_mod, b_mod, w_in, lru_conv_w, lru_conv_b,
              lru_wa, lru_ba, lru_wx, lru_bx, lru_lambda, w_o_lru, rwkv_mu, rwkv_w0, rwkv_w2, rwkv_a0,
              rwkv_a2, rwkv_g2, rwkv_k_k, rwkv_k_a, rwkv_r_k, rwkv_ln_g, rwkv_ln_b, w_o_rwkv, w_out,
              w_ffn_in, w_ffn_out, norm_final_g):
    rows = x.shape[1] // GRID_W
    for l in range(DEPTH):
        last = l == DEPTH - 1
        p = {
            'w_in': w_in[l], 'lru_conv_w': lru_conv_w[l], 'lru_conv_b': lru_conv_b[l],
            'lru_wa': lru_wa[l], 'lru_ba': lru_ba[l], 'lru_wx': lru_wx[l], 'lru_bx': lru_bx[l],
            'lru_lambda': lru_lambda[l], 'w_o_lru': w_o_lru[l], 'rwkv_mu': rwkv_mu[l],
            'rwkv_w0': rwkv_w0[l], 'rwkv_w2': rwkv_w2[l], 'rwkv_a0': rwkv_a0[l], 'rwkv_a2': rwkv_a2[l],
            'rwkv_g2': rwkv_g2[l], 'rwkv_k_k': rwkv_k_k[l], 'rwkv_k_a': rwkv_k_a[l],
            'rwkv_r_k': rwkv_r_k[l], 'rwkv_ln_g': rwkv_ln_g[l], 'rwkv_ln_b': rwkv_ln_b[l],
            'w_o_rwkv': w_o_rwkv[l], 'w_out': w_out[l],
        }
        sh_m, sc_m, g_m, sh_f, sc_f, g_f = adaln(c, w_mod[l], b_mod[l])
        csh_m, csc_m, cg_m, csh_f, csc_f, cg_f = adaln(c_ctx[None], w_mod[l], b_mod[l])
        h_lat = modulate(x, norm_mix_g[l], sh_m, sc_m)
        h_ctx = modulate(ctx, norm_mix_g[l], csh_m, csc_m)
        mix_c, mix_l = mixer(h_ctx, h_lat, p, rows, not last)
        x = x + g_m * mix_l
        x = x + g_f * swiglu(modulate(x, norm_ffn_g[l], sh_f, sc_f), w_ffn_in[l], w_ffn_out[l])
        if not last:
            ctx = ctx + cg_m * mix_c
            ctx = ctx + cg_f * swiglu(modulate(ctx, norm_ffn_g[l], csh_f, csc_f), w_ffn_in[l], w_ffn_out[l])
    return rms_norm(x, norm_final_g)
```

```python
import functools
import math

import jax
import jax.numpy as jnp
from jax import lax
from jax.experimental import pallas as pl
from jax.experimental.pallas import tpu as pltpu

F32 = jnp.float32
BF16 = jnp.bfloat16

D_MODEL = 1024
BATCH = 8
SEQ = 2048
CTX_LEN = 256
GRID_W = 64
GRID_ROWS = SEQ // GRID_W

LRU_WIDTH = 1280
LRU_BLOCKS = 16
LRU_BLOCK = LRU_WIDTH // LRU_BLOCKS
LRU_CONV = 4
LRU_C = 8.0

RWKV_HEAD = 64
RWKV_WIDTH = 1024
LORA_W = 64
LORA_A = 64
LORA_G = 160
RWKV_IN = 3 * RWKV_WIDTH + 2 * LORA_W + 2 * LORA_A + LORA_G
RWKV_IN_PAD = 3584
LORA_G_PAD = RWKV_IN_PAD - (3 * RWKV_WIDTH + 2 * LORA_W + 2 * LORA_A)
D_FF = 2816

RMS_EPS = 1e-6
GN_EPS = 64e-5
L2_EPS = 1e-12

T_ALL = CTX_LEN + SEQ
CHUNK_T = 64
CHUNK_ROWS = CHUNK_T * BATCH
N_CTX_CHUNKS = CTX_LEN // CHUNK_T
N_LAT_CHUNKS = SEQ // CHUNK_T
N_CHUNKS = N_CTX_CHUNKS + N_LAT_CHUNKS

PAIR = 2 * RWKV_HEAD
N_PAIRS = RWKV_WIDTH // PAIR

LRU_GATE_TILE = 256
LRU_GATE_K = 512
LRU_GATE_K0 = (0, 128, 384, 640, 768)

VMEM_LIMIT = 56 * 1024 * 1024


def _params(n_axes):
    return pltpu.CompilerParams(dimension_semantics=("arbitrary",) * n_axes,
                                vmem_limit_bytes=VMEM_LIMIT)


def _const_spec(shape):
    nd = len(shape)
    return pl.BlockSpec(shape, lambda *_: (0,) * nd)


def _bdot(a, b):
    return jnp.dot(a.astype(BF16), b.astype(BF16), preferred_element_type=F32)


def _bdot_nt(a, b):
    return lax.dot_general(a.astype(BF16), b.astype(BF16), (((1,), (1,)), ((), ())),
                           preferred_element_type=F32)


def _split3(x):
    hi = x.astype(BF16)
    r1 = x - hi.astype(F32)
    mid = r1.astype(BF16)
    lo = (r1 - mid.astype(F32)).astype(BF16)
    return hi, mid, lo


def _softplus(x):
    return jnp.maximum(x, 0.0) + jnp.log1p(jnp.exp(-jnp.abs(x)))


def _sigmoid(x):
    return 1.0 / (1.0 + jnp.exp(-x))


def _silu(x):
    return x * _sigmoid(x)


def _gelu_tanh(x):
    c = math.sqrt(2.0 / math.pi)
    return 0.5 * x * (1.0 + jnp.tanh(c * (x + 0.044715 * (x * x * x))))


def _bcast_rows(v8, rows):
    c = v8.shape[-1]
    return jnp.broadcast_to(v8[None], (rows // BATCH, BATCH, c)).reshape(rows, c)


def _rms_modulate(x, g, scale8, shift8):
    rows = x.shape[0]
    ms = jnp.mean(x * x, axis=-1, keepdims=True)
    y = x * lax.rsqrt(ms + RMS_EPS) * g
    return y * (1.0 + _bcast_rows(scale8, rows)) + _bcast_rows(shift8, rows)


def _adaln_kernel(c_ref, w_ref, b_ref, o_ref):
    s = _silu(c_ref[...])
    o_ref[...] = jnp.dot(s, w_ref[...], preferred_element_type=F32,
                         precision=lax.Precision.HIGHEST) + b_ref[...]


def _adaln(c16, w_mod, b_mod):
    n = w_mod.shape[1]
    tn = 1536
    return pl.pallas_call(
        _adaln_kernel,
        grid=(n // tn,),
        in_specs=[_const_spec((16, D_MODEL)),
                  pl.BlockSpec((D_MODEL, tn), lambda j: (0, j)),
                  pl.BlockSpec((1, tn), lambda j: (0, j))],
        out_specs=pl.BlockSpec((16, tn), lambda j: (0, j)),
        out_shape=jax.ShapeDtypeStruct((16, n), F32),
        compiler_params=_params(1),
        name="adaln",
    )(c16, w_mod, b_mod)


def _proj_kernel(x_ref, g_ref, sc_ref, sh_ref, w_ref, o_ref):
    h = _rms_modulate(x_ref[...], g_ref[...], sc_ref[0], sh_ref[0])
    res = jnp.dot(h.astype(BF16), w_ref[...], preferred_element_type=F32)
    o_ref[...] = res.reshape(o_ref.shape)


def _proj(x_tb, g, sc2, sh2, w, *, first_chunk, n_chunks, colmajor_out):
    n = w.shape[1]
    seg = lambda i: ((i + first_chunk) >= N_CTX_CHUNKS).astype(jnp.int32)
    if colmajor_out:
        out_shape = jax.ShapeDtypeStruct((GRID_W, GRID_ROWS, BATCH, n), F32)
        out_spec = pl.BlockSpec((GRID_W, 1, BATCH, n), lambda i: (0, i, 0, 0))
    else:
        out_shape = jax.ShapeDtypeStruct((n_chunks * CHUNK_ROWS, n), F32)
        out_spec = pl.BlockSpec((CHUNK_ROWS, n), lambda i: (i, 0))
    return pl.pallas_call(
        _proj_kernel,
        grid=(n_chunks,),
        in_specs=[pl.BlockSpec((CHUNK_ROWS, D_MODEL), lambda i: (i + first_chunk, 0)),
                  _const_spec((1, D_MODEL)),
                  pl.BlockSpec((1, BATCH, D_MODEL), lambda i: (seg(i), 0, 0)),
                  pl.BlockSpec((1, BATCH, D_MODEL), lambda i: (seg(i), 0, 0)),
                  _const_spec((D_MODEL, n))],
        out_specs=out_spec,
        out_shape=out_shape,
        compiler_params=_params(1),
        name="in_proj",
    )(x_tb, g, sc2, sh2, w)


def _lru_kernel(*refs, reverse):
    if reverse:
        (u_ref, uy_ref, hf_ref, cw_ref, cb_ref, wt_ref, ba_ref, bx_ref, lam_ref,
         o_ref, h_s, halo_s, ext_s, a_s, b_s) = refs
    else:
        (u_ref, cw_ref, cb_ref, wt_ref, ba_ref, bx_ref, lam_ref,
         o_ref, h_s, halo_s, ext_s, a_s, b_s) = refs
    i = pl.program_id(0)
    halo_rows = (LRU_CONV - 1) * BATCH

    @pl.when(i == 0)
    def _():
        h_s[...] = jnp.zeros_like(h_s)

    @pl.when((i == 0) | (i == N_CTX_CHUNKS))
    def _():
        halo_s[...] = jnp.zeros_like(halo_s)

    u = u_ref[...]
    if reverse:
        ext_s[0:CHUNK_ROWS] = u
        ext_s[CHUNK_ROWS:CHUNK_ROWS + halo_rows] = halo_s[...]
        halo_s[...] = u[0:halo_rows]
    else:
        ext_s[0:halo_rows] = halo_s[...]
        ext_s[halo_rows:halo_rows + CHUNK_ROWS] = u
        halo_s[...] = u[CHUNK_ROWS - halo_rows:CHUNK_ROWS]

    xc = jnp.broadcast_to(cb_ref[...], (CHUNK_ROWS, LRU_WIDTH))
    for j in range(LRU_CONV):
        off = (LRU_CONV - 1 - j) if reverse else j
        xc = xc + cw_ref[j:j + 1, :] * ext_s[off * BATCH:off * BATCH + CHUNK_ROWS]

    xcb = xc.astype(BF16)
    sp = _softplus(-lam_ref[...])
    for j in range(LRU_WIDTH // LRU_GATE_TILE):
        k0 = LRU_GATE_K0[j]
        cs = slice(j * LRU_GATE_TILE, (j + 1) * LRU_GATE_TILE)
        g = jnp.dot(xcb[:, k0:k0 + LRU_GATE_K], wt_ref[j], preferred_element_type=F32)
        gate_r = _sigmoid(g[:, :LRU_GATE_TILE] + ba_ref[:, cs])
        gate_i = _sigmoid(g[:, LRU_GATE_TILE:] + bx_ref[:, cs])
        log_a = -LRU_C * gate_r * sp[:, cs]
        a = jnp.exp(log_a)
        a_s[:, cs] = a
        b_s[:, cs] = jnp.sqrt(1.0 - a * a) * (gate_i * xc[:, cs])

    def step(k, h):
        t = (CHUNK_T - 1 - k) if reverse else k
        rows = pl.ds(pl.multiple_of(t * BATCH, BATCH), BATCH)
        h = a_s[rows, :] * h + b_s[rows, :]
        if reverse:
            o_ref[rows, :] = (hf_ref[rows, :] + h) * _gelu_tanh(uy_ref[rows, :])
        else:
            o_ref[rows, :] = h
        return h

    h_s[...] = lax.fori_loop(0, CHUNK_T, step, h_s[...], unroll=8)


def _lru_scan(u_all, hf, lru_p, d, *, reverse):
    cw, cb, wt, ba, bx, lam = lru_p
    if reverse:
        chunk = lambda i: jnp.where(i < N_CTX_CHUNKS, N_CTX_CHUNKS - 1 - i,
                                    N_CHUNKS + N_CTX_CHUNKS - 1 - i)
    else:
        chunk = lambda i: i
    lat = lambda i: jnp.maximum(chunk(jnp.maximum(i, N_CTX_CHUNKS)) - N_CTX_CHUNKS, 0)
    blk = (CHUNK_ROWS, LRU_WIDTH)
    in_specs = [pl.BlockSpec(blk, lambda i: (chunk(i), 0))]
    args = [u_all]
    if reverse:
        in_specs += [pl.BlockSpec(blk, lambda i: (lat(i) + N_CTX_CHUNKS, 1)),
                     pl.BlockSpec(blk, lambda i: (lat(i), 0))]
        args += [u_all, hf]
    in_specs += [_const_spec((LRU_CONV, LRU_WIDTH)), _const_spec((1, LRU_WIDTH)),
                 _const_spec(wt[d].shape), _const_spec((1, LRU_WIDTH)),
                 _const_spec((1, LRU_WIDTH)), _const_spec((1, LRU_WIDTH))]
    args += [cw[d], cb[d][None], wt[d], ba[d][None], bx[d][None], lam[d][None]]
    halo_rows = (LRU_CONV - 1) * BATCH
    return pl.pallas_call(
        functools.partial(_lru_kernel, reverse=reverse),
        grid=(N_CHUNKS,),
        in_specs=in_specs,
        out_specs=pl.BlockSpec(blk, lambda i: (lat(i), 0)),
        out_shape=jax.ShapeDtypeStruct((SEQ * BATCH, LRU_WIDTH), F32),
        scratch_shapes=[pltpu.VMEM((BATCH, LRU_WIDTH), F32),
                        pltpu.VMEM((halo_rows, LRU_WIDTH), F32),
                        pltpu.VMEM((CHUNK_ROWS + halo_rows, LRU_WIDTH), F32),
                        pltpu.VMEM(blk, F32),
                        pltpu.VMEM(blk, F32)],
        compiler_params=_params(1),
        name="lru_bwd" if reverse else "lru_fwd",
    )(*args)


PREP_T = 32
PREP_ROWS = PREP_T * BATCH
PREP_CTX_BLOCKS = CTX_LEN // PREP_T
PREP_LAT_BLOCKS = SEQ // PREP_T
PREP_BLOCKS = PREP_CTX_BLOCKS + PREP_LAT_BLOCKS


def _prep_kernel(zc_ref, zl_ref, zcp_ref, zcn_ref, zlp_ref, zln_ref, mu_ref,
                 w2_ref, w0_ref, a2_ref, a0_ref, g2_ref, kk_ref, ka_ref, rk_ref, ones_ref,
                 r_o, v_o, kn_o, ld_o, kd_o, b_o, bon_o, g_o, ext_s):
    i = pl.program_id(0)
    is_ctx = i < PREP_CTX_BLOCKS
    first = (i == 0) | (i == PREP_CTX_BLOCKS)
    last = (i == PREP_CTX_BLOCKS - 1) | (i == PREP_BLOCKS - 1)
    z = jnp.where(is_ctx, zc_ref[...], zl_ref[...])
    zp = jnp.where(is_ctx, zcp_ref[...], zlp_ref[...])
    zn = jnp.where(is_ctx, zcn_ref[...], zln_ref[...])
    ext_s[0:BATCH] = jnp.where(first, 0.0, zp)
    ext_s[BATCH:BATCH + PREP_ROWS] = z
    ext_s[BATCH + PREP_ROWS:2 * BATCH + PREP_ROWS] = jnp.where(last, 0.0, zn)

    def shifted(c0, c1):
        zc = ext_s[BATCH:BATCH + PREP_ROWS, c0:c1]
        zprev = ext_s[0:PREP_ROWS, c0:c1]
        znext = ext_s[2 * BATCH:2 * BATCH + PREP_ROWS, c0:c1]
        return zc + mu_ref[0:1, c0:c1] * (zprev - zc) + mu_ref[1:2, c0:c1] * (znext - zc)

    W = RWKV_WIDTH
    r = shifted(0, W)
    k = shifted(W, 2 * W)
    v = shifted(2 * W, 3 * W)
    wd = shifted(3 * W, 3 * W + 2 * LORA_W)
    ad = shifted(3 * W + 2 * LORA_W, 3 * W + 2 * LORA_W + 2 * LORA_A)
    gd = shifted(3 * W + 2 * LORA_W + 2 * LORA_A, RWKV_IN_PAD)

    w_pre = _bdot(jnp.tanh(wd), w2_ref[...]) + w0_ref[...]
    a_pre = _bdot(ad, a2_ref[...]) + a0_ref[...]
    kk = k * kk_ref[...]
    ss = jnp.dot((kk * kk).astype(BF16), ones_ref[...], preferred_element_type=F32)
    kn = kk / jnp.maximum(jnp.sqrt(ss), L2_EPS)

    r_o[...] = r
    v_o[...] = v
    kn_o[...] = kn
    kd_sum = jnp.zeros_like(k)
    for d in range(2):
        cs = slice(d * W, (d + 1) * W)
        w_log = -_softplus(-w_pre[:, cs]) - 0.5
        ld_o[d] = -jnp.exp(w_log)
        asig = _sigmoid(a_pre[:, cs])
        kd = k * (1.0 + (asig - 1.0) * ka_ref[...])
        kd_o[d] = kd
        b_o[d] = kn * asig
        kd_sum = kd_sum + kd
    bon = jnp.dot((r * kd_sum * rk_ref[...]).astype(BF16), ones_ref[...],
                  preferred_element_type=F32)
    bon_o[...] = bon * v
    g_o[...] = _bdot(_sigmoid(gd), g2_ref[...])


def _rwkv_prep(zr_c, zr_l, rw_p):
    mu, w2, w0, a2, a0, g2, k_k, k_a, r_k, ones = rw_p
    W = RWKV_WIDTH
    hb = PREP_T
    cmain = lambda i: jnp.minimum(i, PREP_CTX_BLOCKS - 1)
    lmain = lambda i: jnp.maximum(i - PREP_CTX_BLOCKS, 0)
    zblk = (PREP_ROWS, RWKV_IN_PAD)
    hblk = (BATCH, RWKV_IN_PAD)
    n_c8 = CTX_LEN - 1
    n_l8 = SEQ - 1
    in_specs = [
        pl.BlockSpec(zblk, lambda i: (cmain(i), 0)),
        pl.BlockSpec(zblk, lambda i: (lmain(i), 0)),
        pl.BlockSpec(hblk, lambda i: (jnp.maximum(cmain(i) * hb - 1, 0), 0)),
        pl.BlockSpec(hblk, lambda i: (jnp.minimum((cmain(i) + 1) * hb, n_c8), 0)),
        pl.BlockSpec(hblk, lambda i: (jnp.maximum(lmain(i) * hb - 1, 0), 0)),
        pl.BlockSpec(hblk, lambda i: (jnp.minimum((lmain(i) + 1) * hb, n_l8), 0)),
        _const_spec(mu.shape), _const_spec(w2.shape), _const_spec(w0.shape),
        _const_spec(a2.shape), _const_spec(a0.shape), _const_spec(g2.shape),
        _const_spec(k_k.shape), _const_spec(k_a.shape), _const_spec(r_k.shape),
        _const_spec(ones.shape),
    ]
    n_rows = T_ALL * BATCH
    n_lat = SEQ * BATCH
    shared = pl.BlockSpec((PREP_ROWS, W), lambda i: (i, 0))
    perdir = pl.BlockSpec((2, PREP_ROWS, W), lambda i: (0, i, 0))
    latonly = pl.BlockSpec((PREP_ROWS, W), lambda i: (lmain(i), 0))
    return pl.pallas_call(
        _prep_kernel,
        grid=(PREP_BLOCKS,),
        in_specs=in_specs,
        out_specs=[shared, shared, shared, perdir, perdir, perdir, latonly, latonly],
        out_shape=[jax.ShapeDtypeStruct((n_rows, W), F32)] * 3
        + [jax.ShapeDtypeStruct((2, n_rows, W), F32)] * 3
        + [jax.ShapeDtypeStruct((n_lat, W), F32)] * 2,
        scratch_shapes=[pltpu.VMEM((PREP_ROWS + 2 * BATCH, RWKV_IN_PAD), F32)],
        compiler_params=_params(1),
        name="rwkv_prep",
    )(zr_c, zr_l, zr_c, zr_c, zr_l, zr_l, mu, w2, w0, a2, a0, g2, k_k, k_a, r_k, ones)


def _wkv_chunk(r, ld, kd, v, kn, b, S, sign):
    T = CHUNK_T
    R2 = 2 * T
    ti = lax.broadcasted_iota(jnp.int32, (T, T), 0)
    si = lax.broadcasted_iota(jnp.int32, (T, T), 1)
    incl = ((ti - si) * sign >= 0)
    lmask = jnp.where(incl, 1.0, 0.0).astype(BF16)
    hi, mid, lo = _split3(ld)
    dot = lambda x: jnp.dot(lmask, x, preferred_element_type=F32)
    cum = dot(hi) + dot(mid) + dot(lo)
    cumx = cum - ld
    tot = jnp.sum(ld, axis=0, keepdims=True)

    e_in = jnp.exp(cum)
    e_out = jnp.exp(-cum)
    e_end = jnp.exp(tot - cum)
    a_t = -kn * jnp.exp(cumx)
    r_t = r * e_in
    b_t = b * e_out
    k_t = kd * e_out
    b_e = b * e_end
    k_e = kd * e_end

    lane = lax.broadcasted_iota(jnp.int32, (R2, PAIR), 1)
    row = lax.broadcasted_iota(jnp.int32, (R2, PAIR), 0)
    head_mask = (lane // RWKV_HEAD) == (row // T)

    def stack_masked(x):
        return jnp.where(head_mask, jnp.concatenate([x, x], axis=0), 0.0)

    def stack(x):
        return jnp.concatenate([x, x], axis=0)

    ri = lax.broadcasted_iota(jnp.int32, (R2, R2), 0)
    ci = lax.broadcasted_iota(jnp.int32, (R2, R2), 1)
    same_head = (ri // T) == (ci // T)
    dt = ((ri % T) - (ci % T)) * sign
    m_strict = same_head & (dt > 0)
    m_incl = same_head & (dt >= 0)

    lhs = jnp.concatenate([stack_masked(a_t), stack_masked(r_t)], axis=0)
    rhs = jnp.concatenate([stack(b_t), stack(k_t), S], axis=0)
    big = _bdot_nt(lhs, rhs)
    a_ab = jnp.where(m_strict, big[0:R2, 0:R2], 0.0)
    a_ak = jnp.where(m_strict, big[0:R2, R2:2 * R2], 0.0)
    a_rb = jnp.where(m_incl, big[R2:2 * R2, 0:R2], 0.0)
    a_rk = jnp.where(m_incl, big[R2:2 * R2, R2:2 * R2], 0.0)
    g_s = big[0:R2, 2 * R2:]
    r_s = big[R2:2 * R2, 2 * R2:]

    v_st = stack_masked(v)
    hv = _bdot(jnp.concatenate([a_ak, a_rk], axis=0), v_st)

    eye = jnp.where(ri == ci, 1.0, 0.0)
    npow = a_ab
    inv = eye + a_ab
    steps = int(math.log2(T))
    for _ in range(steps - 1):
        npow = _bdot(npow, npow)
        inv = inv + _bdot(inv, npow)

    u = _bdot(inv, g_s + hv[0:R2])
    y_st = r_s + _bdot(a_rb, u) + hv[R2:2 * R2]
    y = y_st[0:T] + y_st[T:R2]

    upd = _bdot(jnp.concatenate([u.T, v_st.T], axis=1),
                jnp.concatenate([stack_masked(b_e), stack_masked(k_e)], axis=0))
    s_new = S * jnp.exp(tot) + upd
    return y, s_new


def _wkv_kernel(r_ref, v_ref, kn_ref, ld_ref, kd_ref, b_ref, y_ref, s_ref):
    d = pl.program_id(0)
    c = pl.program_id(2)
    sign = 1 - 2 * d

    @pl.when(c == 0)
    def _():
        s_ref[...] = jnp.zeros_like(s_ref)

    def body(bi, carry):
        rows = pl.ds(bi, CHUNK_T, stride=BATCH)
        y, s_new = _wkv_chunk(r_ref[rows, :], ld_ref[0, rows, :], kd_ref[0, rows, :],
                              v_ref[rows, :], kn_ref[rows, :], b_ref[0, rows, :],
                              s_ref[bi], sign)
        y_ref[0, rows, :] = y
        s_ref[bi] = s_new
        return carry

    lax.fori_loop(0, BATCH, body, 0)


def _wkv(r, v, kn, ld, kd, b):
    def chunk(d, c):
        fwd = c
        bwd = jnp.where(c < N_CTX_CHUNKS, N_CTX_CHUNKS - 1 - c, N_CHUNKS + N_CTX_CHUNKS - 1 - c)
        return jnp.where(d == 0, fwd, bwd)
    lat = lambda d, c: chunk(d, jnp.maximum(c, N_CTX_CHUNKS)) - N_CTX_CHUNKS
    shared = pl.BlockSpec((CHUNK_ROWS, PAIR), lambda d, p, c: (chunk(d, c), p))
    perdir = pl.BlockSpec((1, CHUNK_ROWS, PAIR), lambda d, p, c: (d, chunk(d, c), p))
    return pl.pallas_call(
        _wkv_kernel,
        grid=(2, N_PAIRS, N_CHUNKS),
        in_specs=[shared, shared, shared, perdir, perdir, perdir],
        out_specs=pl.BlockSpec((1, CHUNK_ROWS, PAIR), lambda d, p, c: (d, lat(d, c), p)),
        out_shape=jax.ShapeDtypeStruct((2, SEQ * BATCH, RWKV_WIDTH), F32),
        scratch_shapes=[pltpu.VMEM((BATCH, PAIR, PAIR), F32)],
        compiler_params=_params(3),
        name="wkv_scan",
    )(r, v, kn, ld, kd, b)


MERGE_T = 32
MERGE_ROWS = MERGE_T * BATCH


def _merge_kernel(lru_ref, y_ref, bon_ref, g_ref, gl_ref, gr_ref, x_ref, gm_ref,
                  lng_ref, lnb_ref, ones_ref, wol_ref, wor_ref, wout_ref, o_ref):
    rows = MERGE_ROWS
    W = RWKV_WIDTH
    y = (y_ref[0] + y_ref[1]).reshape(rows, W)
    inv_n = 1.0 / RWKV_HEAD

    def seg_sum(t):
        hi = t.astype(BF16)
        lo = (t - hi.astype(F32)).astype(BF16)
        return (jnp.dot(hi, ones_ref[...], preferred_element_type=F32)
                + jnp.dot(lo, ones_ref[...], preferred_element_type=F32))

    mu = seg_sum(y) * inv_n
    dy = y - mu
    var = seg_sum(dy * dy) * inv_n
    yn = dy * lax.rsqrt(var + GN_EPS) * lng_ref[...] + lnb_ref[...]
    rw = (yn + bon_ref[...].reshape(rows, W)) * g_ref[...].reshape(rows, W)
    m = (_sigmoid(gl_ref[...]) * _bdot(lru_ref[...], wol_ref[...])
         + _sigmoid(gr_ref[...]) * _bdot(rw, wor_ref[...]))
    mix = _bdot(m, wout_ref[...])
    o_ref[...] = x_ref[...] + _bcast_rows(gm_ref[...], rows) * mix


def _merge(lru_l, y, bon, g, gates, x_tb, g_m8, ln_g, ln_b, ones, wol, wor, wout):
    W = RWKV_WIDTH
    n_blocks = SEQ // MERGE_T
    per_row = GRID_W // MERGE_T
    cm4 = lambda i: (i % per_row, i // per_row, 0, 0)
    y5 = y.reshape(2, GRID_W, GRID_ROWS, BATCH, W)
    bon4 = bon.reshape(GRID_W, GRID_ROWS, BATCH, W)
    g4 = g.reshape(GRID_W, GRID_ROWS, BATCH, W)
    first_lat = CTX_LEN // MERGE_T
    return pl.pallas_call(
        _merge_kernel,
        grid=(n_blocks,),
        in_specs=[
            pl.BlockSpec((MERGE_ROWS, LRU_WIDTH), lambda i: (i, 0)),
            pl.BlockSpec((2, MERGE_T, 1, BATCH, W), lambda i: (0,) + cm4(i)),
            pl.BlockSpec((MERGE_T, 1, BATCH, W), cm4),
            pl.BlockSpec((MERGE_T, 1, BATCH, W), cm4),
            pl.BlockSpec((MERGE_ROWS, D_MODEL), lambda i: (i, 0)),
            pl.BlockSpec((MERGE_ROWS, D_MODEL), lambda i: (i, 1)),
            pl.BlockSpec((MERGE_ROWS, D_MODEL), lambda i: (i + first_lat, 0)),
            _const_spec((BATCH, D_MODEL)),
            _const_spec((1, W)), _const_spec((1, W)), _const_spec(ones.shape),
            _const_spec(wol.shape), _const_spec(wor.shape), _const_spec(wout.shape),
        ],
        out_specs=pl.BlockSpec((MERGE_ROWS, D_MODEL), lambda i: (i, 0)),
        out_shape=jax.ShapeDtypeStruct((SEQ * BATCH, D_MODEL), F32),
        compiler_params=_params(1),
        name="merge",
    )(lru_l, y5, bon4, g4, gates, gates, x_tb, g_m8, ln_g, ln_b, ones, wol, wor, wout)


FFN_ROWS = 512
FFN_TILE = 256


def _ffn_up_kernel(x_ref, g_ref, sc_ref, sh_ref, w_ref, o_ref):
    h = _rms_modulate(x_ref[...], g_ref[...], sc_ref[...], sh_ref[...]).astype(BF16)
    for j in range(D_FF // FFN_TILE):
        cs = slice(j * FFN_TILE, (j + 1) * FFN_TILE)
        gate = jnp.dot(h, w_ref[:, cs], preferred_element_type=F32)
        up = jnp.dot(h, w_ref[:, D_FF + j * FFN_TILE:D_FF + (j + 1) * FFN_TILE],
                     preferred_element_type=F32)
        o_ref[:, cs] = (_silu(gate) * up).astype(BF16)


def _ffn_up(x1, g, sc8, sh8, w_in):
    return pl.pallas_call(
        _ffn_up_kernel,
        grid=(SEQ * BATCH // FFN_ROWS,),
        in_specs=[pl.BlockSpec((FFN_ROWS, D_MODEL), lambda i: (i, 0)),
                  _const_spec((1, D_MODEL)), _const_spec((BATCH, D_MODEL)),
                  _const_spec((BATCH, D_MODEL)), _const_spec(w_in.shape)],
        out_specs=pl.BlockSpec((FFN_ROWS, D_FF), lambda i: (i, 0)),
        out_shape=jax.ShapeDtypeStruct((SEQ * BATCH, D_FF), BF16),
        compiler_params=_params(1),
        name="ffn_up",
    )(x1, g, sc8, sh8, w_in)


def _ffn_down_kernel(act_ref, x_ref, gf_ref, w_ref, gfin_ref, o_ref):
    y = jnp.dot(act_ref[...], w_ref[...], preferred_element_type=F32)
    x2 = x_ref[...] + _bcast_rows(gf_ref[...], FFN_ROWS) * y
    ms = jnp.mean(x2 * x2, axis=-1, keepdims=True)
    o_ref[...] = x2 * lax.rsqrt(ms + RMS_EPS) * gfin_ref[...]


def _ffn_down(act, x1, g_f8, w_out, g_final):
    return pl.pallas_call(
        _ffn_down_kernel,
        grid=(SEQ * BATCH // FFN_ROWS,),
        in_specs=[pl.BlockSpec((FFN_ROWS, D_FF), lambda i: (i, 0)),
                  pl.BlockSpec((FFN_ROWS, D_MODEL), lambda i: (i, 0)),
                  _const_spec((BATCH, D_MODEL)), _const_spec(w_out.shape),
                  _const_spec((1, D_MODEL))],
        out_specs=pl.BlockSpec((FFN_ROWS, D_MODEL), lambda i: (i, 0)),
        out_shape=jax.ShapeDtypeStruct((SEQ * BATCH, D_MODEL), F32),
        compiler_params=_params(1),
        name="ffn_down",
    )(act, x1, g_f8, w_out, g_final)


def _block_diag(w):
    n, c, _ = w.shape
    eye = jnp.eye(n, dtype=w.dtype)
    return jnp.einsum('ncd,nm->ncmd', w, eye).reshape(n * c, n * c)


def _lru_gate_tiles(wa, wx):
    da, dx = _block_diag(wa), _block_diag(wx)
    tiles = []
    for j, k0 in enumerate(LRU_GATE_K0):
        cs = slice(j * LRU_GATE_TILE, (j + 1) * LRU_GATE_TILE)
        tiles.append(jnp.concatenate([da[k0:k0 + LRU_GATE_K, cs], dx[k0:k0 + LRU_GATE_K, cs]], axis=1))
    return jnp.stack(tiles).astype(BF16)


def _two_dir_lora(w):
    z = jnp.zeros_like(w[0])
    return jnp.concatenate([jnp.concatenate([w[0], z], axis=1),
                            jnp.concatenate([z, w[1]], axis=1)], axis=0).astype(BF16)


def kernel(x, c, ctx, c_ctx, norm_mix_g, norm_ffn_g, w_mod, b_mod, w_in, lru_conv_w, lru_conv_b, lru_wa, lru_ba, lru_wx, lru_bx, lru_lambda, w_o_lru, rwkv_mu, rwkv_w0, rwkv_w2, rwkv_a0, rwkv_a2, rwkv_g2, rwkv_k_k, rwkv_k_a, rwkv_r_k, rwkv_ln_g, rwkv_ln_b, w_o_rwkv, w_out, w_ffn_in, w_ffn_out, norm_final_g):
    assert x.shape == (BATCH, SEQ, D_MODEL) and ctx.shape == (BATCH, CTX_LEN, D_MODEL)
    assert w_mod.shape[0] == 1, "single layer only"
    D, W = D_MODEL, RWKV_WIDTH

    x_tb = jnp.concatenate([ctx.transpose(1, 0, 2), x.transpose(1, 0, 2)], axis=0)
    x_tb = x_tb.reshape(T_ALL * BATCH, D)

    c16 = jnp.concatenate([c, c_ctx[None], jnp.zeros((16 - BATCH - 1, D), F32)], axis=0)
    mod = _adaln(c16, w_mod[0], b_mod[0][None])
    mod_lat = mod[:BATCH].reshape(BATCH, 6, D)
    mod_ctx = jnp.broadcast_to(mod[BATCH:BATCH + 1], (BATCH, 6 * D)).reshape(BATCH, 6, D)
    sh_m, sc_m, g_m, sh_f, sc_f, g_f = [mod_lat[:, k] for k in range(6)]
    sh2 = jnp.stack([mod_ctx[:, 0], sh_m])
    sc2 = jnp.stack([mod_ctx[:, 1], sc_m])

    w_in0 = w_in[0]
    n_lru = 2 * LRU_WIDTH
    w_lru = w_in0[:, :n_lru].astype(BF16)
    w_rw = jnp.pad(w_in0[:, n_lru:n_lru + RWKV_IN], ((0, 0), (0, RWKV_IN_PAD - RWKV_IN))).astype(BF16)
    w_gate = w_in0[:, n_lru + RWKV_IN:].astype(BF16)
    g_mix = norm_mix_g[0][None]

    u_all = _proj(x_tb, g_mix, sc2, sh2, w_lru, first_chunk=0, n_chunks=N_CHUNKS, colmajor_out=False)
    zr_c = _proj(x_tb, g_mix, sc2, sh2, w_rw, first_chunk=0, n_chunks=N_CTX_CHUNKS, colmajor_out=False)
    zr_l = _proj(x_tb, g_mix, sc2, sh2, w_rw, first_chunk=N_CTX_CHUNKS, n_chunks=N_LAT_CHUNKS,
                 colmajor_out=True).reshape(SEQ * BATCH, RWKV_IN_PAD)
    gates = _proj(x_tb, g_mix, sc2, sh2, w_gate, first_chunk=N_CTX_CHUNKS, n_chunks=N_LAT_CHUNKS,
                  colmajor_out=False)

    wt = jnp.stack([_lru_gate_tiles(lru_wa[0, d], lru_wx[0, d]) for d in range(2)])
    lru_p = (lru_conv_w[0], lru_conv_b[0], wt, lru_ba[0], lru_bx[0], lru_lambda[0])
    hf = _lru_scan(u_all, None, lru_p, 0, reverse=False)
    lru_l = _lru_scan(u_all, hf, lru_p, 1, reverse=True)

    mu_pad = jnp.pad(rwkv_mu[0], ((0, 0), (0, RWKV_IN_PAD - RWKV_IN)))
    ones = _block_diag(jnp.ones((W // RWKV_HEAD, RWKV_HEAD, RWKV_HEAD), F32)).astype(BF16)
    rw_p = (mu_pad, _two_dir_lora(rwkv_w2[0]), rwkv_w0[0].reshape(1, 2 * W),
            _two_dir_lora(rwkv_a2[0]), rwkv_a0[0].reshape(1, 2 * W),
            jnp.pad(rwkv_g2[0], ((0, LORA_G_PAD - LORA_G), (0, 0))).astype(BF16),
            rwkv_k_k[0][None], rwkv_k_a[0][None], rwkv_r_k[0].reshape(1, W), ones)
    r, v, kn, ld, kd, b, bon, g = _rwkv_prep(zr_c, zr_l, rw_p)
    y = _wkv(r, v, kn, ld, kd, b)

    x1 = _merge(lru_l, y, bon, g, gates, x_tb, g_m, rwkv_ln_g[0][None], rwkv_ln_b[0][None], ones,
                w_o_lru[0].astype(BF16), w_o_rwkv[0].astype(BF16), w_out[0].astype(BF16))

    act = _ffn_up(x1, norm_ffn_g[0][None], sc_f, sh_f, w_ffn_in[0].astype(BF16))
    out_tb = _ffn_down(act, x1, g_f, w_ffn_out[0].astype(BF16), norm_final_g[None])
    return out_tb.reshape(SEQ, BATCH, D).transpose(1, 0, 2)
```

```python
import functools
import math

import jax
import jax.numpy as jnp
from jax import lax
from jax.experimental import pallas as pl
from jax.experimental.pallas import tpu as pltpu

F32 = jnp.float32
BF16 = jnp.bfloat16

D_MODEL = 1024
BATCH = 8
SEQ = 2048
CTX_LEN = 256
GRID_W = 64
GRID_ROWS = SEQ // GRID_W

LRU_WIDTH = 1280
LRU_BLOCKS = 16
LRU_BLOCK = LRU_WIDTH // LRU_BLOCKS
LRU_CONV = 4
LRU_C = 8.0

RWKV_HEAD = 64
RWKV_WIDTH = 1024
LORA_W = 64
LORA_A = 64
LORA_G = 160
RWKV_IN = 3 * RWKV_WIDTH + 2 * LORA_W + 2 * LORA_A + LORA_G
RWKV_IN_PAD = 3584
LORA_G_PAD = RWKV_IN_PAD - (3 * RWKV_WIDTH + 2 * LORA_W + 2 * LORA_A)
D_FF = 2816

RMS_EPS = 1e-6
GN_EPS = 64e-5
L2_EPS = 1e-12

T_ALL = CTX_LEN + SEQ
CHUNK_T = 64
CHUNK_ROWS = CHUNK_T * BATCH
N_CTX_CHUNKS = CTX_LEN // CHUNK_T
N_LAT_CHUNKS = SEQ // CHUNK_T
N_CHUNKS = N_CTX_CHUNKS + N_LAT_CHUNKS

PAIR = 2 * RWKV_HEAD
N_PAIRS = RWKV_WIDTH // PAIR

LRU_GATE_TILE = 256
LRU_GATE_K = 512
LRU_GATE_K0 = (0, 128, 384, 640, 768)

VMEM_LIMIT = 56 * 1024 * 1024


def _params(n_axes):
    return pltpu.CompilerParams(dimension_semantics=("arbitrary",) * n_axes,
                                vmem_limit_bytes=VMEM_LIMIT)


def _const_spec(shape):
    nd = len(shape)
    return pl.BlockSpec(shape, lambda *_: (0,) * nd)


def _bdot(a, b):
    return jnp.dot(a.astype(BF16), b.astype(BF16), preferred_element_type=F32)


def _bdot_nt(a, b):
    return lax.dot_general(a.astype(BF16), b.astype(BF16), (((1,), (1,)), ((), ())),
                           preferred_element_type=F32)


def _split3(x):
    hi = x.astype(BF16)
    r1 = x - hi.astype(F32)
    mid = r1.astype(BF16)
    lo = (r1 - mid.astype(F32)).astype(BF16)
    return hi, mid, lo


def _softplus(x):
    return jnp.maximum(x, 0.0) + jnp.log1p(jnp.exp(-jnp.abs(x)))


def _sigmoid(x):
    return 1.0 / (1.0 + jnp.exp(-x))


def _silu(x):
    return x * _sigmoid(x)


def _gelu_tanh(x):
    c = math.sqrt(2.0 / math.pi)
    return 0.5 * x * (1.0 + jnp.tanh(c * (x + 0.044715 * (x * x * x))))


def _bcast_rows(v8, rows):
    c = v8.shape[-1]
    return jnp.broadcast_to(v8[None], (rows // BATCH, BATCH, c)).reshape(rows, c)


def _rms_modulate(x, g, scale8, shift8):
    rows = x.shape[0]
    ms = jnp.mean(x * x, axis=-1, keepdims=True)
    y = x * lax.rsqrt(ms + RMS_EPS) * g
    return y * (1.0 + _bcast_rows(scale8, rows)) + _bcast_rows(shift8, rows)


def _adaln_kernel(c_ref, w_ref, b_ref, o_ref):
    s = _silu(c_ref[...])
    o_ref[...] = jnp.dot(s, w_ref[...], preferred_element_type=F32,
                         precision=lax.Precision.HIGHEST) + b_ref[...]


def _adaln(c16, w_mod, b_mod):
    n = w_mod.shape[1]
    tn = 1536
    return pl.pallas_call(
        _adaln_kernel,
        grid=(n // tn,),
        in_specs=[_const_spec((16, D_MODEL)),
                  pl.BlockSpec((D_MODEL, tn), lambda j: (0, j)),
                  pl.BlockSpec((1, tn), lambda j: (0, j))],
        out_specs=pl.BlockSpec((16, tn), lambda j: (0, j)),
        out_shape=jax.ShapeDtypeStruct((16, n), F32),
        compiler_params=_params(1),
        name="adaln",
    )(c16, w_mod, b_mod)


def _proj_kernel(x_ref, g_ref, sc_ref, sh_ref, w_ref, o_ref):
    h = _rms_modulate(x_ref[...], g_ref[...], sc_ref[0], sh_ref[0])
    res = jnp.dot(h.astype(BF16), w_ref[...], preferred_element_type=F32)
    o_ref[...] = res.reshape(o_ref.shape)


def _proj(x_tb, g, sc2, sh2, w, *, first_chunk, n_chunks, colmajor_out):
    n = w.shape[1]
    seg = lambda i: ((i + first_chunk) >= N_CTX_CHUNKS).astype(jnp.int32)
    if colmajor_out:
        out_shape = jax.ShapeDtypeStruct((GRID_W, GRID_ROWS, BATCH, n), F32)
        out_spec = pl.BlockSpec((GRID_W, 1, BATCH, n), lambda i: (0, i, 0, 0))
    else:
        out_shape = jax.ShapeDtypeStruct((n_chunks * CHUNK_ROWS, n), F32)
        out_spec = pl.BlockSpec((CHUNK_ROWS, n), lambda i: (i, 0))
    return pl.pallas_call(
        _proj_kernel,
        grid=(n_chunks,),
        in_specs=[pl.BlockSpec((CHUNK_ROWS, D_MODEL), lambda i: (i + first_chunk, 0)),
                  _const_spec((1, D_MODEL)),
                  pl.BlockSpec((1, BATCH, D_MODEL), lambda i: (seg(i), 0, 0)),
                  pl.BlockSpec((1, BATCH, D_MODEL), lambda i: (seg(i), 0, 0)),
                  _const_spec((D_MODEL, n))],
        out_specs=out_spec,
        out_shape=out_shape,
        compiler_params=_params(1),
        name="in_proj",
    )(x_tb, g, sc2, sh2, w)


def _lru_kernel(*refs, reverse):
    if reverse:
        (u_ref, uy_ref, hf_ref, cw_ref, cb_ref, wt_ref, ba_ref, bx_ref, lam_ref,
         o_ref, h_s, halo_s, ext_s, a_s, b_s) = refs
    else:
        (u_ref, cw_ref, cb_ref, wt_ref, ba_ref, bx_ref, lam_ref,
         o_ref, h_s, halo_s, ext_s, a_s, b_s) = refs
    i = pl.program_id(0)
    halo_rows = (LRU_CONV - 1) * BATCH

    @pl.when(i == 0)
    def _():
        h_s[...] = jnp.zeros_like(h_s)

    @pl.when((i == 0) | (i == N_CTX_CHUNKS))
    def _():
        halo_s[...] = jnp.zeros_like(halo_s)

    u = u_ref[...]
    if reverse:
        ext_s[0:CHUNK_ROWS] = u
        ext_s[CHUNK_ROWS:CHUNK_ROWS + halo_rows] = halo_s[...]
        halo_s[...] = u[0:halo_rows]
    else:
        ext_s[0:halo_rows] = halo_s[...]
        ext_s[halo_rows:halo_rows + CHUNK_ROWS] = u
        halo_s[...] = u[CHUNK_ROWS - halo_rows:CHUNK_ROWS]

    xc = jnp.broadcast_to(cb_ref[...], (CHUNK_ROWS, LRU_WIDTH))
    for j in range(LRU_CONV):
        off = (LRU_CONV - 1 - j) if reverse else j
        xc = xc + cw_ref[j:j + 1, :] * ext_s[off * BATCH:off * BATCH + CHUNK_ROWS]

    xcb = xc.astype(BF16)
    sp = _softplus(-lam_ref[...])
    for j in range(LRU_WIDTH // LRU_GATE_TILE):
        k0 = LRU_GATE_K0[j]
        cs = slice(j * LRU_GATE_TILE, (j + 1) * LRU_GATE_TILE)
        g = jnp.dot(xcb[:, k0:k0 + LRU_GATE_K], wt_ref[j], preferred_element_type=F32)
        gate_r = _sigmoid(g[:, :LRU_GATE_TILE] + ba_ref[:, cs])
        gate_i = _sigmoid(g[:, LRU_GATE_TILE:] + bx_ref[:, cs])
        log_a = -LRU_C * gate_r * sp[:, cs]
        a = jnp.exp(log_a)
        a_s[:, cs] = a
        b_s[:, cs] = jnp.sqrt(1.0 - a * a) * (gate_i * xc[:, cs])

    def step(k, h):
        t = (CHUNK_T - 1 - k) if reverse else k
        rows = pl.ds(pl.multiple_of(t * BATCH, BATCH), BATCH)
        h = a_s[rows, :] * h + b_s[rows, :]
        if reverse:
            o_ref[rows, :] = (hf_ref[rows, :] + h) * _gelu_tanh(uy_ref[rows, :])
        else:
            o_ref[rows, :] = h
        return h

    h_s[...] = lax.fori_loop(0, CHUNK_T, step, h_s[...], unroll=8)


def _lru_scan(u_all, hf, lru_p, d, *, reverse):
    cw, cb, wt, ba, bx, lam = lru_p
    if reverse:
        chunk = lambda i: jnp.where(i < N_CTX_CHUNKS, N_CTX_CHUNKS - 1 - i,
                                    N_CHUNKS + N_CTX_CHUNKS - 1 - i)
    else:
        chunk = lambda i: i
    lat = lambda i: jnp.maximum(chunk(jnp.maximum(i, N_CTX_CHUNKS)) - N_CTX_CHUNKS, 0)
    blk = (CHUNK_ROWS, LRU_WIDTH)
    in_specs = [pl.BlockSpec(blk, lambda i: (chunk(i), 0))]
    args = [u_all]
    if reverse:
        in_specs += [pl.BlockSpec(blk, lambda i: (lat(i) + N_CTX_CHUNKS, 1)),
                     pl.BlockSpec(blk, lambda i: (lat(i), 0))]
        args += [u_all, hf]
    in_specs += [_const_spec((LRU_CONV, LRU_WIDTH)), _const_spec((1, LRU_WIDTH)),
                 _const_spec(wt[d].shape), _const_spec((1, LRU_WIDTH)),
                 _const_spec((1, LRU_WIDTH)), _const_spec((1, LRU_WIDTH))]
    args += [cw[d], cb[d][None], wt[d], ba[d][None], bx[d][None], lam[d][None]]
    halo_rows = (LRU_CONV - 1) * BATCH
    return pl.pallas_call(
        functools.partial(_lru_kernel, reverse=reverse),
        grid=(N_CHUNKS,),
        in_specs=in_specs,
        out_specs=pl.BlockSpec(blk, lambda i: (lat(i), 0)),
        out_shape=jax.ShapeDtypeStruct((SEQ * BATCH, LRU_WIDTH), F32),
        scratch_shapes=[pltpu.VMEM((BATCH, LRU_WIDTH), F32),
                        pltpu.VMEM((halo_rows, LRU_WIDTH), F32),
                        pltpu.VMEM((CHUNK_ROWS + halo_rows, LRU_WIDTH), F32),
                        pltpu.VMEM(blk, F32),
                        pltpu.VMEM(blk, F32)],
        compiler_params=_params(1),
        name="lru_bwd" if reverse else "lru_fwd",
    )(*args)


PREP_T = 32
PREP_ROWS = PREP_T * BATCH
PREP_CTX_BLOCKS = CTX_LEN // PREP_T
PREP_LAT_BLOCKS = SEQ // PREP_T
PREP_BLOCKS = PREP_CTX_BLOCKS + PREP_LAT_BLOCKS


def _prep_kernel(zc_ref, zl_ref, zcp_ref, zcn_ref, zlp_ref, zln_ref, mu_ref,
                 w2_ref, w0_ref, a2_ref, a0_ref, g2_ref, kk_ref, ka_ref, rk_ref, ones_ref,
                 r_o, v_o, kn_o, ld_o, kd_o, b_o, bon_o, g_o, ext_s):
    i = pl.program_id(0)
    is_ctx = i < PREP_CTX_BLOCKS
    first = (i == 0) | (i == PREP_CTX_BLOCKS)
    last = (i == PREP_CTX_BLOCKS - 1) | (i == PREP_BLOCKS - 1)
    z = jnp.where(is_ctx, zc_ref[...], zl_ref[...])
    zp = jnp.where(is_ctx, zcp_ref[...], zlp_ref[...])
    zn = jnp.where(is_ctx, zcn_ref[...], zln_ref[...])
    ext_s[0:BATCH] = jnp.where(first, 0.0, zp)
    ext_s[BATCH:BATCH + PREP_ROWS] = z
    ext_s[BATCH + PREP_ROWS:2 * BATCH + PREP_ROWS] = jnp.where(last, 0.0, zn)

    def shifted(c0, c1):
        zc = ext_s[BATCH:BATCH + PREP_ROWS, c0:c1]
        zprev = ext_s[0:PREP_ROWS, c0:c1]
        znext = ext_s[2 * BATCH:2 * BATCH + PREP_ROWS, c0:c1]
        return zc + mu_ref[0:1, c0:c1] * (zprev - zc) + mu_ref[1:2, c0:c1] * (znext - zc)

    W = RWKV_WIDTH
    r = shifted(0, W)
    k = shifted(W, 2 * W)
    v = shifted(2 * W, 3 * W)
    wd = shifted(3 * W, 3 * W + 2 * LORA_W)
    ad = shifted(3 * W + 2 * LORA_W, 3 * W + 2 * LORA_W + 2 * LORA_A)
    gd = shifted(3 * W + 2 * LORA_W + 2 * LORA_A, RWKV_IN_PAD)

    w_pre = _bdot(jnp.tanh(wd), w2_ref[...]) + w0_ref[...]
    a_pre = _bdot(ad, a2_ref[...]) + a0_ref[...]
    kk = k * kk_ref[...]
    ss = jnp.dot((kk * kk).astype(BF16), ones_ref[...], preferred_element_type=F32)
    kn = kk / jnp.maximum(jnp.sqrt(ss), L2_EPS)

    r_o[...] = r
    v_o[...] = v
    kn_o[...] = kn
    kd_sum = jnp.zeros_like(k)
    for d in range(2):
        cs = slice(d * W, (d + 1) * W)
        w_log = -_softplus(-w_pre[:, cs]) - 0.5
        ld_o[d] = -jnp.exp(w_log)
        asig = _sigmoid(a_pre[:, cs])
        kd = k * (1.0 + (asig - 1.0) * ka_ref[...])
        kd_o[d] = kd
        b_o[d] = kn * asig
        kd_sum = kd_sum + kd
    bon = jnp.dot((r * kd_sum * rk_ref[...]).astype(BF16), ones_ref[...],
                  preferred_element_type=F32)
    bon_o[...] = bon * v
    g_o[...] = _bdot(_sigmoid(gd), g2_ref[...])


def _rwkv_prep(zr_c, zr_l, rw_p):
    mu, w2, w0, a2, a0, g2, k_k, k_a, r_k, ones = rw_p
    W = RWKV_WIDTH
    hb = PREP_T
    cmain = lambda i: jnp.minimum(i, PREP_CTX_BLOCKS - 1)
    lmain = lambda i: jnp.maximum(i - PREP_CTX_BLOCKS, 0)
    zblk = (PREP_ROWS, RWKV_IN_PAD)
    hblk = (BATCH, RWKV_IN_PAD)
    n_c8 = CTX_LEN - 1
    n_l8 = SEQ - 1
    in_specs = [
        pl.BlockSpec(zblk, lambda i: (cmain(i), 0)),
        pl.BlockSpec(zblk, lambda i: (lmain(i), 0)),
        pl.BlockSpec(hblk, lambda i: (jnp.maximum(cmain(i) * hb - 1, 0), 0)),
        pl.BlockSpec(hblk, lambda i: (jnp.minimum((cmain(i) + 1) * hb, n_c8), 0)),
        pl.BlockSpec(hblk, lambda i: (jnp.maximum(lmain(i) * hb - 1, 0), 0)),
        pl.BlockSpec(hblk, lambda i: (jnp.minimum((lmain(i) + 1) * hb, n_l8), 0)),
        _const_spec(mu.shape), _const_spec(w2.shape), _const_spec(w0.shape),
        _const_spec(a2.shape), _const_spec(a0.shape), _const_spec(g2.shape),
        _const_spec(k_k.shape), _const_spec(k_a.shape), _const_spec(r_k.shape),
        _const_spec(ones.shape),
    ]
    n_rows = T_ALL * BATCH
    n_lat = SEQ * BATCH
    shared = pl.BlockSpec((PREP_ROWS, W), lambda i: (i, 0))
    perdir = pl.BlockSpec((2, PREP_ROWS, W), lambda i: (0, i, 0))
    latonly = pl.BlockSpec((PREP_ROWS, W), lambda i: (lmain(i), 0))
    return pl.pallas_call(
        _prep_kernel,
        grid=(PREP_BLOCKS,),
        in_specs=in_specs,
        out_specs=[shared, shared, shared, perdir, perdir, perdir, latonly, latonly],
        out_shape=[jax.ShapeDtypeStruct((n_rows, W), F32)] * 3
        + [jax.ShapeDtypeStruct((2, n_rows, W), F32)] * 3
        + [jax.ShapeDtypeStruct((n_lat, W), F32)] * 2,
        scratch_shapes=[pltpu.VMEM((PREP_ROWS + 2 * BATCH, RWKV_IN_PAD), F32)],
        compiler_params=_params(1),
        name="rwkv_prep",
    )(zr_c, zr_l, zr_c, zr_c, zr_l, zr_l, mu, w2, w0, a2, a0, g2, k_k, k_a, r_k, ones)


def _wkv_chunks(inputs, states, sign):
    T = CHUNK_T
    R2 = 2 * T
    nb = len(inputs)
    ti = lax.broadcasted_iota(jnp.int32, (T, T), 0)
    si = lax.broadcasted_iota(jnp.int32, (T, T), 1)
    lmask = jnp.where((ti - si) * sign >= 0, 1.0, 0.0).astype(BF16)

    lane = lax.broadcasted_iota(jnp.int32, (R2, PAIR), 1)
    row = lax.broadcasted_iota(jnp.int32, (R2, PAIR), 0)
    head_mask = (lane // RWKV_HEAD) == (row // T)

    def stack_masked(x):
        return jnp.where(head_mask, jnp.concatenate([x, x], axis=0), 0.0).astype(BF16)

    def stack(x):
        return jnp.concatenate([x, x], axis=0).astype(BF16)

    ri = lax.broadcasted_iota(jnp.int32, (R2, R2), 0)
    ci = lax.broadcasted_iota(jnp.int32, (R2, R2), 1)
    same_head = (ri // T) == (ci // T)
    dt = ((ri % T) - (ci % T)) * sign
    m_strict = same_head & (dt > 0)
    m_incl = same_head & (dt >= 0)
    eye = jnp.where(ri == ci, 1.0, 0.0)

    lhs, rhs, v_st, end_st, s_dec = [], [], [], [], []
    for (r, ld, kd, v, kn, b), S in zip(inputs, states):
        hi, mid, lo = _split3(ld)
        dot = lambda x: jnp.dot(lmask, x, preferred_element_type=F32)
        cum = dot(hi) + dot(mid) + dot(lo)
        tot = jnp.sum(ld, axis=0, keepdims=True)
        e_in = jnp.exp(cum)
        e_out = jnp.exp(-cum)
        e_end = jnp.exp(tot - cum)
        a_t = -kn * jnp.exp(cum - ld)
        lhs.append(jnp.concatenate([stack_masked(a_t), stack_masked(r * e_in)], axis=0))
        rhs.append(jnp.concatenate([stack(b * e_out), stack(kd * e_out), S.astype(BF16)], axis=0))
        v_st.append(stack_masked(v))
        end_st.append(jnp.concatenate([stack_masked(b * e_end), stack_masked(kd * e_end)], axis=0))
        s_dec.append(S * jnp.exp(tot))

    nt = lambda x, y: lax.dot_general(x, y, (((1,), (1,)), ((), ())), preferred_element_type=F32)
    mm = lambda x, y: jnp.dot(x, y, preferred_element_type=F32)
    big = [nt(lhs[i], rhs[i]) for i in range(nb)]
    a_ab = [jnp.where(m_strict, g[0:R2, 0:R2], 0.0) for g in big]
    a_kk = [jnp.concatenate([jnp.where(m_strict, g[0:R2, R2:2 * R2], 0.0),
                             jnp.where(m_incl, g[R2:2 * R2, R2:2 * R2], 0.0)], axis=0).astype(BF16)
            for g in big]
    a_rb = [jnp.where(m_incl, g[R2:2 * R2, 0:R2], 0.0).astype(BF16) for g in big]
    hv = [mm(a_kk[i], v_st[i]) for i in range(nb)]

    npow = [a.astype(BF16) for a in a_ab]
    inv = [eye + a for a in a_ab]
    for _ in range(int(math.log2(T)) - 1):
        npow = [mm(p, p).astype(BF16) for p in npow]
        inv = [inv[i] + mm(inv[i].astype(BF16), npow[i]) for i in range(nb)]

    u = [mm(inv[i].astype(BF16), (big[i][0:R2, 2 * R2:] + hv[i][0:R2]).astype(BF16))
         for i in range(nb)]
    ys, s_new = [], []
    for i in range(nb):
        ub = u[i].astype(BF16)
        y_st = big[i][R2:2 * R2, 2 * R2:] + mm(a_rb[i], ub) + hv[i][R2:2 * R2]
        ys.append(y_st[0:T] + y_st[T:R2])
        upd = mm(jnp.concatenate([u[i].T.astype(BF16), v_st[i].T], axis=1), end_st[i])
        s_new.append(s_dec[i] + upd)
    return ys, s_new


def _wkv_kernel(r_ref, v_ref, kn_ref, ld_ref, kd_ref, b_ref, y_ref, s_ref):
    d = pl.program_id(0)
    c = pl.program_id(2)
    sign = 1 - 2 * d

    @pl.when(c == 0)
    def _():
        s_ref[...] = jnp.zeros_like(s_ref)

    rows = [pl.ds(bi, CHUNK_T, stride=BATCH) for bi in range(BATCH)]
    inputs = [(r_ref[rw, :], ld_ref[0, rw, :], kd_ref[0, rw, :], v_ref[rw, :], kn_ref[rw, :],
               b_ref[0, rw, :]) for rw in rows]
    ys, s_new = _wkv_chunks(inputs, [s_ref[bi] for bi in range(BATCH)], sign)
    for bi in range(BATCH):
        y_ref[0, rows[bi], :] = ys[bi]
        s_ref[bi] = s_new[bi]


def _wkv(r, v, kn, ld, kd, b):
    def chunk(d, c):
        fwd = c
        bwd = jnp.where(c < N_CTX_CHUNKS, N_CTX_CHUNKS - 1 - c, N_CHUNKS + N_CTX_CHUNKS - 1 - c)
        return jnp.where(d == 0, fwd, bwd)
    lat = lambda d, c: chunk(d, jnp.maximum(c, N_CTX_CHUNKS)) - N_CTX_CHUNKS
    shared = pl.BlockSpec((CHUNK_ROWS, PAIR), lambda d, p, c: (chunk(d, c), p))
    perdir = pl.BlockSpec((1, CHUNK_ROWS, PAIR), lambda d, p, c: (d, chunk(d, c), p))
    return pl.pallas_call(
        _wkv_kernel,
        grid=(2, N_PAIRS, N_CHUNKS),
        in_specs=[shared, shared, shared, perdir, perdir, perdir],
        out_specs=pl.BlockSpec((1, CHUNK_ROWS, PAIR), lambda d, p, c: (d, lat(d, c), p)),
        out_shape=jax.ShapeDtypeStruct((2, SEQ * BATCH, RWKV_WIDTH), F32),
        scratch_shapes=[pltpu.VMEM((BATCH, PAIR, PAIR), F32)],
        compiler_params=_params(3),
        name="wkv_scan",
    )(r, v, kn, ld, kd, b)


MERGE_T = 32
MERGE_ROWS = MERGE_T * BATCH


def _merge_kernel(lru_ref, y_ref, bon_ref, g_ref, gl_ref, gr_ref, x_ref, gm_ref,
                  lng_ref, lnb_ref, ones_ref, wol_ref, wor_ref, wout_ref, o_ref):
    rows = MERGE_ROWS
    W = RWKV_WIDTH
    y = (y_ref[0] + y_ref[1]).reshape(rows, W)
    inv_n = 1.0 / RWKV_HEAD

    def seg_sum(t):
        hi = t.astype(BF16)
        lo = (t - hi.astype(F32)).astype(BF16)
        return (jnp.dot(hi, ones_ref[...], preferred_element_type=F32)
                + jnp.dot(lo, ones_ref[...], preferred_element_type=F32))

    mu = seg_sum(y) * inv_n
    dy = y - mu
    var = seg_sum(dy * dy) * inv_n
    yn = dy * lax.rsqrt(var + GN_EPS) * lng_ref[...] + lnb_ref[...]
    rw = (yn + bon_ref[...].reshape(rows, W)) * g_ref[...].reshape(rows, W)
    m = (_sigmoid(gl_ref[...]) * _bdot(lru_ref[...], wol_ref[...])
         + _sigmoid(gr_ref[...]) * _bdot(rw, wor_ref[...]))
    mix = _bdot(m, wout_ref[...])
    o_ref[...] = x_ref[...] + _bcast_rows(gm_ref[...], rows) * mix


def _merge(lru_l, y, bon, g, gates, x_tb, g_m8, ln_g, ln_b, ones, wol, wor, wout):
    W = RWKV_WIDTH
    n_blocks = SEQ // MERGE_T
    per_row = GRID_W // MERGE_T
    cm4 = lambda i: (i % per_row, i // per_row, 0, 0)
    y5 = y.reshape(2, GRID_W, GRID_ROWS, BATCH, W)
    bon4 = bon.reshape(GRID_W, GRID_ROWS, BATCH, W)
    g4 = g.reshape(GRID_W, GRID_ROWS, BATCH, W)
    first_lat = CTX_LEN // MERGE_T
    return pl.pallas_call(
        _merge_kernel,
        grid=(n_blocks,),
        in_specs=[
            pl.BlockSpec((MERGE_ROWS, LRU_WIDTH), lambda i: (i, 0)),
            pl.BlockSpec((2, MERGE_T, 1, BATCH, W), lambda i: (0,) + cm4(i)),
            pl.BlockSpec((MERGE_T, 1, BATCH, W), cm4),
            pl.BlockSpec((MERGE_T, 1, BATCH, W), cm4),
            pl.BlockSpec((MERGE_ROWS, D_MODEL), lambda i: (i, 0)),
            pl.BlockSpec((MERGE_ROWS, D_MODEL), lambda i: (i, 1)),
            pl.BlockSpec((MERGE_ROWS, D_MODEL), lambda i: (i + first_lat, 0)),
            _const_spec((BATCH, D_MODEL)),
            _const_spec((1, W)), _const_spec((1, W)), _const_spec(ones.shape),
            _const_spec(wol.shape), _const_spec(wor.shape), _const_spec(wout.shape),
        ],
        out_specs=pl.BlockSpec((MERGE_ROWS, D_MODEL), lambda i: (i, 0)),
        out_shape=jax.ShapeDtypeStruct((SEQ * BATCH, D_MODEL), F32),
        compiler_params=_params(1),
        name="merge",
    )(lru_l, y5, bon4, g4, gates, gates, x_tb, g_m8, ln_g, ln_b, ones, wol, wor, wout)


FFN_ROWS = 512
FFN_TILE = 256


def _ffn_up_kernel(x_ref, g_ref, sc_ref, sh_ref, w_ref, o_ref):
    h = _rms_modulate(x_ref[...], g_ref[...], sc_ref[...], sh_ref[...]).astype(BF16)
    for j in range(D_FF // FFN_TILE):
        cs = slice(j * FFN_TILE, (j + 1) * FFN_TILE)
        gate = jnp.dot(h, w_ref[:, cs], preferred_element_type=F32)
        up = jnp.dot(h, w_ref[:, D_FF + j * FFN_TILE:D_FF + (j + 1) * FFN_TILE],
                     preferred_element_type=F32)
        o_ref[:, cs] = (_silu(gate) * up).astype(BF16)


def _ffn_up(x1, g, sc8, sh8, w_in):
    return pl.pallas_call(
        _ffn_up_kernel,
        grid=(SEQ * BATCH // FFN_ROWS,),
        in_specs=[pl.BlockSpec((FFN_ROWS, D_MODEL), lambda i: (i, 0)),
                  _const_spec((1, D_MODEL)), _const_spec((BATCH, D_MODEL)),
                  _const_spec((BATCH, D_MODEL)), _const_spec(w_in.shape)],
        out_specs=pl.BlockSpec((FFN_ROWS, D_FF), lambda i: (i, 0)),
        out_shape=jax.ShapeDtypeStruct((SEQ * BATCH, D_FF), BF16),
        compiler_params=_params(1),
        name="ffn_up",
    )(x1, g, sc8, sh8, w_in)


def _ffn_down_kernel(act_ref, x_ref, gf_ref, w_ref, gfin_ref, o_ref):
    y = jnp.dot(act_ref[...], w_ref[...], preferred_element_type=F32)
    x2 = x_ref[...] + _bcast_rows(gf_ref[...], FFN_ROWS) * y
    ms = jnp.mean(x2 * x2, axis=-1, keepdims=True)
    o_ref[...] = x2 * lax.rsqrt(ms + RMS_EPS) * gfin_ref[...]


def _ffn_down(act, x1, g_f8, w_out, g_final):
    return pl.pallas_call(
        _ffn_down_kernel,
        grid=(SEQ * BATCH // FFN_ROWS,),
        in_specs=[pl.BlockSpec((FFN_ROWS, D_FF), lambda i: (i, 0)),
                  pl.BlockSpec((FFN_ROWS, D_MODEL), lambda i: (i, 0)),
                  _const_spec((BATCH, D_MODEL)), _const_spec(w_out.shape),
                  _const_spec((1, D_MODEL))],
        out_specs=pl.BlockSpec((FFN_ROWS, D_MODEL), lambda i: (i, 0)),
        out_shape=jax.ShapeDtypeStruct((SEQ * BATCH, D_MODEL), F32),
        compiler_params=_params(1),
        name="ffn_down",
    )(act, x1, g_f8, w_out, g_final)


def _block_diag(w):
    n, c, _ = w.shape
    eye = jnp.eye(n, dtype=w.dtype)
    return jnp.einsum('ncd,nm->ncmd', w, eye).reshape(n * c, n * c)


def _lru_gate_tiles(wa, wx):
    da, dx = _block_diag(wa), _block_diag(wx)
    tiles = []
    for j, k0 in enumerate(LRU_GATE_K0):
        cs = slice(j * LRU_GATE_TILE, (j + 1) * LRU_GATE_TILE)
        tiles.append(jnp.concatenate([da[k0:k0 + LRU_GATE_K, cs], dx[k0:k0 + LRU_GATE_K, cs]], axis=1))
    return jnp.stack(tiles).astype(BF16)


def _two_dir_lora(w):
    z = jnp.zeros_like(w[0])
    return jnp.concatenate([jnp.concatenate([w[0], z], axis=1),
                            jnp.concatenate([z, w[1]], axis=1)], axis=0).astype(BF16)


def kernel(x, c, ctx, c_ctx, norm_mix_g, norm_ffn_g, w_mod, b_mod, w_in, lru_conv_w, lru_conv_b, lru_wa, lru_ba, lru_wx, lru_bx, lru_lambda, w_o_lru, rwkv_mu, rwkv_w0, rwkv_w2, rwkv_a0, rwkv_a2, rwkv_g2, rwkv_k_k, rwkv_k_a, rwkv_r_k, rwkv_ln_g, rwkv_ln_b, w_o_rwkv, w_out, w_ffn_in, w_ffn_out, norm_final_g):
    assert x.shape == (BATCH, SEQ, D_MODEL) and ctx.shape == (BATCH, CTX_LEN, D_MODEL)
    assert w_mod.shape[0] == 1, "single layer only"
    D, W = D_MODEL, RWKV_WIDTH

    x_tb = jnp.concatenate([ctx.transpose(1, 0, 2), x.transpose(1, 0, 2)], axis=0)
    x_tb = x_tb.reshape(T_ALL * BATCH, D)

    c16 = jnp.concatenate([c, c_ctx[None], jnp.zeros((16 - BATCH - 1, D), F32)], axis=0)
    mod = _adaln(c16, w_mod[0], b_mod[0][None])
    mod_lat = mod[:BATCH].reshape(BATCH, 6, D)
    mod_ctx = jnp.broadcast_to(mod[BATCH:BATCH + 1], (BATCH, 6 * D)).reshape(BATCH, 6, D)
    sh_m, sc_m, g_m, sh_f, sc_f, g_f = [mod_lat[:, k] for k in range(6)]
    sh2 = jnp.stack([mod_ctx[:, 0], sh_m])
    sc2 = jnp.stack([mod_ctx[:, 1], sc_m])

    w_in0 = w_in[0]
    n_lru = 2 * LRU_WIDTH
    w_lru = w_in0[:, :n_lru].astype(BF16)
    w_rw = jnp.pad(w_in0[:, n_lru:n_lru + RWKV_IN], ((0, 0), (0, RWKV_IN_PAD - RWKV_IN))).astype(BF16)
    w_gate = w_in0[:, n_lru + RWKV_IN:].astype(BF16)
    g_mix = norm_mix_g[0][None]

    u_all = _proj(x_tb, g_mix, sc2, sh2, w_lru, first_chunk=0, n_chunks=N_CHUNKS, colmajor_out=False)
    zr_c = _proj(x_tb, g_mix, sc2, sh2, w_rw, first_chunk=0, n_chunks=N_CTX_CHUNKS, colmajor_out=False)
    zr_l = _proj(x_tb, g_mix, sc2, sh2, w_rw, first_chunk=N_CTX_CHUNKS, n_chunks=N_LAT_CHUNKS,
                 colmajor_out=True).reshape(SEQ * BATCH, RWKV_IN_PAD)
    gates = _proj(x_tb, g_mix, sc2, sh2, w_gate, first_chunk=N_CTX_CHUNKS, n_chunks=N_LAT_CHUNKS,
                  colmajor_out=False)

    wt = jnp.stack([_lru_gate_tiles(lru_wa[0, d], lru_wx[0, d]) for d in range(2)])
    lru_p = (lru_conv_w[0], lru_conv_b[0], wt, lru_ba[0], lru_bx[0], lru_lambda[0])
    hf = _lru_scan(u_all, None, lru_p, 0, reverse=False)
    lru_l = _lru_scan(u_all, hf, lru_p, 1, reverse=True)

    mu_pad = jnp.pad(rwkv_mu[0], ((0, 0), (0, RWKV_IN_PAD - RWKV_IN)))
    ones = _block_diag(jnp.ones((W // RWKV_HEAD, RWKV_HEAD, RWKV_HEAD), F32)).astype(BF16)
    rw_p = (mu_pad, _two_dir_lora(rwkv_w2[0]), rwkv_w0[0].reshape(1, 2 * W),
            _two_dir_lora(rwkv_a2[0]), rwkv_a0[0].reshape(1, 2 * W),
            jnp.pad(rwkv_g2[0], ((0, LORA_G_PAD - LORA_G), (0, 0))).astype(BF16),
            rwkv_k_k[0][None], rwkv_k_a[0][None], rwkv_r_k[0].reshape(1, W), ones)
    r, v, kn, ld, kd, b, bon, g = _rwkv_prep(zr_c, zr_l, rw_p)
    y = _wkv(r, v, kn, ld, kd, b)

    x1 = _merge(lru_l, y, bon, g, gates, x_tb, g_m, rwkv_ln_g[0][None], rwkv_ln_b[0][None], ones,
                w_o_lru[0].astype(BF16), w_o_rwkv[0].astype(BF16), w_out[0].astype(BF16))

    act = _ffn_up(x1, norm_ffn_g[0][None], sc_f, sh_f, w_ffn_in[0].astype(BF16))
    out_tb = _ffn_down(act, x1, g_f, w_ffn_out[0].astype(BF16), norm_final_g[None])
    return out_tb.reshape(SEQ, BATCH, D).transpose(1, 0, 2)
```

```python
import functools
import math

import jax
import jax.numpy as jnp
from jax import lax
from jax.experimental import pallas as pl
from jax.experimental.pallas import tpu as pltpu

F32 = jnp.float32
BF16 = jnp.bfloat16

D_MODEL = 1024
BATCH = 8
SEQ = 2048
CTX_LEN = 256
GRID_W = 64
GRID_ROWS = SEQ // GRID_W

LRU_WIDTH = 1280
LRU_BLOCKS = 16
LRU_BLOCK = LRU_WIDTH // LRU_BLOCKS
LRU_CONV = 4
LRU_C = 8.0

RWKV_HEAD = 64
RWKV_WIDTH = 1024
LORA_W = 64
LORA_A = 64
LORA_G = 160
RWKV_IN = 3 * RWKV_WIDTH + 2 * LORA_W + 2 * LORA_A + LORA_G
RWKV_IN_PAD = 3584
LORA_G_PAD = RWKV_IN_PAD - (3 * RWKV_WIDTH + 2 * LORA_W + 2 * LORA_A)
D_FF = 2816

RMS_EPS = 1e-6
GN_EPS = 64e-5
L2_EPS = 1e-12

T_ALL = CTX_LEN + SEQ
CHUNK_T = 64
CHUNK_ROWS = CHUNK_T * BATCH
N_CTX_CHUNKS = CTX_LEN // CHUNK_T
N_LAT_CHUNKS = SEQ // CHUNK_T
N_CHUNKS = N_CTX_CHUNKS + N_LAT_CHUNKS

PAIR = 2 * RWKV_HEAD
N_PAIRS = RWKV_WIDTH // PAIR

LRU_GATE_TILE = 256
LRU_GATE_K = 512
LRU_GATE_K0 = (0, 128, 384, 640, 768)

VMEM_LIMIT = 56 * 1024 * 1024


def _params(n_axes):
    return pltpu.CompilerParams(dimension_semantics=("arbitrary",) * n_axes,
                                vmem_limit_bytes=VMEM_LIMIT)


def _const_spec(shape):
    nd = len(shape)
    return pl.BlockSpec(shape, lambda *_: (0,) * nd)


def _bdot(a, b):
    return jnp.dot(a.astype(BF16), b.astype(BF16), preferred_element_type=F32)


def _bdot_nt(a, b):
    return lax.dot_general(a.astype(BF16), b.astype(BF16), (((1,), (1,)), ((), ())),
                           preferred_element_type=F32)


def _softplus(x):
    return jnp.maximum(x, 0.0) + jnp.log1p(jnp.exp(-jnp.abs(x)))


def _sigmoid(x):
    return 0.5 * jnp.tanh(0.5 * x) + 0.5


def _head_sums(t, ones_pair):
    tb = t.astype(BF16)
    return jnp.concatenate(
        [jnp.dot(tb[:, s * PAIR:(s + 1) * PAIR], ones_pair, preferred_element_type=F32)
         for s in range(t.shape[1] // PAIR)], axis=1)


def _silu(x):
    return x * _sigmoid(x)


def _gelu_tanh(x):
    c = math.sqrt(2.0 / math.pi)
    return 0.5 * x * (1.0 + jnp.tanh(c * (x + 0.044715 * (x * x * x))))


def _bcast_rows(v8, rows):
    c = v8.shape[-1]
    return jnp.broadcast_to(v8[None], (rows // BATCH, BATCH, c)).reshape(rows, c)


def _rms_modulate(x, g, scale8, shift8):
    rows = x.shape[0]
    ms = jnp.mean(x * x, axis=-1, keepdims=True)
    y = x * lax.rsqrt(ms + RMS_EPS) * g
    return y * (1.0 + _bcast_rows(scale8, rows)) + _bcast_rows(shift8, rows)


def _adaln_kernel(c_ref, w_ref, b_ref, o_ref):
    s = _silu(c_ref[...])
    o_ref[...] = jnp.dot(s, w_ref[...], preferred_element_type=F32,
                         precision=lax.Precision.HIGHEST) + b_ref[...]


def _adaln(c16, w_mod, b_mod):
    n = w_mod.shape[1]
    tn = 1536
    return pl.pallas_call(
        _adaln_kernel,
        grid=(n // tn,),
        in_specs=[_const_spec((16, D_MODEL)),
                  pl.BlockSpec((D_MODEL, tn), lambda j: (0, j)),
                  pl.BlockSpec((1, tn), lambda j: (0, j))],
        out_specs=pl.BlockSpec((16, tn), lambda j: (0, j)),
        out_shape=jax.ShapeDtypeStruct((16, n), F32),
        compiler_params=_params(1),
        name="adaln",
    )(c16, w_mod, b_mod)


def _proj_kernel(x_ref, g_ref, sc_ref, sh_ref, w_ref, o_ref):
    h = _rms_modulate(x_ref[...], g_ref[...], sc_ref[0], sh_ref[0])
    res = jnp.dot(h.astype(BF16), w_ref[...], preferred_element_type=F32)
    o_ref[...] = res.reshape(o_ref.shape)


def _proj(x_tb, g, sc2, sh2, w, *, first_chunk, n_chunks, colmajor_out):
    n = w.shape[1]
    seg = lambda i: ((i + first_chunk) >= N_CTX_CHUNKS).astype(jnp.int32)
    if colmajor_out:
        out_shape = jax.ShapeDtypeStruct((GRID_W, GRID_ROWS, BATCH, n), F32)
        out_spec = pl.BlockSpec((GRID_W, 1, BATCH, n), lambda i: (0, i, 0, 0))
    else:
        out_shape = jax.ShapeDtypeStruct((n_chunks * CHUNK_ROWS, n), F32)
        out_spec = pl.BlockSpec((CHUNK_ROWS, n), lambda i: (i, 0))
    return pl.pallas_call(
        _proj_kernel,
        grid=(n_chunks,),
        in_specs=[pl.BlockSpec((CHUNK_ROWS, D_MODEL), lambda i: (i + first_chunk, 0)),
                  _const_spec((1, D_MODEL)),
                  pl.BlockSpec((1, BATCH, D_MODEL), lambda i: (seg(i), 0, 0)),
                  pl.BlockSpec((1, BATCH, D_MODEL), lambda i: (seg(i), 0, 0)),
                  _const_spec((D_MODEL, n))],
        out_specs=out_spec,
        out_shape=out_shape,
        compiler_params=_params(1),
        name="in_proj",
    )(x_tb, g, sc2, sh2, w)


def _lru_kernel(*refs, reverse):
    if reverse:
        (u_ref, uy_ref, hf_ref, cw_ref, cb_ref, wt_ref, ba_ref, bx_ref, lam_ref,
         o_ref, h_s, halo_s, ext_s, a_s, b_s) = refs
    else:
        (u_ref, cw_ref, cb_ref, wt_ref, ba_ref, bx_ref, lam_ref,
         o_ref, h_s, halo_s, ext_s, a_s, b_s) = refs
    i = pl.program_id(0)
    halo_rows = (LRU_CONV - 1) * BATCH

    @pl.when(i == 0)
    def _():
        h_s[...] = jnp.zeros_like(h_s)

    @pl.when((i == 0) | (i == N_CTX_CHUNKS))
    def _():
        halo_s[...] = jnp.zeros_like(halo_s)

    u = u_ref[...]
    if reverse:
        ext_s[0:CHUNK_ROWS] = u
        ext_s[CHUNK_ROWS:CHUNK_ROWS + halo_rows] = halo_s[...]
        halo_s[...] = u[0:halo_rows]
    else:
        ext_s[0:halo_rows] = halo_s[...]
        ext_s[halo_rows:halo_rows + CHUNK_ROWS] = u
        halo_s[...] = u[CHUNK_ROWS - halo_rows:CHUNK_ROWS]

    xc = jnp.broadcast_to(cb_ref[...], (CHUNK_ROWS, LRU_WIDTH))
    for j in range(LRU_CONV):
        off = (LRU_CONV - 1 - j) if reverse else j
        xc = xc + cw_ref[j:j + 1, :] * ext_s[off * BATCH:off * BATCH + CHUNK_ROWS]

    xcb = xc.astype(BF16)
    sp = _softplus(-lam_ref[...])
    for j in range(LRU_WIDTH // LRU_GATE_TILE):
        k0 = LRU_GATE_K0[j]
        cs = slice(j * LRU_GATE_TILE, (j + 1) * LRU_GATE_TILE)
        g = jnp.dot(xcb[:, k0:k0 + LRU_GATE_K], wt_ref[j], preferred_element_type=F32)
        gate_r = _sigmoid(g[:, :LRU_GATE_TILE] + ba_ref[:, cs])
        gate_i = _sigmoid(g[:, LRU_GATE_TILE:] + bx_ref[:, cs])
        log_a = -LRU_C * gate_r * sp[:, cs]
        a = jnp.exp(log_a)
        a_s[:, cs] = a
        b_s[:, cs] = jnp.sqrt(1.0 - a * a) * (gate_i * xc[:, cs])

    def step(k, h):
        t = (CHUNK_T - 1 - k) if reverse else k
        rows = pl.ds(pl.multiple_of(t * BATCH, BATCH), BATCH)
        h = a_s[rows, :] * h + b_s[rows, :]
        if reverse:
            o_ref[rows, :] = (hf_ref[rows, :] + h) * _gelu_tanh(uy_ref[rows, :])
        else:
            o_ref[rows, :] = h
        return h

    h_s[...] = lax.fori_loop(0, CHUNK_T, step, h_s[...], unroll=8)


def _lru_scan(u_all, hf, lru_p, d, *, reverse):
    cw, cb, wt, ba, bx, lam = lru_p
    if reverse:
        chunk = lambda i: jnp.where(i < N_CTX_CHUNKS, N_CTX_CHUNKS - 1 - i,
                                    N_CHUNKS + N_CTX_CHUNKS - 1 - i)
    else:
        chunk = lambda i: i
    lat = lambda i: jnp.maximum(chunk(jnp.maximum(i, N_CTX_CHUNKS)) - N_CTX_CHUNKS, 0)
    blk = (CHUNK_ROWS, LRU_WIDTH)
    in_specs = [pl.BlockSpec(blk, lambda i: (chunk(i), 0))]
    args = [u_all]
    if reverse:
        in_specs += [pl.BlockSpec(blk, lambda i: (lat(i) + N_CTX_CHUNKS, 1)),
                     pl.BlockSpec(blk, lambda i: (lat(i), 0))]
        args += [u_all, hf]
    in_specs += [_const_spec((LRU_CONV, LRU_WIDTH)), _const_spec((1, LRU_WIDTH)),
                 _const_spec(wt[d].shape), _const_spec((1, LRU_WIDTH)),
                 _const_spec((1, LRU_WIDTH)), _const_spec((1, LRU_WIDTH))]
    args += [cw[d], cb[d][None], wt[d], ba[d][None], bx[d][None], lam[d][None]]
    halo_rows = (LRU_CONV - 1) * BATCH
    return pl.pallas_call(
        functools.partial(_lru_kernel, reverse=reverse),
        grid=(N_CHUNKS,),
        in_specs=in_specs,
        out_specs=pl.BlockSpec(blk, lambda i: (lat(i), 0)),
        out_shape=jax.ShapeDtypeStruct((SEQ * BATCH, LRU_WIDTH), F32),
        scratch_shapes=[pltpu.VMEM((BATCH, LRU_WIDTH), F32),
                        pltpu.VMEM((halo_rows, LRU_WIDTH), F32),
                        pltpu.VMEM((CHUNK_ROWS + halo_rows, LRU_WIDTH), F32),
                        pltpu.VMEM(blk, F32),
                        pltpu.VMEM(blk, F32)],
        compiler_params=_params(1),
        name="lru_bwd" if reverse else "lru_fwd",
    )(*args)


PREP_T = 32
PREP_ROWS = PREP_T * BATCH
PREP_CTX_BLOCKS = CTX_LEN // PREP_T
PREP_LAT_BLOCKS = SEQ // PREP_T
PREP_BLOCKS = PREP_CTX_BLOCKS + PREP_LAT_BLOCKS


def _prep_kernel(zc_ref, zl_ref, zcp_ref, zcn_ref, zlp_ref, zln_ref, mu_ref,
                 w2_ref, w0_ref, a2_ref, a0_ref, g2_ref, kk_ref, ka_ref, rk_ref, ones_ref,
                 r_o, v_o, kn_o, ld_o, kd_o, b_o, bon_o, g_o, ext_s):
    i = pl.program_id(0)
    is_ctx = i < PREP_CTX_BLOCKS
    first = (i == 0) | (i == PREP_CTX_BLOCKS)
    last = (i == PREP_CTX_BLOCKS - 1) | (i == PREP_BLOCKS - 1)
    z = jnp.where(is_ctx, zc_ref[...], zl_ref[...])
    zp = jnp.where(is_ctx, zcp_ref[...], zlp_ref[...])
    zn = jnp.where(is_ctx, zcn_ref[...], zln_ref[...])
    ext_s[0:BATCH] = jnp.where(first, 0.0, zp)
    ext_s[BATCH:BATCH + PREP_ROWS] = z
    ext_s[BATCH + PREP_ROWS:2 * BATCH + PREP_ROWS] = jnp.where(last, 0.0, zn)

    def shifted(c0, c1):
        zc = ext_s[BATCH:BATCH + PREP_ROWS, c0:c1]
        zprev = ext_s[0:PREP_ROWS, c0:c1]
        znext = ext_s[2 * BATCH:2 * BATCH + PREP_ROWS, c0:c1]
        return zc + mu_ref[0:1, c0:c1] * (zprev - zc) + mu_ref[1:2, c0:c1] * (znext - zc)

    W = RWKV_WIDTH
    r = shifted(0, W)
    k = shifted(W, 2 * W)
    v = shifted(2 * W, 3 * W)
    wd = shifted(3 * W, 3 * W + 2 * LORA_W)
    ad = shifted(3 * W + 2 * LORA_W, 3 * W + 2 * LORA_W + 2 * LORA_A)
    gd = shifted(3 * W + 2 * LORA_W + 2 * LORA_A, RWKV_IN_PAD)

    w_pre = _bdot(jnp.tanh(wd), w2_ref[...]) + w0_ref[...]
    a_pre = _bdot(ad, a2_ref[...]) + a0_ref[...]
    kk = k * kk_ref[...]
    ss = _head_sums(kk * kk, ones_ref[...])
    kn = kk / jnp.maximum(jnp.sqrt(ss), L2_EPS)

    r_o[...] = r
    v_o[...] = v
    kn_o[...] = kn
    kd_sum = jnp.zeros_like(k)
    for d in range(2):
        cs = slice(d * W, (d + 1) * W)
        ld_o[d] = -math.exp(-0.5) * _sigmoid(w_pre[:, cs])
        asig = _sigmoid(a_pre[:, cs])
        kd = k * (1.0 + (asig - 1.0) * ka_ref[...])
        kd_o[d] = kd
        b_o[d] = kn * asig
        kd_sum = kd_sum + kd
    bon_o[...] = _head_sums(r * kd_sum * rk_ref[...], ones_ref[...]) * v
    g_o[...] = _bdot(_sigmoid(gd), g2_ref[...])


def _rwkv_prep(zr_c, zr_l, rw_p):
    mu, w2, w0, a2, a0, g2, k_k, k_a, r_k, ones = rw_p
    W = RWKV_WIDTH
    hb = PREP_T
    cmain = lambda i: jnp.minimum(i, PREP_CTX_BLOCKS - 1)
    lmain = lambda i: jnp.maximum(i - PREP_CTX_BLOCKS, 0)
    zblk = (PREP_ROWS, RWKV_IN_PAD)
    hblk = (BATCH, RWKV_IN_PAD)
    n_c8 = CTX_LEN - 1
    n_l8 = SEQ - 1
    in_specs = [
        pl.BlockSpec(zblk, lambda i: (cmain(i), 0)),
        pl.BlockSpec(zblk, lambda i: (lmain(i), 0)),
        pl.BlockSpec(hblk, lambda i: (jnp.maximum(cmain(i) * hb - 1, 0), 0)),
        pl.BlockSpec(hblk, lambda i: (jnp.minimum((cmain(i) + 1) * hb, n_c8), 0)),
        pl.BlockSpec(hblk, lambda i: (jnp.maximum(lmain(i) * hb - 1, 0), 0)),
        pl.BlockSpec(hblk, lambda i: (jnp.minimum((lmain(i) + 1) * hb, n_l8), 0)),
        _const_spec(mu.shape), _const_spec(w2.shape), _const_spec(w0.shape),
        _const_spec(a2.shape), _const_spec(a0.shape), _const_spec(g2.shape),
        _const_spec(k_k.shape), _const_spec(k_a.shape), _const_spec(r_k.shape),
        _const_spec(ones.shape),
    ]
    n_rows = T_ALL * BATCH
    n_lat = SEQ * BATCH
    shared = pl.BlockSpec((PREP_ROWS, W), lambda i: (i, 0))
    perdir = pl.BlockSpec((2, PREP_ROWS, W), lambda i: (0, i, 0))
    latonly = pl.BlockSpec((PREP_ROWS, W), lambda i: (lmain(i), 0))
    return pl.pallas_call(
        _prep_kernel,
        grid=(PREP_BLOCKS,),
        in_specs=in_specs,
        out_specs=[shared, shared, shared, perdir, perdir, perdir, latonly, latonly],
        out_shape=[jax.ShapeDtypeStruct((n_rows, W), F32)] * 3
        + [jax.ShapeDtypeStruct((2, n_rows, W), F32)] * 3
        + [jax.ShapeDtypeStruct((n_lat, W), F32)] * 2,
        scratch_shapes=[pltpu.VMEM((PREP_ROWS + 2 * BATCH, RWKV_IN_PAD), F32)],
        compiler_params=_params(1),
        name="rwkv_prep",
    )(zr_c, zr_l, zr_c, zr_c, zr_l, zr_l, mu, w2, w0, a2, a0, g2, k_k, k_a, r_k, ones)


def _wkv_chunks(inputs, states, decays, sign):
    T = CHUNK_T
    R2 = 2 * T
    nb = len(inputs)

    lane = lax.broadcasted_iota(jnp.int32, (R2, PAIR), 1)
    row = lax.broadcasted_iota(jnp.int32, (R2, PAIR), 0)
    head_mask = (lane // RWKV_HEAD) == (row // T)

    def stack_masked(x):
        return jnp.where(head_mask, jnp.concatenate([x, x], axis=0), 0.0).astype(BF16)

    def stack(x):
        return jnp.concatenate([x, x], axis=0).astype(BF16)

    ri = lax.broadcasted_iota(jnp.int32, (R2, R2), 0)
    ci = lax.broadcasted_iota(jnp.int32, (R2, R2), 1)
    same_head = (ri // T) == (ci // T)
    dt = ((ri % T) - (ci % T)) * sign
    m_strict = same_head & (dt > 0)
    m_incl = same_head & (dt >= 0)
    eye = jnp.where(ri == ci, 1.0, 0.0)

    nt = lambda x, y: lax.dot_general(x, y, (((1,), (1,)), ((), ())), preferred_element_type=F32)
    mm = lambda x, y: jnp.dot(x, y, preferred_element_type=F32)

    a_ab, a_kk, a_rb, v_st, end_st, from_state = [], [], [], [], [], []
    for (a_t, r_t, b_t, k_t, b_e, k_e, v), S in zip(inputs, states):
        lhs = jnp.concatenate([stack_masked(a_t), stack_masked(r_t)], axis=0)
        rhs = jnp.concatenate([stack(b_t), stack(k_t)], axis=0)
        big = nt(lhs, rhs)
        a_ab.append(jnp.where(m_strict, big[0:R2, 0:R2], 0.0))
        a_kk.append(jnp.concatenate([jnp.where(m_strict, big[0:R2, R2:2 * R2], 0.0),
                                     jnp.where(m_incl, big[R2:2 * R2, R2:2 * R2], 0.0)],
                                    axis=0).astype(BF16))
        a_rb.append(jnp.where(m_incl, big[R2:2 * R2, 0:R2], 0.0).astype(BF16))
        v_st.append(stack_masked(v))
        end_st.append(jnp.concatenate([stack_masked(b_e), stack_masked(k_e)], axis=0))
        from_state.append(nt(jnp.concatenate([a_t, r_t], axis=0).astype(BF16), S.astype(BF16)))
    hv = [mm(a_kk[i], v_st[i]) for i in range(nb)]

    npow = [mm(a.astype(BF16), a.astype(BF16)).astype(BF16) for a in a_ab]
    inv = [eye + a for a in a_ab]
    levels = int(math.log2(T))
    for lvl in range(1, levels):
        last = lvl == levels - 1
        nxt_pow, nxt_inv = [], []
        for i in range(nb):
            if last:
                nxt_inv.append(inv[i] + mm(inv[i].astype(BF16), npow[i]))
            else:
                both = mm(jnp.concatenate([npow[i], inv[i].astype(BF16)], axis=0), npow[i])
                nxt_pow.append(both[0:R2].astype(BF16))
                nxt_inv.append(inv[i] + both[R2:2 * R2])
        npow, inv = nxt_pow, nxt_inv

    u = []
    for i in range(nb):
        g_st = jnp.where(head_mask, jnp.concatenate([from_state[i][0:T]] * 2, axis=0), 0.0)
        u.append(mm(inv[i].astype(BF16), (g_st + hv[i][0:R2]).astype(BF16)))
    y_st = [mm(a_rb[i], u[i].astype(BF16)) + hv[i][R2:2 * R2] for i in range(nb)]
    ys = [from_state[i][T:R2] + y_st[i][0:T] + y_st[i][T:R2] for i in range(nb)]
    uv_t = [jnp.concatenate([u[i].T.astype(BF16), v_st[i].T], axis=1) for i in range(nb)]
    s_new = [states[i] * decays[i] + mm(uv_t[i], end_st[i]) for i in range(nb)]
    return ys, s_new


def _wkv_kernel(r_ref, v_ref, kn_ref, ld_ref, kd_ref, b_ref, y_ref, s_ref, cum_s):
    d = pl.program_id(0)
    c = pl.program_id(2)
    sign = 1 - 2 * d

    @pl.when(c == 0)
    def _():
        s_ref[...] = jnp.zeros_like(s_ref)

    ld = ld_ref[0]
    cum_up = ld
    for lvl in range(int(math.log2(CHUNK_T))):
        sh = BATCH << lvl
        cum_up = cum_up + jnp.concatenate([jnp.zeros((sh, PAIR), F32), cum_up[:-sh]], axis=0)
    tot8 = cum_up[CHUNK_ROWS - BATCH:]
    tot = _bcast_rows(tot8, CHUNK_ROWS)
    cum_s[...] = jnp.where(d == 0, cum_up, tot - cum_up + ld)
    dec8 = jnp.exp(tot8)

    rows = [pl.ds(bi, CHUNK_T, stride=BATCH) for bi in range(BATCH)]
    inputs = []
    for bi, rw in enumerate(rows):
        cum = cum_s[rw, :]
        ld_b = ld_ref[0, rw, :]
        b = b_ref[0, rw, :]
        kd = kd_ref[0, rw, :]
        e_out = jnp.exp(-cum)
        e_end = jnp.exp(tot8[bi:bi + 1] - cum)
        inputs.append((-kn_ref[rw, :] * jnp.exp(cum - ld_b), r_ref[rw, :] * jnp.exp(cum),
                       b * e_out, kd * e_out, b * e_end, kd * e_end, v_ref[rw, :]))
    ys, s_new = _wkv_chunks(inputs, [s_ref[bi] for bi in range(BATCH)],
                            [dec8[bi:bi + 1] for bi in range(BATCH)], sign)
    for bi in range(BATCH):
        y_ref[0, rows[bi], :] = ys[bi]
        s_ref[bi] = s_new[bi]


def _wkv(r, v, kn, ld, kd, b):
    def chunk(d, c):
        fwd = c
        bwd = jnp.where(c < N_CTX_CHUNKS, N_CTX_CHUNKS - 1 - c, N_CHUNKS + N_CTX_CHUNKS - 1 - c)
        return jnp.where(d == 0, fwd, bwd)
    lat = lambda d, c: chunk(d, jnp.maximum(c, N_CTX_CHUNKS)) - N_CTX_CHUNKS
    shared = pl.BlockSpec((CHUNK_ROWS, PAIR), lambda d, p, c: (chunk(d, c), p))
    perdir = pl.BlockSpec((1, CHUNK_ROWS, PAIR), lambda d, p, c: (d, chunk(d, c), p))
    return pl.pallas_call(
        _wkv_kernel,
        grid=(2, N_PAIRS, N_CHUNKS),
        in_specs=[shared, shared, shared, perdir, perdir, perdir],
        out_specs=pl.BlockSpec((1, CHUNK_ROWS, PAIR), lambda d, p, c: (d, lat(d, c), p)),
        out_shape=jax.ShapeDtypeStruct((2, SEQ * BATCH, RWKV_WIDTH), F32),
        scratch_shapes=[pltpu.VMEM((BATCH, PAIR, PAIR), F32),
                        pltpu.VMEM((CHUNK_ROWS, PAIR), F32)],
        compiler_params=_params(3),
        name="wkv_scan",
    )(r, v, kn, ld, kd, b)


MERGE_T = 32
MERGE_ROWS = MERGE_T * BATCH


def _merge_kernel(lru_ref, y_ref, bon_ref, g_ref, gl_ref, gr_ref, x_ref, gm_ref,
                  lng_ref, lnb_ref, ones_ref, wol_ref, wor_ref, wout_ref, o_ref):
    rows = MERGE_ROWS
    W = RWKV_WIDTH
    y = (y_ref[0] + y_ref[1]).reshape(rows, W)
    inv_n = 1.0 / RWKV_HEAD

    ones = ones_ref[...]
    y_hi = y.astype(BF16).astype(F32)
    mu = (_head_sums(y_hi, ones) + _head_sums(y - y_hi, ones)) * inv_n
    dy = y - mu
    var = _head_sums(dy * dy, ones) * inv_n
    yn = dy * lax.rsqrt(var + GN_EPS) * lng_ref[...] + lnb_ref[...]
    rw = (yn + bon_ref[...].reshape(rows, W)) * g_ref[...].reshape(rows, W)
    m = (_sigmoid(gl_ref[...]) * _bdot(lru_ref[...], wol_ref[...])
         + _sigmoid(gr_ref[...]) * _bdot(rw, wor_ref[...]))
    mix = _bdot(m, wout_ref[...])
    o_ref[...] = x_ref[...] + _bcast_rows(gm_ref[...], rows) * mix


def _merge(lru_l, y, bon, g, gates, x_tb, g_m8, ln_g, ln_b, ones, wol, wor, wout):
    W = RWKV_WIDTH
    n_blocks = SEQ // MERGE_T
    per_row = GRID_W // MERGE_T
    cm4 = lambda i: (i % per_row, i // per_row, 0, 0)
    y5 = y.reshape(2, GRID_W, GRID_ROWS, BATCH, W)
    bon4 = bon.reshape(GRID_W, GRID_ROWS, BATCH, W)
    g4 = g.reshape(GRID_W, GRID_ROWS, BATCH, W)
    first_lat = CTX_LEN // MERGE_T
    return pl.pallas_call(
        _merge_kernel,
        grid=(n_blocks,),
        in_specs=[
            pl.BlockSpec((MERGE_ROWS, LRU_WIDTH), lambda i: (i, 0)),
            pl.BlockSpec((2, MERGE_T, 1, BATCH, W), lambda i: (0,) + cm4(i)),
            pl.BlockSpec((MERGE_T, 1, BATCH, W), cm4),
            pl.BlockSpec((MERGE_T, 1, BATCH, W), cm4),
            pl.BlockSpec((MERGE_ROWS, D_MODEL), lambda i: (i, 0)),
            pl.BlockSpec((MERGE_ROWS, D_MODEL), lambda i: (i, 1)),
            pl.BlockSpec((MERGE_ROWS, D_MODEL), lambda i: (i + first_lat, 0)),
            _const_spec((BATCH, D_MODEL)),
            _const_spec((1, W)), _const_spec((1, W)), _const_spec(ones.shape),
            _const_spec(wol.shape), _const_spec(wor.shape), _const_spec(wout.shape),
        ],
        out_specs=pl.BlockSpec((MERGE_ROWS, D_MODEL), lambda i: (i, 0)),
        out_shape=jax.ShapeDtypeStruct((SEQ * BATCH, D_MODEL), F32),
        compiler_params=_params(1),
        name="merge",
    )(lru_l, y5, bon4, g4, gates, gates, x_tb, g_m8, ln_g, ln_b, ones, wol, wor, wout)


FFN_ROWS = 512
FFN_TILE = 256


def _ffn_up_kernel(x_ref, g_ref, sc_ref, sh_ref, w_ref, o_ref):
    h = _rms_modulate(x_ref[...], g_ref[...], sc_ref[...], sh_ref[...]).astype(BF16)
    for j in range(D_FF // FFN_TILE):
        cs = slice(j * FFN_TILE, (j + 1) * FFN_TILE)
        gate = jnp.dot(h, w_ref[:, cs], preferred_element_type=F32)
        up = jnp.dot(h, w_ref[:, D_FF + j * FFN_TILE:D_FF + (j + 1) * FFN_TILE],
                     preferred_element_type=F32)
        o_ref[:, cs] = (_silu(gate) * up).astype(BF16)


def _ffn_up(x1, g, sc8, sh8, w_in):
    return pl.pallas_call(
        _ffn_up_kernel,
        grid=(SEQ * BATCH // FFN_ROWS,),
        in_specs=[pl.BlockSpec((FFN_ROWS, D_MODEL), lambda i: (i, 0)),
                  _const_spec((1, D_MODEL)), _const_spec((BATCH, D_MODEL)),
                  _const_spec((BATCH, D_MODEL)), _const_spec(w_in.shape)],
        out_specs=pl.BlockSpec((FFN_ROWS, D_FF), lambda i: (i, 0)),
        out_shape=jax.ShapeDtypeStruct((SEQ * BATCH, D_FF), BF16),
        compiler_params=_params(1),
        name="ffn_up",
    )(x1, g, sc8, sh8, w_in)


def _ffn_down_kernel(act_ref, x_ref, gf_ref, w_ref, gfin_ref, o_ref):
    y = jnp.dot(act_ref[...], w_ref[...], preferred_element_type=F32)
    x2 = x_ref[...] + _bcast_rows(gf_ref[...], FFN_ROWS) * y
    ms = jnp.mean(x2 * x2, axis=-1, keepdims=True)
    o_ref[...] = x2 * lax.rsqrt(ms + RMS_EPS) * gfin_ref[...]


def _ffn_down(act, x1, g_f8, w_out, g_final):
    return pl.pallas_call(
        _ffn_down_kernel,
        grid=(SEQ * BATCH // FFN_ROWS,),
        in_specs=[pl.BlockSpec((FFN_ROWS, D_FF), lambda i: (i, 0)),
                  pl.BlockSpec((FFN_ROWS, D_MODEL), lambda i: (i, 0)),
                  _const_spec((BATCH, D_MODEL)), _const_spec(w_out.shape),
                  _const_spec((1, D_MODEL))],
        out_specs=pl.BlockSpec((FFN_ROWS, D_MODEL), lambda i: (i, 0)),
        out_shape=jax.ShapeDtypeStruct((SEQ * BATCH, D_MODEL), F32),
        compiler_params=_params(1),
        name="ffn_down",
    )(act, x1, g_f8, w_out, g_final)


def _block_diag(w):
    n, c, _ = w.shape
    eye = jnp.eye(n, dtype=w.dtype)
    return jnp.einsum('ncd,nm->ncmd', w, eye).reshape(n * c, n * c)


def _lru_gate_tiles(wa, wx):
    da, dx = _block_diag(wa), _block_diag(wx)
    tiles = []
    for j, k0 in enumerate(LRU_GATE_K0):
        cs = slice(j * LRU_GATE_TILE, (j + 1) * LRU_GATE_TILE)
        tiles.append(jnp.concatenate([da[k0:k0 + LRU_GATE_K, cs], dx[k0:k0 + LRU_GATE_K, cs]], axis=1))
    return jnp.stack(tiles).astype(BF16)


def _two_dir_lora(w):
    z = jnp.zeros_like(w[0])
    return jnp.concatenate([jnp.concatenate([w[0], z], axis=1),
                            jnp.concatenate([z, w[1]], axis=1)], axis=0).astype(BF16)


def kernel(x, c, ctx, c_ctx, norm_mix_g, norm_ffn_g, w_mod, b_mod, w_in, lru_conv_w, lru_conv_b, lru_wa, lru_ba, lru_wx, lru_bx, lru_lambda, w_o_lru, rwkv_mu, rwkv_w0, rwkv_w2, rwkv_a0, rwkv_a2, rwkv_g2, rwkv_k_k, rwkv_k_a, rwkv_r_k, rwkv_ln_g, rwkv_ln_b, w_o_rwkv, w_out, w_ffn_in, w_ffn_out, norm_final_g):
    assert x.shape == (BATCH, SEQ, D_MODEL) and ctx.shape == (BATCH, CTX_LEN, D_MODEL)
    assert w_mod.shape[0] == 1, "single layer only"
    D, W = D_MODEL, RWKV_WIDTH

    x_tb = jnp.concatenate([ctx.transpose(1, 0, 2), x.transpose(1, 0, 2)], axis=0)
    x_tb = x_tb.reshape(T_ALL * BATCH, D)

    c16 = jnp.concatenate([c, c_ctx[None], jnp.zeros((16 - BATCH - 1, D), F32)], axis=0)
    mod = _adaln(c16, w_mod[0], b_mod[0][None])
    mod_lat = mod[:BATCH].reshape(BATCH, 6, D)
    mod_ctx = jnp.broadcast_to(mod[BATCH:BATCH + 1], (BATCH, 6 * D)).reshape(BATCH, 6, D)
    sh_m, sc_m, g_m, sh_f, sc_f, g_f = [mod_lat[:, k] for k in range(6)]
    sh2 = jnp.stack([mod_ctx[:, 0], sh_m])
    sc2 = jnp.stack([mod_ctx[:, 1], sc_m])

    w_in0 = w_in[0]
    n_lru = 2 * LRU_WIDTH
    w_lru = w_in0[:, :n_lru].astype(BF16)
    w_rw = jnp.pad(w_in0[:, n_lru:n_lru + RWKV_IN], ((0, 0), (0, RWKV_IN_PAD - RWKV_IN))).astype(BF16)
    w_gate = w_in0[:, n_lru + RWKV_IN:].astype(BF16)
    g_mix = norm_mix_g[0][None]

    u_all = _proj(x_tb, g_mix, sc2, sh2, w_lru, first_chunk=0, n_chunks=N_CHUNKS, colmajor_out=False)
    zr_c = _proj(x_tb, g_mix, sc2, sh2, w_rw, first_chunk=0, n_chunks=N_CTX_CHUNKS, colmajor_out=False)
    zr_l = _proj(x_tb, g_mix, sc2, sh2, w_rw, first_chunk=N_CTX_CHUNKS, n_chunks=N_LAT_CHUNKS,
                 colmajor_out=True).reshape(SEQ * BATCH, RWKV_IN_PAD)
    gates = _proj(x_tb, g_mix, sc2, sh2, w_gate, first_chunk=N_CTX_CHUNKS, n_chunks=N_LAT_CHUNKS,
                  colmajor_out=False)

    wt = jnp.stack([_lru_gate_tiles(lru_wa[0, d], lru_wx[0, d]) for d in range(2)])
    lru_p = (lru_conv_w[0], lru_conv_b[0], wt, lru_ba[0], lru_bx[0], lru_lambda[0])
    hf = _lru_scan(u_all, None, lru_p, 0, reverse=False)
    lru_l = _lru_scan(u_all, hf, lru_p, 1, reverse=True)

    mu_pad = jnp.pad(rwkv_mu[0], ((0, 0), (0, RWKV_IN_PAD - RWKV_IN)))
    ones = _block_diag(jnp.ones((PAIR // RWKV_HEAD, RWKV_HEAD, RWKV_HEAD), F32)).astype(BF16)
    rw_p = (mu_pad, _two_dir_lora(rwkv_w2[0]), rwkv_w0[0].reshape(1, 2 * W),
            _two_dir_lora(rwkv_a2[0]), rwkv_a0[0].reshape(1, 2 * W),
            jnp.pad(rwkv_g2[0], ((0, LORA_G_PAD - LORA_G), (0, 0))).astype(BF16),
            rwkv_k_k[0][None], rwkv_k_a[0][None], rwkv_r_k[0].reshape(1, W), ones)
    r, v, kn, ld, kd, b, bon, g = _rwkv_prep(zr_c, zr_l, rw_p)
    y = _wkv(r, v, kn, ld, kd, b)

    x1 = _merge(lru_l, y, bon, g, gates, x_tb, g_m, rwkv_ln_g[0][None], rwkv_ln_b[0][None], ones,
                w_o_lru[0].astype(BF16), w_o_rwkv[0].astype(BF16), w_out[0].astype(BF16))

    act = _ffn_up(x1, norm_ffn_g[0][None], sc_f, sh_f, w_ffn_in[0].astype(BF16))
    out_tb = _ffn_down(act, x1, g_f, w_ffn_out[0].astype(BF16), norm_final_g[None])
    return out_tb.reshape(SEQ, BATCH, D).transpose(1, 0, 2)
```

```python
import functools
import math

import jax
import jax.numpy as jnp
from jax import lax
from jax.experimental import pallas as pl
from jax.experimental.pallas import tpu as pltpu

F32 = jnp.float32
BF16 = jnp.bfloat16

D_MODEL = 1024
BATCH = 8
SEQ = 2048
CTX_LEN = 256
GRID_W = 64
GRID_ROWS = SEQ // GRID_W

LRU_WIDTH = 1280
LRU_BLOCKS = 16
LRU_BLOCK = LRU_WIDTH // LRU_BLOCKS
LRU_CONV = 4
LRU_C = 8.0

RWKV_HEAD = 64
RWKV_WIDTH = 1024
LORA_W = 64
LORA_A = 64
LORA_G = 160
RWKV_IN = 3 * RWKV_WIDTH + 2 * LORA_W + 2 * LORA_A + LORA_G
RWKV_IN_PAD = 3584
LORA_G_PAD = RWKV_IN_PAD - (3 * RWKV_WIDTH + 2 * LORA_W + 2 * LORA_A)
D_FF = 2816

RMS_EPS = 1e-6
GN_EPS = 64e-5
L2_EPS = 1e-12

T_ALL = CTX_LEN + SEQ
CHUNK_T = 64
CHUNK_ROWS = CHUNK_T * BATCH
N_CTX_CHUNKS = CTX_LEN // CHUNK_T
N_LAT_CHUNKS = SEQ // CHUNK_T
N_CHUNKS = N_CTX_CHUNKS + N_LAT_CHUNKS

LANE = 128
D_TILES = D_MODEL // LANE
PAIR = 2 * RWKV_HEAD
N_PAIRS = RWKV_WIDTH // PAIR

LRU_GATE_TILE = 256
LRU_GATE_K = 512
LRU_GATE_K0 = (0, 128, 384, 640, 768)

VMEM_LIMIT = 56 * 1024 * 1024


def _params(n_axes):
    return pltpu.CompilerParams(dimension_semantics=("arbitrary",) * n_axes,
                                vmem_limit_bytes=VMEM_LIMIT)


def _const_spec(shape):
    nd = len(shape)
    return pl.BlockSpec(shape, lambda *_: (0,) * nd)


def _bdot(a, b):
    return jnp.dot(a.astype(BF16), b.astype(BF16), preferred_element_type=F32)


def _bdot_nt(a, b):
    return lax.dot_general(a.astype(BF16), b.astype(BF16), (((1,), (1,)), ((), ())),
                           preferred_element_type=F32)


def _softplus(x):
    return jnp.maximum(x, 0.0) + jnp.log1p(jnp.exp(-jnp.abs(x)))


def _sigmoid(x):
    return 0.5 * jnp.tanh(0.5 * x) + 0.5


def _head_sums(t, ones_pair):
    tb = t.astype(BF16)
    return jnp.concatenate(
        [jnp.dot(tb[:, s * PAIR:(s + 1) * PAIR], ones_pair, preferred_element_type=F32)
         for s in range(t.shape[1] // PAIR)], axis=1)


def _silu(x):
    return x * _sigmoid(x)


def _gelu_tanh(x):
    c = math.sqrt(2.0 / math.pi)
    return 0.5 * x * (1.0 + jnp.tanh(c * (x + 0.044715 * (x * x * x))))


def _bcast_rows(v8, rows):
    c = v8.shape[-1]
    return jnp.broadcast_to(v8[None], (rows // BATCH, BATCH, c)).reshape(rows, c)


def _rms_modulate(x, g, scale8, shift8):
    rows = x.shape[0]
    ms = jnp.mean(x * x, axis=-1, keepdims=True)
    y = x * lax.rsqrt(ms + RMS_EPS) * g
    return y * (1.0 + _bcast_rows(scale8, rows)) + _bcast_rows(shift8, rows)


def _adaln_kernel(c_ref, w_ref, b_ref, o_ref):
    s = _silu(c_ref[...])
    o_ref[...] = jnp.dot(s, w_ref[...], preferred_element_type=F32,
                         precision=lax.Precision.HIGHEST) + b_ref[...]


def _adaln(c16, w_mod, b_mod):
    n = w_mod.shape[1]
    tn = 1536
    return pl.pallas_call(
        _adaln_kernel,
        grid=(n // tn,),
        in_specs=[_const_spec((16, D_MODEL)),
                  pl.BlockSpec((D_MODEL, tn), lambda j: (0, j)),
                  pl.BlockSpec((1, tn), lambda j: (0, j))],
        out_specs=pl.BlockSpec((16, tn), lambda j: (0, j)),
        out_shape=jax.ShapeDtypeStruct((16, n), F32),
        compiler_params=_params(1),
        name="adaln",
    )(c16, w_mod, b_mod)


LANE = 128
D_TILES = D_MODEL // LANE


def _proj_kernel(x_ref, g_ref, sc_ref, sh_ref, w_ref, o_ref):
    h = _rms_modulate(x_ref[...], g_ref[...], sc_ref[0], sh_ref[0])
    res = jnp.dot(h.astype(BF16), w_ref[...], preferred_element_type=F32)
    o_ref[...] = res.reshape(o_ref.shape)


def _proj(x_tb, g, sc2, sh2, w, *, first_chunk, n_chunks, colmajor_out):
    n = w.shape[1]
    seg = lambda i: ((i + first_chunk) >= N_CTX_CHUNKS).astype(jnp.int32)
    if colmajor_out:
        out_shape = jax.ShapeDtypeStruct((GRID_W, GRID_ROWS, BATCH, n), F32)
        out_spec = pl.BlockSpec((GRID_W, 1, BATCH, n), lambda i: (0, i, 0, 0))
    else:
        out_shape = jax.ShapeDtypeStruct((n_chunks * CHUNK_ROWS, n), F32)
        out_spec = pl.BlockSpec((CHUNK_ROWS, n), lambda i: (i, 0))
    return pl.pallas_call(
        _proj_kernel,
        grid=(n_chunks,),
        in_specs=[pl.BlockSpec((CHUNK_ROWS, D_MODEL), lambda i: (i + first_chunk, 0)),
                  _const_spec((1, D_MODEL)),
                  pl.BlockSpec((1, BATCH, D_MODEL), lambda i: (seg(i), 0, 0)),
                  pl.BlockSpec((1, BATCH, D_MODEL), lambda i: (seg(i), 0, 0)),
                  _const_spec((D_MODEL, n))],
        out_specs=out_spec,
        out_shape=out_shape,
        compiler_params=_params(1),
        name="in_proj",
    )(x_tb, g, sc2, sh2, w)


def _lru_kernel(*refs, reverse):
    if reverse:
        (u_ref, uy_ref, hf_ref, cw_ref, cb_ref, wt_ref, ba_ref, bx_ref, lam_ref,
         o_ref, h_s, halo_s, ext_s, a_s, b_s) = refs
    else:
        (u_ref, cw_ref, cb_ref, wt_ref, ba_ref, bx_ref, lam_ref,
         o_ref, h_s, halo_s, ext_s, a_s, b_s) = refs
    i = pl.program_id(0)
    halo_rows = (LRU_CONV - 1) * BATCH

    @pl.when(i == 0)
    def _():
        h_s[...] = jnp.zeros_like(h_s)

    @pl.when((i == 0) | (i == N_CTX_CHUNKS))
    def _():
        halo_s[...] = jnp.zeros_like(halo_s)

    u = u_ref[...]
    if reverse:
        ext_s[0:CHUNK_ROWS] = u
        ext_s[CHUNK_ROWS:CHUNK_ROWS + halo_rows] = halo_s[...]
        halo_s[...] = u[0:halo_rows]
    else:
        ext_s[0:halo_rows] = halo_s[...]
        ext_s[halo_rows:halo_rows + CHUNK_ROWS] = u
        halo_s[...] = u[CHUNK_ROWS - halo_rows:CHUNK_ROWS]

    xc = jnp.broadcast_to(cb_ref[...], (CHUNK_ROWS, LRU_WIDTH))
    for j in range(LRU_CONV):
        off = (LRU_CONV - 1 - j) if reverse else j
        xc = xc + cw_ref[j:j + 1, :] * ext_s[off * BATCH:off * BATCH + CHUNK_ROWS]

    xcb = xc.astype(BF16)
    sp = _softplus(-lam_ref[...])
    for j in range(LRU_WIDTH // LRU_GATE_TILE):
        k0 = LRU_GATE_K0[j]
        cs = slice(j * LRU_GATE_TILE, (j + 1) * LRU_GATE_TILE)
        g = jnp.dot(xcb[:, k0:k0 + LRU_GATE_K], wt_ref[j], preferred_element_type=F32)
        gate_r = _sigmoid(g[:, :LRU_GATE_TILE] + ba_ref[:, cs])
        gate_i = _sigmoid(g[:, LRU_GATE_TILE:] + bx_ref[:, cs])
        log_a = -LRU_C * gate_r * sp[:, cs]
        a = jnp.exp(log_a)
        a_s[:, cs] = a
        b_s[:, cs] = jnp.sqrt(1.0 - a * a) * (gate_i * xc[:, cs])

    def step(k, h):
        t = (CHUNK_T - 1 - k) if reverse else k
        rows = pl.ds(pl.multiple_of(t * BATCH, BATCH), BATCH)
        h = a_s[rows, :] * h + b_s[rows, :]
        if reverse:
            o_ref[rows, :] = (hf_ref[rows, :] + h) * _gelu_tanh(uy_ref[rows, :])
        else:
            o_ref[rows, :] = h
        return h

    h_s[...] = lax.fori_loop(0, CHUNK_T, step, h_s[...], unroll=8)


def _lru_scan(u_all, hf, lru_p, d, *, reverse):
    cw, cb, wt, ba, bx, lam = lru_p
    if reverse:
        chunk = lambda i: jnp.where(i < N_CTX_CHUNKS, N_CTX_CHUNKS - 1 - i,
                                    N_CHUNKS + N_CTX_CHUNKS - 1 - i)
    else:
        chunk = lambda i: i
    lat = lambda i: jnp.maximum(chunk(jnp.maximum(i, N_CTX_CHUNKS)) - N_CTX_CHUNKS, 0)
    blk = (CHUNK_ROWS, LRU_WIDTH)
    in_specs = [pl.BlockSpec(blk, lambda i: (chunk(i), 0))]
    args = [u_all]
    if reverse:
        in_specs += [pl.BlockSpec(blk, lambda i: (lat(i) + N_CTX_CHUNKS, 1)),
                     pl.BlockSpec(blk, lambda i: (lat(i), 0))]
        args += [u_all, hf]
    in_specs += [_const_spec((LRU_CONV, LRU_WIDTH)), _const_spec((1, LRU_WIDTH)),
                 _const_spec(wt[d].shape), _const_spec((1, LRU_WIDTH)),
                 _const_spec((1, LRU_WIDTH)), _const_spec((1, LRU_WIDTH))]
    args += [cw[d], cb[d][None], wt[d], ba[d][None], bx[d][None], lam[d][None]]
    halo_rows = (LRU_CONV - 1) * BATCH
    return pl.pallas_call(
        functools.partial(_lru_kernel, reverse=reverse),
        grid=(N_CHUNKS,),
        in_specs=in_specs,
        out_specs=pl.BlockSpec(blk, lambda i: (lat(i), 0)),
        out_shape=jax.ShapeDtypeStruct((SEQ * BATCH, LRU_WIDTH), F32),
        scratch_shapes=[pltpu.VMEM((BATCH, LRU_WIDTH), F32),
                        pltpu.VMEM((halo_rows, LRU_WIDTH), F32),
                        pltpu.VMEM((CHUNK_ROWS + halo_rows, LRU_WIDTH), F32),
                        pltpu.VMEM(blk, F32),
                        pltpu.VMEM(blk, F32)],
        compiler_params=_params(1),
        name="lru_bwd" if reverse else "lru_fwd",
    )(*args)


PREP_T = 32
PREP_ROWS = PREP_T * BATCH
PREP_CTX_BLOCKS = CTX_LEN // PREP_T
PREP_LAT_BLOCKS = SEQ // PREP_T
PREP_BLOCKS = PREP_CTX_BLOCKS + PREP_LAT_BLOCKS


def _prep_kernel(zc_ref, zl_ref, zcp_ref, zcn_ref, zlp_ref, zln_ref, mu_ref,
                 w2_ref, w0_ref, a2_ref, a0_ref, g2_ref, kk_ref, ka_ref, rk_ref, ones_ref,
                 r_o, v_o, kn_o, ld_o, kd_o, b_o, bon_o, g_o, ext_s):
    i = pl.program_id(0)
    is_ctx = i < PREP_CTX_BLOCKS
    first = (i == 0) | (i == PREP_CTX_BLOCKS)
    last = (i == PREP_CTX_BLOCKS - 1) | (i == PREP_BLOCKS - 1)
    z = jnp.where(is_ctx, zc_ref[...], zl_ref[...])
    zp = jnp.where(is_ctx, zcp_ref[...], zlp_ref[...])
    zn = jnp.where(is_ctx, zcn_ref[...], zln_ref[...])
    ext_s[0:BATCH] = jnp.where(first, 0.0, zp)
    ext_s[BATCH:BATCH + PREP_ROWS] = z
    ext_s[BATCH + PREP_ROWS:2 * BATCH + PREP_ROWS] = jnp.where(last, 0.0, zn)

    def shifted(c0, c1):
        zc = ext_s[BATCH:BATCH + PREP_ROWS, c0:c1]
        zprev = ext_s[0:PREP_ROWS, c0:c1]
        znext = ext_s[2 * BATCH:2 * BATCH + PREP_ROWS, c0:c1]
        return zc + mu_ref[0:1, c0:c1] * (zprev - zc) + mu_ref[1:2, c0:c1] * (znext - zc)

    W = RWKV_WIDTH
    r = shifted(0, W)
    k = shifted(W, 2 * W)
    v = shifted(2 * W, 3 * W)
    wd = shifted(3 * W, 3 * W + 2 * LORA_W)
    ad = shifted(3 * W + 2 * LORA_W, 3 * W + 2 * LORA_W + 2 * LORA_A)
    gd = shifted(3 * W + 2 * LORA_W + 2 * LORA_A, RWKV_IN_PAD)

    w_pre = _bdot(jnp.tanh(wd), w2_ref[...]) + w0_ref[...]
    a_pre = _bdot(ad, a2_ref[...]) + a0_ref[...]
    kk = k * kk_ref[...]
    ss = _head_sums(kk * kk, ones_ref[...])
    kn = kk / jnp.maximum(jnp.sqrt(ss), L2_EPS)

    r_o[...] = r
    v_o[...] = v
    kn_o[...] = kn
    kd_sum = jnp.zeros_like(k)
    for d in range(2):
        cs = slice(d * W, (d + 1) * W)
        ld_o[d] = -math.exp(-0.5) * _sigmoid(w_pre[:, cs])
        asig = _sigmoid(a_pre[:, cs])
        kd = k * (1.0 + (asig - 1.0) * ka_ref[...])
        kd_o[d] = kd
        b_o[d] = kn * asig
        kd_sum = kd_sum + kd
    bon_o[...] = _head_sums(r * kd_sum * rk_ref[...], ones_ref[...]) * v
    g_o[...] = _bdot(_sigmoid(gd), g2_ref[...])


def _rwkv_prep(zr_c, zr_l, rw_p):
    mu, w2, w0, a2, a0, g2, k_k, k_a, r_k, ones = rw_p
    W = RWKV_WIDTH
    hb = PREP_T
    cmain = lambda i: jnp.minimum(i, PREP_CTX_BLOCKS - 1)
    lmain = lambda i: jnp.maximum(i - PREP_CTX_BLOCKS, 0)
    zblk = (PREP_ROWS, RWKV_IN_PAD)
    hblk = (BATCH, RWKV_IN_PAD)
    n_c8 = CTX_LEN - 1
    n_l8 = SEQ - 1
    in_specs = [
        pl.BlockSpec(zblk, lambda i: (cmain(i), 0)),
        pl.BlockSpec(zblk, lambda i: (lmain(i), 0)),
        pl.BlockSpec(hblk, lambda i: (jnp.maximum(cmain(i) * hb - 1, 0), 0)),
        pl.BlockSpec(hblk, lambda i: (jnp.minimum((cmain(i) + 1) * hb, n_c8), 0)),
        pl.BlockSpec(hblk, lambda i: (jnp.maximum(lmain(i) * hb - 1, 0), 0)),
        pl.BlockSpec(hblk, lambda i: (jnp.minimum((lmain(i) + 1) * hb, n_l8), 0)),
        _const_spec(mu.shape), _const_spec(w2.shape), _const_spec(w0.shape),
        _const_spec(a2.shape), _const_spec(a0.shape), _const_spec(g2.shape),
        _const_spec(k_k.shape), _const_spec(k_a.shape), _const_spec(r_k.shape),
        _const_spec(ones.shape),
    ]
    n_rows = T_ALL * BATCH
    n_lat = SEQ * BATCH
    shared = pl.BlockSpec((PREP_ROWS, W), lambda i: (i, 0))
    perdir = pl.BlockSpec((2, PREP_ROWS, W), lambda i: (0, i, 0))
    latonly = pl.BlockSpec((PREP_ROWS, W), lambda i: (lmain(i), 0))
    return pl.pallas_call(
        _prep_kernel,
        grid=(PREP_BLOCKS,),
        in_specs=in_specs,
        out_specs=[shared, shared, shared, perdir, perdir, perdir, latonly, latonly],
        out_shape=[jax.ShapeDtypeStruct((n_rows, W), F32)] * 3
        + [jax.ShapeDtypeStruct((2, n_rows, W), F32)] * 3
        + [jax.ShapeDtypeStruct((n_lat, W), F32)] * 2,
        scratch_shapes=[pltpu.VMEM((PREP_ROWS + 2 * BATCH, RWKV_IN_PAD), F32)],
        compiler_params=_params(1),
        name="rwkv_prep",
    )(zr_c, zr_l, zr_c, zr_c, zr_l, zr_l, mu, w2, w0, a2, a0, g2, k_k, k_a, r_k, ones)


def _wkv_chunks(inputs, states, decays, sign):
    T = CHUNK_T
    R2 = 2 * T
    nb = len(inputs)

    lane = lax.broadcasted_iota(jnp.int32, (R2, PAIR), 1)
    row = lax.broadcasted_iota(jnp.int32, (R2, PAIR), 0)
    head_mask = (lane // RWKV_HEAD) == (row // T)

    def stack_masked(x):
        return jnp.where(head_mask, jnp.concatenate([x, x], axis=0), 0.0).astype(BF16)

    def stack(x):
        return jnp.concatenate([x, x], axis=0).astype(BF16)

    ri = lax.broadcasted_iota(jnp.int32, (R2, R2), 0)
    ci = lax.broadcasted_iota(jnp.int32, (R2, R2), 1)
    same_head = (ri // T) == (ci // T)
    dt = ((ri % T) - (ci % T)) * sign
    m_strict = same_head & (dt > 0)
    m_incl = same_head & (dt >= 0)
    eye = jnp.where(ri == ci, 1.0, 0.0)

    nt = lambda x, y: lax.dot_general(x, y, (((1,), (1,)), ((), ())), preferred_element_type=F32)
    mm = lambda x, y: jnp.dot(x, y, preferred_element_type=F32)

    a_ab, a_kk, a_rb, v_st, end_st, from_state = [], [], [], [], [], []
    for (a_t, r_t, b_t, k_t, b_e, k_e, v), S in zip(inputs, states):
        lhs = jnp.concatenate([stack_masked(a_t), stack_masked(r_t)], axis=0)
        rhs = jnp.concatenate([stack(b_t), stack(k_t)], axis=0)
        big = nt(lhs, rhs)
        a_ab.append(jnp.where(m_strict, big[0:R2, 0:R2], 0.0))
        a_kk.append(jnp.concatenate([jnp.where(m_strict, big[0:R2, R2:2 * R2], 0.0),
                                     jnp.where(m_incl, big[R2:2 * R2, R2:2 * R2], 0.0)],
                                    axis=0).astype(BF16))
        a_rb.append(jnp.where(m_incl, big[R2:2 * R2, 0:R2], 0.0).astype(BF16))
        v_st.append(stack_masked(v))
        end_st.append(jnp.concatenate([stack_masked(b_e), stack_masked(k_e)], axis=0))
        from_state.append(nt(jnp.concatenate([a_t, r_t], axis=0).astype(BF16), S.astype(BF16)))
    hv = [mm(a_kk[i], v_st[i]) for i in range(nb)]

    npow = [mm(a.astype(BF16), a.astype(BF16)).astype(BF16) for a in a_ab]
    inv = [eye + a for a in a_ab]
    levels = int(math.log2(T))
    for lvl in range(1, levels):
        last = lvl == levels - 1
        nxt_pow, nxt_inv = [], []
        for i in range(nb):
            if last:
                nxt_inv.append(inv[i] + mm(inv[i].astype(BF16), npow[i]))
            else:
                both = mm(jnp.concatenate([npow[i], inv[i].astype(BF16)], axis=0), npow[i])
                nxt_pow.append(both[0:R2].astype(BF16))
                nxt_inv.append(inv[i] + both[R2:2 * R2])
        npow, inv = nxt_pow, nxt_inv

    u = []
    for i in range(nb):
        g_st = jnp.where(head_mask, jnp.concatenate([from_state[i][0:T]] * 2, axis=0), 0.0)
        u.append(mm(inv[i].astype(BF16), (g_st + hv[i][0:R2]).astype(BF16)))
    y_st = [mm(a_rb[i], u[i].astype(BF16)) + hv[i][R2:2 * R2] for i in range(nb)]
    ys = [from_state[i][T:R2] + y_st[i][0:T] + y_st[i][T:R2] for i in range(nb)]
    uv_t = [jnp.concatenate([u[i].T.astype(BF16), v_st[i].T], axis=1) for i in range(nb)]
    s_new = [states[i] * decays[i] + mm(uv_t[i], end_st[i]) for i in range(nb)]
    return ys, s_new


WKV_PAIRS_PER_STEP = 2
WKV_LANES = WKV_PAIRS_PER_STEP * PAIR
WKV_STAGED = 7


def _wkv_kernel(r_ref, v_ref, kn_ref, ld_ref, kd_ref, b_ref, y_ref, s_ref, in_s, out_s):
    d = pl.program_id(0)
    c = pl.program_id(2)
    sign = 1 - 2 * d

    @pl.when(c == 0)
    def _():
        s_ref[...] = jnp.zeros_like(s_ref)

    ld = ld_ref[0]
    cum_up = ld
    for lvl in range(int(math.log2(CHUNK_T))):
        sh = BATCH << lvl
        cum_up = cum_up + jnp.concatenate([jnp.zeros((sh, WKV_LANES), F32), cum_up[:-sh]], axis=0)
    tot8 = cum_up[CHUNK_ROWS - BATCH:]
    tot = _bcast_rows(tot8, CHUNK_ROWS)
    cum_full = jnp.where(d == 0, cum_up, tot - cum_up + ld)
    dec8 = jnp.exp(tot8)

    srcs = (ld_ref.at[0], b_ref.at[0], kd_ref.at[0], kn_ref, r_ref, v_ref)
    for pi in range(WKV_PAIRS_PER_STEP):
        ln = slice(pi * PAIR, (pi + 1) * PAIR)
        in_s[0, pi] = cum_full[:, ln]
        for k, src in enumerate(srcs):
            in_s[k + 1, pi] = src[:, ln]

    problems = [(pl.ds(bi, CHUNK_T, stride=BATCH), pi, bi)
                for pi in range(WKV_PAIRS_PER_STEP) for bi in range(BATCH)]
    inputs, states, decays = [], [], []
    for rw, pi, bi in problems:
        cum, ld_b, b, kd, kn, r, v = [in_s[k, pi, rw, :] for k in range(WKV_STAGED)]
        tot_b = tot8[bi:bi + 1, pi * PAIR:(pi + 1) * PAIR]
        e_out = jnp.exp(-cum)
        e_end = jnp.exp(tot_b - cum)
        inputs.append((-kn * jnp.exp(cum - ld_b), r * jnp.exp(cum),
                       b * e_out, kd * e_out, b * e_end, kd * e_end, v))
        states.append(s_ref[pi, bi])
        decays.append(dec8[bi:bi + 1, pi * PAIR:(pi + 1) * PAIR])
    ys, s_new = _wkv_chunks(inputs, states, decays, sign)
    for k, (rw, pi, bi) in enumerate(problems):
        out_s[pi, rw, :] = ys[k]
        s_ref[pi, bi] = s_new[k]
    for pi in range(WKV_PAIRS_PER_STEP):
        y_ref[0, :, pi * PAIR:(pi + 1) * PAIR] = out_s[pi]


def _wkv(r, v, kn, ld, kd, b):
    def chunk(d, c):
        fwd = c
        bwd = jnp.where(c < N_CTX_CHUNKS, N_CTX_CHUNKS - 1 - c, N_CHUNKS + N_CTX_CHUNKS - 1 - c)
        return jnp.where(d == 0, fwd, bwd)
    lat = lambda d, c: chunk(d, jnp.maximum(c, N_CTX_CHUNKS)) - N_CTX_CHUNKS
    shared = pl.BlockSpec((CHUNK_ROWS, WKV_LANES), lambda d, p, c: (chunk(d, c), p))
    perdir = pl.BlockSpec((1, CHUNK_ROWS, WKV_LANES), lambda d, p, c: (d, chunk(d, c), p))
    return pl.pallas_call(
        _wkv_kernel,
        grid=(2, N_PAIRS // WKV_PAIRS_PER_STEP, N_CHUNKS),
        in_specs=[shared, shared, shared, perdir, perdir, perdir],
        out_specs=pl.BlockSpec((1, CHUNK_ROWS, WKV_LANES), lambda d, p, c: (d, lat(d, c), p)),
        out_shape=jax.ShapeDtypeStruct((2, SEQ * BATCH, RWKV_WIDTH), F32),
        scratch_shapes=[pltpu.VMEM((WKV_PAIRS_PER_STEP, BATCH, PAIR, PAIR), F32),
                        pltpu.VMEM((WKV_STAGED, WKV_PAIRS_PER_STEP, CHUNK_ROWS, PAIR), F32),
                        pltpu.VMEM((WKV_PAIRS_PER_STEP, CHUNK_ROWS, PAIR), F32)],
        compiler_params=_params(3),
        name="wkv_scan",
    )(r, v, kn, ld, kd, b)


MERGE_T = 32
MERGE_ROWS = MERGE_T * BATCH


def _merge_kernel(lru_ref, y_ref, bon_ref, g_ref, gl_ref, gr_ref, x_ref, gm_ref,
                  lng_ref, lnb_ref, ones_ref, wol_ref, wor_ref, wout_ref, o_ref):
    rows = MERGE_ROWS
    W = RWKV_WIDTH
    y = (y_ref[0] + y_ref[1]).reshape(rows, W)
    inv_n = 1.0 / RWKV_HEAD

    ones = ones_ref[...]
    y_hi = y.astype(BF16).astype(F32)
    mu = (_head_sums(y_hi, ones) + _head_sums(y - y_hi, ones)) * inv_n
    dy = y - mu
    var = _head_sums(dy * dy, ones) * inv_n
    yn = dy * lax.rsqrt(var + GN_EPS) * lng_ref[...] + lnb_ref[...]
    rw = (yn + bon_ref[...].reshape(rows, W)) * g_ref[...].reshape(rows, W)
    m = (_sigmoid(gl_ref[...]) * _bdot(lru_ref[...], wol_ref[...])
         + _sigmoid(gr_ref[...]) * _bdot(rw, wor_ref[...]))
    mix = _bdot(m, wout_ref[...])
    o_ref[...] = x_ref[...] + _bcast_rows(gm_ref[...], rows) * mix


def _merge(lru_l, y, bon, g, gates, x_tb, g_m8, ln_g, ln_b, ones, wol, wor, wout):
    W = RWKV_WIDTH
    n_blocks = SEQ // MERGE_T
    per_row = GRID_W // MERGE_T
    cm4 = lambda i: (i % per_row, i // per_row, 0, 0)
    y5 = y.reshape(2, GRID_W, GRID_ROWS, BATCH, W)
    bon4 = bon.reshape(GRID_W, GRID_ROWS, BATCH, W)
    g4 = g.reshape(GRID_W, GRID_ROWS, BATCH, W)
    return pl.pallas_call(
        _merge_kernel,
        grid=(n_blocks,),
        in_specs=[
            pl.BlockSpec((MERGE_ROWS, LRU_WIDTH), lambda i: (i, 0)),
            pl.BlockSpec((2, MERGE_T, 1, BATCH, W), lambda i: (0,) + cm4(i)),
            pl.BlockSpec((MERGE_T, 1, BATCH, W), cm4),
            pl.BlockSpec((MERGE_T, 1, BATCH, W), cm4),
            pl.BlockSpec((MERGE_ROWS, D_MODEL), lambda i: (i, 0)),
            pl.BlockSpec((MERGE_ROWS, D_MODEL), lambda i: (i, 1)),
            pl.BlockSpec((MERGE_ROWS, D_MODEL), lambda i: (i + CTX_LEN // MERGE_T, 0)),
            _const_spec((BATCH, D_MODEL)),
            _const_spec((1, W)), _const_spec((1, W)), _const_spec(ones.shape),
            _const_spec(wol.shape), _const_spec(wor.shape), _const_spec(wout.shape),
        ],
        out_specs=pl.BlockSpec((MERGE_ROWS, D_MODEL), lambda i: (i, 0)),
        out_shape=jax.ShapeDtypeStruct((SEQ * BATCH, D_MODEL), F32),
        compiler_params=_params(1),
        name="merge",
    )(lru_l, y5, bon4, g4, gates, gates, x_tb, g_m8, ln_g, ln_b, ones, wol, wor, wout)


FFN_T = 64
FFN_ROWS = FFN_T * BATCH
FFN_TILE = 256


def _ffn_up_kernel(x_ref, g_ref, sc_ref, sh_ref, w_ref, o_ref):
    h = _rms_modulate(x_ref[...], g_ref[...], sc_ref[...], sh_ref[...]).astype(BF16)
    for j in range(D_FF // FFN_TILE):
        cs = slice(j * FFN_TILE, (j + 1) * FFN_TILE)
        gate = jnp.dot(h, w_ref[:, cs], preferred_element_type=F32)
        up = jnp.dot(h, w_ref[:, D_FF + j * FFN_TILE:D_FF + (j + 1) * FFN_TILE],
                     preferred_element_type=F32)
        o_ref[:, cs] = (_silu(gate) * up).astype(BF16)


def _ffn_up(x1, g, sc8, sh8, w_in):
    return pl.pallas_call(
        _ffn_up_kernel,
        grid=(SEQ * BATCH // FFN_ROWS,),
        in_specs=[pl.BlockSpec((FFN_ROWS, D_MODEL), lambda i: (i, 0)),
                  _const_spec((1, D_MODEL)), _const_spec((BATCH, D_MODEL)),
                  _const_spec((BATCH, D_MODEL)), _const_spec(w_in.shape)],
        out_specs=pl.BlockSpec((FFN_ROWS, D_FF), lambda i: (i, 0)),
        out_shape=jax.ShapeDtypeStruct((SEQ * BATCH, D_FF), BF16),
        compiler_params=_params(1),
        name="ffn_up",
    )(x1, g, sc8, sh8, w_in)


def _ffn_down_kernel(act_ref, x_ref, gf_ref, w_ref, gfin_ref, o_ref, stage_s):
    y = jnp.dot(act_ref[...], w_ref[...], preferred_element_type=F32)
    x2 = x_ref[...] + _bcast_rows(gf_ref[...], FFN_ROWS) * y
    ms = jnp.mean(x2 * x2, axis=-1, keepdims=True)
    out = x2 * lax.rsqrt(ms + RMS_EPS) * gfin_ref[...]
    for s in range(D_TILES):
        stage_s[s] = out[:, s * LANE:(s + 1) * LANE]
    for bi in range(BATCH):
        rows = pl.ds(bi, FFN_T, stride=BATCH)
        for s in range(D_TILES):
            o_ref[bi, :, s * LANE:(s + 1) * LANE] = stage_s[s, rows, :]


def _ffn_down(act, x1, g_f8, w_out, g_final):
    return pl.pallas_call(
        _ffn_down_kernel,
        grid=(SEQ * BATCH // FFN_ROWS,),
        in_specs=[pl.BlockSpec((FFN_ROWS, D_FF), lambda i: (i, 0)),
                  pl.BlockSpec((FFN_ROWS, D_MODEL), lambda i: (i, 0)),
                  _const_spec((BATCH, D_MODEL)), _const_spec(w_out.shape),
                  _const_spec((1, D_MODEL))],
        out_specs=pl.BlockSpec((BATCH, FFN_T, D_MODEL), lambda i: (0, i, 0)),
        out_shape=jax.ShapeDtypeStruct((BATCH, SEQ, D_MODEL), F32),
        scratch_shapes=[pltpu.VMEM((D_TILES, FFN_ROWS, LANE), F32)],
        compiler_params=_params(1),
        name="ffn_down",
    )(act, x1, g_f8, w_out, g_final)


def _block_diag(w):
    n, c, _ = w.shape
    eye = jnp.eye(n, dtype=w.dtype)
    return jnp.einsum('ncd,nm->ncmd', w, eye).reshape(n * c, n * c)


def _lru_gate_tiles(wa, wx):
    da, dx = _block_diag(wa), _block_diag(wx)
    tiles = []
    for j, k0 in enumerate(LRU_GATE_K0):
        cs = slice(j * LRU_GATE_TILE, (j + 1) * LRU_GATE_TILE)
        tiles.append(jnp.concatenate([da[k0:k0 + LRU_GATE_K, cs], dx[k0:k0 + LRU_GATE_K, cs]], axis=1))
    return jnp.stack(tiles).astype(BF16)


def _two_dir_lora(w):
    z = jnp.zeros_like(w[0])
    return jnp.concatenate([jnp.concatenate([w[0], z], axis=1),
                            jnp.concatenate([z, w[1]], axis=1)], axis=0).astype(BF16)


def kernel(x, c, ctx, c_ctx, norm_mix_g, norm_ffn_g, w_mod, b_mod, w_in, lru_conv_w, lru_conv_b, lru_wa, lru_ba, lru_wx, lru_bx, lru_lambda, w_o_lru, rwkv_mu, rwkv_w0, rwkv_w2, rwkv_a0, rwkv_a2, rwkv_g2, rwkv_k_k, rwkv_k_a, rwkv_r_k, rwkv_ln_g, rwkv_ln_b, w_o_rwkv, w_out, w_ffn_in, w_ffn_out, norm_final_g):
    assert x.shape == (BATCH, SEQ, D_MODEL) and ctx.shape == (BATCH, CTX_LEN, D_MODEL)
    assert w_mod.shape[0] == 1, "single layer only"
    D, W = D_MODEL, RWKV_WIDTH

    c16 = jnp.concatenate([c, c_ctx[None], jnp.zeros((16 - BATCH - 1, D), F32)], axis=0)
    mod = _adaln(c16, w_mod[0], b_mod[0][None])
    mod_lat = mod[:BATCH].reshape(BATCH, 6, D)
    mod_ctx = jnp.broadcast_to(mod[BATCH:BATCH + 1], (BATCH, 6 * D)).reshape(BATCH, 6, D)
    sh_m, sc_m, g_m, sh_f, sc_f, g_f = [mod_lat[:, k] for k in range(6)]
    sh2 = jnp.stack([mod_ctx[:, 0], sh_m])
    sc2 = jnp.stack([mod_ctx[:, 1], sc_m])

    w_in0 = w_in[0]
    n_lru = 2 * LRU_WIDTH
    w_lru = w_in0[:, :n_lru].astype(BF16)
    w_rw = jnp.pad(w_in0[:, n_lru:n_lru + RWKV_IN], ((0, 0), (0, RWKV_IN_PAD - RWKV_IN))).astype(BF16)
    w_gate = w_in0[:, n_lru + RWKV_IN:].astype(BF16)
    g_mix = norm_mix_g[0][None]

    x_tb = jnp.concatenate([ctx.transpose(1, 0, 2), x.transpose(1, 0, 2)], axis=0)
    x_tb = x_tb.reshape(T_ALL * BATCH, D)
    proj = functools.partial(_proj, x_tb, g_mix, sc2, sh2)
    u_all = proj(w_lru, first_chunk=0, n_chunks=N_CHUNKS, colmajor_out=False)
    zr_c = proj(w_rw, first_chunk=0, n_chunks=N_CTX_CHUNKS, colmajor_out=False)
    zr_l = proj(w_rw, first_chunk=N_CTX_CHUNKS, n_chunks=N_LAT_CHUNKS,
                colmajor_out=True).reshape(SEQ * BATCH, RWKV_IN_PAD)
    gates = proj(w_gate, first_chunk=N_CTX_CHUNKS, n_chunks=N_LAT_CHUNKS, colmajor_out=False)

    wt = jnp.stack([_lru_gate_tiles(lru_wa[0, d], lru_wx[0, d]) for d in range(2)])
    lru_p = (lru_conv_w[0], lru_conv_b[0], wt, lru_ba[0], lru_bx[0], lru_lambda[0])
    hf = _lru_scan(u_all, None, lru_p, 0, reverse=False)
    lru_l = _lru_scan(u_all, hf, lru_p, 1, reverse=True)

    mu_pad = jnp.pad(rwkv_mu[0], ((0, 0), (0, RWKV_IN_PAD - RWKV_IN)))
    ones = _block_diag(jnp.ones((PAIR // RWKV_HEAD, RWKV_HEAD, RWKV_HEAD), F32)).astype(BF16)
    rw_p = (mu_pad, _two_dir_lora(rwkv_w2[0]), rwkv_w0[0].reshape(1, 2 * W),
            _two_dir_lora(rwkv_a2[0]), rwkv_a0[0].reshape(1, 2 * W),
            jnp.pad(rwkv_g2[0], ((0, LORA_G_PAD - LORA_G), (0, 0))).astype(BF16),
            rwkv_k_k[0][None], rwkv_k_a[0][None], rwkv_r_k[0].reshape(1, W), ones)
    r, v, kn, ld, kd, b, bon, g = _rwkv_prep(zr_c, zr_l, rw_p)
    y = _wkv(r, v, kn, ld, kd, b)

    x1 = _merge(lru_l, y, bon, g, gates, x_tb, g_m, rwkv_ln_g[0][None], rwkv_ln_b[0][None], ones,
                w_o_lru[0].astype(BF16), w_o_rwkv[0].astype(BF16), w_out[0].astype(BF16))

    act = _ffn_up(x1, norm_ffn_g[0][None], sc_f, sh_f, w_ffn_in[0].astype(BF16))
    return _ffn_down(act, x1, g_f, w_ffn_out[0].astype(BF16), norm_final_g[None])
```

```python
import functools
import math

import jax
import jax.numpy as jnp
from jax import lax
from jax.experimental import pallas as pl
from jax.experimental.pallas import tpu as pltpu

F32 = jnp.float32
BF16 = jnp.bfloat16

D_MODEL = 1024
BATCH = 8
SEQ = 2048
CTX_LEN = 256
GRID_W = 64
GRID_ROWS = SEQ // GRID_W

LRU_WIDTH = 1280
LRU_BLOCKS = 16
LRU_BLOCK = LRU_WIDTH // LRU_BLOCKS
LRU_CONV = 4
LRU_C = 8.0

RWKV_HEAD = 64
RWKV_WIDTH = 1024
LORA_W = 64
LORA_A = 64
LORA_G = 160
RWKV_IN = 3 * RWKV_WIDTH + 2 * LORA_W + 2 * LORA_A + LORA_G
RWKV_IN_PAD = 3584
LORA_G_PAD = RWKV_IN_PAD - (3 * RWKV_WIDTH + 2 * LORA_W + 2 * LORA_A)
D_FF = 2816

RMS_EPS = 1e-6
GN_EPS = 64e-5
L2_EPS = 1e-12

T_ALL = CTX_LEN + SEQ
CHUNK_T = 64
CHUNK_ROWS = CHUNK_T * BATCH
N_CTX_CHUNKS = CTX_LEN // CHUNK_T
N_LAT_CHUNKS = SEQ // CHUNK_T
N_CHUNKS = N_CTX_CHUNKS + N_LAT_CHUNKS

LANE = 128
D_TILES = D_MODEL // LANE
PAIR = 2 * RWKV_HEAD
N_PAIRS = RWKV_WIDTH // PAIR

LRU_GATE_TILE = 256
LRU_GATE_K = 512
LRU_GATE_K0 = (0, 128, 384, 640, 768)

VMEM_LIMIT = 56 * 1024 * 1024


def _params(n_axes):
    return pltpu.CompilerParams(dimension_semantics=("arbitrary",) * n_axes,
                                vmem_limit_bytes=VMEM_LIMIT)


def _const_spec(shape):
    nd = len(shape)
    return pl.BlockSpec(shape, lambda *_: (0,) * nd)


def _bdot(a, b):
    return jnp.dot(a.astype(BF16), b.astype(BF16), preferred_element_type=F32)


def _bdot_nt(a, b):
    return lax.dot_general(a.astype(BF16), b.astype(BF16), (((1,), (1,)), ((), ())),
                           preferred_element_type=F32)


def _softplus(x):
    return jnp.maximum(x, 0.0) + jnp.log1p(jnp.exp(-jnp.abs(x)))


def _sigmoid(x):
    return 0.5 * jnp.tanh(0.5 * x) + 0.5


def _head_sums(t, ones_pair):
    tb = t.astype(BF16)
    return jnp.concatenate(
        [jnp.dot(tb[:, s * PAIR:(s + 1) * PAIR], ones_pair, preferred_element_type=F32)
         for s in range(t.shape[1] // PAIR)], axis=1)


def _silu(x):
    return x * _sigmoid(x)


def _gelu_tanh(x):
    c = math.sqrt(2.0 / math.pi)
    return 0.5 * x * (1.0 + jnp.tanh(c * (x + 0.044715 * (x * x * x))))


def _bcast_rows(v8, rows):
    c = v8.shape[-1]
    return jnp.broadcast_to(v8[None], (rows // BATCH, BATCH, c)).reshape(rows, c)


def _rms_modulate(x, g, scale8, shift8):
    rows = x.shape[0]
    ms = jnp.mean(x * x, axis=-1, keepdims=True)
    y = x * lax.rsqrt(ms + RMS_EPS) * g
    return y * (1.0 + _bcast_rows(scale8, rows)) + _bcast_rows(shift8, rows)


def _adaln_kernel(c_ref, w_ref, b_ref, o_ref):
    s = _silu(c_ref[...])
    o_ref[...] = jnp.dot(s, w_ref[...], preferred_element_type=F32,
                         precision=lax.Precision.HIGHEST) + b_ref[...]


def _adaln(c16, w_mod, b_mod):
    n = w_mod.shape[1]
    tn = 1536
    return pl.pallas_call(
        _adaln_kernel,
        grid=(n // tn,),
        in_specs=[_const_spec((16, D_MODEL)),
                  pl.BlockSpec((D_MODEL, tn), lambda j: (0, j)),
                  pl.BlockSpec((1, tn), lambda j: (0, j))],
        out_specs=pl.BlockSpec((16, tn), lambda j: (0, j)),
        out_shape=jax.ShapeDtypeStruct((16, n), F32),
        compiler_params=_params(1),
        name="adaln",
    )(c16, w_mod, b_mod)


LANE = 128
D_TILES = D_MODEL // LANE


def _proj_kernel(x_ref, g_ref, sc_ref, sh_ref, w_ref, o_ref):
    h = _rms_modulate(x_ref[...], g_ref[...], sc_ref[0], sh_ref[0])
    res = jnp.dot(h.astype(BF16), w_ref[...], preferred_element_type=F32)
    o_ref[...] = res.reshape(o_ref.shape)


def _proj(x_tb, g, sc2, sh2, w, *, first_chunk, n_chunks, colmajor_out):
    n = w.shape[1]
    seg = lambda i: ((i + first_chunk) >= N_CTX_CHUNKS).astype(jnp.int32)
    if colmajor_out:
        out_shape = jax.ShapeDtypeStruct((GRID_W, GRID_ROWS, BATCH, n), F32)
        out_spec = pl.BlockSpec((GRID_W, 1, BATCH, n), lambda i: (0, i, 0, 0))
    else:
        out_shape = jax.ShapeDtypeStruct((n_chunks * CHUNK_ROWS, n), F32)
        out_spec = pl.BlockSpec((CHUNK_ROWS, n), lambda i: (i, 0))
    return pl.pallas_call(
        _proj_kernel,
        grid=(n_chunks,),
        in_specs=[pl.BlockSpec((CHUNK_ROWS, D_MODEL), lambda i: (i + first_chunk, 0)),
                  _const_spec((1, D_MODEL)),
                  pl.BlockSpec((1, BATCH, D_MODEL), lambda i: (seg(i), 0, 0)),
                  pl.BlockSpec((1, BATCH, D_MODEL), lambda i: (seg(i), 0, 0)),
                  _const_spec((D_MODEL, n))],
        out_specs=out_spec,
        out_shape=out_shape,
        compiler_params=_params(1),
        name="in_proj",
    )(x_tb, g, sc2, sh2, w)


def _lru_kernel(*refs, reverse):
    if reverse:
        (u_ref, uy_ref, hf_ref, cw_ref, cb_ref, wt_ref, ba_ref, bx_ref, lam_ref,
         o_ref, h_s, halo_s, ext_s, a_s, b_s) = refs
    else:
        (u_ref, cw_ref, cb_ref, wt_ref, ba_ref, bx_ref, lam_ref,
         o_ref, h_s, halo_s, ext_s, a_s, b_s) = refs
    i = pl.program_id(0)
    halo_rows = (LRU_CONV - 1) * BATCH

    @pl.when(i == 0)
    def _():
        h_s[...] = jnp.zeros_like(h_s)

    @pl.when((i == 0) | (i == N_CTX_CHUNKS))
    def _():
        halo_s[...] = jnp.zeros_like(halo_s)

    u = u_ref[...]
    if reverse:
        ext_s[0:CHUNK_ROWS] = u
        ext_s[CHUNK_ROWS:CHUNK_ROWS + halo_rows] = halo_s[...]
        halo_s[...] = u[0:halo_rows]
    else:
        ext_s[0:halo_rows] = halo_s[...]
        ext_s[halo_rows:halo_rows + CHUNK_ROWS] = u
        halo_s[...] = u[CHUNK_ROWS - halo_rows:CHUNK_ROWS]

    xc = jnp.broadcast_to(cb_ref[...], (CHUNK_ROWS, LRU_WIDTH))
    for j in range(LRU_CONV):
        off = (LRU_CONV - 1 - j) if reverse else j
        xc = xc + cw_ref[j:j + 1, :] * ext_s[off * BATCH:off * BATCH + CHUNK_ROWS]

    xcb = xc.astype(BF16)
    sp = _softplus(-lam_ref[...])
    for j in range(LRU_WIDTH // LRU_GATE_TILE):
        k0 = LRU_GATE_K0[j]
        cs = slice(j * LRU_GATE_TILE, (j + 1) * LRU_GATE_TILE)
        g = jnp.dot(xcb[:, k0:k0 + LRU_GATE_K], wt_ref[j], preferred_element_type=F32)
        gate_r = _sigmoid(g[:, :LRU_GATE_TILE] + ba_ref[:, cs])
        gate_i = _sigmoid(g[:, LRU_GATE_TILE:] + bx_ref[:, cs])
        log_a = -LRU_C * gate_r * sp[:, cs]
        a = jnp.exp(log_a)
        a_s[:, cs] = a
        b_s[:, cs] = jnp.sqrt(1.0 - a * a) * (gate_i * xc[:, cs])

    def step(k, h):
        t = (CHUNK_T - 1 - k) if reverse else k
        rows = pl.ds(pl.multiple_of(t * BATCH, BATCH), BATCH)
        h = a_s[rows, :] * h + b_s[rows, :]
        if reverse:
            o_ref[rows, :] = (hf_ref[rows, :] + h) * _gelu_tanh(uy_ref[rows, :])
        else:
            o_ref[rows, :] = h
        return h

    h_s[...] = lax.fori_loop(0, CHUNK_T, step, h_s[...], unroll=8)


def _lru_scan(u_all, hf, lru_p, d, *, reverse):
    cw, cb, wt, ba, bx, lam = lru_p
    if reverse:
        chunk = lambda i: jnp.where(i < N_CTX_CHUNKS, N_CTX_CHUNKS - 1 - i,
                                    N_CHUNKS + N_CTX_CHUNKS - 1 - i)
    else:
        chunk = lambda i: i
    lat = lambda i: jnp.maximum(chunk(jnp.maximum(i, N_CTX_CHUNKS)) - N_CTX_CHUNKS, 0)
    blk = (CHUNK_ROWS, LRU_WIDTH)
    in_specs = [pl.BlockSpec(blk, lambda i: (chunk(i), 0))]
    args = [u_all]
    if reverse:
        in_specs += [pl.BlockSpec(blk, lambda i: (lat(i) + N_CTX_CHUNKS, 1)),
                     pl.BlockSpec(blk, lambda i: (lat(i), 0))]
        args += [u_all, hf]
    in_specs += [_const_spec((LRU_CONV, LRU_WIDTH)), _const_spec((1, LRU_WIDTH)),
                 _const_spec(wt[d].shape), _const_spec((1, LRU_WIDTH)),
                 _const_spec((1, LRU_WIDTH)), _const_spec((1, LRU_WIDTH))]
    args += [cw[d], cb[d][None], wt[d], ba[d][None], bx[d][None], lam[d][None]]
    halo_rows = (LRU_CONV - 1) * BATCH
    return pl.pallas_call(
        functools.partial(_lru_kernel, reverse=reverse),
        grid=(N_CHUNKS,),
        in_specs=in_specs,
        out_specs=pl.BlockSpec(blk, lambda i: (lat(i), 0)),
        out_shape=jax.ShapeDtypeStruct((SEQ * BATCH, LRU_WIDTH), F32),
        scratch_shapes=[pltpu.VMEM((BATCH, LRU_WIDTH), F32),
                        pltpu.VMEM((halo_rows, LRU_WIDTH), F32),
                        pltpu.VMEM((CHUNK_ROWS + halo_rows, LRU_WIDTH), F32),
                        pltpu.VMEM(blk, F32),
                        pltpu.VMEM(blk, F32)],
        compiler_params=_params(1),
        name="lru_bwd" if reverse else "lru_fwd",
    )(*args)


PREP_T = 32
PREP_ROWS = PREP_T * BATCH
PREP_CTX_BLOCKS = CTX_LEN // PREP_T
PREP_LAT_BLOCKS = SEQ // PREP_T
PREP_BLOCKS = PREP_CTX_BLOCKS + PREP_LAT_BLOCKS


def _prep_kernel(zc_ref, zl_ref, zcp_ref, zcn_ref, zlp_ref, zln_ref, mu_ref,
                 w2_ref, w0_ref, a2_ref, a0_ref, g2_ref, kk_ref, ka_ref, rk_ref, ones_ref,
                 r_o, v_o, kn_o, ld_o, kd_o, b_o, bon_o, g_o, ext_s):
    i = pl.program_id(0)
    is_ctx = i < PREP_CTX_BLOCKS
    first = (i == 0) | (i == PREP_CTX_BLOCKS)
    last = (i == PREP_CTX_BLOCKS - 1) | (i == PREP_BLOCKS - 1)
    z = jnp.where(is_ctx, zc_ref[...], zl_ref[...])
    zp = jnp.where(is_ctx, zcp_ref[...], zlp_ref[...])
    zn = jnp.where(is_ctx, zcn_ref[...], zln_ref[...])
    ext_s[0:BATCH] = jnp.where(first, 0.0, zp)
    ext_s[BATCH:BATCH + PREP_ROWS] = z
    ext_s[BATCH + PREP_ROWS:2 * BATCH + PREP_ROWS] = jnp.where(last, 0.0, zn)

    def shifted(c0, c1):
        zc = ext_s[BATCH:BATCH + PREP_ROWS, c0:c1]
        zprev = ext_s[0:PREP_ROWS, c0:c1]
        znext = ext_s[2 * BATCH:2 * BATCH + PREP_ROWS, c0:c1]
        return zc + mu_ref[0:1, c0:c1] * (zprev - zc) + mu_ref[1:2, c0:c1] * (znext - zc)

    W = RWKV_WIDTH
    r = shifted(0, W)
    k = shifted(W, 2 * W)
    v = shifted(2 * W, 3 * W)
    wd = shifted(3 * W, 3 * W + 2 * LORA_W)
    ad = shifted(3 * W + 2 * LORA_W, 3 * W + 2 * LORA_W + 2 * LORA_A)
    gd = shifted(3 * W + 2 * LORA_W + 2 * LORA_A, RWKV_IN_PAD)

    w_pre = _bdot(jnp.tanh(wd), w2_ref[...]) + w0_ref[...]
    a_pre = _bdot(ad, a2_ref[...]) + a0_ref[...]
    kk = k * kk_ref[...]
    ss = _head_sums(kk * kk, ones_ref[...])
    kn = kk / jnp.maximum(jnp.sqrt(ss), L2_EPS)

    def store_pairs(o, val):
        for s in range(N_PAIRS):
            o[s] = val[:, s * PAIR:(s + 1) * PAIR]

    store_pairs(r_o, r)
    store_pairs(v_o, v)
    store_pairs(kn_o, kn)
    kd_sum = jnp.zeros_like(k)
    for d in range(2):
        cs = slice(d * W, (d + 1) * W)
        store_pairs(ld_o.at[d], -math.exp(-0.5) * _sigmoid(w_pre[:, cs]))
        asig = _sigmoid(a_pre[:, cs])
        kd = k * (1.0 + (asig - 1.0) * ka_ref[...])
        store_pairs(kd_o.at[d], kd)
        store_pairs(b_o.at[d], kn * asig)
        kd_sum = kd_sum + kd
    bon_o[...] = _head_sums(r * kd_sum * rk_ref[...], ones_ref[...]) * v
    g_o[...] = _bdot(_sigmoid(gd), g2_ref[...])


def _rwkv_prep(zr_c, zr_l, rw_p):
    mu, w2, w0, a2, a0, g2, k_k, k_a, r_k, ones = rw_p
    W = RWKV_WIDTH
    hb = PREP_T
    cmain = lambda i: jnp.minimum(i, PREP_CTX_BLOCKS - 1)
    lmain = lambda i: jnp.maximum(i - PREP_CTX_BLOCKS, 0)
    zblk = (PREP_ROWS, RWKV_IN_PAD)
    hblk = (BATCH, RWKV_IN_PAD)
    n_c8 = CTX_LEN - 1
    n_l8 = SEQ - 1
    in_specs = [
        pl.BlockSpec(zblk, lambda i: (cmain(i), 0)),
        pl.BlockSpec(zblk, lambda i: (lmain(i), 0)),
        pl.BlockSpec(hblk, lambda i: (jnp.maximum(cmain(i) * hb - 1, 0), 0)),
        pl.BlockSpec(hblk, lambda i: (jnp.minimum((cmain(i) + 1) * hb, n_c8), 0)),
        pl.BlockSpec(hblk, lambda i: (jnp.maximum(lmain(i) * hb - 1, 0), 0)),
        pl.BlockSpec(hblk, lambda i: (jnp.minimum((lmain(i) + 1) * hb, n_l8), 0)),
        _const_spec(mu.shape), _const_spec(w2.shape), _const_spec(w0.shape),
        _const_spec(a2.shape), _const_spec(a0.shape), _const_spec(g2.shape),
        _const_spec(k_k.shape), _const_spec(k_a.shape), _const_spec(r_k.shape),
        _const_spec(ones.shape),
    ]
    n_rows = T_ALL * BATCH
    n_lat = SEQ * BATCH
    shared = pl.BlockSpec((N_PAIRS, PREP_ROWS, PAIR), lambda i: (0, i, 0))
    perdir = pl.BlockSpec((2, N_PAIRS, PREP_ROWS, PAIR), lambda i: (0, 0, i, 0))
    latonly = pl.BlockSpec((PREP_ROWS, W), lambda i: (lmain(i), 0))
    return pl.pallas_call(
        _prep_kernel,
        grid=(PREP_BLOCKS,),
        in_specs=in_specs,
        out_specs=[shared, shared, shared, perdir, perdir, perdir, latonly, latonly],
        out_shape=[jax.ShapeDtypeStruct((N_PAIRS, n_rows, PAIR), F32)] * 3
        + [jax.ShapeDtypeStruct((2, N_PAIRS, n_rows, PAIR), F32)] * 3
        + [jax.ShapeDtypeStruct((n_lat, W), F32)] * 2,
        scratch_shapes=[pltpu.VMEM((PREP_ROWS + 2 * BATCH, RWKV_IN_PAD), F32)],
        compiler_params=_params(1),
        name="rwkv_prep",
    )(zr_c, zr_l, zr_c, zr_c, zr_l, zr_l, mu, w2, w0, a2, a0, g2, k_k, k_a, r_k, ones)


def _wkv_chunks(inputs, states, decays, sign):
    T = CHUNK_T
    R2 = 2 * T
    nb = len(inputs)

    lane = lax.broadcasted_iota(jnp.int32, (R2, PAIR), 1)
    row = lax.broadcasted_iota(jnp.int32, (R2, PAIR), 0)
    head_mask = (lane // RWKV_HEAD) == (row // T)

    def stack_masked(x):
        return jnp.where(head_mask, jnp.concatenate([x, x], axis=0), 0.0).astype(BF16)

    tw = lax.broadcasted_iota(jnp.int32, (T, 2 * R2), 0)
    sw = lax.broadcasted_iota(jnp.int32, (T, 2 * R2), 1) % T
    dtw = (tw - sw) * sign
    strict_w = dtw[:, 0:R2] > 0
    incl_w2 = dtw >= 0
    eye_w = jnp.where(dtw[:, 0:R2] == 0, 1.0, 0.0)

    nt = lambda x, y: lax.dot_general(x, y, (((1,), (1,)), ((), ())), preferred_element_type=F32)
    mm = lambda x, y: jnp.dot(x, y, preferred_element_type=F32)

    p_w, a_rbk, v_st, x_rhs, r_s, uv_rhs = [], [], [], [], [], []
    for (a_t, r_t, b_t, k_t, b_e, k_e, v), S in zip(inputs, states):
        ar = jnp.concatenate([a_t, r_t], axis=0).astype(BF16)
        bk = jnp.concatenate([stack_masked(b_t), stack_masked(k_t)], axis=0)
        big = nt(ar, bk)
        p_w.append(jnp.where(strict_w, big[0:T, 0:R2], 0.0))
        a_ak = jnp.where(strict_w, big[0:T, R2:2 * R2], 0.0).astype(BF16)
        a_rbk.append(jnp.where(incl_w2, big[T:R2], 0.0).astype(BF16))
        v_st.append(stack_masked(v))
        fs = nt(ar, S.astype(BF16))
        x_rhs.append(fs[0:T] + mm(a_ak, v_st[-1]))
        r_s.append(fs[T:R2])
        uv_rhs.append(jnp.concatenate([b_e, k_e], axis=0).astype(BF16))

    inv_w = [eye_w + p for p in p_w]
    p_bd = [stack_masked(p) for p in p_w]
    p_w = [mm(p_w[i].astype(BF16), p_bd[i]) for i in range(nb)]
    levels = int(math.log2(T))
    for lvl in range(1, levels):
        last = lvl == levels - 1
        p_bd = [stack_masked(p) for p in p_w]
        nxt_p, nxt_inv = [], []
        for i in range(nb):
            if last:
                nxt_inv.append(inv_w[i] + mm(inv_w[i].astype(BF16), p_bd[i]))
            else:
                both = mm(jnp.concatenate([p_w[i], inv_w[i]], axis=0).astype(BF16), p_bd[i])
                nxt_p.append(both[0:T])
                nxt_inv.append(inv_w[i] + both[T:R2])
        p_w, inv_w = nxt_p, nxt_inv

    u = [mm(inv_w[i].astype(BF16), stack_masked(x_rhs[i])) for i in range(nb)]
    ys = [r_s[i] + mm(a_rbk[i], jnp.concatenate([stack_masked(u[i]), v_st[i]], axis=0))
          for i in range(nb)]
    uv_t = [jnp.concatenate([u[i], inputs[i][6]], axis=0).T.astype(BF16) for i in range(nb)]
    s_new = [states[i] * decays[i] + jnp.where(head_mask, mm(uv_t[i], uv_rhs[i]), 0.0)
             for i in range(nb)]
    return ys, s_new


WKV_PAIRS_PER_STEP = 2


def _wkv_kernel(r_ref, v_ref, kn_ref, ld_ref, kd_ref, b_ref, y_ref, s_ref, cum_s):
    d = pl.program_id(0)
    c = pl.program_id(2)
    sign = 1 - 2 * d

    @pl.when(c == 0)
    def _():
        s_ref[...] = jnp.zeros_like(s_ref)

    tot8, dec8 = [], []
    for pi in range(WKV_PAIRS_PER_STEP):
        ld = ld_ref[0, pi]
        cum_up = ld
        for lvl in range(int(math.log2(CHUNK_T))):
            sh = BATCH << lvl
            cum_up = cum_up + jnp.concatenate([jnp.zeros((sh, PAIR), F32), cum_up[:-sh]], axis=0)
        tot8.append(cum_up[CHUNK_ROWS - BATCH:])
        tot = _bcast_rows(tot8[pi], CHUNK_ROWS)
        cum_s[pi] = jnp.where(d == 0, cum_up, tot - cum_up + ld)
        dec8.append(jnp.exp(tot8[pi]))

    problems = [(pl.ds(bi, CHUNK_T, stride=BATCH), pi, bi)
                for pi in range(WKV_PAIRS_PER_STEP) for bi in range(BATCH)]
    inputs, states, decays = [], [], []
    for rw, pi, bi in problems:
        cum = cum_s[pi, rw, :]
        ld_b = ld_ref[0, pi, rw, :]
        b = b_ref[0, pi, rw, :]
        kd = kd_ref[0, pi, rw, :]
        e_out = jnp.exp(-cum)
        e_end = jnp.exp(tot8[pi][bi:bi + 1] - cum)
        inputs.append((-kn_ref[pi, rw, :] * jnp.exp(cum - ld_b), r_ref[pi, rw, :] * jnp.exp(cum),
                       b * e_out, kd * e_out, b * e_end, kd * e_end, v_ref[pi, rw, :]))
        states.append(s_ref[pi, bi])
        decays.append(dec8[pi][bi:bi + 1])
    ys, s_new = _wkv_chunks(inputs, states, decays, sign)
    for k, (rw, pi, bi) in enumerate(problems):
        y_ref[0, pi, rw, :] = ys[k]
        s_ref[pi, bi] = s_new[k]


def _wkv(r, v, kn, ld, kd, b):
    def chunk(d, c):
        fwd = c
        bwd = jnp.where(c < N_CTX_CHUNKS, N_CTX_CHUNKS - 1 - c, N_CHUNKS + N_CTX_CHUNKS - 1 - c)
        return jnp.where(d == 0, fwd, bwd)
    lat = lambda d, c: chunk(d, jnp.maximum(c, N_CTX_CHUNKS)) - N_CTX_CHUNKS
    pps = WKV_PAIRS_PER_STEP
    shared = pl.BlockSpec((pps, CHUNK_ROWS, PAIR), lambda d, p, c: (p, chunk(d, c), 0))
    perdir = pl.BlockSpec((1, pps, CHUNK_ROWS, PAIR), lambda d, p, c: (d, p, chunk(d, c), 0))
    return pl.pallas_call(
        _wkv_kernel,
        grid=(2, N_PAIRS // pps, N_CHUNKS),
        in_specs=[shared, shared, shared, perdir, perdir, perdir],
        out_specs=pl.BlockSpec((1, pps, CHUNK_ROWS, PAIR), lambda d, p, c: (d, p, lat(d, c), 0)),
        out_shape=jax.ShapeDtypeStruct((2, N_PAIRS, SEQ * BATCH, PAIR), F32),
        scratch_shapes=[pltpu.VMEM((pps, BATCH, PAIR, PAIR), F32),
                        pltpu.VMEM((pps, CHUNK_ROWS, PAIR), F32)],
        compiler_params=_params(3),
        name="wkv_scan",
    )(r, v, kn, ld, kd, b)


MERGE_T = 32
MERGE_ROWS = MERGE_T * BATCH


def _merge_kernel(lru_ref, y_ref, bon_ref, g_ref, gl_ref, gr_ref, x_ref, gm_ref,
                  lng_ref, lnb_ref, ones_ref, wol_ref, wor_ref, wout_ref, o_ref):
    rows = MERGE_ROWS
    W = RWKV_WIDTH
    y2 = (y_ref[0] + y_ref[1]).reshape(N_PAIRS, rows, PAIR)
    y = jnp.concatenate([y2[s] for s in range(N_PAIRS)], axis=1)
    inv_n = 1.0 / RWKV_HEAD

    ones = ones_ref[...]
    y_hi = y.astype(BF16).astype(F32)
    mu = (_head_sums(y_hi, ones) + _head_sums(y - y_hi, ones)) * inv_n
    dy = y - mu
    var = _head_sums(dy * dy, ones) * inv_n
    yn = dy * lax.rsqrt(var + GN_EPS) * lng_ref[...] + lnb_ref[...]
    rw = (yn + bon_ref[...].reshape(rows, W)) * g_ref[...].reshape(rows, W)
    m = (_sigmoid(gl_ref[...]) * _bdot(lru_ref[...], wol_ref[...])
         + _sigmoid(gr_ref[...]) * _bdot(rw, wor_ref[...]))
    mix = _bdot(m, wout_ref[...])
    o_ref[...] = x_ref[...] + _bcast_rows(gm_ref[...], rows) * mix


def _merge(lru_l, y, bon, g, gates, x_tb, g_m8, ln_g, ln_b, ones, wol, wor, wout):
    W = RWKV_WIDTH
    n_blocks = SEQ // MERGE_T
    per_row = GRID_W // MERGE_T
    cm4 = lambda i: (i % per_row, i // per_row, 0, 0)
    y6 = y.reshape(2, N_PAIRS, GRID_W, GRID_ROWS, BATCH, PAIR)
    bon4 = bon.reshape(GRID_W, GRID_ROWS, BATCH, W)
    g4 = g.reshape(GRID_W, GRID_ROWS, BATCH, W)
    return pl.pallas_call(
        _merge_kernel,
        grid=(n_blocks,),
        in_specs=[
            pl.BlockSpec((MERGE_ROWS, LRU_WIDTH), lambda i: (i, 0)),
            pl.BlockSpec((2, N_PAIRS, MERGE_T, 1, BATCH, PAIR), lambda i: (0, 0) + cm4(i)),
            pl.BlockSpec((MERGE_T, 1, BATCH, W), cm4),
            pl.BlockSpec((MERGE_T, 1, BATCH, W), cm4),
            pl.BlockSpec((MERGE_ROWS, D_MODEL), lambda i: (i, 0)),
            pl.BlockSpec((MERGE_ROWS, D_MODEL), lambda i: (i, 1)),
            pl.BlockSpec((MERGE_ROWS, D_MODEL), lambda i: (i + CTX_LEN // MERGE_T, 0)),
            _const_spec((BATCH, D_MODEL)),
            _const_spec((1, W)), _const_spec((1, W)), _const_spec(ones.shape),
            _const_spec(wol.shape), _const_spec(wor.shape), _const_spec(wout.shape),
        ],
        out_specs=pl.BlockSpec((MERGE_ROWS, D_MODEL), lambda i: (i, 0)),
        out_shape=jax.ShapeDtypeStruct((SEQ * BATCH, D_MODEL), F32),
        compiler_params=_params(1),
        name="merge",
    )(lru_l, y6, bon4, g4, gates, gates, x_tb, g_m8, ln_g, ln_b, ones, wol, wor, wout)


FFN_T = 64
FFN_ROWS = FFN_T * BATCH
FFN_TILE = 256


def _ffn_up_kernel(x_ref, g_ref, sc_ref, sh_ref, w_ref, o_ref):
    h = _rms_modulate(x_ref[...], g_ref[...], sc_ref[...], sh_ref[...]).astype(BF16)
    for j in range(D_FF // FFN_TILE):
        cs = slice(j * FFN_TILE, (j + 1) * FFN_TILE)
        gate = jnp.dot(h, w_ref[:, cs], preferred_element_type=F32)
        up = jnp.dot(h, w_ref[:, D_FF + j * FFN_TILE:D_FF + (j + 1) * FFN_TILE],
                     preferred_element_type=F32)
        o_ref[:, cs] = (_silu(gate) * up).astype(BF16)


def _ffn_up(x1, g, sc8, sh8, w_in):
    return pl.pallas_call(
        _ffn_up_kernel,
        grid=(SEQ * BATCH // FFN_ROWS,),
        in_specs=[pl.BlockSpec((FFN_ROWS, D_MODEL), lambda i: (i, 0)),
                  _const_spec((1, D_MODEL)), _const_spec((BATCH, D_MODEL)),
                  _const_spec((BATCH, D_MODEL)), _const_spec(w_in.shape)],
        out_specs=pl.BlockSpec((FFN_ROWS, D_FF), lambda i: (i, 0)),
        out_shape=jax.ShapeDtypeStruct((SEQ * BATCH, D_FF), BF16),
        compiler_params=_params(1),
        name="ffn_up",
    )(x1, g, sc8, sh8, w_in)


def _ffn_down_kernel(act_ref, x_ref, gf_ref, w_ref, gfin_ref, o_ref, stage_s):
    y = jnp.dot(act_ref[...], w_ref[...], preferred_element_type=F32)
    x2 = x_ref[...] + _bcast_rows(gf_ref[...], FFN_ROWS) * y
    ms = jnp.mean(x2 * x2, axis=-1, keepdims=True)
    out = x2 * lax.rsqrt(ms + RMS_EPS) * gfin_ref[...]
    for s in range(D_TILES):
        stage_s[s] = out[:, s * LANE:(s + 1) * LANE]
    for bi in range(BATCH):
        rows = pl.ds(bi, FFN_T, stride=BATCH)
        for s in range(D_TILES):
            o_ref[bi, :, s * LANE:(s + 1) * LANE] = stage_s[s, rows, :]


def _ffn_down(act, x1, g_f8, w_out, g_final):
    return pl.pallas_call(
        _ffn_down_kernel,
        grid=(SEQ * BATCH // FFN_ROWS,),
        in_specs=[pl.BlockSpec((FFN_ROWS, D_FF), lambda i: (i, 0)),
                  pl.BlockSpec((FFN_ROWS, D_MODEL), lambda i: (i, 0)),
                  _const_spec((BATCH, D_MODEL)), _const_spec(w_out.shape),
                  _const_spec((1, D_MODEL))],
        out_specs=pl.BlockSpec((BATCH, FFN_T, D_MODEL), lambda i: (0, i, 0)),
        out_shape=jax.ShapeDtypeStruct((BATCH, SEQ, D_MODEL), F32),
        scratch_shapes=[pltpu.VMEM((D_TILES, FFN_ROWS, LANE), F32)],
        compiler_params=_params(1),
        name="ffn_down",
    )(act, x1, g_f8, w_out, g_final)


def _block_diag(w):
    n, c, _ = w.shape
    eye = jnp.eye(n, dtype=w.dtype)
    return jnp.einsum('ncd,nm->ncmd', w, eye).reshape(n * c, n * c)


def _lru_gate_tiles(wa, wx):
    da, dx = _block_diag(wa), _block_diag(wx)
    tiles = []
    for j, k0 in enumerate(LRU_GATE_K0):
        cs = slice(j * LRU_GATE_TILE, (j + 1) * LRU_GATE_TILE)
        tiles.append(jnp.concatenate([da[k0:k0 + LRU_GATE_K, cs], dx[k0:k0 + LRU_GATE_K, cs]], axis=1))
    return jnp.stack(tiles).astype(BF16)


def _two_dir_lora(w):
    z = jnp.zeros_like(w[0])
    return jnp.concatenate([jnp.concatenate([w[0], z], axis=1),
                            jnp.concatenate([z, w[1]], axis=1)], axis=0).astype(BF16)


def kernel(x, c, ctx, c_ctx, norm_mix_g, norm_ffn_g, w_mod, b_mod, w_in, lru_conv_w, lru_conv_b, lru_wa, lru_ba, lru_wx, lru_bx, lru_lambda, w_o_lru, rwkv_mu, rwkv_w0, rwkv_w2, rwkv_a0, rwkv_a2, rwkv_g2, rwkv_k_k, rwkv_k_a, rwkv_r_k, rwkv_ln_g, rwkv_ln_b, w_o_rwkv, w_out, w_ffn_in, w_ffn_out, norm_final_g):
    assert x.shape == (BATCH, SEQ, D_MODEL) and ctx.shape == (BATCH, CTX_LEN, D_MODEL)
    assert w_mod.shape[0] == 1, "single layer only"
    D, W = D_MODEL, RWKV_WIDTH

    c16 = jnp.concatenate([c, c_ctx[None], jnp.zeros((16 - BATCH - 1, D), F32)], axis=0)
    mod = _adaln(c16, w_mod[0], b_mod[0][None])
    mod_lat = mod[:BATCH].reshape(BATCH, 6, D)
    mod_ctx = jnp.broadcast_to(mod[BATCH:BATCH + 1], (BATCH, 6 * D)).reshape(BATCH, 6, D)
    sh_m, sc_m, g_m, sh_f, sc_f, g_f = [mod_lat[:, k] for k in range(6)]
    sh2 = jnp.stack([mod_ctx[:, 0], sh_m])
    sc2 = jnp.stack([mod_ctx[:, 1], sc_m])

    w_in0 = w_in[0]
    n_lru = 2 * LRU_WIDTH
    w_lru = w_in0[:, :n_lru].astype(BF16)
    w_rw = jnp.pad(w_in0[:, n_lru:n_lru + RWKV_IN], ((0, 0), (0, RWKV_IN_PAD - RWKV_IN))).astype(BF16)
    w_gate = w_in0[:, n_lru + RWKV_IN:].astype(BF16)
    g_mix = norm_mix_g[0][None]

    x_tb = jnp.concatenate([ctx.transpose(1, 0, 2), x.transpose(1, 0, 2)], axis=0)
    x_tb = x_tb.reshape(T_ALL * BATCH, D)
    proj = functools.partial(_proj, x_tb, g_mix, sc2, sh2)
    u_all = proj(w_lru, first_chunk=0, n_chunks=N_CHUNKS, colmajor_out=False)
    zr_c = proj(w_rw, first_chunk=0, n_chunks=N_CTX_CHUNKS, colmajor_out=False)
    zr_l = proj(w_rw, first_chunk=N_CTX_CHUNKS, n_chunks=N_LAT_CHUNKS,
                colmajor_out=True).reshape(SEQ * BATCH, RWKV_IN_PAD)
    gates = proj(w_gate, first_chunk=N_CTX_CHUNKS, n_chunks=N_LAT_CHUNKS, colmajor_out=False)

    wt = jnp.stack([_lru_gate_tiles(lru_wa[0, d], lru_wx[0, d]) for d in range(2)])
    lru_p = (lru_conv_w[0], lru_conv_b[0], wt, lru_ba[0], lru_bx[0], lru_lambda[0])
    hf = _lru_scan(u_all, None, lru_p, 0, reverse=False)
    lru_l = _lru_scan(u_all, hf, lru_p, 1, reverse=True)

    mu_pad = jnp.pad(rwkv_mu[0], ((0, 0), (0, RWKV_IN_PAD - RWKV_IN)))
    ones = _block_diag(jnp.ones((PAIR // RWKV_HEAD, RWKV_HEAD, RWKV_HEAD), F32)).astype(BF16)
    rw_p = (mu_pad, _two_dir_lora(rwkv_w2[0]), rwkv_w0[0].reshape(1, 2 * W),
            _two_dir_lora(rwkv_a2[0]), rwkv_a0[0].reshape(1, 2 * W),
            jnp.pad(rwkv_g2[0], ((0, LORA_G_PAD - LORA_G), (0, 0))).astype(BF16),
            rwkv_k_k[0][None], rwkv_k_a[0][None], rwkv_r_k[0].reshape(1, W), ones)
    r, v, kn, ld, kd, b, bon, g = _rwkv_prep(zr_c, zr_l, rw_p)
    y = _wkv(r, v, kn, ld, kd, b)

    x1 = _merge(lru_l, y, bon, g, gates, x_tb, g_m, rwkv_ln_g[0][None], rwkv_ln_b[0][None], ones,
                w_o_lru[0].astype(BF16), w_o_rwkv[0].astype(BF16), w_out[0].astype(BF16))

    act = _ffn_up(x1, norm_ffn_g[0][None], sc_f, sh_f, w_ffn_in[0].astype(BF16))
    return _ffn_down(act, x1, g_f, w_ffn_out[0].astype(BF16), norm_final_g[None])
```

```python
import functools
import math

import jax
import jax.numpy as jnp
from jax import lax
from jax.experimental import pallas as pl
from jax.experimental.pallas import tpu as pltpu

F32 = jnp.float32
BF16 = jnp.bfloat16

D_MODEL = 1024
BATCH = 8
SEQ = 2048
CTX_LEN = 256
GRID_W = 64
GRID_ROWS = SEQ // GRID_W

LRU_WIDTH = 1280
LRU_BLOCKS = 16
LRU_BLOCK = LRU_WIDTH // LRU_BLOCKS
LRU_CONV = 4
LRU_C = 8.0

RWKV_HEAD = 64
RWKV_WIDTH = 1024
LORA_W = 64
LORA_A = 64
LORA_G = 160
RWKV_IN = 3 * RWKV_WIDTH + 2 * LORA_W + 2 * LORA_A + LORA_G
RWKV_IN_PAD = 3584
LORA_G_PAD = RWKV_IN_PAD - (3 * RWKV_WIDTH + 2 * LORA_W + 2 * LORA_A)
D_FF = 2816

RMS_EPS = 1e-6
GN_EPS = 64e-5
L2_EPS = 1e-12

T_ALL = CTX_LEN + SEQ
CHUNK_T = 64
CHUNK_ROWS = CHUNK_T * BATCH
N_CTX_CHUNKS = CTX_LEN // CHUNK_T
N_LAT_CHUNKS = SEQ // CHUNK_T
N_CHUNKS = N_CTX_CHUNKS + N_LAT_CHUNKS

LANE = 128
D_TILES = D_MODEL // LANE
PAIR = 2 * RWKV_HEAD
N_PAIRS = RWKV_WIDTH // PAIR

LRU_GATE_TILE = 256
LRU_GATE_K = 512
LRU_GATE_K0 = (0, 128, 384, 640, 768)

VMEM_LIMIT = 56 * 1024 * 1024


def _params(n_axes):
    return pltpu.CompilerParams(dimension_semantics=("arbitrary",) * n_axes,
                                vmem_limit_bytes=VMEM_LIMIT)


def _const_spec(shape):
    nd = len(shape)
    return pl.BlockSpec(shape, lambda *_: (0,) * nd)


def _bdot(a, b):
    return jnp.dot(a.astype(BF16), b.astype(BF16), preferred_element_type=F32)


def _bdot_nt(a, b):
    return lax.dot_general(a.astype(BF16), b.astype(BF16), (((1,), (1,)), ((), ())),
                           preferred_element_type=F32)


def _softplus(x):
    return jnp.maximum(x, 0.0) + jnp.log1p(jnp.exp(-jnp.abs(x)))


def _sigmoid(x):
    return 0.5 * jnp.tanh(0.5 * x) + 0.5


def _head_sums(t, ones_pair):
    tb = t.astype(BF16)
    return jnp.concatenate(
        [jnp.dot(tb[:, s * PAIR:(s + 1) * PAIR], ones_pair, preferred_element_type=F32)
         for s in range(t.shape[1] // PAIR)], axis=1)


def _silu(x):
    return x * _sigmoid(x)


def _gelu_tanh(x):
    c = math.sqrt(2.0 / math.pi)
    return 0.5 * x * (1.0 + jnp.tanh(c * (x + 0.044715 * (x * x * x))))


def _bcast_rows(v8, rows):
    c = v8.shape[-1]
    return jnp.broadcast_to(v8[None], (rows // BATCH, BATCH, c)).reshape(rows, c)


def _rms_modulate(x, g, scale8, shift8):
    rows = x.shape[0]
    ms = jnp.mean(x * x, axis=-1, keepdims=True)
    y = x * lax.rsqrt(ms + RMS_EPS) * g
    return y * (1.0 + _bcast_rows(scale8, rows)) + _bcast_rows(shift8, rows)


def _adaln_kernel(c_ref, w_ref, b_ref, o_ref):
    s = _silu(c_ref[...])
    o_ref[...] = jnp.dot(s, w_ref[...], preferred_element_type=F32,
                         precision=lax.Precision.HIGHEST) + b_ref[...]


def _adaln(c16, w_mod, b_mod):
    n = w_mod.shape[1]
    tn = 1536
    return pl.pallas_call(
        _adaln_kernel,
        grid=(n // tn,),
        in_specs=[_const_spec((16, D_MODEL)),
                  pl.BlockSpec((D_MODEL, tn), lambda j: (0, j)),
                  pl.BlockSpec((1, tn), lambda j: (0, j))],
        out_specs=pl.BlockSpec((16, tn), lambda j: (0, j)),
        out_shape=jax.ShapeDtypeStruct((16, n), F32),
        compiler_params=_params(1),
        name="adaln",
    )(c16, w_mod, b_mod)


LANE = 128
D_TILES = D_MODEL // LANE


def _proj_kernel(x_ref, g_ref, sc_ref, sh_ref, w_ref, o_ref):
    h = _rms_modulate(x_ref[...], g_ref[...], sc_ref[0], sh_ref[0])
    res = jnp.dot(h.astype(BF16), w_ref[...], preferred_element_type=F32)
    o_ref[...] = res.reshape(o_ref.shape)


def _proj(x_tb, g, sc2, sh2, w, *, first_chunk, n_chunks, colmajor_out):
    n = w.shape[1]
    seg = lambda i: ((i + first_chunk) >= N_CTX_CHUNKS).astype(jnp.int32)
    if colmajor_out:
        out_shape = jax.ShapeDtypeStruct((GRID_W, GRID_ROWS, BATCH, n), F32)
        out_spec = pl.BlockSpec((GRID_W, 1, BATCH, n), lambda i: (0, i, 0, 0))
    else:
        out_shape = jax.ShapeDtypeStruct((n_chunks * CHUNK_ROWS, n), F32)
        out_spec = pl.BlockSpec((CHUNK_ROWS, n), lambda i: (i, 0))
    return pl.pallas_call(
        _proj_kernel,
        grid=(n_chunks,),
        in_specs=[pl.BlockSpec((CHUNK_ROWS, D_MODEL), lambda i: (i + first_chunk, 0)),
                  _const_spec((1, D_MODEL)),
                  pl.BlockSpec((1, BATCH, D_MODEL), lambda i: (seg(i), 0, 0)),
                  pl.BlockSpec((1, BATCH, D_MODEL), lambda i: (seg(i), 0, 0)),
                  _const_spec((D_MODEL, n))],
        out_specs=out_spec,
        out_shape=out_shape,
        compiler_params=_params(1),
        name="in_proj",
    )(x_tb, g, sc2, sh2, w)


def _lru_kernel(*refs, reverse):
    if reverse:
        (u_ref, uy_ref, hf_ref, cw_ref, cb_ref, wt_ref, ba_ref, bx_ref, lam_ref,
         o_ref, h_s, halo_s, ext_s, a_s, b_s) = refs
    else:
        (u_ref, cw_ref, cb_ref, wt_ref, ba_ref, bx_ref, lam_ref,
         o_ref, h_s, halo_s, ext_s, a_s, b_s) = refs
    i = pl.program_id(0)
    halo_rows = (LRU_CONV - 1) * BATCH

    @pl.when(i == 0)
    def _():
        h_s[...] = jnp.zeros_like(h_s)

    @pl.when((i == 0) | (i == N_CTX_CHUNKS))
    def _():
        halo_s[...] = jnp.zeros_like(halo_s)

    u = u_ref[...]
    if reverse:
        ext_s[0:CHUNK_ROWS] = u
        ext_s[CHUNK_ROWS:CHUNK_ROWS + halo_rows] = halo_s[...]
        halo_s[...] = u[0:halo_rows]
    else:
        ext_s[0:halo_rows] = halo_s[...]
        ext_s[halo_rows:halo_rows + CHUNK_ROWS] = u
        halo_s[...] = u[CHUNK_ROWS - halo_rows:CHUNK_ROWS]

    xc = jnp.broadcast_to(cb_ref[...], (CHUNK_ROWS, LRU_WIDTH))
    for j in range(LRU_CONV):
        off = (LRU_CONV - 1 - j) if reverse else j
        xc = xc + cw_ref[j:j + 1, :] * ext_s[off * BATCH:off * BATCH + CHUNK_ROWS]

    xcb = xc.astype(BF16)
    c_half = (-0.5 * LRU_C) * _softplus(-lam_ref[...])
    for j in range(LRU_WIDTH // LRU_GATE_TILE):
        k0 = LRU_GATE_K0[j]
        cs = slice(j * LRU_GATE_TILE, (j + 1) * LRU_GATE_TILE)
        g = jnp.dot(xcb[:, k0:k0 + LRU_GATE_K], wt_ref[j], preferred_element_type=F32)
        tanh_r = jnp.tanh(g[:, :LRU_GATE_TILE] + ba_ref[:, cs])
        gate_i = 0.5 * jnp.tanh(g[:, LRU_GATE_TILE:] + bx_ref[:, cs]) + 0.5
        log_a = c_half[:, cs] * tanh_r + c_half[:, cs]
        a = jnp.exp(log_a)
        a_s[:, cs] = a
        b_s[:, cs] = jnp.sqrt(1.0 - a * a) * (gate_i * xc[:, cs])

    def step(k, h):
        t = (CHUNK_T - 1 - k) if reverse else k
        rows = pl.ds(pl.multiple_of(t * BATCH, BATCH), BATCH)
        h = a_s[rows, :] * h + b_s[rows, :]
        if reverse:
            o_ref[rows, :] = (hf_ref[rows, :] + h) * _gelu_tanh(uy_ref[rows, :])
        else:
            o_ref[rows, :] = h
        return h

    h_s[...] = lax.fori_loop(0, CHUNK_T, step, h_s[...], unroll=8)


def _lru_scan(u_all, hf, lru_p, d, *, reverse):
    cw, cb, wt, ba, bx, lam = lru_p
    if reverse:
        chunk = lambda i: jnp.where(i < N_CTX_CHUNKS, N_CTX_CHUNKS - 1 - i,
                                    N_CHUNKS + N_CTX_CHUNKS - 1 - i)
    else:
        chunk = lambda i: i
    lat = lambda i: jnp.maximum(chunk(jnp.maximum(i, N_CTX_CHUNKS)) - N_CTX_CHUNKS, 0)
    blk = (CHUNK_ROWS, LRU_WIDTH)
    in_specs = [pl.BlockSpec(blk, lambda i: (chunk(i), 0))]
    args = [u_all]
    if reverse:
        in_specs += [pl.BlockSpec(blk, lambda i: (lat(i) + N_CTX_CHUNKS, 1)),
                     pl.BlockSpec(blk, lambda i: (lat(i), 0))]
        args += [u_all, hf]
    in_specs += [_const_spec((LRU_CONV, LRU_WIDTH)), _const_spec((1, LRU_WIDTH)),
                 _const_spec(wt[d].shape), _const_spec((1, LRU_WIDTH)),
                 _const_spec((1, LRU_WIDTH)), _const_spec((1, LRU_WIDTH))]
    args += [cw[d], cb[d][None], wt[d], ba[d][None], bx[d][None], lam[d][None]]
    halo_rows = (LRU_CONV - 1) * BATCH
    return pl.pallas_call(
        functools.partial(_lru_kernel, reverse=reverse),
        grid=(N_CHUNKS,),
        in_specs=in_specs,
        out_specs=pl.BlockSpec(blk, lambda i: (lat(i), 0)),
        out_shape=jax.ShapeDtypeStruct((SEQ * BATCH, LRU_WIDTH), F32),
        scratch_shapes=[pltpu.VMEM((BATCH, LRU_WIDTH), F32),
                        pltpu.VMEM((halo_rows, LRU_WIDTH), F32),
                        pltpu.VMEM((CHUNK_ROWS + halo_rows, LRU_WIDTH), F32),
                        pltpu.VMEM(blk, F32),
                        pltpu.VMEM(blk, F32)],
        compiler_params=_params(1),
        name="lru_bwd" if reverse else "lru_fwd",
    )(*args)


PREP_T = 32
PREP_ROWS = PREP_T * BATCH
PREP_CTX_BLOCKS = CTX_LEN // PREP_T
PREP_LAT_BLOCKS = SEQ // PREP_T
PREP_BLOCKS = PREP_CTX_BLOCKS + PREP_LAT_BLOCKS


def _prep_kernel(zc_ref, zl_ref, zcp_ref, zcn_ref, zlp_ref, zln_ref, mu_ref,
                 w2_ref, w0_ref, a2_ref, a0_ref, g2_ref, kk_ref, ka_ref, rk_ref, ones_ref,
                 r_o, v_o, kn_o, ld_o, kd_o, b_o, bon_o, g_o, ext_s):
    i = pl.program_id(0)
    is_ctx = i < PREP_CTX_BLOCKS
    first = (i == 0) | (i == PREP_CTX_BLOCKS)
    last = (i == PREP_CTX_BLOCKS - 1) | (i == PREP_BLOCKS - 1)
    z = jnp.where(is_ctx, zc_ref[...], zl_ref[...])
    zp = jnp.where(is_ctx, zcp_ref[...], zlp_ref[...])
    zn = jnp.where(is_ctx, zcn_ref[...], zln_ref[...])
    ext_s[0:BATCH] = jnp.where(first, 0.0, zp)
    ext_s[BATCH:BATCH + PREP_ROWS] = z
    ext_s[BATCH + PREP_ROWS:2 * BATCH + PREP_ROWS] = jnp.where(last, 0.0, zn)

    def shifted(c0, c1):
        zc = ext_s[BATCH:BATCH + PREP_ROWS, c0:c1]
        zprev = ext_s[0:PREP_ROWS, c0:c1]
        znext = ext_s[2 * BATCH:2 * BATCH + PREP_ROWS, c0:c1]
        return zc + mu_ref[0:1, c0:c1] * (zprev - zc) + mu_ref[1:2, c0:c1] * (znext - zc)

    W = RWKV_WIDTH
    r = shifted(0, W)
    k = shifted(W, 2 * W)
    v = shifted(2 * W, 3 * W)
    wd = shifted(3 * W, 3 * W + 2 * LORA_W)
    ad = shifted(3 * W + 2 * LORA_W, 3 * W + 2 * LORA_W + 2 * LORA_A)
    gd = shifted(3 * W + 2 * LORA_W + 2 * LORA_A, RWKV_IN_PAD)

    w_pre = _bdot(jnp.tanh(wd), w2_ref[...]) + w0_ref[...]
    a_pre = _bdot(ad, a2_ref[...]) + a0_ref[...]
    kk = k * kk_ref[...]
    ss = _head_sums(kk * kk, ones_ref[...])
    kn = kk / jnp.maximum(jnp.sqrt(ss), L2_EPS)

    def store_pairs(o, val):
        for s in range(N_PAIRS):
            o[s] = val[:, s * PAIR:(s + 1) * PAIR]

    store_pairs(r_o, r)
    store_pairs(v_o, v)
    store_pairs(kn_o, kn)
    kd_sum = jnp.zeros_like(k)
    for d in range(2):
        cs = slice(d * W, (d + 1) * W)
        store_pairs(ld_o.at[d], -math.exp(-0.5) * _sigmoid(w_pre[:, cs]))
        asig = _sigmoid(a_pre[:, cs])
        kd = k * (1.0 + (asig - 1.0) * ka_ref[...])
        store_pairs(kd_o.at[d], kd)
        store_pairs(b_o.at[d], kn * asig)
        kd_sum = kd_sum + kd
    bon_o[...] = _head_sums(r * kd_sum * rk_ref[...], ones_ref[...]) * v
    g_o[...] = _bdot(_sigmoid(gd), g2_ref[...])


def _rwkv_prep(zr_c, zr_l, rw_p):
    mu, w2, w0, a2, a0, g2, k_k, k_a, r_k, ones = rw_p
    W = RWKV_WIDTH
    hb = PREP_T
    cmain = lambda i: jnp.minimum(i, PREP_CTX_BLOCKS - 1)
    lmain = lambda i: jnp.maximum(i - PREP_CTX_BLOCKS, 0)
    zblk = (PREP_ROWS, RWKV_IN_PAD)
    hblk = (BATCH, RWKV_IN_PAD)
    n_c8 = CTX_LEN - 1
    n_l8 = SEQ - 1
    in_specs = [
        pl.BlockSpec(zblk, lambda i: (cmain(i), 0)),
        pl.BlockSpec(zblk, lambda i: (lmain(i), 0)),
        pl.BlockSpec(hblk, lambda i: (jnp.maximum(cmain(i) * hb - 1, 0), 0)),
        pl.BlockSpec(hblk, lambda i: (jnp.minimum((cmain(i) + 1) * hb, n_c8), 0)),
        pl.BlockSpec(hblk, lambda i: (jnp.maximum(lmain(i) * hb - 1, 0), 0)),
        pl.BlockSpec(hblk, lambda i: (jnp.minimum((lmain(i) + 1) * hb, n_l8), 0)),
        _const_spec(mu.shape), _const_spec(w2.shape), _const_spec(w0.shape),
        _const_spec(a2.shape), _const_spec(a0.shape), _const_spec(g2.shape),
        _const_spec(k_k.shape), _const_spec(k_a.shape), _const_spec(r_k.shape),
        _const_spec(ones.shape),
    ]
    n_rows = T_ALL * BATCH
    n_lat = SEQ * BATCH
    shared = pl.BlockSpec((N_PAIRS, PREP_ROWS, PAIR), lambda i: (0, i, 0))
    perdir = pl.BlockSpec((2, N_PAIRS, PREP_ROWS, PAIR), lambda i: (0, 0, i, 0))
    latonly = pl.BlockSpec((PREP_ROWS, W), lambda i: (lmain(i), 0))
    return pl.pallas_call(
        _prep_kernel,
        grid=(PREP_BLOCKS,),
        in_specs=in_specs,
        out_specs=[shared, shared, shared, perdir, perdir, perdir, latonly, latonly],
        out_shape=[jax.ShapeDtypeStruct((N_PAIRS, n_rows, PAIR), F32)] * 3
        + [jax.ShapeDtypeStruct((2, N_PAIRS, n_rows, PAIR), F32)] * 3
        + [jax.ShapeDtypeStruct((n_lat, W), F32)] * 2,
        scratch_shapes=[pltpu.VMEM((PREP_ROWS + 2 * BATCH, RWKV_IN_PAD), F32)],
        compiler_params=_params(1),
        name="rwkv_prep",
    )(zr_c, zr_l, zr_c, zr_c, zr_l, zr_l, mu, w2, w0, a2, a0, g2, k_k, k_a, r_k, ones)


def _wkv_chunks(inputs, states, decays, sign):
    T = CHUNK_T
    R2 = 2 * T
    nb = len(inputs)

    lane = lax.broadcasted_iota(jnp.int32, (R2, PAIR), 1)
    row = lax.broadcasted_iota(jnp.int32, (R2, PAIR), 0)
    head_mask = (lane // RWKV_HEAD) == (row // T)

    def stack_masked(x):
        return jnp.where(head_mask, jnp.concatenate([x, x], axis=0), 0.0).astype(BF16)

    tw = lax.broadcasted_iota(jnp.int32, (T, 2 * R2), 0)
    sw = lax.broadcasted_iota(jnp.int32, (T, 2 * R2), 1) % T
    dtw = (tw - sw) * sign
    strict_w = dtw[:, 0:R2] > 0
    incl_w2 = dtw >= 0
    eye_w = jnp.where(dtw[:, 0:R2] == 0, 1.0, 0.0)

    nt = lambda x, y: lax.dot_general(x, y, (((1,), (1,)), ((), ())), preferred_element_type=F32)
    mm = lambda x, y: jnp.dot(x, y, preferred_element_type=F32)

    p_w, a_rbk, v_st, x_rhs, r_s, uv_rhs = [], [], [], [], [], []
    for (a_t, r_t, b_t, k_t, b_e, k_e, v), S in zip(inputs, states):
        ar = jnp.concatenate([a_t, r_t], axis=0).astype(BF16)
        bk = jnp.concatenate([stack_masked(b_t), stack_masked(k_t)], axis=0)
        big = nt(ar, bk)
        p_w.append(jnp.where(strict_w, big[0:T, 0:R2], 0.0))
        a_ak = jnp.where(strict_w, big[0:T, R2:2 * R2], 0.0).astype(BF16)
        a_rbk.append(jnp.where(incl_w2, big[T:R2], 0.0).astype(BF16))
        v_st.append(stack_masked(v))
        fs = nt(ar, S.astype(BF16))
        x_rhs.append(fs[0:T] + mm(a_ak, v_st[-1]))
        r_s.append(fs[T:R2])
        uv_rhs.append(jnp.concatenate([b_e, k_e], axis=0).astype(BF16))

    inv_w = [eye_w + p for p in p_w]
    p_bd = [stack_masked(p) for p in p_w]
    p_w = [mm(p_w[i].astype(BF16), p_bd[i]) for i in range(nb)]
    levels = int(math.log2(T))
    for lvl in range(1, levels):
        last = lvl == levels - 1
        p_bd = [stack_masked(p) for p in p_w]
        nxt_p, nxt_inv = [], []
        for i in range(nb):
            if last:
                nxt_inv.append(inv_w[i] + mm(inv_w[i].astype(BF16), p_bd[i]))
            else:
                both = mm(jnp.concatenate([p_w[i], inv_w[i]], axis=0).astype(BF16), p_bd[i])
                nxt_p.append(both[0:T])
                nxt_inv.append(inv_w[i] + both[T:R2])
        p_w, inv_w = nxt_p, nxt_inv

    u = [mm(inv_w[i].astype(BF16), stack_masked(x_rhs[i])) for i in range(nb)]
    ys = [r_s[i] + mm(a_rbk[i], jnp.concatenate([stack_masked(u[i]), v_st[i]], axis=0))
          for i in range(nb)]
    uv_t = [jnp.concatenate([u[i], inputs[i][6]], axis=0).T.astype(BF16) for i in range(nb)]
    s_new = [states[i] * decays[i] + jnp.where(head_mask, mm(uv_t[i], uv_rhs[i]), 0.0)
             for i in range(nb)]
    return ys, s_new


WKV_PAIRS_PER_STEP = 2


def _wkv_kernel(r_ref, v_ref, kn_ref, ld_ref, kd_ref, b_ref, y_ref, s_ref, cum_s):
    d = pl.program_id(0)
    c = pl.program_id(2)
    sign = 1 - 2 * d

    @pl.when(c == 0)
    def _():
        s_ref[...] = jnp.zeros_like(s_ref)

    tot8, dec8 = [], []
    for pi in range(WKV_PAIRS_PER_STEP):
        ld = ld_ref[0, pi]
        cum_up = ld
        for lvl in range(int(math.log2(CHUNK_T))):
            sh = BATCH << lvl
            cum_up = cum_up + jnp.concatenate([jnp.zeros((sh, PAIR), F32), cum_up[:-sh]], axis=0)
        tot8.append(cum_up[CHUNK_ROWS - BATCH:])
        tot = _bcast_rows(tot8[pi], CHUNK_ROWS)
        cum_s[pi] = jnp.where(d == 0, cum_up, tot - cum_up + ld)
        dec8.append(jnp.exp(tot8[pi]))

    problems = [(pl.ds(bi, CHUNK_T, stride=BATCH), pi, bi)
                for pi in range(WKV_PAIRS_PER_STEP) for bi in range(BATCH)]
    inputs, states, decays = [], [], []
    for rw, pi, bi in problems:
        cum = cum_s[pi, rw, :]
        ld_b = ld_ref[0, pi, rw, :]
        b = b_ref[0, pi, rw, :]
        kd = kd_ref[0, pi, rw, :]
        e_out = jnp.exp(-cum)
        e_end = jnp.exp(tot8[pi][bi:bi + 1] - cum)
        inputs.append((-kn_ref[pi, rw, :] * jnp.exp(cum - ld_b), r_ref[pi, rw, :] * jnp.exp(cum),
                       b * e_out, kd * e_out, b * e_end, kd * e_end, v_ref[pi, rw, :]))
        states.append(s_ref[pi, bi])
        decays.append(dec8[pi][bi:bi + 1])
    ys, s_new = _wkv_chunks(inputs, states, decays, sign)
    for k, (rw, pi, bi) in enumerate(problems):
        y_ref[0, pi, rw, :] = ys[k]
        s_ref[pi, bi] = s_new[k]


def _wkv(r, v, kn, ld, kd, b):
    def chunk(d, c):
        fwd = c
        bwd = jnp.where(c < N_CTX_CHUNKS, N_CTX_CHUNKS - 1 - c, N_CHUNKS + N_CTX_CHUNKS - 1 - c)
        return jnp.where(d == 0, fwd, bwd)
    lat = lambda d, c: chunk(d, jnp.maximum(c, N_CTX_CHUNKS)) - N_CTX_CHUNKS
    pps = WKV_PAIRS_PER_STEP
    shared = pl.BlockSpec((pps, CHUNK_ROWS, PAIR), lambda d, p, c: (p, chunk(d, c), 0))
    perdir = pl.BlockSpec((1, pps, CHUNK_ROWS, PAIR), lambda d, p, c: (d, p, chunk(d, c), 0))
    return pl.pallas_call(
        _wkv_kernel,
        grid=(2, N_PAIRS // pps, N_CHUNKS),
        in_specs=[shared, shared, shared, perdir, perdir, perdir],
        out_specs=pl.BlockSpec((1, pps, CHUNK_ROWS, PAIR), lambda d, p, c: (d, p, lat(d, c), 0)),
        out_shape=jax.ShapeDtypeStruct((2, N_PAIRS, SEQ * BATCH, PAIR), F32),
        scratch_shapes=[pltpu.VMEM((pps, BATCH, PAIR, PAIR), F32),
                        pltpu.VMEM((pps, CHUNK_ROWS, PAIR), F32)],
        compiler_params=_params(3),
        name="wkv_scan",
    )(r, v, kn, ld, kd, b)


MERGE_T = 32
MERGE_ROWS = MERGE_T * BATCH


def _merge_kernel(lru_ref, y_ref, bon_ref, g_ref, x_ref, gmix_ref, mod_ref,
                  lng_ref, lnb_ref, ones_ref, wgate_ref, wol_ref, wor_ref, wout_ref, o_ref):
    rows = MERGE_ROWS
    W = RWKV_WIDTH
    x = x_ref[...]
    h = _rms_modulate(x, gmix_ref[...], mod_ref[1], mod_ref[0])
    gates = _bdot(h, wgate_ref[...])
    y2 = (y_ref[0] + y_ref[1]).reshape(N_PAIRS, rows, PAIR)
    y = jnp.concatenate([y2[s] for s in range(N_PAIRS)], axis=1)
    inv_n = 1.0 / RWKV_HEAD

    ones = ones_ref[...]
    y_hi = y.astype(BF16).astype(F32)
    mu = (_head_sums(y_hi, ones) + _head_sums(y - y_hi, ones)) * inv_n
    dy = y - mu
    var = _head_sums(dy * dy, ones) * inv_n
    yn = dy * lax.rsqrt(var + GN_EPS) * lng_ref[...] + lnb_ref[...]
    rw = (yn + bon_ref[...].reshape(rows, W)) * g_ref[...].reshape(rows, W)
    m = (_sigmoid(gates[:, :D_MODEL]) * _bdot(lru_ref[...], wol_ref[...])
         + _sigmoid(gates[:, D_MODEL:]) * _bdot(rw, wor_ref[...]))
    mix = _bdot(m, wout_ref[...])
    o_ref[...] = x + _bcast_rows(mod_ref[2], rows) * mix


def _merge(lru_l, y, bon, g, x_tb, g_mix, mod_m, ln_g, ln_b, ones, wgate, wol, wor, wout):
    W = RWKV_WIDTH
    n_blocks = SEQ // MERGE_T
    per_row = GRID_W // MERGE_T
    cm4 = lambda i: (i % per_row, i // per_row, 0, 0)
    y6 = y.reshape(2, N_PAIRS, GRID_W, GRID_ROWS, BATCH, PAIR)
    bon4 = bon.reshape(GRID_W, GRID_ROWS, BATCH, W)
    g4 = g.reshape(GRID_W, GRID_ROWS, BATCH, W)
    return pl.pallas_call(
        _merge_kernel,
        grid=(n_blocks,),
        in_specs=[
            pl.BlockSpec((MERGE_ROWS, LRU_WIDTH), lambda i: (i, 0)),
            pl.BlockSpec((2, N_PAIRS, MERGE_T, 1, BATCH, PAIR), lambda i: (0, 0) + cm4(i)),
            pl.BlockSpec((MERGE_T, 1, BATCH, W), cm4),
            pl.BlockSpec((MERGE_T, 1, BATCH, W), cm4),
            pl.BlockSpec((MERGE_ROWS, D_MODEL), lambda i: (i + CTX_LEN // MERGE_T, 0)),
            _const_spec((1, D_MODEL)), _const_spec((3, BATCH, D_MODEL)),
            _const_spec((1, W)), _const_spec((1, W)), _const_spec(ones.shape),
            _const_spec(wgate.shape), _const_spec(wol.shape), _const_spec(wor.shape),
            _const_spec(wout.shape),
        ],
        out_specs=pl.BlockSpec((MERGE_ROWS, D_MODEL), lambda i: (i, 0)),
        out_shape=jax.ShapeDtypeStruct((SEQ * BATCH, D_MODEL), F32),
        compiler_params=_params(1),
        name="merge",
    )(lru_l, y6, bon4, g4, x_tb, g_mix, mod_m, ln_g, ln_b, ones, wgate, wol, wor, wout)


FFN_T = 64
FFN_ROWS = FFN_T * BATCH
FFN_TILE = 256


def _ffn_up_kernel(x_ref, g_ref, sc_ref, sh_ref, w_ref, o_ref):
    h = _rms_modulate(x_ref[...], g_ref[...], sc_ref[...], sh_ref[...]).astype(BF16)
    for j in range(D_FF // FFN_TILE):
        cs = slice(j * FFN_TILE, (j + 1) * FFN_TILE)
        gate = jnp.dot(h, w_ref[:, cs], preferred_element_type=F32)
        up = jnp.dot(h, w_ref[:, D_FF + j * FFN_TILE:D_FF + (j + 1) * FFN_TILE],
                     preferred_element_type=F32)
        o_ref[:, cs] = (_silu(gate) * up).astype(BF16)


def _ffn_up(x1, g, sc8, sh8, w_in):
    return pl.pallas_call(
        _ffn_up_kernel,
        grid=(SEQ * BATCH // FFN_ROWS,),
        in_specs=[pl.BlockSpec((FFN_ROWS, D_MODEL), lambda i: (i, 0)),
                  _const_spec((1, D_MODEL)), _const_spec((BATCH, D_MODEL)),
                  _const_spec((BATCH, D_MODEL)), _const_spec(w_in.shape)],
        out_specs=pl.BlockSpec((FFN_ROWS, D_FF), lambda i: (i, 0)),
        out_shape=jax.ShapeDtypeStruct((SEQ * BATCH, D_FF), BF16),
        compiler_params=_params(1),
        name="ffn_up",
    )(x1, g, sc8, sh8, w_in)


def _ffn_down_kernel(act_ref, x_ref, gf_ref, w_ref, gfin_ref, o_ref, stage_s):
    y = jnp.dot(act_ref[...], w_ref[...], preferred_element_type=F32)
    x2 = x_ref[...] + _bcast_rows(gf_ref[...], FFN_ROWS) * y
    ms = jnp.mean(x2 * x2, axis=-1, keepdims=True)
    out = x2 * lax.rsqrt(ms + RMS_EPS) * gfin_ref[...]
    for s in range(D_TILES):
        stage_s[s] = out[:, s * LANE:(s + 1) * LANE]
    for bi in range(BATCH):
        rows = pl.ds(bi, FFN_T, stride=BATCH)
        for s in range(D_TILES):
            o_ref[bi, :, s * LANE:(s + 1) * LANE] = stage_s[s, rows, :]


def _ffn_down(act, x1, g_f8, w_out, g_final):
    return pl.pallas_call(
        _ffn_down_kernel,
        grid=(SEQ * BATCH // FFN_ROWS,),
        in_specs=[pl.BlockSpec((FFN_ROWS, D_FF), lambda i: (i, 0)),
                  pl.BlockSpec((FFN_ROWS, D_MODEL), lambda i: (i, 0)),
                  _const_spec((BATCH, D_MODEL)), _const_spec(w_out.shape),
                  _const_spec((1, D_MODEL))],
        out_specs=pl.BlockSpec((BATCH, FFN_T, D_MODEL), lambda i: (0, i, 0)),
        out_shape=jax.ShapeDtypeStruct((BATCH, SEQ, D_MODEL), F32),
        scratch_shapes=[pltpu.VMEM((D_TILES, FFN_ROWS, LANE), F32)],
        compiler_params=_params(1),
        name="ffn_down",
    )(act, x1, g_f8, w_out, g_final)


def _block_diag(w):
    n, c, _ = w.shape
    tiled = jnp.tile(w.reshape(n * c, c), (1, n))
    rb = lax.broadcasted_iota(jnp.int32, (n * c, n * c), 0) // c
    cb = lax.broadcasted_iota(jnp.int32, (n * c, n * c), 1) // c
    return jnp.where(rb == cb, tiled, 0.0)


def _lru_gate_tiles(wa, wx):
    da, dx = _block_diag(0.5 * wa), _block_diag(0.5 * wx)
    tiles = []
    for j, k0 in enumerate(LRU_GATE_K0):
        cs = slice(j * LRU_GATE_TILE, (j + 1) * LRU_GATE_TILE)
        tiles.append(jnp.concatenate([da[k0:k0 + LRU_GATE_K, cs], dx[k0:k0 + LRU_GATE_K, cs]], axis=1))
    return jnp.stack(tiles).astype(BF16)


def _two_dir_lora(w):
    z = jnp.zeros_like(w[0])
    return jnp.concatenate([jnp.concatenate([w[0], z], axis=1),
                            jnp.concatenate([z, w[1]], axis=1)], axis=0).astype(BF16)


def kernel(x, c, ctx, c_ctx, norm_mix_g, norm_ffn_g, w_mod, b_mod, w_in, lru_conv_w, lru_conv_b, lru_wa, lru_ba, lru_wx, lru_bx, lru_lambda, w_o_lru, rwkv_mu, rwkv_w0, rwkv_w2, rwkv_a0, rwkv_a2, rwkv_g2, rwkv_k_k, rwkv_k_a, rwkv_r_k, rwkv_ln_g, rwkv_ln_b, w_o_rwkv, w_out, w_ffn_in, w_ffn_out, norm_final_g):
    assert x.shape == (BATCH, SEQ, D_MODEL) and ctx.shape == (BATCH, CTX_LEN, D_MODEL)
    assert w_mod.shape[0] == 1, "single layer only"
    D, W = D_MODEL, RWKV_WIDTH

    c16 = jnp.concatenate([c, c_ctx[None], jnp.zeros((16 - BATCH - 1, D), F32)], axis=0)
    mod = _adaln(c16, w_mod[0], b_mod[0][None])
    mod_lat = mod[:BATCH].reshape(BATCH, 6, D)
    mod_ctx = jnp.broadcast_to(mod[BATCH:BATCH + 1], (BATCH, 6 * D)).reshape(BATCH, 6, D)
    sh_m, sc_m, g_m, sh_f, sc_f, g_f = [mod_lat[:, k] for k in range(6)]
    sh2 = jnp.stack([mod_ctx[:, 0], sh_m])
    sc2 = jnp.stack([mod_ctx[:, 1], sc_m])

    w_in0 = w_in[0]
    n_lru = 2 * LRU_WIDTH
    w_lru = w_in0[:, :n_lru].astype(BF16)
    w_rw = jnp.pad(w_in0[:, n_lru:n_lru + RWKV_IN], ((0, 0), (0, RWKV_IN_PAD - RWKV_IN))).astype(BF16)
    w_gate = w_in0[:, n_lru + RWKV_IN:].astype(BF16)
    g_mix = norm_mix_g[0][None]

    x_tb = jnp.concatenate([ctx.transpose(1, 0, 2), x.transpose(1, 0, 2)], axis=0)
    x_tb = x_tb.reshape(T_ALL * BATCH, D)
    proj = functools.partial(_proj, x_tb, g_mix, sc2, sh2)
    u_all = proj(w_lru, first_chunk=0, n_chunks=N_CHUNKS, colmajor_out=False)
    zr_c = proj(w_rw, first_chunk=0, n_chunks=N_CTX_CHUNKS, colmajor_out=False)
    zr_l = proj(w_rw, first_chunk=N_CTX_CHUNKS, n_chunks=N_LAT_CHUNKS,
                colmajor_out=True).reshape(SEQ * BATCH, RWKV_IN_PAD)

    wt = jnp.stack([_lru_gate_tiles(lru_wa[0, d], lru_wx[0, d]) for d in range(2)])
    lru_p = (lru_conv_w[0], lru_conv_b[0], wt, 0.5 * lru_ba[0], 0.5 * lru_bx[0], lru_lambda[0])
    hf = _lru_scan(u_all, None, lru_p, 0, reverse=False)
    lru_l = _lru_scan(u_all, hf, lru_p, 1, reverse=True)

    mu_pad = jnp.pad(rwkv_mu[0], ((0, 0), (0, RWKV_IN_PAD - RWKV_IN)))
    ones = _block_diag(jnp.ones((PAIR // RWKV_HEAD, RWKV_HEAD, RWKV_HEAD), F32)).astype(BF16)
    rw_p = (mu_pad, _two_dir_lora(rwkv_w2[0]), rwkv_w0[0].reshape(1, 2 * W),
            _two_dir_lora(rwkv_a2[0]), rwkv_a0[0].reshape(1, 2 * W),
            jnp.pad(rwkv_g2[0], ((0, LORA_G_PAD - LORA_G), (0, 0))).astype(BF16),
            rwkv_k_k[0][None], rwkv_k_a[0][None], rwkv_r_k[0].reshape(1, W), ones)
    r, v, kn, ld, kd, b, bon, g = _rwkv_prep(zr_c, zr_l, rw_p)
    y = _wkv(r, v, kn, ld, kd, b)

    x1 = _merge(lru_l, y, bon, g, x_tb, g_mix, jnp.stack([sh_m, sc_m, g_m]),
                rwkv_ln_g[0][None], rwkv_ln_b[0][None], ones, w_gate,
                w_o_lru[0].astype(BF16), w_o_rwkv[0].astype(BF16), w_out[0].astype(BF16))

    act = _ffn_up(x1, norm_ffn_g[0][None], sc_f, sh_f, w_ffn_in[0].astype(BF16))
    return _ffn_down(act, x1, g_f, w_ffn_out[0].astype(BF16), norm_final_g[None])
```

```python
import functools
import math

import jax
import jax.numpy as jnp
from jax import lax
from jax.experimental import pallas as pl
from jax.experimental.pallas import tpu as pltpu

F32 = jnp.float32
BF16 = jnp.bfloat16

D_MODEL = 1024
BATCH = 8
SEQ = 2048
CTX_LEN = 256
GRID_W = 64
GRID_ROWS = SEQ // GRID_W

LRU_WIDTH = 1280
LRU_BLOCKS = 16
LRU_BLOCK = LRU_WIDTH // LRU_BLOCKS
LRU_CONV = 4
LRU_C = 8.0

RWKV_HEAD = 64
RWKV_WIDTH = 1024
LORA_W = 64
LORA_A = 64
LORA_G = 160
RWKV_IN = 3 * RWKV_WIDTH + 2 * LORA_W + 2 * LORA_A + LORA_G
RWKV_IN_PAD = 3584
LORA_G_PAD = RWKV_IN_PAD - (3 * RWKV_WIDTH + 2 * LORA_W + 2 * LORA_A)
D_FF = 2816

RMS_EPS = 1e-6
GN_EPS = 64e-5
L2_EPS = 1e-12

T_ALL = CTX_LEN + SEQ
CHUNK_T = 64
CHUNK_ROWS = CHUNK_T * BATCH
N_CTX_CHUNKS = CTX_LEN // CHUNK_T
N_LAT_CHUNKS = SEQ // CHUNK_T
N_CHUNKS = N_CTX_CHUNKS + N_LAT_CHUNKS

LANE = 128
D_TILES = D_MODEL // LANE
PAIR = 2 * RWKV_HEAD
N_PAIRS = RWKV_WIDTH // PAIR

LRU_GATE_TILE = 256
LRU_GATE_K = 512
LRU_GATE_K0 = (0, 128, 384, 640, 768)

VMEM_LIMIT = 56 * 1024 * 1024


def _params(n_axes):
    return pltpu.CompilerParams(dimension_semantics=("arbitrary",) * n_axes,
                                vmem_limit_bytes=VMEM_LIMIT)


def _const_spec(shape):
    nd = len(shape)
    return pl.BlockSpec(shape, lambda *_: (0,) * nd)


def _bdot(a, b):
    return jnp.dot(a.astype(BF16), b.astype(BF16), preferred_element_type=F32)


def _bdot_nt(a, b):
    return lax.dot_general(a.astype(BF16), b.astype(BF16), (((1,), (1,)), ((), ())),
                           preferred_element_type=F32)


def _softplus(x):
    return jnp.maximum(x, 0.0) + jnp.log1p(jnp.exp(-jnp.abs(x)))


def _sigmoid(x):
    return 0.5 * jnp.tanh(0.5 * x) + 0.5


def _head_sums(t, ones_pair):
    tb = t.astype(BF16)
    return jnp.concatenate(
        [jnp.dot(tb[:, s * PAIR:(s + 1) * PAIR], ones_pair, preferred_element_type=F32)
         for s in range(t.shape[1] // PAIR)], axis=1)


def _silu(x):
    return x * _sigmoid(x)


def _gelu_tanh(x):
    c = math.sqrt(2.0 / math.pi)
    return 0.5 * x * (1.0 + jnp.tanh(c * (x + 0.044715 * (x * x * x))))


def _bcast_rows(v8, rows):
    c = v8.shape[-1]
    return jnp.broadcast_to(v8[None], (rows // BATCH, BATCH, c)).reshape(rows, c)


def _rms_modulate(x, g, scale8, shift8):
    rows = x.shape[0]
    ms = jnp.mean(x * x, axis=-1, keepdims=True)
    y = x * lax.rsqrt(ms + RMS_EPS) * g
    return y * (1.0 + _bcast_rows(scale8, rows)) + _bcast_rows(shift8, rows)


def _adaln_kernel(c_ref, w_ref, b_ref, o_ref):
    s = _silu(c_ref[...])
    o_ref[...] = jnp.dot(s, w_ref[...], preferred_element_type=F32,
                         precision=lax.Precision.HIGHEST) + b_ref[...]


def _adaln(c16, w_mod, b_mod):
    n = w_mod.shape[1]
    tn = 1536
    return pl.pallas_call(
        _adaln_kernel,
        grid=(n // tn,),
        in_specs=[_const_spec((16, D_MODEL)),
                  pl.BlockSpec((D_MODEL, tn), lambda j: (0, j)),
                  pl.BlockSpec((1, tn), lambda j: (0, j))],
        out_specs=pl.BlockSpec((16, tn), lambda j: (0, j)),
        out_shape=jax.ShapeDtypeStruct((16, n), F32),
        compiler_params=_params(1),
        name="adaln",
    )(c16, w_mod, b_mod)


LANE = 128
D_TILES = D_MODEL // LANE


def _proj_kernel(xc_ref, xl_ref, g_ref, sc_ref, sh_ref, w_ref, o_ref):
    is_ctx = pl.program_id(0) < N_CTX_CHUNKS
    x = jnp.where(is_ctx, xc_ref[...], xl_ref[...])
    h = _rms_modulate(x, g_ref[...], sc_ref[0], sh_ref[0])
    o_ref[...] = jnp.dot(h.astype(BF16), w_ref[...], preferred_element_type=F32)


def _proj(xc_tb, xl_tb, g, sc2, sh2, w):
    n = w.shape[1]
    seg = lambda i: (i >= N_CTX_CHUNKS).astype(jnp.int32)
    return pl.pallas_call(
        _proj_kernel,
        grid=(N_CHUNKS,),
        in_specs=[pl.BlockSpec((CHUNK_ROWS, D_MODEL), lambda i: (jnp.minimum(i, N_CTX_CHUNKS - 1), 0)),
                  pl.BlockSpec((CHUNK_ROWS, D_MODEL), lambda i: (jnp.maximum(i - N_CTX_CHUNKS, 0), 0)),
                  _const_spec((1, D_MODEL)),
                  pl.BlockSpec((1, BATCH, D_MODEL), lambda i: (seg(i), 0, 0)),
                  pl.BlockSpec((1, BATCH, D_MODEL), lambda i: (seg(i), 0, 0)),
                  _const_spec((D_MODEL, n))],
        out_specs=pl.BlockSpec((CHUNK_ROWS, n), lambda i: (i, 0)),
        out_shape=jax.ShapeDtypeStruct((N_CHUNKS * CHUNK_ROWS, n), F32),
        compiler_params=_params(1),
        name="in_proj",
    )(xc_tb, xl_tb, g, sc2, sh2, w)


def _lru_kernel(*refs, reverse):
    if reverse:
        (u_ref, uy_ref, hf_ref, cw_ref, cb_ref, wt_ref, ba_ref, bx_ref, lam_ref,
         o_ref, h_s, halo_s, ext_s, a_s, b_s) = refs
    else:
        (u_ref, cw_ref, cb_ref, wt_ref, ba_ref, bx_ref, lam_ref,
         o_ref, h_s, halo_s, ext_s, a_s, b_s) = refs
    i = pl.program_id(0)
    halo_rows = (LRU_CONV - 1) * BATCH

    @pl.when(i == 0)
    def _():
        h_s[...] = jnp.zeros_like(h_s)

    @pl.when((i == 0) | (i == N_CTX_CHUNKS))
    def _():
        halo_s[...] = jnp.zeros_like(halo_s)

    u = u_ref[...]
    if reverse:
        ext_s[0:CHUNK_ROWS] = u
        ext_s[CHUNK_ROWS:CHUNK_ROWS + halo_rows] = halo_s[...]
        halo_s[...] = u[0:halo_rows]
    else:
        ext_s[0:halo_rows] = halo_s[...]
        ext_s[halo_rows:halo_rows + CHUNK_ROWS] = u
        halo_s[...] = u[CHUNK_ROWS - halo_rows:CHUNK_ROWS]

    xc = jnp.broadcast_to(cb_ref[...], (CHUNK_ROWS, LRU_WIDTH))
    for j in range(LRU_CONV):
        off = (LRU_CONV - 1 - j) if reverse else j
        xc = xc + cw_ref[j:j + 1, :] * ext_s[off * BATCH:off * BATCH + CHUNK_ROWS]

    xcb = xc.astype(BF16)
    c_half = (-0.5 * LRU_C) * _softplus(-lam_ref[...])
    for j in range(LRU_WIDTH // LRU_GATE_TILE):
        k0 = LRU_GATE_K0[j]
        cs = slice(j * LRU_GATE_TILE, (j + 1) * LRU_GATE_TILE)
        g = jnp.dot(xcb[:, k0:k0 + LRU_GATE_K], wt_ref[j], preferred_element_type=F32)
        tanh_r = jnp.tanh(g[:, :LRU_GATE_TILE] + ba_ref[:, cs])
        gate_i = 0.5 * jnp.tanh(g[:, LRU_GATE_TILE:] + bx_ref[:, cs]) + 0.5
        log_a = c_half[:, cs] * tanh_r + c_half[:, cs]
        a = jnp.exp(log_a)
        a_s[:, cs] = a
        om = 1.0 - a * a
        b_s[:, cs] = jnp.where(om > 0.0, om * lax.rsqrt(om), 0.0) * (gate_i * xc[:, cs])

    def step(k, h):
        t = (CHUNK_T - 1 - k) if reverse else k
        rows = pl.ds(pl.multiple_of(t * BATCH, BATCH), BATCH)
        h = a_s[rows, :] * h + b_s[rows, :]
        if reverse:
            o_ref[rows, :] = (hf_ref[rows, :] + h) * _gelu_tanh(uy_ref[rows, :])
        else:
            o_ref[rows, :] = h
        return h

    h_s[...] = lax.fori_loop(0, CHUNK_T, step, h_s[...], unroll=8)


def _lru_scan(u_all, hf, lru_p, d, *, reverse):
    cw, cb, wt, ba, bx, lam = lru_p
    if reverse:
        chunk = lambda i: jnp.where(i < N_CTX_CHUNKS, N_CTX_CHUNKS - 1 - i,
                                    N_CHUNKS + N_CTX_CHUNKS - 1 - i)
    else:
        chunk = lambda i: i
    lat = lambda i: jnp.maximum(chunk(jnp.maximum(i, N_CTX_CHUNKS)) - N_CTX_CHUNKS, 0)
    blk = (CHUNK_ROWS, LRU_WIDTH)
    in_specs = [pl.BlockSpec(blk, lambda i: (chunk(i), 0))]
    args = [u_all]
    if reverse:
        in_specs += [pl.BlockSpec(blk, lambda i: (lat(i) + N_CTX_CHUNKS, 1)),
                     pl.BlockSpec(blk, lambda i: (lat(i), 0))]
        args += [u_all, hf]
    in_specs += [_const_spec((LRU_CONV, LRU_WIDTH)), _const_spec((1, LRU_WIDTH)),
                 _const_spec(wt[d].shape), _const_spec((1, LRU_WIDTH)),
                 _const_spec((1, LRU_WIDTH)), _const_spec((1, LRU_WIDTH))]
    args += [cw[d], cb[d][None], wt[d], ba[d][None], bx[d][None], lam[d][None]]
    halo_rows = (LRU_CONV - 1) * BATCH
    return pl.pallas_call(
        functools.partial(_lru_kernel, reverse=reverse),
        grid=(N_CHUNKS,),
        in_specs=in_specs,
        out_specs=pl.BlockSpec(blk, lambda i: (lat(i), 0)),
        out_shape=jax.ShapeDtypeStruct((SEQ * BATCH, LRU_WIDTH), F32),
        scratch_shapes=[pltpu.VMEM((BATCH, LRU_WIDTH), F32),
                        pltpu.VMEM((halo_rows, LRU_WIDTH), F32),
                        pltpu.VMEM((CHUNK_ROWS + halo_rows, LRU_WIDTH), F32),
                        pltpu.VMEM(blk, F32),
                        pltpu.VMEM(blk, F32)],
        compiler_params=_params(1),
        name="lru_bwd" if reverse else "lru_fwd",
    )(*args)


PREP_T = 32
PREP_ROWS = PREP_T * BATCH
PREP_CTX_BLOCKS = CTX_LEN // PREP_T
PREP_LAT_BLOCKS = SEQ // PREP_T
PREP_BLOCKS = PREP_CTX_BLOCKS + PREP_LAT_BLOCKS


def _prep_kernel(xc_ref, xl_ref, xcp_ref, xcn_ref, xlp_ref, xln_ref, gmix_ref, sc_ref, sh_ref,
                 win_ref, mu_ref,
                 w2_ref, w0_ref, a2_ref, a0_ref, g2_ref, kk_ref, ka_ref, rk_ref, ones_ref,
                 r_o, v_o, kn_o, ld_o, kd_o, b_o, bon_o, g_o, ext_s):
    i = pl.program_id(0)
    is_ctx = i < PREP_CTX_BLOCKS
    first = (i == 0) | (i == PREP_CTX_BLOCKS)
    last = (i == PREP_CTX_BLOCKS - 1) | (i == PREP_BLOCKS - 1)
    lat_rows = lambda ref: ref[...].reshape(-1, D_MODEL)
    x = jnp.concatenate([
        jnp.where(is_ctx, xcp_ref[...], lat_rows(xlp_ref)),
        jnp.where(is_ctx, xc_ref[...], lat_rows(xl_ref)),
        jnp.where(is_ctx, xcn_ref[...], lat_rows(xln_ref))], axis=0)
    seg = jnp.where(is_ctx, 0, 1)
    h = _rms_modulate(x, gmix_ref[...], sc_ref[seg], sh_ref[seg])
    hb = h.astype(BF16)
    W = RWKV_WIDTH
    for c0, c1 in ((3 * W, RWKV_IN_PAD), (W, 2 * W), (0, W), (2 * W, 3 * W)):
        zt = jnp.dot(hb, win_ref[:, c0:c1], preferred_element_type=F32)
        ext_s[0:BATCH, c0:c1] = jnp.where(first, 0.0, zt[0:BATCH])
        ext_s[BATCH:BATCH + PREP_ROWS, c0:c1] = zt[BATCH:BATCH + PREP_ROWS]
        ext_s[BATCH + PREP_ROWS:2 * BATCH + PREP_ROWS, c0:c1] = jnp.where(
            last, 0.0, zt[BATCH + PREP_ROWS:2 * BATCH + PREP_ROWS])

    def shifted(c0, c1):
        zc = ext_s[BATCH:BATCH + PREP_ROWS, c0:c1]
        zprev = ext_s[0:PREP_ROWS, c0:c1]
        znext = ext_s[2 * BATCH:2 * BATCH + PREP_ROWS, c0:c1]
        return zc + mu_ref[0:1, c0:c1] * (zprev - zc) + mu_ref[1:2, c0:c1] * (znext - zc)

    wd = shifted(3 * W, 3 * W + 2 * LORA_W)
    ad = shifted(3 * W + 2 * LORA_W, 3 * W + 2 * LORA_W + 2 * LORA_A)
    gd = shifted(3 * W + 2 * LORA_W + 2 * LORA_A, RWKV_IN_PAD)
    k = shifted(W, 2 * W)
    r = shifted(0, W)
    v = shifted(2 * W, 3 * W)

    w_pre = _bdot(jnp.tanh(wd), w2_ref[...]) + w0_ref[...]
    a_pre = _bdot(ad, a2_ref[...]) + a0_ref[...]
    kk = k * kk_ref[...]
    ss = _head_sums(kk * kk, ones_ref[...])
    kn = kk / jnp.maximum(jnp.sqrt(ss), L2_EPS)

    def store_pairs(o, val):
        for s in range(N_PAIRS):
            o[s] = val[:, s * PAIR:(s + 1) * PAIR]

    store_pairs(r_o, r)
    store_pairs(v_o, v)
    store_pairs(kn_o, kn)
    kd_sum = jnp.zeros_like(k)
    for d in range(2):
        cs = slice(d * W, (d + 1) * W)
        store_pairs(ld_o.at[d], -math.exp(-0.5) * _sigmoid(w_pre[:, cs]))
        asig = _sigmoid(a_pre[:, cs])
        kd = k * (1.0 + (asig - 1.0) * ka_ref[...])
        store_pairs(kd_o.at[d], kd)
        store_pairs(b_o.at[d], kn * asig)
        kd_sum = kd_sum + kd
    bon_o[...] = _head_sums(r * kd_sum * rk_ref[...], ones_ref[...]) * v
    g_o[...] = _bdot(_sigmoid(gd), g2_ref[...])


def _rwkv_prep(xc_tb, xl_tb, g_mix, sc2, sh2, w_rw, rw_p):
    mu, w2, w0, a2, a0, g2, k_k, k_a, r_k, ones = rw_p
    W = RWKV_WIDTH
    hb = PREP_T
    cmain = lambda i: jnp.minimum(i, PREP_CTX_BLOCKS - 1)
    lmain = lambda i: jnp.maximum(i - PREP_CTX_BLOCKS, 0)
    assert PREP_T == GRID_ROWS
    xl4 = xl_tb.reshape(GRID_ROWS, GRID_W, BATCH, D_MODEL)
    hblk = (BATCH, D_MODEL)
    n_c8 = CTX_LEN - 1
    in_specs = [
        pl.BlockSpec((PREP_ROWS, D_MODEL), lambda i: (cmain(i), 0)),
        pl.BlockSpec((GRID_ROWS, 1, BATCH, D_MODEL), lambda i: (0, lmain(i), 0, 0)),
        pl.BlockSpec(hblk, lambda i: (jnp.maximum(cmain(i) * hb - 1, 0), 0)),
        pl.BlockSpec(hblk, lambda i: (jnp.minimum((cmain(i) + 1) * hb, n_c8), 0)),
        pl.BlockSpec((1, 1, BATCH, D_MODEL), lambda i: (GRID_ROWS - 1, jnp.maximum(lmain(i) - 1, 0), 0, 0)),
        pl.BlockSpec((1, 1, BATCH, D_MODEL), lambda i: (0, jnp.minimum(lmain(i) + 1, GRID_W - 1), 0, 0)),
        _const_spec((1, D_MODEL)), _const_spec(sc2.shape), _const_spec(sh2.shape),
        _const_spec(w_rw.shape),
        _const_spec(mu.shape), _const_spec(w2.shape), _const_spec(w0.shape),
        _const_spec(a2.shape), _const_spec(a0.shape), _const_spec(g2.shape),
        _const_spec(k_k.shape), _const_spec(k_a.shape), _const_spec(r_k.shape),
        _const_spec(ones.shape),
    ]
    n_rows = T_ALL * BATCH
    n_lat = SEQ * BATCH
    shared = pl.BlockSpec((N_PAIRS, PREP_ROWS, PAIR), lambda i: (0, i, 0))
    perdir = pl.BlockSpec((2, N_PAIRS, PREP_ROWS, PAIR), lambda i: (0, 0, i, 0))
    latonly = pl.BlockSpec((PREP_ROWS, W), lambda i: (lmain(i), 0))
    return pl.pallas_call(
        _prep_kernel,
        grid=(PREP_BLOCKS,),
        in_specs=in_specs,
        out_specs=[shared, shared, shared, perdir, perdir, perdir, latonly, latonly],
        out_shape=[jax.ShapeDtypeStruct((N_PAIRS, n_rows, PAIR), F32)] * 3
        + [jax.ShapeDtypeStruct((2, N_PAIRS, n_rows, PAIR), F32)] * 3
        + [jax.ShapeDtypeStruct((n_lat, W), F32)] * 2,
        scratch_shapes=[pltpu.VMEM((PREP_ROWS + 2 * BATCH, RWKV_IN_PAD), F32)],
        compiler_params=_params(1),
        name="rwkv_prep",
    )(xc_tb, xl4, xc_tb, xc_tb, xl4, xl4, g_mix, sc2, sh2, w_rw,
      mu, w2, w0, a2, a0, g2, k_k, k_a, r_k, ones)


def _wkv_chunks(inputs, states, decays, sign):
    T = CHUNK_T
    R2 = 2 * T
    nb = len(inputs)

    lane = lax.broadcasted_iota(jnp.int32, (R2, PAIR), 1)
    row = lax.broadcasted_iota(jnp.int32, (R2, PAIR), 0)
    head_mask = (lane // RWKV_HEAD) == (row // T)

    def stack_masked(x):
        return jnp.where(head_mask, jnp.concatenate([x, x], axis=0), 0.0).astype(BF16)

    tw = lax.broadcasted_iota(jnp.int32, (T, 2 * R2), 0)
    sw = lax.broadcasted_iota(jnp.int32, (T, 2 * R2), 1) % T
    dtw = (tw - sw) * sign
    strict_w = dtw[:, 0:R2] > 0
    incl_w2 = dtw >= 0
    eye_w = jnp.where(dtw[:, 0:R2] == 0, 1.0, 0.0)

    nt = lambda x, y: lax.dot_general(x, y, (((1,), (1,)), ((), ())), preferred_element_type=F32)
    mm = lambda x, y: jnp.dot(x, y, preferred_element_type=F32)

    p_w, a_rbk, v_st, x_rhs, r_s, uv_rhs = [], [], [], [], [], []
    for (a_t, r_t, b_t, k_t, b_e, k_e, v), S in zip(inputs, states):
        ar = jnp.concatenate([a_t, r_t], axis=0).astype(BF16)
        bk = jnp.concatenate([stack_masked(b_t), stack_masked(k_t)], axis=0)
        big = nt(ar, bk)
        p_w.append(jnp.where(strict_w, big[0:T, 0:R2], 0.0))
        a_ak = jnp.where(strict_w, big[0:T, R2:2 * R2], 0.0).astype(BF16)
        a_rbk.append(jnp.where(incl_w2, big[T:R2], 0.0).astype(BF16))
        v_st.append(stack_masked(v))
        fs = nt(ar, S.astype(BF16))
        x_rhs.append(fs[0:T] + mm(a_ak, v_st[-1]))
        r_s.append(fs[T:R2])
        uv_rhs.append(jnp.concatenate([b_e, k_e], axis=0).astype(BF16))

    inv_w = [eye_w + p for p in p_w]
    p_bd = [stack_masked(p) for p in p_w]
    p_w = [mm(p_w[i].astype(BF16), p_bd[i]) for i in range(nb)]
    levels = int(math.log2(T))
    for lvl in range(1, levels):
        last = lvl == levels - 1
        p_bd = [stack_masked(p) for p in p_w]
        nxt_p, nxt_inv = [], []
        for i in range(nb):
            if last:
                nxt_inv.append(inv_w[i] + mm(inv_w[i].astype(BF16), p_bd[i]))
            else:
                both = mm(jnp.concatenate([p_w[i], inv_w[i]], axis=0).astype(BF16), p_bd[i])
                nxt_p.append(both[0:T])
                nxt_inv.append(inv_w[i] + both[T:R2])
        p_w, inv_w = nxt_p, nxt_inv

    u = [mm(inv_w[i].astype(BF16), stack_masked(x_rhs[i])) for i in range(nb)]
    ys = [r_s[i] + mm(a_rbk[i], jnp.concatenate([stack_masked(u[i]), v_st[i]], axis=0))
          for i in range(nb)]
    uv_t = [jnp.concatenate([u[i], inputs[i][6]], axis=0).T.astype(BF16) for i in range(nb)]
    s_new = [states[i] * decays[i] + jnp.where(head_mask, mm(uv_t[i], uv_rhs[i]), 0.0)
             for i in range(nb)]
    return ys, s_new


WKV_PAIRS_PER_STEP = 2


def _wkv_kernel(r_ref, v_ref, kn_ref, ld_ref, kd_ref, b_ref, y_ref, s_ref, cum_s):
    d = pl.program_id(0)
    c = pl.program_id(2)
    sign = 1 - 2 * d

    @pl.when(c == 0)
    def _():
        s_ref[...] = jnp.zeros_like(s_ref)

    tot8, dec8 = [], []
    for pi in range(WKV_PAIRS_PER_STEP):
        ld = ld_ref[0, pi]
        cum_up = ld
        for lvl in range(int(math.log2(CHUNK_T))):
            sh = BATCH << lvl
            cum_up = cum_up + jnp.concatenate([jnp.zeros((sh, PAIR), F32), cum_up[:-sh]], axis=0)
        tot8.append(cum_up[CHUNK_ROWS - BATCH:])
        tot = _bcast_rows(tot8[pi], CHUNK_ROWS)
        cum_s[pi] = jnp.where(d == 0, cum_up, tot - cum_up + ld)
        dec8.append(jnp.exp(tot8[pi]))

    problems = [(pl.ds(bi, CHUNK_T, stride=BATCH), pi, bi)
                for pi in range(WKV_PAIRS_PER_STEP) for bi in range(BATCH)]
    inputs, states, decays = [], [], []
    for rw, pi, bi in problems:
        cum = cum_s[pi, rw, :]
        ld_b = ld_ref[0, pi, rw, :]
        b = b_ref[0, pi, rw, :]
        kd = kd_ref[0, pi, rw, :]
        e_out = jnp.exp(-cum)
        e_end = jnp.exp(tot8[pi][bi:bi + 1] - cum)
        inputs.append((-kn_ref[pi, rw, :] * jnp.exp(cum - ld_b), r_ref[pi, rw, :] * jnp.exp(cum),
                       b * e_out, kd * e_out, b * e_end, kd * e_end, v_ref[pi, rw, :]))
        states.append(s_ref[pi, bi])
        decays.append(dec8[pi][bi:bi + 1])
    ys, s_new = _wkv_chunks(inputs, states, decays, sign)
    for k, (rw, pi, bi) in enumerate(problems):
        y_ref[0, pi, rw, :] = ys[k]
        s_ref[pi, bi] = s_new[k]


def _wkv(r, v, kn, ld, kd, b):
    def chunk(d, c):
        fwd = c
        bwd = jnp.where(c < N_CTX_CHUNKS, N_CTX_CHUNKS - 1 - c, N_CHUNKS + N_CTX_CHUNKS - 1 - c)
        return jnp.where(d == 0, fwd, bwd)
    lat = lambda d, c: chunk(d, jnp.maximum(c, N_CTX_CHUNKS)) - N_CTX_CHUNKS
    pps = WKV_PAIRS_PER_STEP
    shared = pl.BlockSpec((pps, CHUNK_ROWS, PAIR), lambda d, p, c: (p, chunk(d, c), 0))
    perdir = pl.BlockSpec((1, pps, CHUNK_ROWS, PAIR), lambda d, p, c: (d, p, chunk(d, c), 0))
    return pl.pallas_call(
        _wkv_kernel,
        grid=(2, N_PAIRS // pps, N_CHUNKS),
        in_specs=[shared, shared, shared, perdir, perdir, perdir],
        out_specs=pl.BlockSpec((1, pps, CHUNK_ROWS, PAIR), lambda d, p, c: (d, p, lat(d, c), 0)),
        out_shape=jax.ShapeDtypeStruct((2, N_PAIRS, SEQ * BATCH, PAIR), F32),
        scratch_shapes=[pltpu.VMEM((pps, BATCH, PAIR, PAIR), F32),
                        pltpu.VMEM((pps, CHUNK_ROWS, PAIR), F32)],
        compiler_params=_params(3),
        name="wkv_scan",
    )(r, v, kn, ld, kd, b)


MERGE_T = 32
MERGE_ROWS = MERGE_T * BATCH


def _merge_kernel(lru_ref, y_ref, bon_ref, g_ref, x_ref, gmix_ref, mod_ref,
                  lng_ref, lnb_ref, ones_ref, wgate_ref, wol_ref, wor_ref, wout_ref, o_ref):
    rows = MERGE_ROWS
    W = RWKV_WIDTH
    x = x_ref[...]
    h = _rms_modulate(x, gmix_ref[...], mod_ref[1], mod_ref[0])
    gates = _bdot(h, wgate_ref[...])
    y2 = (y_ref[0] + y_ref[1]).reshape(N_PAIRS, rows, PAIR)
    y = jnp.concatenate([y2[s] for s in range(N_PAIRS)], axis=1)
    inv_n = 1.0 / RWKV_HEAD

    ones = ones_ref[...]
    y_hi = y.astype(BF16).astype(F32)
    mu = (_head_sums(y_hi, ones) + _head_sums(y - y_hi, ones)) * inv_n
    dy = y - mu
    var = _head_sums(dy * dy, ones) * inv_n
    yn = dy * lax.rsqrt(var + GN_EPS) * lng_ref[...] + lnb_ref[...]
    rw = (yn + bon_ref[...].reshape(rows, W)) * g_ref[...].reshape(rows, W)
    m = (_sigmoid(gates[:, :D_MODEL]) * _bdot(lru_ref[...], wol_ref[...])
         + _sigmoid(gates[:, D_MODEL:]) * _bdot(rw, wor_ref[...]))
    mix = _bdot(m, wout_ref[...])
    o_ref[...] = x + _bcast_rows(mod_ref[2], rows) * mix


def _merge(lru_l, y, bon, g, x_tb, g_mix, mod_m, ln_g, ln_b, ones, wgate, wol, wor, wout):
    W = RWKV_WIDTH
    n_blocks = SEQ // MERGE_T
    per_row = GRID_W // MERGE_T
    cm4 = lambda i: (i % per_row, i // per_row, 0, 0)
    y6 = y.reshape(2, N_PAIRS, GRID_W, GRID_ROWS, BATCH, PAIR)
    bon4 = bon.reshape(GRID_W, GRID_ROWS, BATCH, W)
    g4 = g.reshape(GRID_W, GRID_ROWS, BATCH, W)
    return pl.pallas_call(
        _merge_kernel,
        grid=(n_blocks,),
        in_specs=[
            pl.BlockSpec((MERGE_ROWS, LRU_WIDTH), lambda i: (i, 0)),
            pl.BlockSpec((2, N_PAIRS, MERGE_T, 1, BATCH, PAIR), lambda i: (0, 0) + cm4(i)),
            pl.BlockSpec((MERGE_T, 1, BATCH, W), cm4),
            pl.BlockSpec((MERGE_T, 1, BATCH, W), cm4),
            pl.BlockSpec((MERGE_ROWS, D_MODEL), lambda i: (i, 0)),
            _const_spec((1, D_MODEL)), _const_spec((3, BATCH, D_MODEL)),
            _const_spec((1, W)), _const_spec((1, W)), _const_spec(ones.shape),
            _const_spec(wgate.shape), _const_spec(wol.shape), _const_spec(wor.shape),
            _const_spec(wout.shape),
        ],
        out_specs=pl.BlockSpec((MERGE_ROWS, D_MODEL), lambda i: (i, 0)),
        out_shape=jax.ShapeDtypeStruct((SEQ * BATCH, D_MODEL), F32),
        compiler_params=_params(1),
        name="merge",
    )(lru_l, y6, bon4, g4, x_tb, g_mix, mod_m, ln_g, ln_b, ones, wgate, wol, wor, wout)


FFN_T = 64
FFN_ROWS = FFN_T * BATCH
FFN_TILE = 256


def _ffn_up_kernel(x_ref, g_ref, sc_ref, sh_ref, w_ref, o_ref):
    h = _rms_modulate(x_ref[...], g_ref[...], sc_ref[...], sh_ref[...]).astype(BF16)
    for j in range(D_FF // FFN_TILE):
        cs = slice(j * FFN_TILE, (j + 1) * FFN_TILE)
        gate = jnp.dot(h, w_ref[:, cs], preferred_element_type=F32)
        up = jnp.dot(h, w_ref[:, D_FF + j * FFN_TILE:D_FF + (j + 1) * FFN_TILE],
                     preferred_element_type=F32)
        o_ref[:, cs] = (_silu(gate) * up).astype(BF16)


def _ffn_up(x1, g, sc8, sh8, w_in):
    return pl.pallas_call(
        _ffn_up_kernel,
        grid=(SEQ * BATCH // FFN_ROWS,),
        in_specs=[pl.BlockSpec((FFN_ROWS, D_MODEL), lambda i: (i, 0)),
                  _const_spec((1, D_MODEL)), _const_spec((BATCH, D_MODEL)),
                  _const_spec((BATCH, D_MODEL)), _const_spec(w_in.shape)],
        out_specs=pl.BlockSpec((FFN_ROWS, D_FF), lambda i: (i, 0)),
        out_shape=jax.ShapeDtypeStruct((SEQ * BATCH, D_FF), BF16),
        compiler_params=_params(1),
        name="ffn_up",
    )(x1, g, sc8, sh8, w_in)


def _ffn_down_kernel(act_ref, x_ref, gf_ref, w_ref, gfin_ref, o_ref, stage_s):
    y = jnp.dot(act_ref[...], w_ref[...], preferred_element_type=F32)
    x2 = x_ref[...] + _bcast_rows(gf_ref[...], FFN_ROWS) * y
    ms = jnp.mean(x2 * x2, axis=-1, keepdims=True)
    out = x2 * lax.rsqrt(ms + RMS_EPS) * gfin_ref[...]
    for s in range(D_TILES):
        stage_s[s] = out[:, s * LANE:(s + 1) * LANE]
    for bi in range(BATCH):
        rows = pl.ds(bi, FFN_T, stride=BATCH)
        for s in range(D_TILES):
            o_ref[bi, :, s * LANE:(s + 1) * LANE] = stage_s[s, rows, :]


def _ffn_down(act, x1, g_f8, w_out, g_final):
    return pl.pallas_call(
        _ffn_down_kernel,
        grid=(SEQ * BATCH // FFN_ROWS,),
        in_specs=[pl.BlockSpec((FFN_ROWS, D_FF), lambda i: (i, 0)),
                  pl.BlockSpec((FFN_ROWS, D_MODEL), lambda i: (i, 0)),
                  _const_spec((BATCH, D_MODEL)), _const_spec(w_out.shape),
                  _const_spec((1, D_MODEL))],
        out_specs=pl.BlockSpec((BATCH, FFN_T, D_MODEL), lambda i: (0, i, 0)),
        out_shape=jax.ShapeDtypeStruct((BATCH, SEQ, D_MODEL), F32),
        scratch_shapes=[pltpu.VMEM((D_TILES, FFN_ROWS, LANE), F32)],
        compiler_params=_params(1),
        name="ffn_down",
    )(act, x1, g_f8, w_out, g_final)


def _block_diag(w):
    n, c, _ = w.shape
    tiled = jnp.tile(w.reshape(n * c, c), (1, n))
    rb = lax.broadcasted_iota(jnp.int32, (n * c, n * c), 0) // c
    cb = lax.broadcasted_iota(jnp.int32, (n * c, n * c), 1) // c
    return jnp.where(rb == cb, tiled, 0.0)


def _lru_gate_tiles(wa, wx):
    da, dx = _block_diag(0.5 * wa), _block_diag(0.5 * wx)
    tiles = []
    for j, k0 in enumerate(LRU_GATE_K0):
        cs = slice(j * LRU_GATE_TILE, (j + 1) * LRU_GATE_TILE)
        tiles.append(jnp.concatenate([da[k0:k0 + LRU_GATE_K, cs], dx[k0:k0 + LRU_GATE_K, cs]], axis=1))
    return jnp.stack(tiles).astype(BF16)


def _two_dir_lora(w):
    z = jnp.zeros_like(w[0])
    return jnp.concatenate([jnp.concatenate([w[0], z], axis=1),
                            jnp.concatenate([z, w[1]], axis=1)], axis=0).astype(BF16)


def kernel(x, c, ctx, c_ctx, norm_mix_g, norm_ffn_g, w_mod, b_mod, w_in, lru_conv_w, lru_conv_b, lru_wa, lru_ba, lru_wx, lru_bx, lru_lambda, w_o_lru, rwkv_mu, rwkv_w0, rwkv_w2, rwkv_a0, rwkv_a2, rwkv_g2, rwkv_k_k, rwkv_k_a, rwkv_r_k, rwkv_ln_g, rwkv_ln_b, w_o_rwkv, w_out, w_ffn_in, w_ffn_out, norm_final_g):
    assert x.shape == (BATCH, SEQ, D_MODEL) and ctx.shape == (BATCH, CTX_LEN, D_MODEL)
    assert w_mod.shape[0] == 1, "single layer only"
    D, W = D_MODEL, RWKV_WIDTH

    c16 = jnp.concatenate([c, c_ctx[None], jnp.zeros((16 - BATCH - 1, D), F32)], axis=0)
    mod = _adaln(c16, w_mod[0], b_mod[0][None])
    mod_lat = mod[:BATCH].reshape(BATCH, 6, D)
    mod_ctx = jnp.broadcast_to(mod[BATCH:BATCH + 1], (BATCH, 6 * D)).reshape(BATCH, 6, D)
    sh_m, sc_m, g_m, sh_f, sc_f, g_f = [mod_lat[:, k] for k in range(6)]
    sh2 = jnp.stack([mod_ctx[:, 0], sh_m])
    sc2 = jnp.stack([mod_ctx[:, 1], sc_m])

    w_in0 = w_in[0]
    n_lru = 2 * LRU_WIDTH
    w_lru = w_in0[:, :n_lru].astype(BF16)
    w_rw = jnp.pad(w_in0[:, n_lru:n_lru + RWKV_IN], ((0, 0), (0, RWKV_IN_PAD - RWKV_IN))).astype(BF16)
    w_gate = w_in0[:, n_lru + RWKV_IN:].astype(BF16)
    g_mix = norm_mix_g[0][None]

    xc_tb = ctx.transpose(1, 0, 2).reshape(CTX_LEN * BATCH, D)
    xl_tb = x.transpose(1, 0, 2).reshape(SEQ * BATCH, D)
    u_all = _proj(xc_tb, xl_tb, g_mix, sc2, sh2, w_lru)

    wt = jnp.stack([_lru_gate_tiles(lru_wa[0, d], lru_wx[0, d]) for d in range(2)])
    lru_p = (lru_conv_w[0], lru_conv_b[0], wt, 0.5 * lru_ba[0], 0.5 * lru_bx[0], lru_lambda[0])
    hf = _lru_scan(u_all, None, lru_p, 0, reverse=False)
    lru_l = _lru_scan(u_all, hf, lru_p, 1, reverse=True)

    mu_pad = jnp.pad(rwkv_mu[0], ((0, 0), (0, RWKV_IN_PAD - RWKV_IN)))
    ones = _block_diag(jnp.ones((PAIR // RWKV_HEAD, RWKV_HEAD, RWKV_HEAD), F32)).astype(BF16)
    rw_p = (mu_pad, _two_dir_lora(rwkv_w2[0]), rwkv_w0[0].reshape(1, 2 * W),
            _two_dir_lora(rwkv_a2[0]), rwkv_a0[0].reshape(1, 2 * W),
            jnp.pad(rwkv_g2[0], ((0, LORA_G_PAD - LORA_G), (0, 0))).astype(BF16),
            rwkv_k_k[0][None], rwkv_k_a[0][None], rwkv_r_k[0].reshape(1, W), ones)
    r, v, kn, ld, kd, b, bon, g = _rwkv_prep(xc_tb, xl_tb, g_mix, sc2, sh2, w_rw, rw_p)
    y = _wkv(r, v, kn, ld, kd, b)

    x1 = _merge(lru_l, y, bon, g, xl_tb, g_mix, jnp.stack([sh_m, sc_m, g_m]),
                rwkv_ln_g[0][None], rwkv_ln_b[0][None], ones, w_gate,
                w_o_lru[0].astype(BF16), w_o_rwkv[0].astype(BF16), w_out[0].astype(BF16))

    act = _ffn_up(x1, norm_ffn_g[0][None], sc_f, sh_f, w_ffn_in[0].astype(BF16))
    return _ffn_down(act, x1, g_f, w_ffn_out[0].astype(BF16), norm_final_g[None])
```

```python
import functools
import math

import jax
import jax.numpy as jnp
from jax import lax
from jax.experimental import pallas as pl
from jax.experimental.pallas import tpu as pltpu

F32 = jnp.float32
BF16 = jnp.bfloat16

D_MODEL = 1024
BATCH = 8
SEQ = 2048
CTX_LEN = 256
GRID_W = 64
GRID_ROWS = SEQ // GRID_W

LRU_WIDTH = 1280
LRU_BLOCKS = 16
LRU_BLOCK = LRU_WIDTH // LRU_BLOCKS
LRU_CONV = 4
LRU_C = 8.0

RWKV_HEAD = 64
RWKV_WIDTH = 1024
LORA_W = 64
LORA_A = 64
LORA_G = 160
RWKV_IN = 3 * RWKV_WIDTH + 2 * LORA_W + 2 * LORA_A + LORA_G
RWKV_IN_PAD = 3584
LORA_G_PAD = RWKV_IN_PAD - (3 * RWKV_WIDTH + 2 * LORA_W + 2 * LORA_A)
D_FF = 2816

RMS_EPS = 1e-6
GN_EPS = 64e-5
L2_EPS = 1e-12

T_ALL = CTX_LEN + SEQ
CHUNK_T = 64
CHUNK_ROWS = CHUNK_T * BATCH
N_CTX_CHUNKS = CTX_LEN // CHUNK_T
N_LAT_CHUNKS = SEQ // CHUNK_T
N_CHUNKS = N_CTX_CHUNKS + N_LAT_CHUNKS

LANE = 128
D_TILES = D_MODEL // LANE
PAIR = 2 * RWKV_HEAD
N_PAIRS = RWKV_WIDTH // PAIR

LRU_GATE_TILE = 256
LRU_GATE_K = 512
LRU_GATE_K0 = (0, 128, 384, 640, 768)

VMEM_LIMIT = 56 * 1024 * 1024


def _params(n_axes):
    return pltpu.CompilerParams(dimension_semantics=("arbitrary",) * n_axes,
                                vmem_limit_bytes=VMEM_LIMIT)


def _const_spec(shape):
    nd = len(shape)
    return pl.BlockSpec(shape, lambda *_: (0,) * nd)


def _bdot(a, b):
    return jnp.dot(a.astype(BF16), b.astype(BF16), preferred_element_type=F32)


def _bdot_nt(a, b):
    return lax.dot_general(a.astype(BF16), b.astype(BF16), (((1,), (1,)), ((), ())),
                           preferred_element_type=F32)


def _softplus(x):
    return jnp.maximum(x, 0.0) + jnp.log1p(jnp.exp(-jnp.abs(x)))


def _sigmoid(x):
    return 0.5 * jnp.tanh(0.5 * x) + 0.5


def _head_sums(t, ones_pair):
    tb = t.astype(BF16)
    return jnp.concatenate(
        [jnp.dot(tb[:, s * PAIR:(s + 1) * PAIR], ones_pair, preferred_element_type=F32)
         for s in range(t.shape[1] // PAIR)], axis=1)


def _silu(x):
    return x * _sigmoid(x)


def _gelu_tanh(x):
    c = math.sqrt(2.0 / math.pi)
    return 0.5 * x * (1.0 + jnp.tanh(c * (x + 0.044715 * (x * x * x))))


def _bcast_rows(v8, rows):
    c = v8.shape[-1]
    return jnp.broadcast_to(v8[None], (rows // BATCH, BATCH, c)).reshape(rows, c)


def _rms_modulate(x, g, scale8, shift8):
    rows = x.shape[0]
    ms = jnp.mean(x * x, axis=-1, keepdims=True)
    y = x * lax.rsqrt(ms + RMS_EPS) * g
    return y * (1.0 + _bcast_rows(scale8, rows)) + _bcast_rows(shift8, rows)


def _adaln_kernel(c_ref, w_ref, b_ref, o_ref):
    s = _silu(c_ref[...])
    o_ref[...] = jnp.dot(s, w_ref[...], preferred_element_type=F32,
                         precision=lax.Precision.HIGHEST) + b_ref[...]


def _adaln(c16, w_mod, b_mod):
    n = w_mod.shape[1]
    tn = 1536
    return pl.pallas_call(
        _adaln_kernel,
        grid=(n // tn,),
        in_specs=[_const_spec((16, D_MODEL)),
                  pl.BlockSpec((D_MODEL, tn), lambda j: (0, j)),
                  pl.BlockSpec((1, tn), lambda j: (0, j))],
        out_specs=pl.BlockSpec((16, tn), lambda j: (0, j)),
        out_shape=jax.ShapeDtypeStruct((16, n), F32),
        compiler_params=_params(1),
        name="adaln",
    )(c16, w_mod, b_mod)


LANE = 128
D_TILES = D_MODEL // LANE


def _proj_kernel(ctx_ref, x_ref, g_ref, sc_ref, sh_ref, w_ref, o_ref, xc_o, xl_o, stage_s):
    is_ctx = pl.program_id(0) < N_CTX_CHUNKS

    def stage(src_ref):
        for bi in range(BATCH):
            rows = pl.ds(bi, CHUNK_T, stride=BATCH)
            for s in range(D_TILES):
                stage_s[s, rows, :] = src_ref[bi, :, s * LANE:(s + 1) * LANE]

    @pl.when(is_ctx)
    def _():
        stage(ctx_ref)

    @pl.when(jnp.logical_not(is_ctx))
    def _():
        stage(x_ref)

    x = jnp.concatenate([stage_s[s] for s in range(D_TILES)], axis=1)

    @pl.when(is_ctx)
    def _():
        xc_o[...] = x

    xl_o[...] = x
    h = _rms_modulate(x, g_ref[...], sc_ref[0], sh_ref[0])
    o_ref[...] = jnp.dot(h.astype(BF16), w_ref[...], preferred_element_type=F32)


def _proj(ctx, x, g, sc2, sh2, w):
    n = w.shape[1]
    seg = lambda i: (i >= N_CTX_CHUNKS).astype(jnp.int32)
    cchunk = lambda i: jnp.minimum(i, N_CTX_CHUNKS - 1)
    lchunk = lambda i: jnp.maximum(i - N_CTX_CHUNKS, 0)
    return pl.pallas_call(
        _proj_kernel,
        grid=(N_CHUNKS,),
        in_specs=[pl.BlockSpec((BATCH, CHUNK_T, D_MODEL), lambda i: (0, cchunk(i), 0)),
                  pl.BlockSpec((BATCH, CHUNK_T, D_MODEL), lambda i: (0, lchunk(i), 0)),
                  _const_spec((1, D_MODEL)),
                  pl.BlockSpec((1, BATCH, D_MODEL), lambda i: (seg(i), 0, 0)),
                  pl.BlockSpec((1, BATCH, D_MODEL), lambda i: (seg(i), 0, 0)),
                  _const_spec((D_MODEL, n))],
        out_specs=[pl.BlockSpec((CHUNK_ROWS, n), lambda i: (i, 0)),
                   pl.BlockSpec((CHUNK_ROWS, D_MODEL), lambda i: (cchunk(i), 0)),
                   pl.BlockSpec((CHUNK_ROWS, D_MODEL), lambda i: (lchunk(i), 0))],
        out_shape=[jax.ShapeDtypeStruct((N_CHUNKS * CHUNK_ROWS, n), F32),
                   jax.ShapeDtypeStruct((CTX_LEN * BATCH, D_MODEL), F32),
                   jax.ShapeDtypeStruct((SEQ * BATCH, D_MODEL), F32)],
        scratch_shapes=[pltpu.VMEM((D_TILES, CHUNK_ROWS, LANE), F32)],
        compiler_params=_params(1),
        name="in_proj",
    )(ctx, x, g, sc2, sh2, w)


def _lru_kernel(*refs, reverse):
    if reverse:
        (u_ref, uy_ref, hf_ref, cw_ref, cb_ref, wt_ref, ba_ref, bx_ref, lam_ref,
         o_ref, h_s, halo_s, ext_s, a_s, b_s) = refs
    else:
        (u_ref, cw_ref, cb_ref, wt_ref, ba_ref, bx_ref, lam_ref,
         o_ref, h_s, halo_s, ext_s, a_s, b_s) = refs
    i = pl.program_id(0)
    halo_rows = (LRU_CONV - 1) * BATCH

    @pl.when(i == 0)
    def _():
        h_s[...] = jnp.zeros_like(h_s)

    @pl.when((i == 0) | (i == N_CTX_CHUNKS))
    def _():
        halo_s[...] = jnp.zeros_like(halo_s)

    u = u_ref[...]
    if reverse:
        ext_s[0:CHUNK_ROWS] = u
        ext_s[CHUNK_ROWS:CHUNK_ROWS + halo_rows] = halo_s[...]
        halo_s[...] = u[0:halo_rows]
    else:
        ext_s[0:halo_rows] = halo_s[...]
        ext_s[halo_rows:halo_rows + CHUNK_ROWS] = u
        halo_s[...] = u[CHUNK_ROWS - halo_rows:CHUNK_ROWS]

    xc = jnp.broadcast_to(cb_ref[...], (CHUNK_ROWS, LRU_WIDTH))
    for j in range(LRU_CONV):
        off = (LRU_CONV - 1 - j) if reverse else j
        xc = xc + cw_ref[j:j + 1, :] * ext_s[off * BATCH:off * BATCH + CHUNK_ROWS]

    xcb = xc.astype(BF16)
    c_half = (-0.5 * LRU_C) * _softplus(-lam_ref[...])
    for j in range(LRU_WIDTH // LRU_GATE_TILE):
        k0 = LRU_GATE_K0[j]
        cs = slice(j * LRU_GATE_TILE, (j + 1) * LRU_GATE_TILE)
        g = jnp.dot(xcb[:, k0:k0 + LRU_GATE_K], wt_ref[j], preferred_element_type=F32)
        tanh_r = jnp.tanh(g[:, :LRU_GATE_TILE] + ba_ref[:, cs])
        gate_i = 0.5 * jnp.tanh(g[:, LRU_GATE_TILE:] + bx_ref[:, cs]) + 0.5
        log_a = c_half[:, cs] * tanh_r + c_half[:, cs]
        a = jnp.exp(log_a)
        a_s[:, cs] = a
        om = 1.0 - a * a
        b_s[:, cs] = jnp.where(om > 0.0, om * lax.rsqrt(om), 0.0) * (gate_i * xc[:, cs])

    def step(k, h):
        t = (CHUNK_T - 1 - k) if reverse else k
        rows = pl.ds(pl.multiple_of(t * BATCH, BATCH), BATCH)
        h = a_s[rows, :] * h + b_s[rows, :]
        if reverse:
            o_ref[rows, :] = (hf_ref[rows, :] + h) * _gelu_tanh(uy_ref[rows, :])
        else:
            o_ref[rows, :] = h
        return h

    h_s[...] = lax.fori_loop(0, CHUNK_T, step, h_s[...], unroll=8)


def _lru_scan(u_all, hf, lru_p, d, *, reverse):
    cw, cb, wt, ba, bx, lam = lru_p
    if reverse:
        chunk = lambda i: jnp.where(i < N_CTX_CHUNKS, N_CTX_CHUNKS - 1 - i,
                                    N_CHUNKS + N_CTX_CHUNKS - 1 - i)
    else:
        chunk = lambda i: i
    lat = lambda i: jnp.maximum(chunk(jnp.maximum(i, N_CTX_CHUNKS)) - N_CTX_CHUNKS, 0)
    blk = (CHUNK_ROWS, LRU_WIDTH)
    in_specs = [pl.BlockSpec(blk, lambda i: (chunk(i), 0))]
    args = [u_all]
    if reverse:
        in_specs += [pl.BlockSpec(blk, lambda i: (lat(i) + N_CTX_CHUNKS, 1)),
                     pl.BlockSpec(blk, lambda i: (lat(i), 0))]
        args += [u_all, hf]
    in_specs += [_const_spec((LRU_CONV, LRU_WIDTH)), _const_spec((1, LRU_WIDTH)),
                 _const_spec(wt[d].shape), _const_spec((1, LRU_WIDTH)),
                 _const_spec((1, LRU_WIDTH)), _const_spec((1, LRU_WIDTH))]
    args += [cw[d], cb[d][None], wt[d], ba[d][None], bx[d][None], lam[d][None]]
    halo_rows = (LRU_CONV - 1) * BATCH
    return pl.pallas_call(
        functools.partial(_lru_kernel, reverse=reverse),
        grid=(N_CHUNKS,),
        in_specs=in_specs,
        out_specs=pl.BlockSpec(blk, lambda i: (lat(i), 0)),
        out_shape=jax.ShapeDtypeStruct((SEQ * BATCH, LRU_WIDTH), F32),
        scratch_shapes=[pltpu.VMEM((BATCH, LRU_WIDTH), F32),
                        pltpu.VMEM((halo_rows, LRU_WIDTH), F32),
                        pltpu.VMEM((CHUNK_ROWS + halo_rows, LRU_WIDTH), F32),
                        pltpu.VMEM(blk, F32),
                        pltpu.VMEM(blk, F32)],
        compiler_params=_params(1),
        name="lru_bwd" if reverse else "lru_fwd",
    )(*args)


PREP_T = 32
PREP_ROWS = PREP_T * BATCH
PREP_CTX_BLOCKS = CTX_LEN // PREP_T
PREP_LAT_BLOCKS = SEQ // PREP_T
PREP_BLOCKS = PREP_CTX_BLOCKS + PREP_LAT_BLOCKS


def _prep_kernel(xc_ref, xl_ref, xcp_ref, xcn_ref, xlp_ref, xln_ref, gmix_ref, sc_ref, sh_ref,
                 win_ref, mu_ref,
                 w2_ref, w0_ref, a2_ref, a0_ref, g2_ref, kk_ref, ka_ref, rk_ref, ones_ref,
                 r_o, v_o, kn_o, ld_o, kd_o, b_o, bon_o, g_o, ext_s):
    i = pl.program_id(0)
    is_ctx = i < PREP_CTX_BLOCKS
    first = (i == 0) | (i == PREP_CTX_BLOCKS)
    last = (i == PREP_CTX_BLOCKS - 1) | (i == PREP_BLOCKS - 1)
    lat_rows = lambda ref: ref[...].reshape(-1, D_MODEL)
    x = jnp.concatenate([
        jnp.where(is_ctx, xcp_ref[...], lat_rows(xlp_ref)),
        jnp.where(is_ctx, xc_ref[...], lat_rows(xl_ref)),
        jnp.where(is_ctx, xcn_ref[...], lat_rows(xln_ref))], axis=0)
    seg = jnp.where(is_ctx, 0, 1)
    h = _rms_modulate(x, gmix_ref[...], sc_ref[seg], sh_ref[seg])
    hb = h.astype(BF16)
    W = RWKV_WIDTH
    for c0, c1 in ((3 * W, RWKV_IN_PAD), (W, 2 * W), (0, W), (2 * W, 3 * W)):
        zt = jnp.dot(hb, win_ref[:, c0:c1], preferred_element_type=F32)
        ext_s[0:BATCH, c0:c1] = jnp.where(first, 0.0, zt[0:BATCH])
        ext_s[BATCH:BATCH + PREP_ROWS, c0:c1] = zt[BATCH:BATCH + PREP_ROWS]
        ext_s[BATCH + PREP_ROWS:2 * BATCH + PREP_ROWS, c0:c1] = jnp.where(
            last, 0.0, zt[BATCH + PREP_ROWS:2 * BATCH + PREP_ROWS])

    def shifted(c0, c1):
        zc = ext_s[BATCH:BATCH + PREP_ROWS, c0:c1]
        zprev = ext_s[0:PREP_ROWS, c0:c1]
        znext = ext_s[2 * BATCH:2 * BATCH + PREP_ROWS, c0:c1]
        return zc + mu_ref[0:1, c0:c1] * (zprev - zc) + mu_ref[1:2, c0:c1] * (znext - zc)

    wd = shifted(3 * W, 3 * W + 2 * LORA_W)
    ad = shifted(3 * W + 2 * LORA_W, 3 * W + 2 * LORA_W + 2 * LORA_A)
    gd = shifted(3 * W + 2 * LORA_W + 2 * LORA_A, RWKV_IN_PAD)
    k = shifted(W, 2 * W)
    r = shifted(0, W)
    v = shifted(2 * W, 3 * W)

    w_pre = _bdot(jnp.tanh(wd), w2_ref[...]) + w0_ref[...]
    a_pre = _bdot(ad, a2_ref[...]) + a0_ref[...]
    kk = k * kk_ref[...]
    ss = _head_sums(kk * kk, ones_ref[...])
    kn = kk / jnp.maximum(jnp.sqrt(ss), L2_EPS)

    def store_pairs(o, val):
        for s in range(N_PAIRS):
            o[s] = val[:, s * PAIR:(s + 1) * PAIR]

    store_pairs(r_o, r)
    store_pairs(v_o, v)
    store_pairs(kn_o, kn)
    kd_sum = jnp.zeros_like(k)
    for d in range(2):
        cs = slice(d * W, (d + 1) * W)
        store_pairs(ld_o.at[d], -math.exp(-0.5) * _sigmoid(w_pre[:, cs]))
        asig = _sigmoid(a_pre[:, cs])
        kd = k * (1.0 + (asig - 1.0) * ka_ref[...])
        store_pairs(kd_o.at[d], kd)
        store_pairs(b_o.at[d], kn * asig)
        kd_sum = kd_sum + kd
    bon_o[...] = _head_sums(r * kd_sum * rk_ref[...], ones_ref[...]) * v
    g_o[...] = _bdot(_sigmoid(gd), g2_ref[...])


def _rwkv_prep(xc_tb, xl_tb, g_mix, sc2, sh2, w_rw, rw_p):
    mu, w2, w0, a2, a0, g2, k_k, k_a, r_k, ones = rw_p
    W = RWKV_WIDTH
    hb = PREP_T
    cmain = lambda i: jnp.minimum(i, PREP_CTX_BLOCKS - 1)
    lmain = lambda i: jnp.maximum(i - PREP_CTX_BLOCKS, 0)
    assert PREP_T == GRID_ROWS
    xl4 = xl_tb.reshape(GRID_ROWS, GRID_W, BATCH, D_MODEL)
    hblk = (BATCH, D_MODEL)
    n_c8 = CTX_LEN - 1
    in_specs = [
        pl.BlockSpec((PREP_ROWS, D_MODEL), lambda i: (cmain(i), 0)),
        pl.BlockSpec((GRID_ROWS, 1, BATCH, D_MODEL), lambda i: (0, lmain(i), 0, 0)),
        pl.BlockSpec(hblk, lambda i: (jnp.maximum(cmain(i) * hb - 1, 0), 0)),
        pl.BlockSpec(hblk, lambda i: (jnp.minimum((cmain(i) + 1) * hb, n_c8), 0)),
        pl.BlockSpec((1, 1, BATCH, D_MODEL), lambda i: (GRID_ROWS - 1, jnp.maximum(lmain(i) - 1, 0), 0, 0)),
        pl.BlockSpec((1, 1, BATCH, D_MODEL), lambda i: (0, jnp.minimum(lmain(i) + 1, GRID_W - 1), 0, 0)),
        _const_spec((1, D_MODEL)), _const_spec(sc2.shape), _const_spec(sh2.shape),
        _const_spec(w_rw.shape),
        _const_spec(mu.shape), _const_spec(w2.shape), _const_spec(w0.shape),
        _const_spec(a2.shape), _const_spec(a0.shape), _const_spec(g2.shape),
        _const_spec(k_k.shape), _const_spec(k_a.shape), _const_spec(r_k.shape),
        _const_spec(ones.shape),
    ]
    n_rows = T_ALL * BATCH
    n_lat = SEQ * BATCH
    shared = pl.BlockSpec((N_PAIRS, PREP_ROWS, PAIR), lambda i: (0, i, 0))
    perdir = pl.BlockSpec((2, N_PAIRS, PREP_ROWS, PAIR), lambda i: (0, 0, i, 0))
    latonly = pl.BlockSpec((PREP_ROWS, W), lambda i: (lmain(i), 0))
    return pl.pallas_call(
        _prep_kernel,
        grid=(PREP_BLOCKS,),
        in_specs=in_specs,
        out_specs=[shared, shared, shared, perdir, perdir, perdir, latonly, latonly],
        out_shape=[jax.ShapeDtypeStruct((N_PAIRS, n_rows, PAIR), F32)] * 3
        + [jax.ShapeDtypeStruct((2, N_PAIRS, n_rows, PAIR), F32)] * 3
        + [jax.ShapeDtypeStruct((n_lat, W), F32)] * 2,
        scratch_shapes=[pltpu.VMEM((PREP_ROWS + 2 * BATCH, RWKV_IN_PAD), F32)],
        compiler_params=_params(1),
        name="rwkv_prep",
    )(xc_tb, xl4, xc_tb, xc_tb, xl4, xl4, g_mix, sc2, sh2, w_rw,
      mu, w2, w0, a2, a0, g2, k_k, k_a, r_k, ones)


def _wkv_chunks(inputs, states, decays, sign):
    T = CHUNK_T
    R2 = 2 * T
    nb = len(inputs)

    lane = lax.broadcasted_iota(jnp.int32, (R2, PAIR), 1)
    row = lax.broadcasted_iota(jnp.int32, (R2, PAIR), 0)
    head_mask = (lane // RWKV_HEAD) == (row // T)

    def stack_masked(x):
        return jnp.where(head_mask, jnp.concatenate([x, x], axis=0), 0.0).astype(BF16)

    tw = lax.broadcasted_iota(jnp.int32, (T, 2 * R2), 0)
    sw = lax.broadcasted_iota(jnp.int32, (T, 2 * R2), 1) % T
    dtw = (tw - sw) * sign
    strict_w = dtw[:, 0:R2] > 0
    incl_w2 = dtw >= 0
    eye_w = jnp.where(dtw[:, 0:R2] == 0, 1.0, 0.0)

    nt = lambda x, y: lax.dot_general(x, y, (((1,), (1,)), ((), ())), preferred_element_type=F32)
    mm = lambda x, y: jnp.dot(x, y, preferred_element_type=F32)

    p_w, a_rbk, v_st, x_rhs, r_s, uv_rhs = [], [], [], [], [], []
    for (a_t, r_t, b_t, k_t, b_e, k_e, v), S in zip(inputs, states):
        ar = jnp.concatenate([a_t, r_t], axis=0).astype(BF16)
        bk = jnp.concatenate([stack_masked(b_t), stack_masked(k_t)], axis=0)
        big = nt(ar, bk)
        p_w.append(jnp.where(strict_w, big[0:T, 0:R2], 0.0))
        a_ak = jnp.where(strict_w, big[0:T, R2:2 * R2], 0.0).astype(BF16)
        a_rbk.append(jnp.where(incl_w2, big[T:R2], 0.0).astype(BF16))
        v_st.append(stack_masked(v))
        fs = nt(ar, S.astype(BF16))
        x_rhs.append(fs[0:T] + mm(a_ak, v_st[-1]))
        r_s.append(fs[T:R2])
        uv_rhs.append(jnp.concatenate([b_e, k_e], axis=0).astype(BF16))

    inv_w = [eye_w + p for p in p_w]
    p_bd = [stack_masked(p) for p in p_w]
    p_w = [mm(p_w[i].astype(BF16), p_bd[i]) for i in range(nb)]
    levels = int(math.log2(T))
    for lvl in range(1, levels):
        last = lvl == levels - 1
        p_bd = [stack_masked(p) for p in p_w]
        nxt_p, nxt_inv = [], []
        for i in range(nb):
            if last:
                nxt_inv.append(inv_w[i] + mm(inv_w[i].astype(BF16), p_bd[i]))
            else:
                both = mm(jnp.concatenate([p_w[i], inv_w[i]], axis=0).astype(BF16), p_bd[i])
                nxt_p.append(both[0:T])
                nxt_inv.append(inv_w[i] + both[T:R2])
        p_w, inv_w = nxt_p, nxt_inv

    u = [mm(inv_w[i].astype(BF16), stack_masked(x_rhs[i])) for i in range(nb)]
    ys = [r_s[i] + mm(a_rbk[i], jnp.concatenate([stack_masked(u[i]), v_st[i]], axis=0))
          for i in range(nb)]
    uv_t = [jnp.concatenate([u[i], inputs[i][6]], axis=0).T.astype(BF16) for i in range(nb)]
    s_new = [states[i] * decays[i] + jnp.where(head_mask, mm(uv_t[i], uv_rhs[i]), 0.0)
             for i in range(nb)]
    return ys, s_new


WKV_PAIRS_PER_STEP = 2


def _wkv_kernel(r_ref, v_ref, kn_ref, ld_ref, kd_ref, b_ref, y_ref, s_ref, cum_s):
    d = pl.program_id(0)
    c = pl.program_id(2)
    sign = 1 - 2 * d

    @pl.when(c == 0)
    def _():
        s_ref[...] = jnp.zeros_like(s_ref)

    tot8, dec8 = [], []
    for pi in range(WKV_PAIRS_PER_STEP):
        ld = ld_ref[0, pi]
        cum_up = ld
        for lvl in range(int(math.log2(CHUNK_T))):
            sh = BATCH << lvl
            cum_up = cum_up + jnp.concatenate([jnp.zeros((sh, PAIR), F32), cum_up[:-sh]], axis=0)
        tot8.append(cum_up[CHUNK_ROWS - BATCH:])
        tot = _bcast_rows(tot8[pi], CHUNK_ROWS)
        cum_s[pi] = jnp.where(d == 0, cum_up, tot - cum_up + ld)
        dec8.append(jnp.exp(tot8[pi]))

    problems = [(pl.ds(bi, CHUNK_T, stride=BATCH), pi, bi)
                for pi in range(WKV_PAIRS_PER_STEP) for bi in range(BATCH)]
    inputs, states, decays = [], [], []
    for rw, pi, bi in problems:
        cum = cum_s[pi, rw, :]
        ld_b = ld_ref[0, pi, rw, :]
        b = b_ref[0, pi, rw, :]
        kd = kd_ref[0, pi, rw, :]
        e_out = jnp.exp(-cum)
        e_end = jnp.exp(tot8[pi][bi:bi + 1] - cum)
        inputs.append((-kn_ref[pi, rw, :] * jnp.exp(cum - ld_b), r_ref[pi, rw, :] * jnp.exp(cum),
                       b * e_out, kd * e_out, b * e_end, kd * e_end, v_ref[pi, rw, :]))
        states.append(s_ref[pi, bi])
        decays.append(dec8[pi][bi:bi + 1])
    ys, s_new = _wkv_chunks(inputs, states, decays, sign)
    for k, (rw, pi, bi) in enumerate(problems):
        y_ref[0, pi, rw, :] = ys[k]
        s_ref[pi, bi] = s_new[k]


def _wkv(r, v, kn, ld, kd, b):
    def chunk(d, c):
        fwd = c
        bwd = jnp.where(c < N_CTX_CHUNKS, N_CTX_CHUNKS - 1 - c, N_CHUNKS + N_CTX_CHUNKS - 1 - c)
        return jnp.where(d == 0, fwd, bwd)
    lat = lambda d, c: chunk(d, jnp.maximum(c, N_CTX_CHUNKS)) - N_CTX_CHUNKS
    pps = WKV_PAIRS_PER_STEP
    shared = pl.BlockSpec((pps, CHUNK_ROWS, PAIR), lambda d, p, c: (p, chunk(d, c), 0))
    perdir = pl.BlockSpec((1, pps, CHUNK_ROWS, PAIR), lambda d, p, c: (d, p, chunk(d, c), 0))
    return pl.pallas_call(
        _wkv_kernel,
        grid=(2, N_PAIRS // pps, N_CHUNKS),
        in_specs=[shared, shared, shared, perdir, perdir, perdir],
        out_specs=pl.BlockSpec((1, pps, CHUNK_ROWS, PAIR), lambda d, p, c: (d, p, lat(d, c), 0)),
        out_shape=jax.ShapeDtypeStruct((2, N_PAIRS, SEQ * BATCH, PAIR), F32),
        scratch_shapes=[pltpu.VMEM((pps, BATCH, PAIR, PAIR), F32),
                        pltpu.VMEM((pps, CHUNK_ROWS, PAIR), F32)],
        compiler_params=_params(3),
        name="wkv_scan",
    )(r, v, kn, ld, kd, b)


MERGE_T = 32
MERGE_ROWS = MERGE_T * BATCH


def _merge_kernel(lru_ref, y_ref, bon_ref, g_ref, x_ref, gmix_ref, mod_ref,
                  lng_ref, lnb_ref, ones_ref, wgate_ref, wol_ref, wor_ref, wout_ref, o_ref):
    rows = MERGE_ROWS
    W = RWKV_WIDTH
    x = x_ref[...]
    h = _rms_modulate(x, gmix_ref[...], mod_ref[1], mod_ref[0])
    gates = _bdot(h, wgate_ref[...])
    y2 = (y_ref[0] + y_ref[1]).reshape(N_PAIRS, rows, PAIR)
    y = jnp.concatenate([y2[s] for s in range(N_PAIRS)], axis=1)
    inv_n = 1.0 / RWKV_HEAD

    ones = ones_ref[...]
    y_hi = y.astype(BF16).astype(F32)
    mu = (_head_sums(y_hi, ones) + _head_sums(y - y_hi, ones)) * inv_n
    dy = y - mu
    var = _head_sums(dy * dy, ones) * inv_n
    yn = dy * lax.rsqrt(var + GN_EPS) * lng_ref[...] + lnb_ref[...]
    rw = (yn + bon_ref[...].reshape(rows, W)) * g_ref[...].reshape(rows, W)
    m = (_sigmoid(gates[:, :D_MODEL]) * _bdot(lru_ref[...], wol_ref[...])
         + _sigmoid(gates[:, D_MODEL:]) * _bdot(rw, wor_ref[...]))
    mix = _bdot(m, wout_ref[...])
    o_ref[...] = x + _bcast_rows(mod_ref[2], rows) * mix


def _merge(lru_l, y, bon, g, x_tb, g_mix, mod_m, ln_g, ln_b, ones, wgate, wol, wor, wout):
    W = RWKV_WIDTH
    n_blocks = SEQ // MERGE_T
    per_row = GRID_W // MERGE_T
    cm4 = lambda i: (i % per_row, i // per_row, 0, 0)
    y6 = y.reshape(2, N_PAIRS, GRID_W, GRID_ROWS, BATCH, PAIR)
    bon4 = bon.reshape(GRID_W, GRID_ROWS, BATCH, W)
    g4 = g.reshape(GRID_W, GRID_ROWS, BATCH, W)
    return pl.pallas_call(
        _merge_kernel,
        grid=(n_blocks,),
        in_specs=[
            pl.BlockSpec((MERGE_ROWS, LRU_WIDTH), lambda i: (i, 0)),
            pl.BlockSpec((2, N_PAIRS, MERGE_T, 1, BATCH, PAIR), lambda i: (0, 0) + cm4(i)),
            pl.BlockSpec((MERGE_T, 1, BATCH, W), cm4),
            pl.BlockSpec((MERGE_T, 1, BATCH, W), cm4),
            pl.BlockSpec((MERGE_ROWS, D_MODEL), lambda i: (i, 0)),
            _const_spec((1, D_MODEL)), _const_spec((3, BATCH, D_MODEL)),
            _const_spec((1, W)), _const_spec((1, W)), _const_spec(ones.shape),
            _const_spec(wgate.shape), _const_spec(wol.shape), _const_spec(wor.shape),
            _const_spec(wout.shape),
        ],
        out_specs=pl.BlockSpec((MERGE_ROWS, D_MODEL), lambda i: (i, 0)),
        out_shape=jax.ShapeDtypeStruct((SEQ * BATCH, D_MODEL), F32),
        compiler_params=_params(1),
        name="merge",
    )(lru_l, y6, bon4, g4, x_tb, g_mix, mod_m, ln_g, ln_b, ones, wgate, wol, wor, wout)


FFN_T = 64
FFN_ROWS = FFN_T * BATCH
FFN_TILE = 256


def _ffn_up_kernel(x_ref, g_ref, sc_ref, sh_ref, w_ref, o_ref):
    h = _rms_modulate(x_ref[...], g_ref[...], sc_ref[...], sh_ref[...]).astype(BF16)
    for j in range(D_FF // FFN_TILE):
        cs = slice(j * FFN_TILE, (j + 1) * FFN_TILE)
        gate = jnp.dot(h, w_ref[:, cs], preferred_element_type=F32)
        up = jnp.dot(h, w_ref[:, D_FF + j * FFN_TILE:D_FF + (j + 1) * FFN_TILE],
                     preferred_element_type=F32)
        o_ref[:, cs] = (_silu(gate) * up).astype(BF16)


def _ffn_up(x1, g, sc8, sh8, w_in):
    return pl.pallas_call(
        _ffn_up_kernel,
        grid=(SEQ * BATCH // FFN_ROWS,),
        in_specs=[pl.BlockSpec((FFN_ROWS, D_MODEL), lambda i: (i, 0)),
                  _const_spec((1, D_MODEL)), _const_spec((BATCH, D_MODEL)),
                  _const_spec((BATCH, D_MODEL)), _const_spec(w_in.shape)],
        out_specs=pl.BlockSpec((FFN_ROWS, D_FF), lambda i: (i, 0)),
        out_shape=jax.ShapeDtypeStruct((SEQ * BATCH, D_FF), BF16),
        compiler_params=_params(1),
        name="ffn_up",
    )(x1, g, sc8, sh8, w_in)


def _ffn_down_kernel(act_ref, x_ref, gf_ref, w_ref, gfin_ref, o_ref, stage_s):
    y = jnp.dot(act_ref[...], w_ref[...], preferred_element_type=F32)
    x2 = x_ref[...] + _bcast_rows(gf_ref[...], FFN_ROWS) * y
    ms = jnp.mean(x2 * x2, axis=-1, keepdims=True)
    out = x2 * lax.rsqrt(ms + RMS_EPS) * gfin_ref[...]
    for s in range(D_TILES):
        stage_s[s] = out[:, s * LANE:(s + 1) * LANE]
    for bi in range(BATCH):
        rows = pl.ds(bi, FFN_T, stride=BATCH)
        for s in range(D_TILES):
            o_ref[bi, :, s * LANE:(s + 1) * LANE] = stage_s[s, rows, :]


def _ffn_down(act, x1, g_f8, w_out, g_final):
    return pl.pallas_call(
        _ffn_down_kernel,
        grid=(SEQ * BATCH // FFN_ROWS,),
        in_specs=[pl.BlockSpec((FFN_ROWS, D_FF), lambda i: (i, 0)),
                  pl.BlockSpec((FFN_ROWS, D_MODEL), lambda i: (i, 0)),
                  _const_spec((BATCH, D_MODEL)), _const_spec(w_out.shape),
                  _const_spec((1, D_MODEL))],
        out_specs=pl.BlockSpec((BATCH, FFN_T, D_MODEL), lambda i: (0, i, 0)),
        out_shape=jax.ShapeDtypeStruct((BATCH, SEQ, D_MODEL), F32),
        scratch_shapes=[pltpu.VMEM((D_TILES, FFN_ROWS, LANE), F32)],
        compiler_params=_params(1),
        name="ffn_down",
    )(act, x1, g_f8, w_out, g_final)


def _block_diag(w):
    n, c, _ = w.shape
    tiled = jnp.tile(w.reshape(n * c, c), (1, n))
    rb = lax.broadcasted_iota(jnp.int32, (n * c, n * c), 0) // c
    cb = lax.broadcasted_iota(jnp.int32, (n * c, n * c), 1) // c
    return jnp.where(rb == cb, tiled, 0.0)


def _lru_gate_tiles(wa, wx):
    da, dx = _block_diag(0.5 * wa), _block_diag(0.5 * wx)
    tiles = []
    for j, k0 in enumerate(LRU_GATE_K0):
        cs = slice(j * LRU_GATE_TILE, (j + 1) * LRU_GATE_TILE)
        tiles.append(jnp.concatenate([da[k0:k0 + LRU_GATE_K, cs], dx[k0:k0 + LRU_GATE_K, cs]], axis=1))
    return jnp.stack(tiles).astype(BF16)


def _two_dir_lora(w):
    z = jnp.zeros_like(w[0])
    return jnp.concatenate([jnp.concatenate([w[0], z], axis=1),
                            jnp.concatenate([z, w[1]], axis=1)], axis=0).astype(BF16)


def kernel(x, c, ctx, c_ctx, norm_mix_g, norm_ffn_g, w_mod, b_mod, w_in, lru_conv_w, lru_conv_b, lru_wa, lru_ba, lru_wx, lru_bx, lru_lambda, w_o_lru, rwkv_mu, rwkv_w0, rwkv_w2, rwkv_a0, rwkv_a2, rwkv_g2, rwkv_k_k, rwkv_k_a, rwkv_r_k, rwkv_ln_g, rwkv_ln_b, w_o_rwkv, w_out, w_ffn_in, w_ffn_out, norm_final_g):
    assert x.shape == (BATCH, SEQ, D_MODEL) and ctx.shape == (BATCH, CTX_LEN, D_MODEL)
    assert w_mod.shape[0] == 1, "single layer only"
    D, W = D_MODEL, RWKV_WIDTH

    c16 = jnp.concatenate([c, c_ctx[None], jnp.zeros((16 - BATCH - 1, D), F32)], axis=0)
    mod = _adaln(c16, w_mod[0], b_mod[0][None])
    mod_lat = mod[:BATCH].reshape(BATCH, 6, D)
    mod_ctx = jnp.broadcast_to(mod[BATCH:BATCH + 1], (BATCH, 6 * D)).reshape(BATCH, 6, D)
    sh_m, sc_m, g_m, sh_f, sc_f, g_f = [mod_lat[:, k] for k in range(6)]
    sh2 = jnp.stack([mod_ctx[:, 0], sh_m])
    sc2 = jnp.stack([mod_ctx[:, 1], sc_m])

    w_in0 = w_in[0]
    n_lru = 2 * LRU_WIDTH
    w_lru = w_in0[:, :n_lru].astype(BF16)
    w_rw = jnp.pad(w_in0[:, n_lru:n_lru + RWKV_IN], ((0, 0), (0, RWKV_IN_PAD - RWKV_IN))).astype(BF16)
    w_gate = w_in0[:, n_lru + RWKV_IN:].astype(BF16)
    g_mix = norm_mix_g[0][None]

    u_all, xc_tb, xl_tb = _proj(ctx, x, g_mix, sc2, sh2, w_lru)

    wt = jnp.stack([_lru_gate_tiles(lru_wa[0, d], lru_wx[0, d]) for d in range(2)])
    lru_p = (lru_conv_w[0], lru_conv_b[0], wt, 0.5 * lru_ba[0], 0.5 * lru_bx[0], lru_lambda[0])
    hf = _lru_scan(u_all, None, lru_p, 0, reverse=False)
    lru_l = _lru_scan(u_all, hf, lru_p, 1, reverse=True)

    mu_pad = jnp.pad(rwkv_mu[0], ((0, 0), (0, RWKV_IN_PAD - RWKV_IN)))
    ones = _block_diag(jnp.ones((PAIR // RWKV_HEAD, RWKV_HEAD, RWKV_HEAD), F32)).astype(BF16)
    rw_p = (mu_pad, _two_dir_lora(rwkv_w2[0]), rwkv_w0[0].reshape(1, 2 * W),
            _two_dir_lora(rwkv_a2[0]), rwkv_a0[0].reshape(1, 2 * W),
            jnp.pad(rwkv_g2[0], ((0, LORA_G_PAD - LORA_G), (0, 0))).astype(BF16),
            rwkv_k_k[0][None], rwkv_k_a[0][None], rwkv_r_k[0].reshape(1, W), ones)
    r, v, kn, ld, kd, b, bon, g = _rwkv_prep(xc_tb, xl_tb, g_mix, sc2, sh2, w_rw, rw_p)
    y = _wkv(r, v, kn, ld, kd, b)

    x1 = _merge(lru_l, y, bon, g, xl_tb, g_mix, jnp.stack([sh_m, sc_m, g_m]),
                rwkv_ln_g[0][None], rwkv_ln_b[0][None], ones, w_gate,
                w_o_lru[0].astype(BF16), w_o_rwkv[0].astype(BF16), w_out[0].astype(BF16))

    act = _ffn_up(x1, norm_ffn_g[0][None], sc_f, sh_f, w_ffn_in[0].astype(BF16))
    return _ffn_down(act, x1, g_f, w_ffn_out[0].astype(BF16), norm_final_g[None])
```

```python
import functools
import math

import jax
import jax.numpy as jnp
from jax import lax
from jax.experimental import pallas as pl
from jax.experimental.pallas import tpu as pltpu

F32 = jnp.float32
BF16 = jnp.bfloat16

D_MODEL = 1024
BATCH = 8
SEQ = 2048
CTX_LEN = 256
GRID_W = 64
GRID_ROWS = SEQ // GRID_W

LRU_WIDTH = 1280
LRU_BLOCKS = 16
LRU_BLOCK = LRU_WIDTH // LRU_BLOCKS
LRU_CONV = 4
LRU_C = 8.0

RWKV_HEAD = 64
RWKV_WIDTH = 1024
LORA_W = 64
LORA_A = 64
LORA_G = 160
RWKV_IN = 3 * RWKV_WIDTH + 2 * LORA_W + 2 * LORA_A + LORA_G
RWKV_IN_PAD = 3584
LORA_G_PAD = RWKV_IN_PAD - (3 * RWKV_WIDTH + 2 * LORA_W + 2 * LORA_A)
D_FF = 2816

RMS_EPS = 1e-6
GN_EPS = 64e-5
L2_EPS = 1e-12

T_ALL = CTX_LEN + SEQ
CHUNK_T = 64
CHUNK_ROWS = CHUNK_T * BATCH
N_CTX_CHUNKS = CTX_LEN // CHUNK_T
N_LAT_CHUNKS = SEQ // CHUNK_T
N_CHUNKS = N_CTX_CHUNKS + N_LAT_CHUNKS

LANE = 128
D_TILES = D_MODEL // LANE
PAIR = 2 * RWKV_HEAD
N_PAIRS = RWKV_WIDTH // PAIR

LRU_GATE_TILE = 256
LRU_GATE_K = 512
LRU_GATE_K0 = (0, 128, 384, 640, 768)

VMEM_LIMIT = 56 * 1024 * 1024


def _params(n_axes):
    return pltpu.CompilerParams(dimension_semantics=("arbitrary",) * n_axes,
                                vmem_limit_bytes=VMEM_LIMIT)


def _const_spec(shape):
    nd = len(shape)
    return pl.BlockSpec(shape, lambda *_: (0,) * nd)


def _bdot(a, b):
    return jnp.dot(a.astype(BF16), b.astype(BF16), preferred_element_type=F32)


def _bdot_nt(a, b):
    return lax.dot_general(a.astype(BF16), b.astype(BF16), (((1,), (1,)), ((), ())),
                           preferred_element_type=F32)


def _softplus(x):
    return jnp.maximum(x, 0.0) + jnp.log1p(jnp.exp(-jnp.abs(x)))


def _sigmoid(x):
    return 0.5 * jnp.tanh(0.5 * x) + 0.5


def _head_sums(t, ones_pair):
    tb = t.astype(BF16)
    return jnp.concatenate(
        [jnp.dot(tb[:, s * PAIR:(s + 1) * PAIR], ones_pair, preferred_element_type=F32)
         for s in range(t.shape[1] // PAIR)], axis=1)


def _silu(x):
    return x * _sigmoid(x)


def _gelu_tanh(x):
    c = math.sqrt(2.0 / math.pi)
    return 0.5 * x * (1.0 + jnp.tanh(c * (x + 0.044715 * (x * x * x))))


def _bcast_rows(v8, rows):
    c = v8.shape[-1]
    return jnp.broadcast_to(v8[None], (rows // BATCH, BATCH, c)).reshape(rows, c)


def _rms_modulate(x, g, scale8, shift8):
    rows = x.shape[0]
    ms = jnp.mean(x * x, axis=-1, keepdims=True)
    y = x * lax.rsqrt(ms + RMS_EPS) * g
    return y * (1.0 + _bcast_rows(scale8, rows)) + _bcast_rows(shift8, rows)


def _adaln_kernel(c_ref, w_ref, b_ref, o_ref):
    s = _silu(c_ref[...])
    o_ref[...] = jnp.dot(s, w_ref[...], preferred_element_type=F32,
                         precision=lax.Precision.HIGHEST) + b_ref[...]


def _adaln(c16, w_mod, b_mod):
    n = w_mod.shape[1]
    tn = 1536
    return pl.pallas_call(
        _adaln_kernel,
        grid=(n // tn,),
        in_specs=[_const_spec((16, D_MODEL)),
                  pl.BlockSpec((D_MODEL, tn), lambda j: (0, j)),
                  pl.BlockSpec((1, tn), lambda j: (0, j))],
        out_specs=pl.BlockSpec((16, tn), lambda j: (0, j)),
        out_shape=jax.ShapeDtypeStruct((16, n), F32),
        compiler_params=_params(1),
        name="adaln",
    )(c16, w_mod, b_mod)


LANE = 128
D_TILES = D_MODEL // LANE


def _proj_kernel(ctx_ref, x_ref, g_ref, sc_ref, sh_ref, w_ref, o_ref, xc_o, xl_o, stage_s):
    is_ctx = pl.program_id(0) < N_CTX_CHUNKS

    def stage(src_ref):
        for bi in range(BATCH):
            rows = pl.ds(bi, CHUNK_T, stride=BATCH)
            for s in range(D_TILES):
                stage_s[s, rows, :] = src_ref[bi, :, s * LANE:(s + 1) * LANE]

    @pl.when(is_ctx)
    def _():
        stage(ctx_ref)

    @pl.when(jnp.logical_not(is_ctx))
    def _():
        stage(x_ref)

    x = jnp.concatenate([stage_s[s] for s in range(D_TILES)], axis=1)

    @pl.when(is_ctx)
    def _():
        xc_o[...] = x

    xl_o[...] = x
    h = _rms_modulate(x, g_ref[...], sc_ref[0], sh_ref[0])
    o_ref[...] = jnp.dot(h.astype(BF16), w_ref[...], preferred_element_type=F32)


def _proj(ctx, x, g, sc2, sh2, w):
    n = w.shape[1]
    seg = lambda i: (i >= N_CTX_CHUNKS).astype(jnp.int32)
    cchunk = lambda i: jnp.minimum(i, N_CTX_CHUNKS - 1)
    lchunk = lambda i: jnp.maximum(i - N_CTX_CHUNKS, 0)
    return pl.pallas_call(
        _proj_kernel,
        grid=(N_CHUNKS,),
        in_specs=[pl.BlockSpec((BATCH, CHUNK_T, D_MODEL), lambda i: (0, cchunk(i), 0)),
                  pl.BlockSpec((BATCH, CHUNK_T, D_MODEL), lambda i: (0, lchunk(i), 0)),
                  _const_spec((1, D_MODEL)),
                  pl.BlockSpec((1, BATCH, D_MODEL), lambda i: (seg(i), 0, 0)),
                  pl.BlockSpec((1, BATCH, D_MODEL), lambda i: (seg(i), 0, 0)),
                  _const_spec((D_MODEL, n))],
        out_specs=[pl.BlockSpec((CHUNK_ROWS, n), lambda i: (i, 0)),
                   pl.BlockSpec((CHUNK_ROWS, D_MODEL), lambda i: (cchunk(i), 0)),
                   pl.BlockSpec((CHUNK_ROWS, D_MODEL), lambda i: (lchunk(i), 0))],
        out_shape=[jax.ShapeDtypeStruct((N_CHUNKS * CHUNK_ROWS, n), F32),
                   jax.ShapeDtypeStruct((CTX_LEN * BATCH, D_MODEL), F32),
                   jax.ShapeDtypeStruct((SEQ * BATCH, D_MODEL), F32)],
        scratch_shapes=[pltpu.VMEM((D_TILES, CHUNK_ROWS, LANE), F32)],
        compiler_params=_params(1),
        name="in_proj",
    )(ctx, x, g, sc2, sh2, w)


def _lru_kernel(*refs, reverse):
    if reverse:
        (u_ref, uy_ref, hf_ref, cw_ref, cb_ref, wt_ref, ba_ref, bx_ref, lam_ref,
         o_ref, h_s, halo_s, ext_s, a_s, b_s) = refs
    else:
        (u_ref, cw_ref, cb_ref, wt_ref, ba_ref, bx_ref, lam_ref,
         o_ref, h_s, halo_s, ext_s, a_s, b_s) = refs
    i = pl.program_id(0)
    halo_rows = (LRU_CONV - 1) * BATCH

    @pl.when(i == 0)
    def _():
        h_s[...] = jnp.zeros_like(h_s)

    @pl.when((i == 0) | (i == N_CTX_CHUNKS))
    def _():
        halo_s[...] = jnp.zeros_like(halo_s)

    u = u_ref[...]
    if reverse:
        ext_s[0:CHUNK_ROWS] = u
        ext_s[CHUNK_ROWS:CHUNK_ROWS + halo_rows] = halo_s[...]
        halo_s[...] = u[0:halo_rows]
    else:
        ext_s[0:halo_rows] = halo_s[...]
        ext_s[halo_rows:halo_rows + CHUNK_ROWS] = u
        halo_s[...] = u[CHUNK_ROWS - halo_rows:CHUNK_ROWS]

    xc = jnp.broadcast_to(cb_ref[...], (CHUNK_ROWS, LRU_WIDTH))
    for j in range(LRU_CONV):
        off = (LRU_CONV - 1 - j) if reverse else j
        xc = xc + cw_ref[j:j + 1, :] * ext_s[off * BATCH:off * BATCH + CHUNK_ROWS]

    xcb = xc.astype(BF16)
    c_half = (-0.5 * LRU_C) * _softplus(-lam_ref[...])
    for j in range(LRU_WIDTH // LRU_GATE_TILE):
        k0 = LRU_GATE_K0[j]
        cs = slice(j * LRU_GATE_TILE, (j + 1) * LRU_GATE_TILE)
        g = jnp.dot(xcb[:, k0:k0 + LRU_GATE_K], wt_ref[j], preferred_element_type=F32)
        tanh_r = jnp.tanh(g[:, :LRU_GATE_TILE] + ba_ref[:, cs])
        gate_i = 0.5 * jnp.tanh(g[:, LRU_GATE_TILE:] + bx_ref[:, cs]) + 0.5
        log_a = c_half[:, cs] * tanh_r + c_half[:, cs]
        a = jnp.exp(log_a)
        a_s[:, cs] = a
        om = 1.0 - a * a
        b_s[:, cs] = jnp.where(om > 0.0, om * lax.rsqrt(om), 0.0) * (gate_i * xc[:, cs])

    def step(k, h):
        t = (CHUNK_T - 1 - k) if reverse else k
        rows = pl.ds(pl.multiple_of(t * BATCH, BATCH), BATCH)
        h = a_s[rows, :] * h + b_s[rows, :]
        if reverse:
            o_ref[rows, :] = (hf_ref[rows, :] + h) * _gelu_tanh(uy_ref[rows, :])
        else:
            o_ref[rows, :] = h
        return h

    h_s[...] = lax.fori_loop(0, CHUNK_T, step, h_s[...], unroll=8)


def _lru_scan(u_all, hf, lru_p, d, *, reverse):
    cw, cb, wt, ba, bx, lam = lru_p
    if reverse:
        chunk = lambda i: jnp.where(i < N_CTX_CHUNKS, N_CTX_CHUNKS - 1 - i,
                                    N_CHUNKS + N_CTX_CHUNKS - 1 - i)
    else:
        chunk = lambda i: i
    lat = lambda i: jnp.maximum(chunk(jnp.maximum(i, N_CTX_CHUNKS)) - N_CTX_CHUNKS, 0)
    blk = (CHUNK_ROWS, LRU_WIDTH)
    in_specs = [pl.BlockSpec(blk, lambda i: (chunk(i), 0))]
    args = [u_all]
    if reverse:
        in_specs += [pl.BlockSpec(blk, lambda i: (lat(i) + N_CTX_CHUNKS, 1)),
                     pl.BlockSpec(blk, lambda i: (lat(i), 0))]
        args += [u_all, hf]
    in_specs += [_const_spec((LRU_CONV, LRU_WIDTH)), _const_spec((1, LRU_WIDTH)),
                 _const_spec(wt[d].shape), _const_spec((1, LRU_WIDTH)),
                 _const_spec((1, LRU_WIDTH)), _const_spec((1, LRU_WIDTH))]
    args += [cw[d], cb[d][None], wt[d], ba[d][None], bx[d][None], lam[d][None]]
    halo_rows = (LRU_CONV - 1) * BATCH
    return pl.pallas_call(
        functools.partial(_lru_kernel, reverse=reverse),
        grid=(N_CHUNKS,),
        in_specs=in_specs,
        out_specs=pl.BlockSpec(blk, lambda i: (lat(i), 0)),
        out_shape=jax.ShapeDtypeStruct((SEQ * BATCH, LRU_WIDTH), F32),
        scratch_shapes=[pltpu.VMEM((BATCH, LRU_WIDTH), F32),
                        pltpu.VMEM((halo_rows, LRU_WIDTH), F32),
                        pltpu.VMEM((CHUNK_ROWS + halo_rows, LRU_WIDTH), F32),
                        pltpu.VMEM(blk, F32),
                        pltpu.VMEM(blk, F32)],
        compiler_params=_params(1),
        name="lru_bwd" if reverse else "lru_fwd",
    )(*args)


PREP_T = 32
PREP_ROWS = PREP_T * BATCH
PREP_CTX_BLOCKS = CTX_LEN // PREP_T
PREP_LAT_BLOCKS = SEQ // PREP_T
PREP_BLOCKS = PREP_CTX_BLOCKS + PREP_LAT_BLOCKS


def _prep_kernel(xc_ref, xl_ref, xcp_ref, xcn_ref, xlp_ref, xln_ref, gmix_ref, sc_ref, sh_ref,
                 win_ref, mu_ref,
                 w2_ref, w0_ref, a2_ref, a0_ref, g2_ref, kk_ref, ka_ref, rk_ref, ones_ref,
                 r_o, v_o, kn_o, ld_o, kd_o, b_o, bon_o, g_o, ext_s):
    i = pl.program_id(0)
    is_ctx = i < PREP_CTX_BLOCKS
    first = (i == 0) | (i == PREP_CTX_BLOCKS)
    last = (i == PREP_CTX_BLOCKS - 1) | (i == PREP_BLOCKS - 1)
    lat_rows = lambda ref: ref[...].reshape(-1, D_MODEL)
    x = jnp.concatenate([
        jnp.where(is_ctx, xcp_ref[...], lat_rows(xlp_ref)),
        jnp.where(is_ctx, xc_ref[...], lat_rows(xl_ref)),
        jnp.where(is_ctx, xcn_ref[...], lat_rows(xln_ref))], axis=0)
    seg = jnp.where(is_ctx, 0, 1)
    h = _rms_modulate(x, gmix_ref[...], sc_ref[seg], sh_ref[seg])
    hb = h.astype(BF16)
    W = RWKV_WIDTH
    for c0, c1 in ((3 * W, RWKV_IN_PAD), (W, 2 * W), (0, W), (2 * W, 3 * W)):
        zt = jnp.dot(hb, win_ref[:, c0:c1], preferred_element_type=F32)
        ext_s[0:BATCH, c0:c1] = jnp.where(first, 0.0, zt[0:BATCH])
        ext_s[BATCH:BATCH + PREP_ROWS, c0:c1] = zt[BATCH:BATCH + PREP_ROWS]
        ext_s[BATCH + PREP_ROWS:2 * BATCH + PREP_ROWS, c0:c1] = jnp.where(
            last, 0.0, zt[BATCH + PREP_ROWS:2 * BATCH + PREP_ROWS])

    def shifted(c0, c1):
        zc = ext_s[BATCH:BATCH + PREP_ROWS, c0:c1]
        zprev = ext_s[0:PREP_ROWS, c0:c1]
        znext = ext_s[2 * BATCH:2 * BATCH + PREP_ROWS, c0:c1]
        return zc + mu_ref[0:1, c0:c1] * (zprev - zc) + mu_ref[1:2, c0:c1] * (znext - zc)

    wd = shifted(3 * W, 3 * W + 2 * LORA_W)
    ad = shifted(3 * W + 2 * LORA_W, 3 * W + 2 * LORA_W + 2 * LORA_A)
    gd = shifted(3 * W + 2 * LORA_W + 2 * LORA_A, RWKV_IN_PAD)
    k = shifted(W, 2 * W)
    r = shifted(0, W)
    v = shifted(2 * W, 3 * W)

    w_pre = _bdot(jnp.tanh(wd), w2_ref[...]) + w0_ref[...]
    a_pre = _bdot(ad, a2_ref[...]) + a0_ref[...]
    kk = k * kk_ref[...]
    ss = _head_sums(kk * kk, ones_ref[...])
    kn = kk / jnp.maximum(jnp.sqrt(ss), L2_EPS)

    def store_pairs(o, val):
        for s in range(N_PAIRS):
            o[s] = val[:, s * PAIR:(s + 1) * PAIR]

    store_pairs(r_o, r)
    store_pairs(v_o, v)
    store_pairs(kn_o, kn)
    kd_sum = jnp.zeros_like(k)
    for d in range(2):
        cs = slice(d * W, (d + 1) * W)
        store_pairs(ld_o.at[d], -math.exp(-0.5) * _sigmoid(w_pre[:, cs]))
        asig = _sigmoid(a_pre[:, cs])
        kd = k * (1.0 + (asig - 1.0) * ka_ref[...])
        store_pairs(kd_o.at[d], kd)
        store_pairs(b_o.at[d], kn * asig)
        kd_sum = kd_sum + kd
    bon_o[...] = _head_sums(r * kd_sum * rk_ref[...], ones_ref[...]) * v
    g_o[...] = _bdot(_sigmoid(gd), g2_ref[...])


def _rwkv_prep(xc_tb, xl_tb, g_mix, sc2, sh2, w_rw, rw_p):
    mu, w2, w0, a2, a0, g2, k_k, k_a, r_k, ones = rw_p
    W = RWKV_WIDTH
    hb = PREP_T
    cmain = lambda i: jnp.minimum(i, PREP_CTX_BLOCKS - 1)
    lmain = lambda i: jnp.maximum(i - PREP_CTX_BLOCKS, 0)
    assert PREP_T == GRID_ROWS
    xl4 = xl_tb.reshape(GRID_ROWS, GRID_W, BATCH, D_MODEL)
    hblk = (BATCH, D_MODEL)
    n_c8 = CTX_LEN - 1
    in_specs = [
        pl.BlockSpec((PREP_ROWS, D_MODEL), lambda i: (cmain(i), 0)),
        pl.BlockSpec((GRID_ROWS, 1, BATCH, D_MODEL), lambda i: (0, lmain(i), 0, 0)),
        pl.BlockSpec(hblk, lambda i: (jnp.maximum(cmain(i) * hb - 1, 0), 0)),
        pl.BlockSpec(hblk, lambda i: (jnp.minimum((cmain(i) + 1) * hb, n_c8), 0)),
        pl.BlockSpec((1, 1, BATCH, D_MODEL), lambda i: (GRID_ROWS - 1, jnp.maximum(lmain(i) - 1, 0), 0, 0)),
        pl.BlockSpec((1, 1, BATCH, D_MODEL), lambda i: (0, jnp.minimum(lmain(i) + 1, GRID_W - 1), 0, 0)),
        _const_spec((1, D_MODEL)), _const_spec(sc2.shape), _const_spec(sh2.shape),
        _const_spec(w_rw.shape),
        _const_spec(mu.shape), _const_spec(w2.shape), _const_spec(w0.shape),
        _const_spec(a2.shape), _const_spec(a0.shape), _const_spec(g2.shape),
        _const_spec(k_k.shape), _const_spec(k_a.shape), _const_spec(r_k.shape),
        _const_spec(ones.shape),
    ]
    n_rows = T_ALL * BATCH
    n_lat = SEQ * BATCH
    shared = pl.BlockSpec((N_PAIRS, PREP_ROWS, PAIR), lambda i: (0, i, 0))
    perdir = pl.BlockSpec((2, N_PAIRS, PREP_ROWS, PAIR), lambda i: (0, 0, i, 0))
    latonly = pl.BlockSpec((PREP_ROWS, W), lambda i: (lmain(i), 0))
    return pl.pallas_call(
        _prep_kernel,
        grid=(PREP_BLOCKS,),
        in_specs=in_specs,
        out_specs=[shared, shared, shared, perdir, perdir, perdir, latonly, latonly],
        out_shape=[jax.ShapeDtypeStruct((N_PAIRS, n_rows, PAIR), F32)] * 3
        + [jax.ShapeDtypeStruct((2, N_PAIRS, n_rows, PAIR), F32)] * 3
        + [jax.ShapeDtypeStruct((n_lat, W), F32)] * 2,
        scratch_shapes=[pltpu.VMEM((PREP_ROWS + 2 * BATCH, RWKV_IN_PAD), F32)],
        compiler_params=_params(1),
        name="rwkv_prep",
    )(xc_tb, xl4, xc_tb, xc_tb, xl4, xl4, g_mix, sc2, sh2, w_rw,
      mu, w2, w0, a2, a0, g2, k_k, k_a, r_k, ones)


WKV_GROUP = 8
WKV_SKEW = 1


def _wkv_chunks(loaders, states, decays, sign):
    T = CHUNK_T
    R2 = 2 * T
    nb = len(loaders)
    levels = int(math.log2(T))

    lane = lax.broadcasted_iota(jnp.int32, (R2, PAIR), 1)
    row = lax.broadcasted_iota(jnp.int32, (R2, PAIR), 0)
    head_mask = (lane // RWKV_HEAD) == (row // T)

    def stack_masked(x):
        return jnp.where(head_mask, jnp.concatenate([x, x], axis=0), 0.0).astype(BF16)

    tw = lax.broadcasted_iota(jnp.int32, (T, 2 * R2), 0)
    sw = lax.broadcasted_iota(jnp.int32, (T, 2 * R2), 1) % T
    dtw = (tw - sw) * sign
    strict_w = dtw[:, 0:R2] > 0
    incl_w2 = dtw >= 0
    eye_w = jnp.where(dtw[:, 0:R2] == 0, 1.0, 0.0)

    nt = lambda x, y: lax.dot_general(x, y, (((1,), (1,)), ((), ())), preferred_element_type=F32)
    mm = lambda x, y: jnp.dot(x, y, preferred_element_type=F32)

    def chain(load, S, decay, out):
        a_t, r_t, b_t, k_t, b_e, k_e, v = load()
        ar = jnp.concatenate([a_t, r_t], axis=0).astype(BF16)
        bk = jnp.concatenate([stack_masked(b_t), stack_masked(k_t)], axis=0)
        big = nt(ar, bk)
        fs = nt(ar, S.astype(BF16))
        uv_rhs = jnp.concatenate([b_e, k_e], axis=0).astype(BF16)
        yield
        p_w = jnp.where(strict_w, big[0:T, 0:R2], 0.0)
        a_ak = jnp.where(strict_w, big[0:T, R2:2 * R2], 0.0).astype(BF16)
        a_rbk = jnp.where(incl_w2, big[T:R2], 0.0).astype(BF16)
        v_st = stack_masked(v)
        h = mm(a_ak, v_st)
        inv_w = eye_w + p_w
        p_w = mm(p_w.astype(BF16), stack_masked(p_w))
        yield
        for lvl in range(1, levels):
            p_bd = stack_masked(p_w)
            if lvl == levels - 1:
                inv_w = inv_w + mm(inv_w.astype(BF16), p_bd)
            else:
                both = mm(jnp.concatenate([p_w, inv_w], axis=0).astype(BF16), p_bd)
                yield
                p_w = both[0:T]
                inv_w = inv_w + both[T:R2]
        yield
        u = mm(inv_w.astype(BF16), stack_masked(fs[0:T] + h))
        yield
        y = fs[T:R2] + mm(a_rbk, jnp.concatenate([stack_masked(u), v_st], axis=0))
        uv_t = jnp.concatenate([u, v], axis=0).T.astype(BF16)
        out.append((y, S * decay + jnp.where(head_mask, mm(uv_t, uv_rhs), 0.0)))

    outs = [[] for _ in range(nb)]
    chains = [chain(loaders[i], states[i], decays[i], outs[i]) for i in range(nb)]
    groups = [chains[g:g + WKV_GROUP] for g in range(0, nb, WKV_GROUP)]
    live = [True] * len(groups)
    tick = 0
    while any(live):
        for g, group in enumerate(groups):
            if live[g] and tick >= g * WKV_SKEW:
                for ch in group:
                    if next(ch, "done") == "done":
                        live[g] = False
        tick += 1
    return [o[0][0] for o in outs], [o[0][1] for o in outs]


WKV_PAIRS_PER_STEP = 4


def _wkv_kernel(r_ref, v_ref, kn_ref, ld_ref, kd_ref, b_ref, y_ref, s_ref, cum_s):
    d = pl.program_id(0)
    c = pl.program_id(2)
    sign = 1 - 2 * d

    @pl.when(c == 0)
    def _():
        s_ref[...] = jnp.zeros_like(s_ref)

    tot8, dec8 = [], []
    for pi in range(WKV_PAIRS_PER_STEP):
        ld = ld_ref[0, pi]
        cum_up = ld
        for lvl in range(int(math.log2(CHUNK_T))):
            sh = BATCH << lvl
            cum_up = cum_up + jnp.concatenate([jnp.zeros((sh, PAIR), F32), cum_up[:-sh]], axis=0)
        tot8.append(cum_up[CHUNK_ROWS - BATCH:])
        tot = _bcast_rows(tot8[pi], CHUNK_ROWS)
        cum_s[pi] = jnp.where(d == 0, cum_up, tot - cum_up + ld)
        dec8.append(jnp.exp(tot8[pi]))

    problems = [(pl.ds(bi, CHUNK_T, stride=BATCH), pi, bi)
                for pi in range(WKV_PAIRS_PER_STEP) for bi in range(BATCH)]
    def loader(rw, pi, bi):
        def load():
            cum = cum_s[pi, rw, :]
            ld_b = ld_ref[0, pi, rw, :]
            b = b_ref[0, pi, rw, :]
            kd = kd_ref[0, pi, rw, :]
            e_out = jnp.exp(-cum)
            e_end = jnp.exp(tot8[pi][bi:bi + 1] - cum)
            return (-kn_ref[pi, rw, :] * jnp.exp(cum - ld_b), r_ref[pi, rw, :] * jnp.exp(cum),
                    b * e_out, kd * e_out, b * e_end, kd * e_end, v_ref[pi, rw, :])
        return load

    ys, s_new = _wkv_chunks([loader(*p) for p in problems],
                            [s_ref[pi, bi] for _, pi, bi in problems],
                            [dec8[pi][bi:bi + 1] for _, pi, bi in problems], sign)
    for k, (rw, pi, bi) in enumerate(problems):
        y_ref[0, pi, rw, :] = ys[k]
        s_ref[pi, bi] = s_new[k]


def _wkv(r, v, kn, ld, kd, b):
    def chunk(d, c):
        fwd = c
        bwd = jnp.where(c < N_CTX_CHUNKS, N_CTX_CHUNKS - 1 - c, N_CHUNKS + N_CTX_CHUNKS - 1 - c)
        return jnp.where(d == 0, fwd, bwd)
    lat = lambda d, c: chunk(d, jnp.maximum(c, N_CTX_CHUNKS)) - N_CTX_CHUNKS
    pps = WKV_PAIRS_PER_STEP
    shared = pl.BlockSpec((pps, CHUNK_ROWS, PAIR), lambda d, p, c: (p, chunk(d, c), 0))
    perdir = pl.BlockSpec((1, pps, CHUNK_ROWS, PAIR), lambda d, p, c: (d, p, chunk(d, c), 0))
    return pl.pallas_call(
        _wkv_kernel,
        grid=(2, N_PAIRS // pps, N_CHUNKS),
        in_specs=[shared, shared, shared, perdir, perdir, perdir],
        out_specs=pl.BlockSpec((1, pps, CHUNK_ROWS, PAIR), lambda d, p, c: (d, p, lat(d, c), 0)),
        out_shape=jax.ShapeDtypeStruct((2, N_PAIRS, SEQ * BATCH, PAIR), F32),
        scratch_shapes=[pltpu.VMEM((pps, BATCH, PAIR, PAIR), F32),
                        pltpu.VMEM((pps, CHUNK_ROWS, PAIR), F32)],
        compiler_params=_params(3),
        name="wkv_scan",
    )(r, v, kn, ld, kd, b)


MERGE_T = 32
MERGE_ROWS = MERGE_T * BATCH


def _merge_kernel(lru_ref, y_ref, bon_ref, g_ref, x_ref, gmix_ref, mod_ref,
                  lng_ref, lnb_ref, ones_ref, wgate_ref, wol_ref, wor_ref, wout_ref, o_ref):
    rows = MERGE_ROWS
    W = RWKV_WIDTH
    x = x_ref[...]
    h = _rms_modulate(x, gmix_ref[...], mod_ref[1], mod_ref[0])
    gates = _bdot(h, wgate_ref[...])
    y2 = (y_ref[0] + y_ref[1]).reshape(N_PAIRS, rows, PAIR)
    y = jnp.concatenate([y2[s] for s in range(N_PAIRS)], axis=1)
    inv_n = 1.0 / RWKV_HEAD

    ones = ones_ref[...]
    y_hi = y.astype(BF16).astype(F32)
    mu = (_head_sums(y_hi, ones) + _head_sums(y - y_hi, ones)) * inv_n
    dy = y - mu
    var = _head_sums(dy * dy, ones) * inv_n
    yn = dy * lax.rsqrt(var + GN_EPS) * lng_ref[...] + lnb_ref[...]
    rw = (yn + bon_ref[...].reshape(rows, W)) * g_ref[...].reshape(rows, W)
    m = (_sigmoid(gates[:, :D_MODEL]) * _bdot(lru_ref[...], wol_ref[...])
         + _sigmoid(gates[:, D_MODEL:]) * _bdot(rw, wor_ref[...]))
    mix = _bdot(m, wout_ref[...])
    o_ref[...] = x + _bcast_rows(mod_ref[2], rows) * mix


def _merge(lru_l, y, bon, g, x_tb, g_mix, mod_m, ln_g, ln_b, ones, wgate, wol, wor, wout):
    W = RWKV_WIDTH
    n_blocks = SEQ // MERGE_T
    per_row = GRID_W // MERGE_T
    cm4 = lambda i: (i % per_row, i // per_row, 0, 0)
    y6 = y.reshape(2, N_PAIRS, GRID_W, GRID_ROWS, BATCH, PAIR)
    bon4 = bon.reshape(GRID_W, GRID_ROWS, BATCH, W)
    g4 = g.reshape(GRID_W, GRID_ROWS, BATCH, W)
    return pl.pallas_call(
        _merge_kernel,
        grid=(n_blocks,),
        in_specs=[
            pl.BlockSpec((MERGE_ROWS, LRU_WIDTH), lambda i: (i, 0)),
            pl.BlockSpec((2, N_PAIRS, MERGE_T, 1, BATCH, PAIR), lambda i: (0, 0) + cm4(i)),
            pl.BlockSpec((MERGE_T, 1, BATCH, W), cm4),
            pl.BlockSpec((MERGE_T, 1, BATCH, W), cm4),
            pl.BlockSpec((MERGE_ROWS, D_MODEL), lambda i: (i, 0)),
            _const_spec((1, D_MODEL)), _const_spec((3, BATCH, D_MODEL)),
            _const_spec((1, W)), _const_spec((1, W)), _const_spec(ones.shape),
            _const_spec(wgate.shape), _const_spec(wol.shape), _const_spec(wor.shape),
            _const_spec(wout.shape),
        ],
        out_specs=pl.BlockSpec((MERGE_ROWS, D_MODEL), lambda i: (i, 0)),
        out_shape=jax.ShapeDtypeStruct((SEQ * BATCH, D_MODEL), F32),
        compiler_params=_params(1),
        name="merge",
    )(lru_l, y6, bon4, g4, x_tb, g_mix, mod_m, ln_g, ln_b, ones, wgate, wol, wor, wout)


FFN_T = 64
FFN_ROWS = FFN_T * BATCH
FFN_TILE = 256


def _ffn_up_kernel(x_ref, g_ref, sc_ref, sh_ref, w_ref, o_ref):
    h = _rms_modulate(x_ref[...], g_ref[...], sc_ref[...], sh_ref[...]).astype(BF16)
    for j in range(D_FF // FFN_TILE):
        cs = slice(j * FFN_TILE, (j + 1) * FFN_TILE)
        gate = jnp.dot(h, w_ref[:, cs], preferred_element_type=F32)
        up = jnp.dot(h, w_ref[:, D_FF + j * FFN_TILE:D_FF + (j + 1) * FFN_TILE],
                     preferred_element_type=F32)
        o_ref[:, cs] = (_silu(gate) * up).astype(BF16)


def _ffn_up(x1, g, sc8, sh8, w_in):
    return pl.pallas_call(
        _ffn_up_kernel,
        grid=(SEQ * BATCH // FFN_ROWS,),
        in_specs=[pl.BlockSpec((FFN_ROWS, D_MODEL), lambda i: (i, 0)),
                  _const_spec((1, D_MODEL)), _const_spec((BATCH, D_MODEL)),
                  _const_spec((BATCH, D_MODEL)), _const_spec(w_in.shape)],
        out_specs=pl.BlockSpec((FFN_ROWS, D_FF), lambda i: (i, 0)),
        out_shape=jax.ShapeDtypeStruct((SEQ * BATCH, D_FF), BF16),
        compiler_params=_params(1),
        name="ffn_up",
    )(x1, g, sc8, sh8, w_in)


def _ffn_down_kernel(act_ref, x_ref, gf_ref, w_ref, gfin_ref, o_ref, stage_s):
    y = jnp.dot(act_ref[...], w_ref[...], preferred_element_type=F32)
    x2 = x_ref[...] + _bcast_rows(gf_ref[...], FFN_ROWS) * y
    ms = jnp.mean(x2 * x2, axis=-1, keepdims=True)
    out = x2 * lax.rsqrt(ms + RMS_EPS) * gfin_ref[...]
    for s in range(D_TILES):
        stage_s[s] = out[:, s * LANE:(s + 1) * LANE]
    for bi in range(BATCH):
        rows = pl.ds(bi, FFN_T, stride=BATCH)
        for s in range(D_TILES):
            o_ref[bi, :, s * LANE:(s + 1) * LANE] = stage_s[s, rows, :]


def _ffn_down(act, x1, g_f8, w_out, g_final):
    return pl.pallas_call(
        _ffn_down_kernel,
        grid=(SEQ * BATCH // FFN_ROWS,),
        in_specs=[pl.BlockSpec((FFN_ROWS, D_FF), lambda i: (i, 0)),
                  pl.BlockSpec((FFN_ROWS, D_MODEL), lambda i: (i, 0)),
                  _const_spec((BATCH, D_MODEL)), _const_spec(w_out.shape),
                  _const_spec((1, D_MODEL))],
        out_specs=pl.BlockSpec((BATCH, FFN_T, D_MODEL), lambda i: (0, i, 0)),
        out_shape=jax.ShapeDtypeStruct((BATCH, SEQ, D_MODEL), F32),
        scratch_shapes=[pltpu.VMEM((D_TILES, FFN_ROWS, LANE), F32)],
        compiler_params=_params(1),
        name="ffn_down",
    )(act, x1, g_f8, w_out, g_final)


def _block_diag(w):
    n, c, _ = w.shape
    tiled = jnp.tile(w.reshape(n * c, c), (1, n))
    rb = lax.broadcasted_iota(jnp.int32, (n * c, n * c), 0) // c
    cb = lax.broadcasted_iota(jnp.int32, (n * c, n * c), 1) // c
    return jnp.where(rb == cb, tiled, 0.0)


def _lru_gate_tiles(wa, wx):
    da, dx = _block_diag(0.5 * wa), _block_diag(0.5 * wx)
    tiles = []
    for j, k0 in enumerate(LRU_GATE_K0):
        cs = slice(j * LRU_GATE_TILE, (j + 1) * LRU_GATE_TILE)
        tiles.append(jnp.concatenate([da[k0:k0 + LRU_GATE_K, cs], dx[k0:k0 + LRU_GATE_K, cs]], axis=1))
    return jnp.stack(tiles).astype(BF16)


def _two_dir_lora(w):
    z = jnp.zeros_like(w[0])
    return jnp.concatenate([jnp.concatenate([w[0], z], axis=1),
                            jnp.concatenate([z, w[1]], axis=1)], axis=0).astype(BF16)


def kernel(x, c, ctx, c_ctx, norm_mix_g, norm_ffn_g, w_mod, b_mod, w_in, lru_conv_w, lru_conv_b, lru_wa, lru_ba, lru_wx, lru_bx, lru_lambda, w_o_lru, rwkv_mu, rwkv_w0, rwkv_w2, rwkv_a0, rwkv_a2, rwkv_g2, rwkv_k_k, rwkv_k_a, rwkv_r_k, rwkv_ln_g, rwkv_ln_b, w_o_rwkv, w_out, w_ffn_in, w_ffn_out, norm_final_g):
    assert x.shape == (BATCH, SEQ, D_MODEL) and ctx.shape == (BATCH, CTX_LEN, D_MODEL)
    assert w_mod.shape[0] == 1, "single layer only"
    D, W = D_MODEL, RWKV_WIDTH

    c16 = jnp.concatenate([c, c_ctx[None], jnp.zeros((16 - BATCH - 1, D), F32)], axis=0)
    mod = _adaln(c16, w_mod[0], b_mod[0][None])
    mod_lat = mod[:BATCH].reshape(BATCH, 6, D)
    mod_ctx = jnp.broadcast_to(mod[BATCH:BATCH + 1], (BATCH, 6 * D)).reshape(BATCH, 6, D)
    sh_m, sc_m, g_m, sh_f, sc_f, g_f = [mod_lat[:, k] for k in range(6)]
    sh2 = jnp.stack([mod_ctx[:, 0], sh_m])
    sc2 = jnp.stack([mod_ctx[:, 1], sc_m])

    w_in0 = w_in[0]
    n_lru = 2 * LRU_WIDTH
    w_lru = w_in0[:, :n_lru].astype(BF16)
    w_rw = jnp.pad(w_in0[:, n_lru:n_lru + RWKV_IN], ((0, 0), (0, RWKV_IN_PAD - RWKV_IN))).astype(BF16)
    w_gate = w_in0[:, n_lru + RWKV_IN:].astype(BF16)
    g_mix = norm_mix_g[0][None]

    u_all, xc_tb, xl_tb = _proj(ctx, x, g_mix, sc2, sh2, w_lru)

    wt = jnp.stack([_lru_gate_tiles(lru_wa[0, d], lru_wx[0, d]) for d in range(2)])
    lru_p = (lru_conv_w[0], lru_conv_b[0], wt, 0.5 * lru_ba[0], 0.5 * lru_bx[0], lru_lambda[0])
    hf = _lru_scan(u_all, None, lru_p, 0, reverse=False)
    lru_l = _lru_scan(u_all, hf, lru_p, 1, reverse=True)

    mu_pad = jnp.pad(rwkv_mu[0], ((0, 0), (0, RWKV_IN_PAD - RWKV_IN)))
    ones = _block_diag(jnp.ones((PAIR // RWKV_HEAD, RWKV_HEAD, RWKV_HEAD), F32)).astype(BF16)
    rw_p = (mu_pad, _two_dir_lora(rwkv_w2[0]), rwkv_w0[0].reshape(1, 2 * W),
            _two_dir_lora(rwkv_a2[0]), rwkv_a0[0].reshape(1, 2 * W),
            jnp.pad(rwkv_g2[0], ((0, LORA_G_PAD - LORA_G), (0, 0))).astype(BF16),
            rwkv_k_k[0][None], rwkv_k_a[0][None], rwkv_r_k[0].reshape(1, W), ones)
    r, v, kn, ld, kd, b, bon, g = _rwkv_prep(xc_tb, xl_tb, g_mix, sc2, sh2, w_rw, rw_p)
    y = _wkv(r, v, kn, ld, kd, b)

    x1 = _merge(lru_l, y, bon, g, xl_tb, g_mix, jnp.stack([sh_m, sc_m, g_m]),
                rwkv_ln_g[0][None], rwkv_ln_b[0][None], ones, w_gate,
                w_o_lru[0].astype(BF16), w_o_rwkv[0].astype(BF16), w_out[0].astype(BF16))

    act = _ffn_up(x1, norm_ffn_g[0][None], sc_f, sh_f, w_ffn_in[0].astype(BF16))
    return _ffn_down(act, x1, g_f, w_ffn_out[0].astype(BF16), norm_final_g[None])
```

```python
import functools
import math

import jax
import jax.numpy as jnp
from jax import lax
from jax.experimental import pallas as pl
from jax.experimental.pallas import tpu as pltpu

F32 = jnp.float32
BF16 = jnp.bfloat16

D_MODEL = 1024
BATCH = 8
SEQ = 2048
CTX_LEN = 256
GRID_W = 64
GRID_ROWS = SEQ // GRID_W

LRU_WIDTH = 1280
LRU_BLOCKS = 16
LRU_BLOCK = LRU_WIDTH // LRU_BLOCKS
LRU_CONV = 4
LRU_C = 8.0

RWKV_HEAD = 64
RWKV_WIDTH = 1024
LORA_W = 64
LORA_A = 64
LORA_G = 160
RWKV_IN = 3 * RWKV_WIDTH + 2 * LORA_W + 2 * LORA_A + LORA_G
RWKV_IN_PAD = 3584
LORA_G_PAD = RWKV_IN_PAD - (3 * RWKV_WIDTH + 2 * LORA_W + 2 * LORA_A)
D_FF = 2816

RMS_EPS = 1e-6
GN_EPS = 64e-5
L2_EPS = 1e-12

T_ALL = CTX_LEN + SEQ
CHUNK_T = 64
CHUNK_ROWS = CHUNK_T * BATCH
N_CTX_CHUNKS = CTX_LEN // CHUNK_T
N_LAT_CHUNKS = SEQ // CHUNK_T
N_CHUNKS = N_CTX_CHUNKS + N_LAT_CHUNKS

LANE = 128
D_TILES = D_MODEL // LANE
PAIR = 2 * RWKV_HEAD
N_PAIRS = RWKV_WIDTH // PAIR

LRU_GATE_TILE = 256
LRU_GATE_K = 512
LRU_GATE_K0 = (0, 128, 384, 640, 768)

VMEM_LIMIT = 56 * 1024 * 1024


def _params(n_axes):
    return pltpu.CompilerParams(dimension_semantics=("arbitrary",) * n_axes,
                                vmem_limit_bytes=VMEM_LIMIT)


def _const_spec(shape):
    nd = len(shape)
    return pl.BlockSpec(shape, lambda *_: (0,) * nd)


def _bdot(a, b):
    return jnp.dot(a.astype(BF16), b.astype(BF16), preferred_element_type=F32)


def _bdot_nt(a, b):
    return lax.dot_general(a.astype(BF16), b.astype(BF16), (((1,), (1,)), ((), ())),
                           preferred_element_type=F32)


def _softplus(x):
    return jnp.maximum(x, 0.0) + jnp.log1p(jnp.exp(-jnp.abs(x)))


def _sigmoid(x):
    return 0.5 * jnp.tanh(0.5 * x) + 0.5


def _head_sums(t, ones_pair):
    tb = t.astype(BF16)
    return jnp.concatenate(
        [jnp.dot(tb[:, s * PAIR:(s + 1) * PAIR], ones_pair, preferred_element_type=F32)
         for s in range(t.shape[1] // PAIR)], axis=1)


def _silu(x):
    return x * _sigmoid(x)


def _gelu_tanh(x):
    c = math.sqrt(2.0 / math.pi)
    return 0.5 * x * (1.0 + jnp.tanh(c * (x + 0.044715 * (x * x * x))))


def _bcast_rows(v8, rows):
    c = v8.shape[-1]
    return jnp.broadcast_to(v8[None], (rows // BATCH, BATCH, c)).reshape(rows, c)


def _rms_modulate(x, g, scale8, shift8):
    rows = x.shape[0]
    ms = jnp.mean(x * x, axis=-1, keepdims=True)
    y = x * lax.rsqrt(ms + RMS_EPS) * g
    return y * (1.0 + _bcast_rows(scale8, rows)) + _bcast_rows(shift8, rows)


def _adaln_kernel(c_ref, w_ref, b_ref, o_ref):
    s = _silu(c_ref[...])
    o_ref[...] = jnp.dot(s, w_ref[...], preferred_element_type=F32,
                         precision=lax.Precision.HIGHEST) + b_ref[...]


def _adaln(c16, w_mod, b_mod):
    n = w_mod.shape[1]
    tn = 1536
    return pl.pallas_call(
        _adaln_kernel,
        grid=(n // tn,),
        in_specs=[_const_spec((16, D_MODEL)),
                  pl.BlockSpec((D_MODEL, tn), lambda j: (0, j)),
                  pl.BlockSpec((1, tn), lambda j: (0, j))],
        out_specs=pl.BlockSpec((16, tn), lambda j: (0, j)),
        out_shape=jax.ShapeDtypeStruct((16, n), F32),
        compiler_params=_params(1),
        name="adaln",
    )(c16, w_mod, b_mod)


LANE = 128
D_TILES = D_MODEL // LANE


def _proj_kernel(ctx_ref, x_ref, g_ref, sc_ref, sh_ref, w_ref, o_ref, xc_o, xl_o, stage_s):
    is_ctx = pl.program_id(0) < N_CTX_CHUNKS

    def stage(src_ref):
        for bi in range(BATCH):
            rows = pl.ds(bi, CHUNK_T, stride=BATCH)
            for s in range(D_TILES):
                stage_s[s, rows, :] = src_ref[bi, :, s * LANE:(s + 1) * LANE]

    @pl.when(is_ctx)
    def _():
        stage(ctx_ref)

    @pl.when(jnp.logical_not(is_ctx))
    def _():
        stage(x_ref)

    x = jnp.concatenate([stage_s[s] for s in range(D_TILES)], axis=1)

    @pl.when(is_ctx)
    def _():
        xc_o[...] = x

    xl_o[...] = x
    h = _rms_modulate(x, g_ref[...], sc_ref[0], sh_ref[0])
    o_ref[...] = jnp.dot(h.astype(BF16), w_ref[...], preferred_element_type=F32)


def _proj(ctx, x, g, sc2, sh2, w):
    n = w.shape[1]
    seg = lambda i: (i >= N_CTX_CHUNKS).astype(jnp.int32)
    cchunk = lambda i: jnp.minimum(i, N_CTX_CHUNKS - 1)
    lchunk = lambda i: jnp.maximum(i - N_CTX_CHUNKS, 0)
    return pl.pallas_call(
        _proj_kernel,
        grid=(N_CHUNKS,),
        in_specs=[pl.BlockSpec((BATCH, CHUNK_T, D_MODEL), lambda i: (0, cchunk(i), 0)),
                  pl.BlockSpec((BATCH, CHUNK_T, D_MODEL), lambda i: (0, lchunk(i), 0)),
                  _const_spec((1, D_MODEL)),
                  pl.BlockSpec((1, BATCH, D_MODEL), lambda i: (seg(i), 0, 0)),
                  pl.BlockSpec((1, BATCH, D_MODEL), lambda i: (seg(i), 0, 0)),
                  _const_spec((D_MODEL, n))],
        out_specs=[pl.BlockSpec((CHUNK_ROWS, n), lambda i: (i, 0)),
                   pl.BlockSpec((CHUNK_ROWS, D_MODEL), lambda i: (cchunk(i), 0)),
                   pl.BlockSpec((CHUNK_ROWS, D_MODEL), lambda i: (lchunk(i), 0))],
        out_shape=[jax.ShapeDtypeStruct((N_CHUNKS * CHUNK_ROWS, n), F32),
                   jax.ShapeDtypeStruct((CTX_LEN * BATCH, D_MODEL), F32),
                   jax.ShapeDtypeStruct((SEQ * BATCH, D_MODEL), F32)],
        scratch_shapes=[pltpu.VMEM((D_TILES, CHUNK_ROWS, LANE), F32)],
        compiler_params=_params(1),
        name="in_proj",
    )(ctx, x, g, sc2, sh2, w)


def _lru_kernel(*refs, reverse):
    if reverse:
        (u_ref, uy_ref, hf_ref, cw_ref, cb_ref, wt_ref, ba_ref, bx_ref, lam_ref,
         o_ref, h_s, halo_s, ext_s, a_s, b_s) = refs
    else:
        (u_ref, cw_ref, cb_ref, wt_ref, ba_ref, bx_ref, lam_ref,
         o_ref, h_s, halo_s, ext_s, a_s, b_s) = refs
    i = pl.program_id(0)
    halo_rows = (LRU_CONV - 1) * BATCH

    @pl.when(i == 0)
    def _():
        h_s[...] = jnp.zeros_like(h_s)

    @pl.when((i == 0) | (i == N_CTX_CHUNKS))
    def _():
        halo_s[...] = jnp.zeros_like(halo_s)

    u = u_ref[...]
    if reverse:
        ext_s[0:CHUNK_ROWS] = u
        ext_s[CHUNK_ROWS:CHUNK_ROWS + halo_rows] = halo_s[...]
        halo_s[...] = u[0:halo_rows]
    else:
        ext_s[0:halo_rows] = halo_s[...]
        ext_s[halo_rows:halo_rows + CHUNK_ROWS] = u
        halo_s[...] = u[CHUNK_ROWS - halo_rows:CHUNK_ROWS]

    xc = jnp.broadcast_to(cb_ref[...], (CHUNK_ROWS, LRU_WIDTH))
    for j in range(LRU_CONV):
        off = (LRU_CONV - 1 - j) if reverse else j
        xc = xc + cw_ref[j:j + 1, :] * ext_s[off * BATCH:off * BATCH + CHUNK_ROWS]

    xcb = xc.astype(BF16)
    c_half = (-0.5 * LRU_C) * _softplus(-lam_ref[...])
    for j in range(LRU_WIDTH // LRU_GATE_TILE):
        k0 = LRU_GATE_K0[j]
        cs = slice(j * LRU_GATE_TILE, (j + 1) * LRU_GATE_TILE)
        g = jnp.dot(xcb[:, k0:k0 + LRU_GATE_K], wt_ref[j], preferred_element_type=F32)
        tanh_r = jnp.tanh(g[:, :LRU_GATE_TILE] + ba_ref[:, cs])
        gate_i = 0.5 * jnp.tanh(g[:, LRU_GATE_TILE:] + bx_ref[:, cs]) + 0.5
        log_a = c_half[:, cs] * tanh_r + c_half[:, cs]
        a = jnp.exp(log_a)
        a_s[:, cs] = a
        om = 1.0 - a * a
        b_s[:, cs] = jnp.where(om > 0.0, om * lax.rsqrt(om), 0.0) * (gate_i * xc[:, cs])

    def step(k, h):
        t = (CHUNK_T - 1 - k) if reverse else k
        rows = pl.ds(pl.multiple_of(t * BATCH, BATCH), BATCH)
        h = a_s[rows, :] * h + b_s[rows, :]
        if reverse:
            o_ref[rows, :] = (hf_ref[rows, :] + h) * _gelu_tanh(uy_ref[rows, :])
        else:
            o_ref[rows, :] = h
        return h

    h_s[...] = lax.fori_loop(0, CHUNK_T, step, h_s[...], unroll=8)


def _lru_scan(u_all, hf, lru_p, d, *, reverse):
    cw, cb, wt, ba, bx, lam = lru_p
    if reverse:
        chunk = lambda i: jnp.where(i < N_CTX_CHUNKS, N_CTX_CHUNKS - 1 - i,
                                    N_CHUNKS + N_CTX_CHUNKS - 1 - i)
    else:
        chunk = lambda i: i
    lat = lambda i: jnp.maximum(chunk(jnp.maximum(i, N_CTX_CHUNKS)) - N_CTX_CHUNKS, 0)
    blk = (CHUNK_ROWS, LRU_WIDTH)
    in_specs = [pl.BlockSpec(blk, lambda i: (chunk(i), 0))]
    args = [u_all]
    if reverse:
        in_specs += [pl.BlockSpec(blk, lambda i: (lat(i) + N_CTX_CHUNKS, 1)),
                     pl.BlockSpec(blk, lambda i: (lat(i), 0))]
        args += [u_all, hf]
    in_specs += [_const_spec((LRU_CONV, LRU_WIDTH)), _const_spec((1, LRU_WIDTH)),
                 _const_spec(wt[d].shape), _const_spec((1, LRU_WIDTH)),
                 _const_spec((1, LRU_WIDTH)), _const_spec((1, LRU_WIDTH))]
    args += [cw[d], cb[d][None], wt[d], ba[d][None], bx[d][None], lam[d][None]]
    halo_rows = (LRU_CONV - 1) * BATCH
    return pl.pallas_call(
        functools.partial(_lru_kernel, reverse=reverse),
        grid=(N_CHUNKS,),
        in_specs=in_specs,
        out_specs=pl.BlockSpec(blk, lambda i: (lat(i), 0)),
        out_shape=jax.ShapeDtypeStruct((SEQ * BATCH, LRU_WIDTH), F32),
        scratch_shapes=[pltpu.VMEM((BATCH, LRU_WIDTH), F32),
                        pltpu.VMEM((halo_rows, LRU_WIDTH), F32),
                        pltpu.VMEM((CHUNK_ROWS + halo_rows, LRU_WIDTH), F32),
                        pltpu.VMEM(blk, F32),
                        pltpu.VMEM(blk, F32)],
        compiler_params=_params(1),
        name="lru_bwd" if reverse else "lru_fwd",
    )(*args)


PREP_T = 32
PREP_ROWS = PREP_T * BATCH
PREP_CTX_BLOCKS = CTX_LEN // PREP_T
PREP_LAT_BLOCKS = SEQ // PREP_T
PREP_BLOCKS = PREP_CTX_BLOCKS + PREP_LAT_BLOCKS


def _prep_kernel(xc_ref, xl_ref, xcp_ref, xcn_ref, xlp_ref, xln_ref, gmix_ref, sc_ref, sh_ref,
                 win_ref, mu_ref,
                 w2_ref, w0_ref, a2_ref, a0_ref, g2_ref, kk_ref, ka_ref, rk_ref, ones_ref,
                 r_o, v_o, kn_o, ld_o, kd_o, b_o, bon_o, g_o, ext_s):
    i = pl.program_id(0)
    is_ctx = i < PREP_CTX_BLOCKS
    first = (i == 0) | (i == PREP_CTX_BLOCKS)
    last = (i == PREP_CTX_BLOCKS - 1) | (i == PREP_BLOCKS - 1)
    lat_rows = lambda ref: ref[...].reshape(-1, D_MODEL)
    x = jnp.concatenate([
        jnp.where(is_ctx, xcp_ref[...], lat_rows(xlp_ref)),
        jnp.where(is_ctx, xc_ref[...], lat_rows(xl_ref)),
        jnp.where(is_ctx, xcn_ref[...], lat_rows(xln_ref))], axis=0)
    seg = jnp.where(is_ctx, 0, 1)
    h = _rms_modulate(x, gmix_ref[...], sc_ref[seg], sh_ref[seg])
    hb = h.astype(BF16)
    W = RWKV_WIDTH
    ones = ones_ref[...]

    def project(c0, c1):
        zt = jnp.dot(hb, win_ref[:, c0:c1], preferred_element_type=F32)
        ext_s[0:BATCH, c0:c1] = jnp.where(first, 0.0, zt[0:BATCH])
        ext_s[BATCH:BATCH + PREP_ROWS, c0:c1] = zt[BATCH:BATCH + PREP_ROWS]
        ext_s[BATCH + PREP_ROWS:2 * BATCH + PREP_ROWS, c0:c1] = jnp.where(
            last, 0.0, zt[BATCH + PREP_ROWS:2 * BATCH + PREP_ROWS])

    def shifted(c0, c1):
        zc = ext_s[BATCH:BATCH + PREP_ROWS, c0:c1]
        zprev = ext_s[0:PREP_ROWS, c0:c1]
        znext = ext_s[2 * BATCH:2 * BATCH + PREP_ROWS, c0:c1]
        return zc + mu_ref[0:1, c0:c1] * (zprev - zc) + mu_ref[1:2, c0:c1] * (znext - zc)

    def project_pairs(g):
        for part in range(3):
            project(part * W + 2 * g * PAIR, part * W + (2 * g + 2) * PAIR)

    def finish_pair(s, w_pre, a_pre):
        ln = slice(s * PAIR, (s + 1) * PAIR)
        r = shifted(s * PAIR, (s + 1) * PAIR)
        k = shifted(W + s * PAIR, W + (s + 1) * PAIR)
        v = shifted(2 * W + s * PAIR, 2 * W + (s + 1) * PAIR)
        kk = k * kk_ref[:, ln]
        ss = jnp.dot((kk * kk).astype(BF16), ones, preferred_element_type=F32)
        kn = kk / jnp.maximum(jnp.sqrt(ss), L2_EPS)
        r_o[s] = r
        v_o[s] = v
        kn_o[s] = kn
        kd_sum = jnp.zeros_like(k)
        for d in range(2):
            cs = slice(d * W + s * PAIR, d * W + (s + 1) * PAIR)
            ld_o[d, s] = -math.exp(-0.5) * _sigmoid(w_pre[:, cs])
            asig = _sigmoid(a_pre[:, cs])
            kd = k * (1.0 + (asig - 1.0) * ka_ref[:, ln])
            kd_o[d, s] = kd
            b_o[d, s] = kn * asig
            kd_sum = kd_sum + kd
        bon = jnp.dot((r * kd_sum * rk_ref[:, ln]).astype(BF16), ones, preferred_element_type=F32)
        bon_o[:, ln] = bon * v

    project(3 * W, RWKV_IN_PAD)
    project_pairs(0)
    wd = shifted(3 * W, 3 * W + 2 * LORA_W)
    ad = shifted(3 * W + 2 * LORA_W, 3 * W + 2 * LORA_W + 2 * LORA_A)
    gd = shifted(3 * W + 2 * LORA_W + 2 * LORA_A, RWKV_IN_PAD)
    w_pre = _bdot(jnp.tanh(wd), w2_ref[...]) + w0_ref[...]
    a_pre = _bdot(ad, a2_ref[...]) + a0_ref[...]
    g_o[...] = _bdot(_sigmoid(gd), g2_ref[...])
    for g in range(N_PAIRS // 2):
        if g + 1 < N_PAIRS // 2:
            project_pairs(g + 1)
        finish_pair(2 * g, w_pre, a_pre)
        finish_pair(2 * g + 1, w_pre, a_pre)


def _rwkv_prep(xc_tb, xl_tb, g_mix, sc2, sh2, w_rw, rw_p):
    mu, w2, w0, a2, a0, g2, k_k, k_a, r_k, ones = rw_p
    W = RWKV_WIDTH
    hb = PREP_T
    cmain = lambda i: jnp.minimum(i, PREP_CTX_BLOCKS - 1)
    lmain = lambda i: jnp.maximum(i - PREP_CTX_BLOCKS, 0)
    assert PREP_T == GRID_ROWS
    xl4 = xl_tb.reshape(GRID_ROWS, GRID_W, BATCH, D_MODEL)
    hblk = (BATCH, D_MODEL)
    n_c8 = CTX_LEN - 1
    in_specs = [
        pl.BlockSpec((PREP_ROWS, D_MODEL), lambda i: (cmain(i), 0)),
        pl.BlockSpec((GRID_ROWS, 1, BATCH, D_MODEL), lambda i: (0, lmain(i), 0, 0)),
        pl.BlockSpec(hblk, lambda i: (jnp.maximum(cmain(i) * hb - 1, 0), 0)),
        pl.BlockSpec(hblk, lambda i: (jnp.minimum((cmain(i) + 1) * hb, n_c8), 0)),
        pl.BlockSpec((1, 1, BATCH, D_MODEL), lambda i: (GRID_ROWS - 1, jnp.maximum(lmain(i) - 1, 0), 0, 0)),
        pl.BlockSpec((1, 1, BATCH, D_MODEL), lambda i: (0, jnp.minimum(lmain(i) + 1, GRID_W - 1), 0, 0)),
        _const_spec((1, D_MODEL)), _const_spec(sc2.shape), _const_spec(sh2.shape),
        _const_spec(w_rw.shape),
        _const_spec(mu.shape), _const_spec(w2.shape), _const_spec(w0.shape),
        _const_spec(a2.shape), _const_spec(a0.shape), _const_spec(g2.shape),
        _const_spec(k_k.shape), _const_spec(k_a.shape), _const_spec(r_k.shape),
        _const_spec(ones.shape),
    ]
    n_rows = T_ALL * BATCH
    n_lat = SEQ * BATCH
    shared = pl.BlockSpec((N_PAIRS, PREP_ROWS, PAIR), lambda i: (0, i, 0))
    perdir = pl.BlockSpec((2, N_PAIRS, PREP_ROWS, PAIR), lambda i: (0, 0, i, 0))
    latonly = pl.BlockSpec((PREP_ROWS, W), lambda i: (lmain(i), 0))
    return pl.pallas_call(
        _prep_kernel,
        grid=(PREP_BLOCKS,),
        in_specs=in_specs,
        out_specs=[shared, shared, shared, perdir, perdir, perdir, latonly, latonly],
        out_shape=[jax.ShapeDtypeStruct((N_PAIRS, n_rows, PAIR), F32)] * 3
        + [jax.ShapeDtypeStruct((2, N_PAIRS, n_rows, PAIR), F32)] * 3
        + [jax.ShapeDtypeStruct((n_lat, W), F32)] * 2,
        scratch_shapes=[pltpu.VMEM((PREP_ROWS + 2 * BATCH, RWKV_IN_PAD), F32)],
        compiler_params=_params(1),
        name="rwkv_prep",
    )(xc_tb, xl4, xc_tb, xc_tb, xl4, xl4, g_mix, sc2, sh2, w_rw,
      mu, w2, w0, a2, a0, g2, k_k, k_a, r_k, ones)


WKV_GROUP = 8
WKV_SKEW = 1


def _wkv_chunks(loaders, states, decays, sign):
    T = CHUNK_T
    R2 = 2 * T
    nb = len(loaders)
    levels = int(math.log2(T))

    lane = lax.broadcasted_iota(jnp.int32, (R2, PAIR), 1)
    row = lax.broadcasted_iota(jnp.int32, (R2, PAIR), 0)
    head_mask = (lane // RWKV_HEAD) == (row // T)

    def stack_masked(x):
        return jnp.where(head_mask, jnp.concatenate([x, x], axis=0), 0.0).astype(BF16)

    tw = lax.broadcasted_iota(jnp.int32, (T, 2 * R2), 0)
    sw = lax.broadcasted_iota(jnp.int32, (T, 2 * R2), 1) % T
    dtw = (tw - sw) * sign
    strict_w = dtw[:, 0:R2] > 0
    incl_w2 = dtw >= 0
    eye_w = jnp.where(dtw[:, 0:R2] == 0, 1.0, 0.0)

    nt = lambda x, y: lax.dot_general(x, y, (((1,), (1,)), ((), ())), preferred_element_type=F32)
    mm = lambda x, y: jnp.dot(x, y, preferred_element_type=F32)

    def chain(load, S, decay, out):
        a_t, r_t, b_t, k_t, b_e, k_e, v = load()
        ar = jnp.concatenate([a_t, r_t], axis=0).astype(BF16)
        bk = jnp.concatenate([stack_masked(b_t), stack_masked(k_t)], axis=0)
        big = nt(ar, bk)
        fs = nt(ar, S.astype(BF16))
        uv_rhs = jnp.concatenate([b_e, k_e], axis=0).astype(BF16)
        yield
        p_w = jnp.where(strict_w, big[0:T, 0:R2], 0.0)
        a_ak = jnp.where(strict_w, big[0:T, R2:2 * R2], 0.0).astype(BF16)
        a_rbk = jnp.where(incl_w2, big[T:R2], 0.0).astype(BF16)
        v_st = stack_masked(v)
        h = mm(a_ak, v_st)
        inv_w = eye_w + p_w
        p_w = mm(p_w.astype(BF16), stack_masked(p_w))
        yield
        for lvl in range(1, levels):
            p_bd = stack_masked(p_w)
            if lvl == levels - 1:
                inv_w = inv_w + mm(inv_w.astype(BF16), p_bd)
            else:
                both = mm(jnp.concatenate([p_w, inv_w], axis=0).astype(BF16), p_bd)
                yield
                p_w = both[0:T]
                inv_w = inv_w + both[T:R2]
        yield
        u = mm(inv_w.astype(BF16), stack_masked(fs[0:T] + h))
        yield
        y = fs[T:R2] + mm(a_rbk, jnp.concatenate([stack_masked(u), v_st], axis=0))
        uv_t = jnp.concatenate([u, v], axis=0).T.astype(BF16)
        out.append((y, S * decay + jnp.where(head_mask, mm(uv_t, uv_rhs), 0.0)))

    outs = [[] for _ in range(nb)]
    chains = [chain(loaders[i], states[i], decays[i], outs[i]) for i in range(nb)]
    groups = [chains[g:g + WKV_GROUP] for g in range(0, nb, WKV_GROUP)]
    live = [True] * len(groups)
    tick = 0
    while any(live):
        for g, group in enumerate(groups):
            if live[g] and tick >= g * WKV_SKEW:
                for ch in group:
                    if next(ch, "done") == "done":
                        live[g] = False
        tick += 1
    return [o[0][0] for o in outs], [o[0][1] for o in outs]


WKV_PAIRS_PER_STEP = 4


def _wkv_kernel(r_ref, v_ref, kn_ref, ld_ref, kd_ref, b_ref, y_ref, s_ref, cum_s):
    d = pl.program_id(0)
    c = pl.program_id(2)
    sign = 1 - 2 * d

    @pl.when(c == 0)
    def _():
        s_ref[...] = jnp.zeros_like(s_ref)

    tot8, dec8 = [], []
    for pi in range(WKV_PAIRS_PER_STEP):
        ld = ld_ref[0, pi]
        cum_up = ld
        for lvl in range(int(math.log2(CHUNK_T))):
            sh = BATCH << lvl
            cum_up = cum_up + jnp.concatenate([jnp.zeros((sh, PAIR), F32), cum_up[:-sh]], axis=0)
        tot8.append(cum_up[CHUNK_ROWS - BATCH:])
        tot = _bcast_rows(tot8[pi], CHUNK_ROWS)
        cum_s[pi] = jnp.where(d == 0, cum_up, tot - cum_up + ld)
        dec8.append(jnp.exp(tot8[pi]))

    problems = [(pl.ds(bi, CHUNK_T, stride=BATCH), pi, bi)
                for pi in range(WKV_PAIRS_PER_STEP) for bi in range(BATCH)]
    def loader(rw, pi, bi):
        def load():
            cum = cum_s[pi, rw, :]
            ld_b = ld_ref[0, pi, rw, :]
            b = b_ref[0, pi, rw, :]
            kd = kd_ref[0, pi, rw, :]
            e_out = jnp.exp(-cum)
            e_end = jnp.exp(tot8[pi][bi:bi + 1] - cum)
            return (-kn_ref[pi, rw, :] * jnp.exp(cum - ld_b), r_ref[pi, rw, :] * jnp.exp(cum),
                    b * e_out, kd * e_out, b * e_end, kd * e_end, v_ref[pi, rw, :])
        return load

    ys, s_new = _wkv_chunks([loader(*p) for p in problems],
                            [s_ref[pi, bi] for _, pi, bi in problems],
                            [dec8[pi][bi:bi + 1] for _, pi, bi in problems], sign)
    for k, (rw, pi, bi) in enumerate(problems):
        y_ref[0, pi, rw, :] = ys[k]
        s_ref[pi, bi] = s_new[k]


def _wkv(r, v, kn, ld, kd, b):
    def chunk(d, c):
        fwd = c
        bwd = jnp.where(c < N_CTX_CHUNKS, N_CTX_CHUNKS - 1 - c, N_CHUNKS + N_CTX_CHUNKS - 1 - c)
        return jnp.where(d == 0, fwd, bwd)
    lat = lambda d, c: chunk(d, jnp.maximum(c, N_CTX_CHUNKS)) - N_CTX_CHUNKS
    pps = WKV_PAIRS_PER_STEP
    shared = pl.BlockSpec((pps, CHUNK_ROWS, PAIR), lambda d, p, c: (p, chunk(d, c), 0))
    perdir = pl.BlockSpec((1, pps, CHUNK_ROWS, PAIR), lambda d, p, c: (d, p, chunk(d, c), 0))
    return pl.pallas_call(
        _wkv_kernel,
        grid=(2, N_PAIRS // pps, N_CHUNKS),
        in_specs=[shared, shared, shared, perdir, perdir, perdir],
        out_specs=pl.BlockSpec((1, pps, CHUNK_ROWS, PAIR), lambda d, p, c: (d, p, lat(d, c), 0)),
        out_shape=jax.ShapeDtypeStruct((2, N_PAIRS, SEQ * BATCH, PAIR), F32),
        scratch_shapes=[pltpu.VMEM((pps, BATCH, PAIR, PAIR), F32),
                        pltpu.VMEM((pps, CHUNK_ROWS, PAIR), F32)],
        compiler_params=_params(3),
        name="wkv_scan",
    )(r, v, kn, ld, kd, b)


MERGE_T = 32
MERGE_ROWS = MERGE_T * BATCH


def _merge_kernel(lru_ref, y_ref, bon_ref, g_ref, x_ref, gmix_ref, mod_ref,
                  lng_ref, lnb_ref, ones_ref, wgate_ref, wol_ref, wor_ref, wout_ref, o_ref):
    rows = MERGE_ROWS
    W = RWKV_WIDTH
    x = x_ref[...]
    h = _rms_modulate(x, gmix_ref[...], mod_ref[1], mod_ref[0])
    gates = _bdot(h, wgate_ref[...])
    y2 = (y_ref[0] + y_ref[1]).reshape(N_PAIRS, rows, PAIR)
    y = jnp.concatenate([y2[s] for s in range(N_PAIRS)], axis=1)
    inv_n = 1.0 / RWKV_HEAD

    ones = ones_ref[...]
    y_hi = y.astype(BF16).astype(F32)
    mu = (_head_sums(y_hi, ones) + _head_sums(y - y_hi, ones)) * inv_n
    dy = y - mu
    var = _head_sums(dy * dy, ones) * inv_n
    yn = dy * lax.rsqrt(var + GN_EPS) * lng_ref[...] + lnb_ref[...]
    rw = (yn + bon_ref[...].reshape(rows, W)) * g_ref[...].reshape(rows, W)
    m = (_sigmoid(gates[:, :D_MODEL]) * _bdot(lru_ref[...], wol_ref[...])
         + _sigmoid(gates[:, D_MODEL:]) * _bdot(rw, wor_ref[...]))
    mix = _bdot(m, wout_ref[...])
    o_ref[...] = x + _bcast_rows(mod_ref[2], rows) * mix


def _merge(lru_l, y, bon, g, x_tb, g_mix, mod_m, ln_g, ln_b, ones, wgate, wol, wor, wout):
    W = RWKV_WIDTH
    n_blocks = SEQ // MERGE_T
    per_row = GRID_W // MERGE_T
    cm4 = lambda i: (i % per_row, i // per_row, 0, 0)
    y6 = y.reshape(2, N_PAIRS, GRID_W, GRID_ROWS, BATCH, PAIR)
    bon4 = bon.reshape(GRID_W, GRID_ROWS, BATCH, W)
    g4 = g.reshape(GRID_W, GRID_ROWS, BATCH, W)
    return pl.pallas_call(
        _merge_kernel,
        grid=(n_blocks,),
        in_specs=[
            pl.BlockSpec((MERGE_ROWS, LRU_WIDTH), lambda i: (i, 0)),
            pl.BlockSpec((2, N_PAIRS, MERGE_T, 1, BATCH, PAIR), lambda i: (0, 0) + cm4(i)),
            pl.BlockSpec((MERGE_T, 1, BATCH, W), cm4),
            pl.BlockSpec((MERGE_T, 1, BATCH, W), cm4),
            pl.BlockSpec((MERGE_ROWS, D_MODEL), lambda i: (i, 0)),
            _const_spec((1, D_MODEL)), _const_spec((3, BATCH, D_MODEL)),
            _const_spec((1, W)), _const_spec((1, W)), _const_spec(ones.shape),
            _const_spec(wgate.shape), _const_spec(wol.shape), _const_spec(wor.shape),
            _const_spec(wout.shape),
        ],
        out_specs=pl.BlockSpec((MERGE_ROWS, D_MODEL), lambda i: (i, 0)),
        out_shape=jax.ShapeDtypeStruct((SEQ * BATCH, D_MODEL), F32),
        compiler_params=_params(1),
        name="merge",
    )(lru_l, y6, bon4, g4, x_tb, g_mix, mod_m, ln_g, ln_b, ones, wgate, wol, wor, wout)


FFN_T = 64
FFN_ROWS = FFN_T * BATCH
FFN_TILE = 256


def _ffn_up_kernel(x_ref, g_ref, sc_ref, sh_ref, w_ref, o_ref):
    h = _rms_modulate(x_ref[...], g_ref[...], sc_ref[...], sh_ref[...]).astype(BF16)
    for j in range(D_FF // FFN_TILE):
        cs = slice(j * FFN_TILE, (j + 1) * FFN_TILE)
        gate = jnp.dot(h, w_ref[:, cs], preferred_element_type=F32)
        up = jnp.dot(h, w_ref[:, D_FF + j * FFN_TILE:D_FF + (j + 1) * FFN_TILE],
                     preferred_element_type=F32)
        o_ref[:, cs] = (_silu(gate) * up).astype(BF16)


def _ffn_up(x1, g, sc8, sh8, w_in):
    return pl.pallas_call(
        _ffn_up_kernel,
        grid=(SEQ * BATCH // FFN_ROWS,),
        in_specs=[pl.BlockSpec((FFN_ROWS, D_MODEL), lambda i: (i, 0)),
                  _const_spec((1, D_MODEL)), _const_spec((BATCH, D_MODEL)),
                  _const_spec((BATCH, D_MODEL)), _const_spec(w_in.shape)],
        out_specs=pl.BlockSpec((FFN_ROWS, D_FF), lambda i: (i, 0)),
        out_shape=jax.ShapeDtypeStruct((SEQ * BATCH, D_FF), BF16),
        compiler_params=_params(1),
        name="ffn_up",
    )(x1, g, sc8, sh8, w_in)


def _ffn_down_kernel(act_ref, x_ref, gf_ref, w_ref, gfin_ref, o_ref, stage_s):
    y = jnp.dot(act_ref[...], w_ref[...], preferred_element_type=F32)
    x2 = x_ref[...] + _bcast_rows(gf_ref[...], FFN_ROWS) * y
    ms = jnp.mean(x2 * x2, axis=-1, keepdims=True)
    out = x2 * lax.rsqrt(ms + RMS_EPS) * gfin_ref[...]
    for s in range(D_TILES):
        stage_s[s] = out[:, s * LANE:(s + 1) * LANE]
    for bi in range(BATCH):
        rows = pl.ds(bi, FFN_T, stride=BATCH)
        for s in range(D_TILES):
            o_ref[bi, :, s * LANE:(s + 1) * LANE] = stage_s[s, rows, :]


def _ffn_down(act, x1, g_f8, w_out, g_final):
    return pl.pallas_call(
        _ffn_down_kernel,
        grid=(SEQ * BATCH // FFN_ROWS,),
        in_specs=[pl.BlockSpec((FFN_ROWS, D_FF), lambda i: (i, 0)),
                  pl.BlockSpec((FFN_ROWS, D_MODEL), lambda i: (i, 0)),
                  _const_spec((BATCH, D_MODEL)), _const_spec(w_out.shape),
                  _const_spec((1, D_MODEL))],
        out_specs=pl.BlockSpec((BATCH, FFN_T, D_MODEL), lambda i: (0, i, 0)),
        out_shape=jax.ShapeDtypeStruct((BATCH, SEQ, D_MODEL), F32),
        scratch_shapes=[pltpu.VMEM((D_TILES, FFN_ROWS, LANE), F32)],
        compiler_params=_params(1),
        name="ffn_down",
    )(act, x1, g_f8, w_out, g_final)


def _block_diag(w):
    n, c, _ = w.shape
    tiled = jnp.tile(w.reshape(n * c, c), (1, n))
    rb = lax.broadcasted_iota(jnp.int32, (n * c, n * c), 0) // c
    cb = lax.broadcasted_iota(jnp.int32, (n * c, n * c), 1) // c
    return jnp.where(rb == cb, tiled, 0.0)


def _lru_gate_tiles(wa, wx):
    da, dx = _block_diag(0.5 * wa), _block_diag(0.5 * wx)
    tiles = []
    for j, k0 in enumerate(LRU_GATE_K0):
        cs = slice(j * LRU_GATE_TILE, (j + 1) * LRU_GATE_TILE)
        tiles.append(jnp.concatenate([da[k0:k0 + LRU_GATE_K, cs], dx[k0:k0 + LRU_GATE_K, cs]], axis=1))
    return jnp.stack(tiles).astype(BF16)


def _two_dir_lora(w):
    z = jnp.zeros_like(w[0])
    return jnp.concatenate([jnp.concatenate([w[0], z], axis=1),
                            jnp.concatenate([z, w[1]], axis=1)], axis=0).astype(BF16)


def kernel(x, c, ctx, c_ctx, norm_mix_g, norm_ffn_g, w_mod, b_mod, w_in, lru_conv_w, lru_conv_b, lru_wa, lru_ba, lru_wx, lru_bx, lru_lambda, w_o_lru, rwkv_mu, rwkv_w0, rwkv_w2, rwkv_a0, rwkv_a2, rwkv_g2, rwkv_k_k, rwkv_k_a, rwkv_r_k, rwkv_ln_g, rwkv_ln_b, w_o_rwkv, w_out, w_ffn_in, w_ffn_out, norm_final_g):
    assert x.shape == (BATCH, SEQ, D_MODEL) and ctx.shape == (BATCH, CTX_LEN, D_MODEL)
    assert w_mod.shape[0] == 1, "single layer only"
    D, W = D_MODEL, RWKV_WIDTH

    c16 = jnp.concatenate([c, c_ctx[None], jnp.zeros((16 - BATCH - 1, D), F32)], axis=0)
    mod = _adaln(c16, w_mod[0], b_mod[0][None])
    mod_lat = mod[:BATCH].reshape(BATCH, 6, D)
    mod_ctx = jnp.broadcast_to(mod[BATCH:BATCH + 1], (BATCH, 6 * D)).reshape(BATCH, 6, D)
    sh_m, sc_m, g_m, sh_f, sc_f, g_f = [mod_lat[:, k] for k in range(6)]
    sh2 = jnp.stack([mod_ctx[:, 0], sh_m])
    sc2 = jnp.stack([mod_ctx[:, 1], sc_m])

    w_in0 = w_in[0]
    n_lru = 2 * LRU_WIDTH
    w_lru = w_in0[:, :n_lru].astype(BF16)
    w_rw = jnp.pad(w_in0[:, n_lru:n_lru + RWKV_IN], ((0, 0), (0, RWKV_IN_PAD - RWKV_IN))).astype(BF16)
    w_gate = w_in0[:, n_lru + RWKV_IN:].astype(BF16)
    g_mix = norm_mix_g[0][None]

    u_all, xc_tb, xl_tb = _proj(ctx, x, g_mix, sc2, sh2, w_lru)

    wt = jnp.stack([_lru_gate_tiles(lru_wa[0, d], lru_wx[0, d]) for d in range(2)])
    lru_p = (lru_conv_w[0], lru_conv_b[0], wt, 0.5 * lru_ba[0], 0.5 * lru_bx[0], lru_lambda[0])
    hf = _lru_scan(u_all, None, lru_p, 0, reverse=False)
    lru_l = _lru_scan(u_all, hf, lru_p, 1, reverse=True)

    mu_pad = jnp.pad(rwkv_mu[0], ((0, 0), (0, RWKV_IN_PAD - RWKV_IN)))
    ones = _block_diag(jnp.ones((PAIR // RWKV_HEAD, RWKV_HEAD, RWKV_HEAD), F32)).astype(BF16)
    rw_p = (mu_pad, _two_dir_lora(rwkv_w2[0]), rwkv_w0[0].reshape(1, 2 * W),
            _two_dir_lora(rwkv_a2[0]), rwkv_a0[0].reshape(1, 2 * W),
            jnp.pad(rwkv_g2[0], ((0, LORA_G_PAD - LORA_G), (0, 0))).astype(BF16),
            rwkv_k_k[0][None], rwkv_k_a[0][None], rwkv_r_k[0].reshape(1, W), ones)
    r, v, kn, ld, kd, b, bon, g = _rwkv_prep(xc_tb, xl_tb, g_mix, sc2, sh2, w_rw, rw_p)
    y = _wkv(r, v, kn, ld, kd, b)

    x1 = _merge(lru_l, y, bon, g, xl_tb, g_mix, jnp.stack([sh_m, sc_m, g_m]),
                rwkv_ln_g[0][None], rwkv_ln_b[0][None], ones, w_gate,
                w_o_lru[0].astype(BF16), w_o_rwkv[0].astype(BF16), w_out[0].astype(BF16))

    act = _ffn_up(x1, norm_ffn_g[0][None], sc_f, sh_f, w_ffn_in[0].astype(BF16))
    return _ffn_down(act, x1, g_f, w_ffn_out[0].astype(BF16), norm_final_g[None])
```

```python
import functools
import math

import jax
import jax.numpy as jnp
from jax import lax
from jax.experimental import pallas as pl
from jax.experimental.pallas import tpu as pltpu

F32 = jnp.float32
BF16 = jnp.bfloat16

D_MODEL = 1024
BATCH = 8
SEQ = 2048
CTX_LEN = 256
GRID_W = 64
GRID_ROWS = SEQ // GRID_W

LRU_WIDTH = 1280
LRU_BLOCKS = 16
LRU_BLOCK = LRU_WIDTH // LRU_BLOCKS
LRU_CONV = 4
LRU_C = 8.0

RWKV_HEAD = 64
RWKV_WIDTH = 1024
LORA_W = 64
LORA_A = 64
LORA_G = 160
RWKV_IN = 3 * RWKV_WIDTH + 2 * LORA_W + 2 * LORA_A + LORA_G
RWKV_IN_PAD = 3584
LORA_G_PAD = RWKV_IN_PAD - (3 * RWKV_WIDTH + 2 * LORA_W + 2 * LORA_A)
D_FF = 2816

RMS_EPS = 1e-6
GN_EPS = 64e-5
L2_EPS = 1e-12

T_ALL = CTX_LEN + SEQ
CHUNK_T = 64
CHUNK_ROWS = CHUNK_T * BATCH
N_CTX_CHUNKS = CTX_LEN // CHUNK_T
N_LAT_CHUNKS = SEQ // CHUNK_T
N_CHUNKS = N_CTX_CHUNKS + N_LAT_CHUNKS

LANE = 128
D_TILES = D_MODEL // LANE
PAIR = 2 * RWKV_HEAD
N_PAIRS = RWKV_WIDTH // PAIR

LRU_GATE_TILE = 256
LRU_GATE_K = 512
LRU_GATE_K0 = (0, 128, 384, 640, 768)

VMEM_LIMIT = 56 * 1024 * 1024


def _params(n_axes):
    return pltpu.CompilerParams(dimension_semantics=("arbitrary",) * n_axes,
                                vmem_limit_bytes=VMEM_LIMIT)


def _const_spec(shape):
    nd = len(shape)
    return pl.BlockSpec(shape, lambda *_: (0,) * nd)


def _bdot(a, b):
    return jnp.dot(a.astype(BF16), b.astype(BF16), preferred_element_type=F32)


def _bdot_nt(a, b):
    return lax.dot_general(a.astype(BF16), b.astype(BF16), (((1,), (1,)), ((), ())),
                           preferred_element_type=F32)


def _softplus(x):
    return jnp.maximum(x, 0.0) + jnp.log1p(jnp.exp(-jnp.abs(x)))


def _sigmoid(x):
    return 0.5 * jnp.tanh(0.5 * x) + 0.5


def _head_sums(t, ones_pair):
    tb = t.astype(BF16)
    return jnp.concatenate(
        [jnp.dot(tb[:, s * PAIR:(s + 1) * PAIR], ones_pair, preferred_element_type=F32)
         for s in range(t.shape[1] // PAIR)], axis=1)


def _silu(x):
    return x * _sigmoid(x)


def _gelu_tanh(x):
    c = math.sqrt(2.0 / math.pi)
    return 0.5 * x * (1.0 + jnp.tanh(c * (x + 0.044715 * (x * x * x))))


def _bcast_rows(v8, rows):
    c = v8.shape[-1]
    return jnp.broadcast_to(v8[None], (rows // BATCH, BATCH, c)).reshape(rows, c)


def _rms_modulate(x, g, scale8, shift8):
    rows = x.shape[0]
    ms = jnp.mean(x * x, axis=-1, keepdims=True)
    y = x * lax.rsqrt(ms + RMS_EPS) * g
    return y * (1.0 + _bcast_rows(scale8, rows)) + _bcast_rows(shift8, rows)


def _adaln_kernel(c_ref, w_ref, b_ref, o_ref):
    s = _silu(c_ref[...])
    o_ref[...] = jnp.dot(s, w_ref[...], preferred_element_type=F32,
                         precision=lax.Precision.HIGHEST) + b_ref[...]


def _adaln(c16, w_mod, b_mod):
    n = w_mod.shape[1]
    tn = 1536
    return pl.pallas_call(
        _adaln_kernel,
        grid=(n // tn,),
        in_specs=[_const_spec((16, D_MODEL)),
                  pl.BlockSpec((D_MODEL, tn), lambda j: (0, j)),
                  pl.BlockSpec((1, tn), lambda j: (0, j))],
        out_specs=pl.BlockSpec((16, tn), lambda j: (0, j)),
        out_shape=jax.ShapeDtypeStruct((16, n), F32),
        compiler_params=_params(1),
        name="adaln",
    )(c16, w_mod, b_mod)


LANE = 128
D_TILES = D_MODEL // LANE


PROJ_GROUPS = 4
PROJ_T = CHUNK_T // PROJ_GROUPS
PROJ_ROWS = PROJ_T * BATCH


def _proj_kernel(ctx_ref, x_ref, g_ref, sc_ref, sh_ref, w_ref, o_ref, xc_o, xl_o, stage_s):
    is_ctx = pl.program_id(0) < N_CTX_CHUNKS

    def stage(gi):
        t0 = gi * PROJ_T
        for bi in range(BATCH):
            rows = pl.ds(gi * PROJ_ROWS + bi, PROJ_T, stride=BATCH)
            for s in range(D_TILES):
                ln = slice(s * LANE, (s + 1) * LANE)
                stage_s[s, rows, :] = jnp.where(is_ctx, ctx_ref[bi, t0:t0 + PROJ_T, ln],
                                                x_ref[bi, t0:t0 + PROJ_T, ln])

    def staged(rows):
        return jnp.concatenate([stage_s[s, rows, :] for s in range(D_TILES)], axis=1)

    def normalise(gi):
        rows = slice(gi * PROJ_ROWS, (gi + 1) * PROJ_ROWS)
        x = staged(rows)
        xl_o[rows, :] = x
        return _rms_modulate(x, g_ref[...], sc_ref[0], sh_ref[0]).astype(BF16)

    h = None
    for gi in range(PROJ_GROUPS + 1):
        if gi < PROJ_GROUPS:
            stage(gi)
        if gi > 0:
            o_ref[(gi - 1) * PROJ_ROWS:gi * PROJ_ROWS, :] = jnp.dot(
                h, w_ref[...], preferred_element_type=F32)
        if gi < PROJ_GROUPS:
            h = normalise(gi)

    @pl.when(is_ctx)
    def _():
        xc_o[...] = staged(slice(0, CHUNK_ROWS))


def _proj(ctx, x, g, sc2, sh2, w):
    n = w.shape[1]
    seg = lambda i: (i >= N_CTX_CHUNKS).astype(jnp.int32)
    cchunk = lambda i: jnp.minimum(i, N_CTX_CHUNKS - 1)
    lchunk = lambda i: jnp.maximum(i - N_CTX_CHUNKS, 0)
    return pl.pallas_call(
        _proj_kernel,
        grid=(N_CHUNKS,),
        in_specs=[pl.BlockSpec((BATCH, CHUNK_T, D_MODEL), lambda i: (0, cchunk(i), 0)),
                  pl.BlockSpec((BATCH, CHUNK_T, D_MODEL), lambda i: (0, lchunk(i), 0)),
                  _const_spec((1, D_MODEL)),
                  pl.BlockSpec((1, BATCH, D_MODEL), lambda i: (seg(i), 0, 0)),
                  pl.BlockSpec((1, BATCH, D_MODEL), lambda i: (seg(i), 0, 0)),
                  _const_spec((D_MODEL, n))],
        out_specs=[pl.BlockSpec((CHUNK_ROWS, n), lambda i: (i, 0)),
                   pl.BlockSpec((CHUNK_ROWS, D_MODEL), lambda i: (cchunk(i), 0)),
                   pl.BlockSpec((CHUNK_ROWS, D_MODEL), lambda i: (lchunk(i), 0))],
        out_shape=[jax.ShapeDtypeStruct((N_CHUNKS * CHUNK_ROWS, n), F32),
                   jax.ShapeDtypeStruct((CTX_LEN * BATCH, D_MODEL), F32),
                   jax.ShapeDtypeStruct((SEQ * BATCH, D_MODEL), F32)],
        scratch_shapes=[pltpu.VMEM((D_TILES, CHUNK_ROWS, LANE), F32)],
        compiler_params=_params(1),
        name="in_proj",
    )(ctx, x, g, sc2, sh2, w)


def _lru_kernel(*refs, reverse):
    if reverse:
        (u_ref, uy_ref, hf_ref, cw_ref, cb_ref, wt_ref, ba_ref, bx_ref, lam_ref,
         o_ref, h_s, halo_s, ext_s, a_s, b_s) = refs
    else:
        (u_ref, cw_ref, cb_ref, wt_ref, ba_ref, bx_ref, lam_ref,
         o_ref, h_s, halo_s, ext_s, a_s, b_s) = refs
    i = pl.program_id(0)
    halo_rows = (LRU_CONV - 1) * BATCH

    @pl.when(i == 0)
    def _():
        h_s[...] = jnp.zeros_like(h_s)

    @pl.when((i == 0) | (i == N_CTX_CHUNKS))
    def _():
        halo_s[...] = jnp.zeros_like(halo_s)

    u = u_ref[...]
    if reverse:
        ext_s[0:CHUNK_ROWS] = u
        ext_s[CHUNK_ROWS:CHUNK_ROWS + halo_rows] = halo_s[...]
        halo_s[...] = u[0:halo_rows]
    else:
        ext_s[0:halo_rows] = halo_s[...]
        ext_s[halo_rows:halo_rows + CHUNK_ROWS] = u
        halo_s[...] = u[CHUNK_ROWS - halo_rows:CHUNK_ROWS]

    xc = jnp.broadcast_to(cb_ref[...], (CHUNK_ROWS, LRU_WIDTH))
    for j in range(LRU_CONV):
        off = (LRU_CONV - 1 - j) if reverse else j
        xc = xc + cw_ref[j:j + 1, :] * ext_s[off * BATCH:off * BATCH + CHUNK_ROWS]

    xcb = xc.astype(BF16)
    c_half = (-0.5 * LRU_C) * _softplus(-lam_ref[...])
    for j in range(LRU_WIDTH // LRU_GATE_TILE):
        k0 = LRU_GATE_K0[j]
        cs = slice(j * LRU_GATE_TILE, (j + 1) * LRU_GATE_TILE)
        g = jnp.dot(xcb[:, k0:k0 + LRU_GATE_K], wt_ref[j], preferred_element_type=F32)
        tanh_r = jnp.tanh(g[:, :LRU_GATE_TILE] + ba_ref[:, cs])
        gate_i = 0.5 * jnp.tanh(g[:, LRU_GATE_TILE:] + bx_ref[:, cs]) + 0.5
        log_a = c_half[:, cs] * tanh_r + c_half[:, cs]
        a = jnp.exp(log_a)
        a_s[:, cs] = a
        om = 1.0 - a * a
        b_s[:, cs] = jnp.where(om > 0.0, om * lax.rsqrt(om), 0.0) * (gate_i * xc[:, cs])

    def step(k, h):
        t = (CHUNK_T - 1 - k) if reverse else k
        rows = pl.ds(pl.multiple_of(t * BATCH, BATCH), BATCH)
        h = a_s[rows, :] * h + b_s[rows, :]
        if reverse:
            o_ref[rows, :] = (hf_ref[rows, :] + h) * _gelu_tanh(uy_ref[rows, :])
        else:
            o_ref[rows, :] = h
        return h

    h_s[...] = lax.fori_loop(0, CHUNK_T, step, h_s[...], unroll=8)


def _lru_scan(u_all, hf, lru_p, d, *, reverse):
    cw, cb, wt, ba, bx, lam = lru_p
    if reverse:
        chunk = lambda i: jnp.where(i < N_CTX_CHUNKS, N_CTX_CHUNKS - 1 - i,
                                    N_CHUNKS + N_CTX_CHUNKS - 1 - i)
    else:
        chunk = lambda i: i
    lat = lambda i: jnp.maximum(chunk(jnp.maximum(i, N_CTX_CHUNKS)) - N_CTX_CHUNKS, 0)
    blk = (CHUNK_ROWS, LRU_WIDTH)
    in_specs = [pl.BlockSpec(blk, lambda i: (chunk(i), 0))]
    args = [u_all]
    if reverse:
        in_specs += [pl.BlockSpec(blk, lambda i: (lat(i) + N_CTX_CHUNKS, 1)),
                     pl.BlockSpec(blk, lambda i: (lat(i), 0))]
        args += [u_all, hf]
    in_specs += [_const_spec((LRU_CONV, LRU_WIDTH)), _const_spec((1, LRU_WIDTH)),
                 _const_spec(wt[d].shape), _const_spec((1, LRU_WIDTH)),
                 _const_spec((1, LRU_WIDTH)), _const_spec((1, LRU_WIDTH))]
    args += [cw[d], cb[d][None], wt[d], ba[d][None], bx[d][None], lam[d][None]]
    halo_rows = (LRU_CONV - 1) * BATCH
    return pl.pallas_call(
        functools.partial(_lru_kernel, reverse=reverse),
        grid=(N_CHUNKS,),
        in_specs=in_specs,
        out_specs=pl.BlockSpec(blk, lambda i: (lat(i), 0)),
        out_shape=jax.ShapeDtypeStruct((SEQ * BATCH, LRU_WIDTH), F32),
        scratch_shapes=[pltpu.VMEM((BATCH, LRU_WIDTH), F32),
                        pltpu.VMEM((halo_rows, LRU_WIDTH), F32),
                        pltpu.VMEM((CHUNK_ROWS + halo_rows, LRU_WIDTH), F32),
                        pltpu.VMEM(blk, F32),
                        pltpu.VMEM(blk, F32)],
        compiler_params=_params(1),
        name="lru_bwd" if reverse else "lru_fwd",
    )(*args)


PREP_T = 32
PREP_ROWS = PREP_T * BATCH
PREP_CTX_BLOCKS = CTX_LEN // PREP_T
PREP_LAT_BLOCKS = SEQ // PREP_T
PREP_BLOCKS = PREP_CTX_BLOCKS + PREP_LAT_BLOCKS


def _prep_kernel(xc_ref, xl_ref, xcp_ref, xcn_ref, xlp_ref, xln_ref, gmix_ref, sc_ref, sh_ref,
                 win_ref, mu_ref,
                 w2_ref, w0_ref, a2_ref, a0_ref, g2_ref, kk_ref, ka_ref, rk_ref, ones_ref,
                 r_o, v_o, kn_o, ld_o, kd_o, b_o, bon_o, g_o, ext_s):
    i = pl.program_id(0)
    is_ctx = i < PREP_CTX_BLOCKS
    first = (i == 0) | (i == PREP_CTX_BLOCKS)
    last = (i == PREP_CTX_BLOCKS - 1) | (i == PREP_BLOCKS - 1)
    lat_rows = lambda ref: ref[...].reshape(-1, D_MODEL)
    x = jnp.concatenate([
        jnp.where(is_ctx, xcp_ref[...], lat_rows(xlp_ref)),
        jnp.where(is_ctx, xc_ref[...], lat_rows(xl_ref)),
        jnp.where(is_ctx, xcn_ref[...], lat_rows(xln_ref))], axis=0)
    seg = jnp.where(is_ctx, 0, 1)
    h = _rms_modulate(x, gmix_ref[...], sc_ref[seg], sh_ref[seg])
    hb = h.astype(BF16)
    W = RWKV_WIDTH
    ones = ones_ref[...]

    def project(c0, c1):
        zt = jnp.dot(hb, win_ref[:, c0:c1], preferred_element_type=F32)
        ext_s[0:BATCH, c0:c1] = jnp.where(first, 0.0, zt[0:BATCH])
        ext_s[BATCH:BATCH + PREP_ROWS, c0:c1] = zt[BATCH:BATCH + PREP_ROWS]
        ext_s[BATCH + PREP_ROWS:2 * BATCH + PREP_ROWS, c0:c1] = jnp.where(
            last, 0.0, zt[BATCH + PREP_ROWS:2 * BATCH + PREP_ROWS])

    def shifted(c0, c1):
        zc = ext_s[BATCH:BATCH + PREP_ROWS, c0:c1]
        zprev = ext_s[0:PREP_ROWS, c0:c1]
        znext = ext_s[2 * BATCH:2 * BATCH + PREP_ROWS, c0:c1]
        return zc + mu_ref[0:1, c0:c1] * (zprev - zc) + mu_ref[1:2, c0:c1] * (znext - zc)

    def project_pairs(g):
        for part in range(3):
            project(part * W + 2 * g * PAIR, part * W + (2 * g + 2) * PAIR)

    def finish_pair(s, w_pre, a_pre):
        ln = slice(s * PAIR, (s + 1) * PAIR)
        r = shifted(s * PAIR, (s + 1) * PAIR)
        k = shifted(W + s * PAIR, W + (s + 1) * PAIR)
        v = shifted(2 * W + s * PAIR, 2 * W + (s + 1) * PAIR)
        kk = k * kk_ref[:, ln]
        ss = jnp.dot((kk * kk).astype(BF16), ones, preferred_element_type=F32)
        kn = kk / jnp.maximum(jnp.sqrt(ss), L2_EPS)
        r_o[s] = r
        v_o[s] = v
        kn_o[s] = kn
        kd_sum = jnp.zeros_like(k)
        for d in range(2):
            cs = slice(d * W + s * PAIR, d * W + (s + 1) * PAIR)
            ld_o[d, s] = -math.exp(-0.5) * _sigmoid(w_pre[:, cs])
            asig = _sigmoid(a_pre[:, cs])
            kd = k * (1.0 + (asig - 1.0) * ka_ref[:, ln])
            kd_o[d, s] = kd
            b_o[d, s] = kn * asig
            kd_sum = kd_sum + kd
        bon = jnp.dot((r * kd_sum * rk_ref[:, ln]).astype(BF16), ones, preferred_element_type=F32)
        bon_o[:, ln] = bon * v

    project(3 * W, RWKV_IN_PAD)
    project_pairs(0)
    wd = shifted(3 * W, 3 * W + 2 * LORA_W)
    ad = shifted(3 * W + 2 * LORA_W, 3 * W + 2 * LORA_W + 2 * LORA_A)
    gd = shifted(3 * W + 2 * LORA_W + 2 * LORA_A, RWKV_IN_PAD)
    w_pre = _bdot(jnp.tanh(wd), w2_ref[...]) + w0_ref[...]
    a_pre = _bdot(ad, a2_ref[...]) + a0_ref[...]
    g_o[...] = _bdot(_sigmoid(gd), g2_ref[...])
    for g in range(N_PAIRS // 2):
        if g + 1 < N_PAIRS // 2:
            project_pairs(g + 1)
        finish_pair(2 * g, w_pre, a_pre)
        finish_pair(2 * g + 1, w_pre, a_pre)


def _rwkv_prep(xc_tb, xl_tb, g_mix, sc2, sh2, w_rw, rw_p):
    mu, w2, w0, a2, a0, g2, k_k, k_a, r_k, ones = rw_p
    W = RWKV_WIDTH
    hb = PREP_T
    cmain = lambda i: jnp.minimum(i, PREP_CTX_BLOCKS - 1)
    lmain = lambda i: jnp.maximum(i - PREP_CTX_BLOCKS, 0)
    assert PREP_T == GRID_ROWS
    xl4 = xl_tb.reshape(GRID_ROWS, GRID_W, BATCH, D_MODEL)
    hblk = (BATCH, D_MODEL)
    n_c8 = CTX_LEN - 1
    in_specs = [
        pl.BlockSpec((PREP_ROWS, D_MODEL), lambda i: (cmain(i), 0)),
        pl.BlockSpec((GRID_ROWS, 1, BATCH, D_MODEL), lambda i: (0, lmain(i), 0, 0)),
        pl.BlockSpec(hblk, lambda i: (jnp.maximum(cmain(i) * hb - 1, 0), 0)),
        pl.BlockSpec(hblk, lambda i: (jnp.minimum((cmain(i) + 1) * hb, n_c8), 0)),
        pl.BlockSpec((1, 1, BATCH, D_MODEL), lambda i: (GRID_ROWS - 1, jnp.maximum(lmain(i) - 1, 0), 0, 0)),
        pl.BlockSpec((1, 1, BATCH, D_MODEL), lambda i: (0, jnp.minimum(lmain(i) + 1, GRID_W - 1), 0, 0)),
        _const_spec((1, D_MODEL)), _const_spec(sc2.shape), _const_spec(sh2.shape),
        _const_spec(w_rw.shape),
        _const_spec(mu.shape), _const_spec(w2.shape), _const_spec(w0.shape),
        _const_spec(a2.shape), _const_spec(a0.shape), _const_spec(g2.shape),
        _const_spec(k_k.shape), _const_spec(k_a.shape), _const_spec(r_k.shape),
        _const_spec(ones.shape),
    ]
    n_rows = T_ALL * BATCH
    n_lat = SEQ * BATCH
    shared = pl.BlockSpec((N_PAIRS, PREP_ROWS, PAIR), lambda i: (0, i, 0))
    perdir = pl.BlockSpec((2, N_PAIRS, PREP_ROWS, PAIR), lambda i: (0, 0, i, 0))
    latonly = pl.BlockSpec((PREP_ROWS, W), lambda i: (lmain(i), 0))
    return pl.pallas_call(
        _prep_kernel,
        grid=(PREP_BLOCKS,),
        in_specs=in_specs,
        out_specs=[shared, shared, shared, perdir, perdir, perdir, latonly, latonly],
        out_shape=[jax.ShapeDtypeStruct((N_PAIRS, n_rows, PAIR), F32)] * 3
        + [jax.ShapeDtypeStruct((2, N_PAIRS, n_rows, PAIR), F32)] * 3
        + [jax.ShapeDtypeStruct((n_lat, W), F32)] * 2,
        scratch_shapes=[pltpu.VMEM((PREP_ROWS + 2 * BATCH, RWKV_IN_PAD), F32)],
        compiler_params=_params(1),
        name="rwkv_prep",
    )(xc_tb, xl4, xc_tb, xc_tb, xl4, xl4, g_mix, sc2, sh2, w_rw,
      mu, w2, w0, a2, a0, g2, k_k, k_a, r_k, ones)


WKV_GROUP = 8
WKV_SKEW = 1


def _wkv_chunks(loaders, states, decays, sign):
    T = CHUNK_T
    R2 = 2 * T
    nb = len(loaders)
    levels = int(math.log2(T))

    lane = lax.broadcasted_iota(jnp.int32, (R2, PAIR), 1)
    row = lax.broadcasted_iota(jnp.int32, (R2, PAIR), 0)
    head_mask = (lane // RWKV_HEAD) == (row // T)

    def stack_masked(x):
        return jnp.where(head_mask, jnp.concatenate([x, x], axis=0), 0.0).astype(BF16)

    tw = lax.broadcasted_iota(jnp.int32, (T, 2 * R2), 0)
    sw = lax.broadcasted_iota(jnp.int32, (T, 2 * R2), 1) % T
    dtw = (tw - sw) * sign
    strict_w = dtw[:, 0:R2] > 0
    incl_w2 = dtw >= 0
    eye_w = jnp.where(dtw[:, 0:R2] == 0, 1.0, 0.0)

    nt = lambda x, y: lax.dot_general(x, y, (((1,), (1,)), ((), ())), preferred_element_type=F32)
    mm = lambda x, y: jnp.dot(x, y, preferred_element_type=F32)

    def chain(load, S, decay, out):
        a_t, r_t, b_t, k_t, b_e, k_e, v = load()
        ar = jnp.concatenate([a_t, r_t], axis=0).astype(BF16)
        bk = jnp.concatenate([stack_masked(b_t), stack_masked(k_t)], axis=0)
        big = nt(ar, bk)
        fs = nt(ar, S.astype(BF16))
        uv_rhs = jnp.concatenate([b_e, k_e], axis=0).astype(BF16)
        yield
        p_w = jnp.where(strict_w, big[0:T, 0:R2], 0.0)
        a_ak = jnp.where(strict_w, big[0:T, R2:2 * R2], 0.0).astype(BF16)
        a_rbk = jnp.where(incl_w2, big[T:R2], 0.0).astype(BF16)
        v_st = stack_masked(v)
        h = mm(a_ak, v_st)
        inv_w = eye_w + p_w
        p_w = mm(p_w.astype(BF16), stack_masked(p_w))
        yield
        for lvl in range(1, levels):
            p_bd = stack_masked(p_w)
            if lvl == levels - 1:
                inv_w = inv_w + mm(inv_w.astype(BF16), p_bd)
            else:
                both = mm(jnp.concatenate([p_w, inv_w], axis=0).astype(BF16), p_bd)
                yield
                p_w = both[0:T]
                inv_w = inv_w + both[T:R2]
        yield
        u = mm(inv_w.astype(BF16), stack_masked(fs[0:T] + h))
        yield
        y = fs[T:R2] + mm(a_rbk, jnp.concatenate([stack_masked(u), v_st], axis=0))
        uv_t = jnp.concatenate([u, v], axis=0).T.astype(BF16)
        out.append((y, S * decay + jnp.where(head_mask, mm(uv_t, uv_rhs), 0.0)))

    outs = [[] for _ in range(nb)]
    chains = [chain(loaders[i], states[i], decays[i], outs[i]) for i in range(nb)]
    groups = [chains[g:g + WKV_GROUP] for g in range(0, nb, WKV_GROUP)]
    live = [True] * len(groups)
    tick = 0
    while any(live):
        for g, group in enumerate(groups):
            if live[g] and tick >= g * WKV_SKEW:
                for ch in group:
                    if next(ch, "done") == "done":
                        live[g] = False
        tick += 1
    return [o[0][0] for o in outs], [o[0][1] for o in outs]


WKV_PAIRS_PER_STEP = 4


def _wkv_kernel(r_ref, v_ref, kn_ref, ld_ref, kd_ref, b_ref, y_ref, s_ref, cum_s):
    d = pl.program_id(0)
    c = pl.program_id(2)
    sign = 1 - 2 * d

    @pl.when(c == 0)
    def _():
        s_ref[...] = jnp.zeros_like(s_ref)

    tot8, dec8 = [], []
    for pi in range(WKV_PAIRS_PER_STEP):
        ld = ld_ref[0, pi]
        cum_up = ld
        for lvl in range(int(math.log2(CHUNK_T))):
            sh = BATCH << lvl
            cum_up = cum_up + jnp.concatenate([jnp.zeros((sh, PAIR), F32), cum_up[:-sh]], axis=0)
        tot8.append(cum_up[CHUNK_ROWS - BATCH:])
        tot = _bcast_rows(tot8[pi], CHUNK_ROWS)
        cum_s[pi] = jnp.where(d == 0, cum_up, tot - cum_up + ld)
        dec8.append(jnp.exp(tot8[pi]))

    problems = [(pl.ds(bi, CHUNK_T, stride=BATCH), pi, bi)
                for pi in range(WKV_PAIRS_PER_STEP) for bi in range(BATCH)]
    def loader(rw, pi, bi):
        def load():
            cum = cum_s[pi, rw, :]
            ld_b = ld_ref[0, pi, rw, :]
            b = b_ref[0, pi, rw, :]
            kd = kd_ref[0, pi, rw, :]
            e_out = jnp.exp(-cum)
            e_end = jnp.exp(tot8[pi][bi:bi + 1] - cum)
            return (-kn_ref[pi, rw, :] * jnp.exp(cum - ld_b), r_ref[pi, rw, :] * jnp.exp(cum),
                    b * e_out, kd * e_out, b * e_end, kd * e_end, v_ref[pi, rw, :])
        return load

    ys, s_new = _wkv_chunks([loader(*p) for p in problems],
                            [s_ref[pi, bi] for _, pi, bi in problems],
                            [dec8[pi][bi:bi + 1] for _, pi, bi in problems], sign)
    for k, (rw, pi, bi) in enumerate(problems):
        y_ref[0, pi, rw, :] = ys[k]
        s_ref[pi, bi] = s_new[k]


def _wkv(r, v, kn, ld, kd, b):
    def chunk(d, c):
        fwd = c
        bwd = jnp.where(c < N_CTX_CHUNKS, N_CTX_CHUNKS - 1 - c, N_CHUNKS + N_CTX_CHUNKS - 1 - c)
        return jnp.where(d == 0, fwd, bwd)
    lat = lambda d, c: chunk(d, jnp.maximum(c, N_CTX_CHUNKS)) - N_CTX_CHUNKS
    pps = WKV_PAIRS_PER_STEP
    shared = pl.BlockSpec((pps, CHUNK_ROWS, PAIR), lambda d, p, c: (p, chunk(d, c), 0))
    perdir = pl.BlockSpec((1, pps, CHUNK_ROWS, PAIR), lambda d, p, c: (d, p, chunk(d, c), 0))
    return pl.pallas_call(
        _wkv_kernel,
        grid=(2, N_PAIRS // pps, N_CHUNKS),
        in_specs=[shared, shared, shared, perdir, perdir, perdir],
        out_specs=pl.BlockSpec((1, pps, CHUNK_ROWS, PAIR), lambda d, p, c: (d, p, lat(d, c), 0)),
        out_shape=jax.ShapeDtypeStruct((2, N_PAIRS, SEQ * BATCH, PAIR), F32),
        scratch_shapes=[pltpu.VMEM((pps, BATCH, PAIR, PAIR), F32),
                        pltpu.VMEM((pps, CHUNK_ROWS, PAIR), F32)],
        compiler_params=_params(3),
        name="wkv_scan",
    )(r, v, kn, ld, kd, b)


MERGE_T = 32
MERGE_ROWS = MERGE_T * BATCH


def _merge_kernel(lru_ref, y_ref, bon_ref, g_ref, x_ref, gmix_ref, mod_ref,
                  lng_ref, lnb_ref, ones_ref, wgate_ref, wol_ref, wor_ref, wout_ref, o_ref):
    rows = MERGE_ROWS
    W = RWKV_WIDTH
    from_lru = _bdot(lru_ref[...], wol_ref[...])
    x = x_ref[...]
    h = _rms_modulate(x, gmix_ref[...], mod_ref[1], mod_ref[0])
    gates = _bdot(h, wgate_ref[...])
    y2 = (y_ref[0] + y_ref[1]).reshape(N_PAIRS, rows, PAIR)
    y = jnp.concatenate([y2[s] for s in range(N_PAIRS)], axis=1)
    inv_n = 1.0 / RWKV_HEAD

    ones = ones_ref[...]
    y_hi = y.astype(BF16).astype(F32)
    mu = (_head_sums(y_hi, ones) + _head_sums(y - y_hi, ones)) * inv_n
    dy = y - mu
    var = _head_sums(dy * dy, ones) * inv_n
    yn = dy * lax.rsqrt(var + GN_EPS) * lng_ref[...] + lnb_ref[...]
    rw = (yn + bon_ref[...].reshape(rows, W)) * g_ref[...].reshape(rows, W)
    m = (_sigmoid(gates[:, :D_MODEL]) * from_lru
         + _sigmoid(gates[:, D_MODEL:]) * _bdot(rw, wor_ref[...]))
    mix = _bdot(m, wout_ref[...])
    o_ref[...] = x + _bcast_rows(mod_ref[2], rows) * mix


def _merge(lru_l, y, bon, g, x_tb, g_mix, mod_m, ln_g, ln_b, ones, wgate, wol, wor, wout):
    W = RWKV_WIDTH
    n_blocks = SEQ // MERGE_T
    per_row = GRID_W // MERGE_T
    cm4 = lambda i: (i % per_row, i // per_row, 0, 0)
    y6 = y.reshape(2, N_PAIRS, GRID_W, GRID_ROWS, BATCH, PAIR)
    bon4 = bon.reshape(GRID_W, GRID_ROWS, BATCH, W)
    g4 = g.reshape(GRID_W, GRID_ROWS, BATCH, W)
    return pl.pallas_call(
        _merge_kernel,
        grid=(n_blocks,),
        in_specs=[
            pl.BlockSpec((MERGE_ROWS, LRU_WIDTH), lambda i: (i, 0)),
            pl.BlockSpec((2, N_PAIRS, MERGE_T, 1, BATCH, PAIR), lambda i: (0, 0) + cm4(i)),
            pl.BlockSpec((MERGE_T, 1, BATCH, W), cm4),
            pl.BlockSpec((MERGE_T, 1, BATCH, W), cm4),
            pl.BlockSpec((MERGE_ROWS, D_MODEL), lambda i: (i, 0)),
            _const_spec((1, D_MODEL)), _const_spec((3, BATCH, D_MODEL)),
            _const_spec((1, W)), _const_spec((1, W)), _const_spec(ones.shape),
            _const_spec(wgate.shape), _const_spec(wol.shape), _const_spec(wor.shape),
            _const_spec(wout.shape),
        ],
        out_specs=pl.BlockSpec((MERGE_ROWS, D_MODEL), lambda i: (i, 0)),
        out_shape=jax.ShapeDtypeStruct((SEQ * BATCH, D_MODEL), F32),
        compiler_params=_params(1),
        name="merge",
    )(lru_l, y6, bon4, g4, x_tb, g_mix, mod_m, ln_g, ln_b, ones, wgate, wol, wor, wout)


FFN_T = 64
FFN_ROWS = FFN_T * BATCH
FFN_TILE = 256


def _ffn_up_kernel(x_ref, g_ref, sc_ref, sh_ref, w_ref, o_ref):
    h = _rms_modulate(x_ref[...], g_ref[...], sc_ref[...], sh_ref[...]).astype(BF16)
    for j in range(D_FF // FFN_TILE):
        cs = slice(j * FFN_TILE, (j + 1) * FFN_TILE)
        gate = jnp.dot(h, w_ref[:, cs], preferred_element_type=F32)
        up = jnp.dot(h, w_ref[:, D_FF + j * FFN_TILE:D_FF + (j + 1) * FFN_TILE],
                     preferred_element_type=F32)
        o_ref[:, cs] = (_silu(gate) * up).astype(BF16)


def _ffn_up(x1, g, sc8, sh8, w_in):
    return pl.pallas_call(
        _ffn_up_kernel,
        grid=(SEQ * BATCH // FFN_ROWS,),
        in_specs=[pl.BlockSpec((FFN_ROWS, D_MODEL), lambda i: (i, 0)),
                  _const_spec((1, D_MODEL)), _const_spec((BATCH, D_MODEL)),
                  _const_spec((BATCH, D_MODEL)), _const_spec(w_in.shape)],
        out_specs=pl.BlockSpec((FFN_ROWS, D_FF), lambda i: (i, 0)),
        out_shape=jax.ShapeDtypeStruct((SEQ * BATCH, D_FF), BF16),
        compiler_params=_params(1),
        name="ffn_up",
    )(x1, g, sc8, sh8, w_in)


def _ffn_down_kernel(act_ref, x_ref, gf_ref, w_ref, gfin_ref, o_ref, stage_s):
    y = jnp.dot(act_ref[...], w_ref[...], preferred_element_type=F32)
    x2 = x_ref[...] + _bcast_rows(gf_ref[...], FFN_ROWS) * y
    ms = jnp.mean(x2 * x2, axis=-1, keepdims=True)
    out = x2 * lax.rsqrt(ms + RMS_EPS) * gfin_ref[...]
    for s in range(D_TILES):
        stage_s[s] = out[:, s * LANE:(s + 1) * LANE]
    for bi in range(BATCH):
        rows = pl.ds(bi, FFN_T, stride=BATCH)
        for s in range(D_TILES):
            o_ref[bi, :, s * LANE:(s + 1) * LANE] = stage_s[s, rows, :]


def _ffn_down(act, x1, g_f8, w_out, g_final):
    return pl.pallas_call(
        _ffn_down_kernel,
        grid=(SEQ * BATCH // FFN_ROWS,),
        in_specs=[pl.BlockSpec((FFN_ROWS, D_FF), lambda i: (i, 0)),
                  pl.BlockSpec((FFN_ROWS, D_MODEL), lambda i: (i, 0)),
                  _const_spec((BATCH, D_MODEL)), _const_spec(w_out.shape),
                  _const_spec((1, D_MODEL))],
        out_specs=pl.BlockSpec((BATCH, FFN_T, D_MODEL), lambda i: (0, i, 0)),
        out_shape=jax.ShapeDtypeStruct((BATCH, SEQ, D_MODEL), F32),
        scratch_shapes=[pltpu.VMEM((D_TILES, FFN_ROWS, LANE), F32)],
        compiler_params=_params(1),
        name="ffn_down",
    )(act, x1, g_f8, w_out, g_final)


def _block_diag(w):
    n, c, _ = w.shape
    tiled = jnp.tile(w.reshape(n * c, c), (1, n))
    rb = lax.broadcasted_iota(jnp.int32, (n * c, n * c), 0) // c
    cb = lax.broadcasted_iota(jnp.int32, (n * c, n * c), 1) // c
    return jnp.where(rb == cb, tiled, 0.0)


def _lru_gate_tiles(wa, wx):
    da, dx = _block_diag(0.5 * wa), _block_diag(0.5 * wx)
    tiles = []
    for j, k0 in enumerate(LRU_GATE_K0):
        cs = slice(j * LRU_GATE_TILE, (j + 1) * LRU_GATE_TILE)
        tiles.append(jnp.concatenate([da[k0:k0 + LRU_GATE_K, cs], dx[k0:k0 + LRU_GATE_K, cs]], axis=1))
    return jnp.stack(tiles).astype(BF16)


def _two_dir_lora(w):
    z = jnp.zeros_like(w[0])
    return jnp.concatenate([jnp.concatenate([w[0], z], axis=1),
                            jnp.concatenate([z, w[1]], axis=1)], axis=0).astype(BF16)


def kernel(x, c, ctx, c_ctx, norm_mix_g, norm_ffn_g, w_mod, b_mod, w_in, lru_conv_w, lru_conv_b, lru_wa, lru_ba, lru_wx, lru_bx, lru_lambda, w_o_lru, rwkv_mu, rwkv_w0, rwkv_w2, rwkv_a0, rwkv_a2, rwkv_g2, rwkv_k_k, rwkv_k_a, rwkv_r_k, rwkv_ln_g, rwkv_ln_b, w_o_rwkv, w_out, w_ffn_in, w_ffn_out, norm_final_g):
    assert x.shape == (BATCH, SEQ, D_MODEL) and ctx.shape == (BATCH, CTX_LEN, D_MODEL)
    assert w_mod.shape[0] == 1, "single layer only"
    D, W = D_MODEL, RWKV_WIDTH

    c16 = jnp.concatenate([c, c_ctx[None], jnp.zeros((16 - BATCH - 1, D), F32)], axis=0)
    mod = _adaln(c16, w_mod[0], b_mod[0][None])
    mod_lat = mod[:BATCH].reshape(BATCH, 6, D)
    mod_ctx = jnp.broadcast_to(mod[BATCH:BATCH + 1], (BATCH, 6 * D)).reshape(BATCH, 6, D)
    sh_m, sc_m, g_m, sh_f, sc_f, g_f = [mod_lat[:, k] for k in range(6)]
    sh2 = jnp.stack([mod_ctx[:, 0], sh_m])
    sc2 = jnp.stack([mod_ctx[:, 1], sc_m])

    w_in0 = w_in[0]
    n_lru = 2 * LRU_WIDTH
    w_lru = w_in0[:, :n_lru].astype(BF16)
    w_rw = jnp.pad(w_in0[:, n_lru:n_lru + RWKV_IN], ((0, 0), (0, RWKV_IN_PAD - RWKV_IN))).astype(BF16)
    w_gate = w_in0[:, n_lru + RWKV_IN:].astype(BF16)
    g_mix = norm_mix_g[0][None]

    u_all, xc_tb, xl_tb = _proj(ctx, x, g_mix, sc2, sh2, w_lru)

    wt = jnp.stack([_lru_gate_tiles(lru_wa[0, d], lru_wx[0, d]) for d in range(2)])
    lru_p = (lru_conv_w[0], lru_conv_b[0], wt, 0.5 * lru_ba[0], 0.5 * lru_bx[0], lru_lambda[0])
    hf = _lru_scan(u_all, None, lru_p, 0, reverse=False)
    lru_l = _lru_scan(u_all, hf, lru_p, 1, reverse=True)

    mu_pad = jnp.pad(rwkv_mu[0], ((0, 0), (0, RWKV_IN_PAD - RWKV_IN)))
    ones = _block_diag(jnp.ones((PAIR // RWKV_HEAD, RWKV_HEAD, RWKV_HEAD), F32)).astype(BF16)
    rw_p = (mu_pad, _two_dir_lora(rwkv_w2[0]), rwkv_w0[0].reshape(1, 2 * W),
            _two_dir_lora(rwkv_a2[0]), rwkv_a0[0].reshape(1, 2 * W),
            jnp.pad(rwkv_g2[0], ((0, LORA_G_PAD - LORA_G), (0, 0))).astype(BF16),
            rwkv_k_k[0][None], rwkv_k_a[0][None], rwkv_r_k[0].reshape(1, W), ones)
    r, v, kn, ld, kd, b, bon, g = _rwkv_prep(xc_tb, xl_tb, g_mix, sc2, sh2, w_rw, rw_p)
    y = _wkv(r, v, kn, ld, kd, b)

    x1 = _merge(lru_l, y, bon, g, xl_tb, g_mix, jnp.stack([sh_m, sc_m, g_m]),
                rwkv_ln_g[0][None], rwkv_ln_b[0][None], ones, w_gate,
                w_o_lru[0].astype(BF16), w_o_rwkv[0].astype(BF16), w_out[0].astype(BF16))

    act = _ffn_up(x1, norm_ffn_g[0][None], sc_f, sh_f, w_ffn_in[0].astype(BF16))
    return _ffn_down(act, x1, g_f, w_ffn_out[0].astype(BF16), norm_final_g[None])
```

```python
import functools
import math

import jax
import jax.numpy as jnp
from jax import lax
from jax.experimental import pallas as pl
from jax.experimental.pallas import tpu as pltpu

F32 = jnp.float32
BF16 = jnp.bfloat16

D_MODEL = 1024
BATCH = 8
SEQ = 2048
CTX_LEN = 256
GRID_W = 64
GRID_ROWS = SEQ // GRID_W

LRU_WIDTH = 1280
LRU_BLOCKS = 16
LRU_BLOCK = LRU_WIDTH // LRU_BLOCKS
LRU_CONV = 4
LRU_C = 8.0

RWKV_HEAD = 64
RWKV_WIDTH = 1024
LORA_W = 64
LORA_A = 64
LORA_G = 160
RWKV_IN = 3 * RWKV_WIDTH + 2 * LORA_W + 2 * LORA_A + LORA_G
RWKV_IN_PAD = 3584
LORA_G_PAD = RWKV_IN_PAD - (3 * RWKV_WIDTH + 2 * LORA_W + 2 * LORA_A)
D_FF = 2816

RMS_EPS = 1e-6
GN_EPS = 64e-5
L2_EPS = 1e-12

T_ALL = CTX_LEN + SEQ
CHUNK_T = 64
CHUNK_ROWS = CHUNK_T * BATCH
N_CTX_CHUNKS = CTX_LEN // CHUNK_T
N_LAT_CHUNKS = SEQ // CHUNK_T
N_CHUNKS = N_CTX_CHUNKS + N_LAT_CHUNKS

LANE = 128
D_TILES = D_MODEL // LANE
PAIR = 2 * RWKV_HEAD
N_PAIRS = RWKV_WIDTH // PAIR

LRU_GATE_TILE = 256
LRU_GATE_K = 512
LRU_GATE_K0 = (0, 128, 384, 640, 768)

VMEM_LIMIT = 56 * 1024 * 1024


def _params(n_axes):
    return pltpu.CompilerParams(dimension_semantics=("arbitrary",) * n_axes,
                                vmem_limit_bytes=VMEM_LIMIT)


def _const_spec(shape):
    nd = len(shape)
    return pl.BlockSpec(shape, lambda *_: (0,) * nd)


def _bdot(a, b):
    return jnp.dot(a.astype(BF16), b.astype(BF16), preferred_element_type=F32)


def _softplus(x):
    return jnp.maximum(x, 0.0) + jnp.log1p(jnp.exp(-jnp.abs(x)))


def _sigmoid(x):
    return 0.5 * jnp.tanh(0.5 * x) + 0.5


def _head_sums(t, ones_pair):
    tb = t.astype(BF16)
    return jnp.concatenate(
        [jnp.dot(tb[:, s * PAIR:(s + 1) * PAIR], ones_pair, preferred_element_type=F32)
         for s in range(t.shape[1] // PAIR)], axis=1)


def _silu(x):
    return x * _sigmoid(x)


def _gelu_tanh(x):
    c = math.sqrt(2.0 / math.pi)
    return 0.5 * x * (1.0 + jnp.tanh(c * (x + 0.044715 * (x * x * x))))


def _bcast_rows(v8, rows):
    c = v8.shape[-1]
    return jnp.broadcast_to(v8[None], (rows // BATCH, BATCH, c)).reshape(rows, c)


def _rms_modulate(x, g, scale8, shift8):
    rows = x.shape[0]
    ms = jnp.mean(x * x, axis=-1, keepdims=True)
    y = x * lax.rsqrt(ms + RMS_EPS) * g
    return y * (1.0 + _bcast_rows(scale8, rows)) + _bcast_rows(shift8, rows)


def _adaln_kernel(c_ref, w_ref, b_ref, o_ref):
    s = _silu(c_ref[...])
    o_ref[...] = jnp.dot(s, w_ref[...], preferred_element_type=F32,
                         precision=lax.Precision.HIGHEST) + b_ref[...]


def _adaln(c16, w_mod, b_mod):
    n = w_mod.shape[1]
    tn = 1536
    return pl.pallas_call(
        _adaln_kernel,
        grid=(n // tn,),
        in_specs=[_const_spec((16, D_MODEL)),
                  pl.BlockSpec((D_MODEL, tn), lambda j: (0, j)),
                  pl.BlockSpec((1, tn), lambda j: (0, j))],
        out_specs=pl.BlockSpec((16, tn), lambda j: (0, j)),
        out_shape=jax.ShapeDtypeStruct((16, n), F32),
        compiler_params=_params(1),
        name="adaln",
    )(c16, w_mod, b_mod)


LANE = 128
D_TILES = D_MODEL // LANE


PROJ_GROUPS = 4
PROJ_T = CHUNK_T // PROJ_GROUPS
PROJ_ROWS = PROJ_T * BATCH


def _proj_kernel(ctx_ref, x_ref, g_ref, sc_ref, sh_ref, w_ref, o_ref, xc_o, xl_o, stage_s):
    is_ctx = pl.program_id(0) < N_CTX_CHUNKS

    def stage(gi):
        t0 = gi * PROJ_T
        for bi in range(BATCH):
            rows = pl.ds(gi * PROJ_ROWS + bi, PROJ_T, stride=BATCH)
            for s in range(D_TILES):
                ln = slice(s * LANE, (s + 1) * LANE)
                stage_s[s, rows, :] = jnp.where(is_ctx, ctx_ref[bi, t0:t0 + PROJ_T, ln],
                                                x_ref[bi, t0:t0 + PROJ_T, ln])

    def staged(rows):
        return jnp.concatenate([stage_s[s, rows, :] for s in range(D_TILES)], axis=1)

    def normalise(gi):
        rows = slice(gi * PROJ_ROWS, (gi + 1) * PROJ_ROWS)
        x = staged(rows)
        xl_o[rows, :] = x
        return _rms_modulate(x, g_ref[...], sc_ref[0], sh_ref[0]).astype(BF16)

    h = None
    for gi in range(PROJ_GROUPS + 1):
        if gi < PROJ_GROUPS:
            stage(gi)
        if gi > 0:
            o_ref[(gi - 1) * PROJ_ROWS:gi * PROJ_ROWS, :] = jnp.dot(
                h, w_ref[...], preferred_element_type=F32)
        if gi < PROJ_GROUPS:
            h = normalise(gi)

    @pl.when(is_ctx)
    def _():
        xc_o[...] = staged(slice(0, CHUNK_ROWS))


def _proj(ctx, x, g, sc2, sh2, w):
    n = w.shape[1]
    seg = lambda i: (i >= N_CTX_CHUNKS).astype(jnp.int32)
    cchunk = lambda i: jnp.minimum(i, N_CTX_CHUNKS - 1)
    lchunk = lambda i: jnp.maximum(i - N_CTX_CHUNKS, 0)
    return pl.pallas_call(
        _proj_kernel,
        grid=(N_CHUNKS,),
        in_specs=[pl.BlockSpec((BATCH, CHUNK_T, D_MODEL), lambda i: (0, cchunk(i), 0)),
                  pl.BlockSpec((BATCH, CHUNK_T, D_MODEL), lambda i: (0, lchunk(i), 0)),
                  _const_spec((1, D_MODEL)),
                  pl.BlockSpec((1, BATCH, D_MODEL), lambda i: (seg(i), 0, 0)),
                  pl.BlockSpec((1, BATCH, D_MODEL), lambda i: (seg(i), 0, 0)),
                  _const_spec((D_MODEL, n))],
        out_specs=[pl.BlockSpec((CHUNK_ROWS, n), lambda i: (i, 0)),
                   pl.BlockSpec((CHUNK_ROWS, D_MODEL), lambda i: (cchunk(i), 0)),
                   pl.BlockSpec((CHUNK_ROWS, D_MODEL), lambda i: (lchunk(i), 0))],
        out_shape=[jax.ShapeDtypeStruct((N_CHUNKS * CHUNK_ROWS, n), F32),
                   jax.ShapeDtypeStruct((CTX_LEN * BATCH, D_MODEL), F32),
                   jax.ShapeDtypeStruct((SEQ * BATCH, D_MODEL), F32)],
        scratch_shapes=[pltpu.VMEM((D_TILES, CHUNK_ROWS, LANE), F32)],
        compiler_params=_params(1),
        name="in_proj",
    )(ctx, x, g, sc2, sh2, w)


def _lru_kernel(*refs, reverse):
    if reverse:
        (u_ref, uy_ref, hf_ref, cw_ref, cb_ref, wt_ref, ba_ref, bx_ref, lam_ref,
         o_ref, h_s, halo_s, ext_s, a_s, b_s) = refs
    else:
        (u_ref, cw_ref, cb_ref, wt_ref, ba_ref, bx_ref, lam_ref,
         o_ref, h_s, halo_s, ext_s, a_s, b_s) = refs
    i = pl.program_id(0)
    halo_rows = (LRU_CONV - 1) * BATCH

    @pl.when(i == 0)
    def _():
        h_s[...] = jnp.zeros_like(h_s)

    @pl.when((i == 0) | (i == N_CTX_CHUNKS))
    def _():
        halo_s[...] = jnp.zeros_like(halo_s)

    u = u_ref[...]
    if reverse:
        ext_s[0:CHUNK_ROWS] = u
        ext_s[CHUNK_ROWS:CHUNK_ROWS + halo_rows] = halo_s[...]
        halo_s[...] = u[0:halo_rows]
    else:
        ext_s[0:halo_rows] = halo_s[...]
        ext_s[halo_rows:halo_rows + CHUNK_ROWS] = u
        halo_s[...] = u[CHUNK_ROWS - halo_rows:CHUNK_ROWS]

    xc = jnp.broadcast_to(cb_ref[...], (CHUNK_ROWS, LRU_WIDTH))
    for j in range(LRU_CONV):
        off = (LRU_CONV - 1 - j) if reverse else j
        xc = xc + cw_ref[j:j + 1, :] * ext_s[off * BATCH:off * BATCH + CHUNK_ROWS]

    xcb = xc.astype(BF16)
    c_half = (-0.5 * LRU_C) * _softplus(-lam_ref[...])
    for j in range(LRU_WIDTH // LRU_GATE_TILE):
        k0 = LRU_GATE_K0[j]
        cs = slice(j * LRU_GATE_TILE, (j + 1) * LRU_GATE_TILE)
        g = jnp.dot(xcb[:, k0:k0 + LRU_GATE_K], wt_ref[j], preferred_element_type=F32)
        tanh_r = jnp.tanh(g[:, :LRU_GATE_TILE] + ba_ref[:, cs])
        gate_i = 0.5 * jnp.tanh(g[:, LRU_GATE_TILE:] + bx_ref[:, cs]) + 0.5
        log_a = c_half[:, cs] * tanh_r + c_half[:, cs]
        a = jnp.exp(log_a)
        a_s[:, cs] = a
        om = 1.0 - a * a
        b_s[:, cs] = jnp.where(om > 0.0, om * lax.rsqrt(om), 0.0) * (gate_i * xc[:, cs])

    def step(k, h):
        t = (CHUNK_T - 1 - k) if reverse else k
        rows = pl.ds(pl.multiple_of(t * BATCH, BATCH), BATCH)
        h = a_s[rows, :] * h + b_s[rows, :]
        if reverse:
            o_ref[rows, :] = (hf_ref[rows, :] + h) * _gelu_tanh(uy_ref[rows, :])
        else:
            o_ref[rows, :] = h
        return h

    h_s[...] = lax.fori_loop(0, CHUNK_T, step, h_s[...], unroll=8)


def _lru_scan(u_all, hf, lru_p, d, *, reverse):
    cw, cb, wt, ba, bx, lam = lru_p
    if reverse:
        chunk = lambda i: jnp.where(i < N_CTX_CHUNKS, N_CTX_CHUNKS - 1 - i,
                                    N_CHUNKS + N_CTX_CHUNKS - 1 - i)
    else:
        chunk = lambda i: i
    lat = lambda i: jnp.maximum(chunk(jnp.maximum(i, N_CTX_CHUNKS)) - N_CTX_CHUNKS, 0)
    blk = (CHUNK_ROWS, LRU_WIDTH)
    in_specs = [pl.BlockSpec(blk, lambda i: (chunk(i), 0))]
    args = [u_all]
    if reverse:
        in_specs += [pl.BlockSpec(blk, lambda i: (lat(i) + N_CTX_CHUNKS, 1)),
                     pl.BlockSpec(blk, lambda i: (lat(i), 0))]
        args += [u_all, hf]
    in_specs += [_const_spec((LRU_CONV, LRU_WIDTH)), _const_spec((1, LRU_WIDTH)),
                 _const_spec(wt[d].shape), _const_spec((1, LRU_WIDTH)),
                 _const_spec((1, LRU_WIDTH)), _const_spec((1, LRU_WIDTH))]
    args += [cw[d], cb[d][None], wt[d], ba[d][None], bx[d][None], lam[d][None]]
    halo_rows = (LRU_CONV - 1) * BATCH
    return pl.pallas_call(
        functools.partial(_lru_kernel, reverse=reverse),
        grid=(N_CHUNKS,),
        in_specs=in_specs,
        out_specs=pl.BlockSpec(blk, lambda i: (lat(i), 0)),
        out_shape=jax.ShapeDtypeStruct((SEQ * BATCH, LRU_WIDTH), F32),
        scratch_shapes=[pltpu.VMEM((BATCH, LRU_WIDTH), F32),
                        pltpu.VMEM((halo_rows, LRU_WIDTH), F32),
                        pltpu.VMEM((CHUNK_ROWS + halo_rows, LRU_WIDTH), F32),
                        pltpu.VMEM(blk, F32),
                        pltpu.VMEM(blk, F32)],
        compiler_params=_params(1),
        name="lru_bwd" if reverse else "lru_fwd",
    )(*args)


PREP_T = 32
PREP_ROWS = PREP_T * BATCH
PREP_CTX_BLOCKS = CTX_LEN // PREP_T
PREP_LAT_BLOCKS = SEQ // PREP_T
PREP_BLOCKS = PREP_CTX_BLOCKS + PREP_LAT_BLOCKS


def _prep_kernel(xc_ref, xl_ref, xcp_ref, xcn_ref, xlp_ref, xln_ref, gmix_ref, sc_ref, sh_ref,
                 win_ref, mu_ref,
                 w2_ref, w0_ref, a2_ref, a0_ref, g2_ref, kk_ref, ka_ref, rk_ref, ones_ref,
                 r_o, v_o, kn_o, ld_o, kd_o, b_o, bon_o, g_o, ext_s):
    i = pl.program_id(0)
    is_ctx = i < PREP_CTX_BLOCKS
    first = (i == 0) | (i == PREP_CTX_BLOCKS)
    last = (i == PREP_CTX_BLOCKS - 1) | (i == PREP_BLOCKS - 1)
    lat_rows = lambda ref: ref[...].reshape(-1, D_MODEL)
    x = jnp.concatenate([
        jnp.where(is_ctx, xcp_ref[...], lat_rows(xlp_ref)),
        jnp.where(is_ctx, xc_ref[...], lat_rows(xl_ref)),
        jnp.where(is_ctx, xcn_ref[...], lat_rows(xln_ref))], axis=0)
    seg = jnp.where(is_ctx, 0, 1)
    h = _rms_modulate(x, gmix_ref[...], sc_ref[seg], sh_ref[seg])
    hb = h.astype(BF16)
    W = RWKV_WIDTH
    ones = ones_ref[...]

    def project(c0, c1):
        zt = jnp.dot(hb, win_ref[:, c0:c1], preferred_element_type=F32)
        ext_s[0:BATCH, c0:c1] = jnp.where(first, 0.0, zt[0:BATCH])
        ext_s[BATCH:BATCH + PREP_ROWS, c0:c1] = zt[BATCH:BATCH + PREP_ROWS]
        ext_s[BATCH + PREP_ROWS:2 * BATCH + PREP_ROWS, c0:c1] = jnp.where(
            last, 0.0, zt[BATCH + PREP_ROWS:2 * BATCH + PREP_ROWS])

    def shifted(c0, c1):
        zc = ext_s[BATCH:BATCH + PREP_ROWS, c0:c1]
        zprev = ext_s[0:PREP_ROWS, c0:c1]
        znext = ext_s[2 * BATCH:2 * BATCH + PREP_ROWS, c0:c1]
        return zc + mu_ref[0:1, c0:c1] * (zprev - zc) + mu_ref[1:2, c0:c1] * (znext - zc)

    def project_pairs(g):
        for part in range(3):
            project(part * W + 2 * g * PAIR, part * W + (2 * g + 2) * PAIR)

    def finish_pair(s, w_pre, a_pre):
        ln = slice(s * PAIR, (s + 1) * PAIR)
        r = shifted(s * PAIR, (s + 1) * PAIR)
        k = shifted(W + s * PAIR, W + (s + 1) * PAIR)
        v = shifted(2 * W + s * PAIR, 2 * W + (s + 1) * PAIR)
        kk = k * kk_ref[:, ln]
        ss = jnp.dot((kk * kk).astype(BF16), ones, preferred_element_type=F32)
        kn = kk / jnp.maximum(jnp.sqrt(ss), L2_EPS)
        r_o[s] = r
        v_o[s] = v
        kn_o[s] = kn
        kd_sum = jnp.zeros_like(k)
        for d in range(2):
            cs = slice(d * W + s * PAIR, d * W + (s + 1) * PAIR)
            ld_o[d, s] = -math.exp(-0.5) * _sigmoid(w_pre[:, cs])
            asig = _sigmoid(a_pre[:, cs])
            kd = k * (1.0 + (asig - 1.0) * ka_ref[:, ln])
            kd_o[d, s] = kd
            b_o[d, s] = kn * asig
            kd_sum = kd_sum + kd
        bon = jnp.dot((r * kd_sum * rk_ref[:, ln]).astype(BF16), ones, preferred_element_type=F32)
        bon_o[:, ln] = bon * v

    project(3 * W, RWKV_IN_PAD)
    project_pairs(0)
    wd = shifted(3 * W, 3 * W + 2 * LORA_W)
    ad = shifted(3 * W + 2 * LORA_W, 3 * W + 2 * LORA_W + 2 * LORA_A)
    gd = shifted(3 * W + 2 * LORA_W + 2 * LORA_A, RWKV_IN_PAD)
    w_pre = _bdot(jnp.tanh(wd), w2_ref[...]) + w0_ref[...]
    a_pre = _bdot(ad, a2_ref[...]) + a0_ref[...]
    g_o[...] = _bdot(_sigmoid(gd), g2_ref[...])
    for g in range(N_PAIRS // 2):
        if g + 1 < N_PAIRS // 2:
            project_pairs(g + 1)
        finish_pair(2 * g, w_pre, a_pre)
        finish_pair(2 * g + 1, w_pre, a_pre)


def _rwkv_prep(xc_tb, xl_tb, g_mix, sc2, sh2, w_rw, rw_p):
    mu, w2, w0, a2, a0, g2, k_k, k_a, r_k, ones = rw_p
    W = RWKV_WIDTH
    hb = PREP_T
    cmain = lambda i: jnp.minimum(i, PREP_CTX_BLOCKS - 1)
    lmain = lambda i: jnp.maximum(i - PREP_CTX_BLOCKS, 0)
    assert PREP_T == GRID_ROWS
    xl4 = xl_tb.reshape(GRID_ROWS, GRID_W, BATCH, D_MODEL)
    hblk = (BATCH, D_MODEL)
    n_c8 = CTX_LEN - 1
    in_specs = [
        pl.BlockSpec((PREP_ROWS, D_MODEL), lambda i: (cmain(i), 0)),
        pl.BlockSpec((GRID_ROWS, 1, BATCH, D_MODEL), lambda i: (0, lmain(i), 0, 0)),
        pl.BlockSpec(hblk, lambda i: (jnp.maximum(cmain(i) * hb - 1, 0), 0)),
        pl.BlockSpec(hblk, lambda i: (jnp.minimum((cmain(i) + 1) * hb, n_c8), 0)),
        pl.BlockSpec((1, 1, BATCH, D_MODEL), lambda i: (GRID_ROWS - 1, jnp.maximum(lmain(i) - 1, 0), 0, 0)),
        pl.BlockSpec((1, 1, BATCH, D_MODEL), lambda i: (0, jnp.minimum(lmain(i) + 1, GRID_W - 1), 0, 0)),
        _const_spec((1, D_MODEL)), _const_spec(sc2.shape), _const_spec(sh2.shape),
        _const_spec(w_rw.shape),
        _const_spec(mu.shape), _const_spec(w2.shape), _const_spec(w0.shape),
        _const_spec(a2.shape), _const_spec(a0.shape), _const_spec(g2.shape),
        _const_spec(k_k.shape), _const_spec(k_a.shape), _const_spec(r_k.shape),
        _const_spec(ones.shape),
    ]
    n_rows = T_ALL * BATCH
    n_lat = SEQ * BATCH
    shared = pl.BlockSpec((N_PAIRS, PREP_ROWS, PAIR), lambda i: (0, i, 0))
    perdir = pl.BlockSpec((2, N_PAIRS, PREP_ROWS, PAIR), lambda i: (0, 0, i, 0))
    latonly = pl.BlockSpec((PREP_ROWS, W), lambda i: (lmain(i), 0))
    return pl.pallas_call(
        _prep_kernel,
        grid=(PREP_BLOCKS,),
        in_specs=in_specs,
        out_specs=[shared, shared, shared, perdir, perdir, perdir, latonly, latonly],
        out_shape=[jax.ShapeDtypeStruct((N_PAIRS, n_rows, PAIR), F32)] * 3
        + [jax.ShapeDtypeStruct((2, N_PAIRS, n_rows, PAIR), F32)] * 3
        + [jax.ShapeDtypeStruct((n_lat, W), F32)] * 2,
        scratch_shapes=[pltpu.VMEM((PREP_ROWS + 2 * BATCH, RWKV_IN_PAD), F32)],
        compiler_params=_params(1),
        name="rwkv_prep",
    )(xc_tb, xl4, xc_tb, xc_tb, xl4, xl4, g_mix, sc2, sh2, w_rw,
      mu, w2, w0, a2, a0, g2, k_k, k_a, r_k, ones)


WKV_GROUP = 8
WKV_SKEW = 1


def _wkv_chunks(loaders, states, decays, sign):
    T = CHUNK_T
    R2 = 2 * T
    nb = len(loaders)
    levels = int(math.log2(T))

    lane = lax.broadcasted_iota(jnp.int32, (R2, PAIR), 1)
    row = lax.broadcasted_iota(jnp.int32, (R2, PAIR), 0)
    head_mask = (lane // RWKV_HEAD) == (row // T)

    def stack_masked(x):
        return jnp.where(head_mask, jnp.concatenate([x, x], axis=0), 0.0).astype(BF16)

    tw = lax.broadcasted_iota(jnp.int32, (T, 2 * R2), 0)
    sw = lax.broadcasted_iota(jnp.int32, (T, 2 * R2), 1) % T
    dtw = (tw - sw) * sign
    strict_w = dtw[:, 0:R2] > 0
    incl_w2 = dtw >= 0
    eye_w = jnp.where(dtw[:, 0:R2] == 0, 1.0, 0.0)

    nt = lambda x, y: lax.dot_general(x, y, (((1,), (1,)), ((), ())), preferred_element_type=F32)
    mm = lambda x, y: jnp.dot(x, y, preferred_element_type=F32)

    def chain(load, S, decay, out):
        a_t, r_t, b_t, k_t, b_e, k_e, v = load()
        ar = jnp.concatenate([a_t, r_t], axis=0).astype(BF16)
        bk = jnp.concatenate([stack_masked(b_t), stack_masked(k_t)], axis=0)
        big = nt(ar, bk)
        fs = nt(ar, S.astype(BF16))
        uv_rhs = jnp.concatenate([b_e, k_e], axis=0).astype(BF16)
        yield
        p_w = jnp.where(strict_w, big[0:T, 0:R2], 0.0)
        a_ak = jnp.where(strict_w, big[0:T, R2:2 * R2], 0.0).astype(BF16)
        a_rbk = jnp.where(incl_w2, big[T:R2], 0.0).astype(BF16)
        v_st = stack_masked(v)
        h = mm(a_ak, v_st)
        inv_w = eye_w + p_w
        p_w = mm(p_w.astype(BF16), stack_masked(p_w))
        yield
        for lvl in range(1, levels):
            p_bd = stack_masked(p_w)
            if lvl == levels - 1:
                inv_w = inv_w + mm(inv_w.astype(BF16), p_bd)
            else:
                both = mm(jnp.concatenate([p_w, inv_w], axis=0).astype(BF16), p_bd)
                yield
                p_w = both[0:T]
                inv_w = inv_w + both[T:R2]
        yield
        u = mm(inv_w.astype(BF16), stack_masked(fs[0:T] + h))
        yield
        y = fs[T:R2] + mm(a_rbk, jnp.concatenate([stack_masked(u), v_st], axis=0))
        uv_t = jnp.concatenate([u, v], axis=0).T.astype(BF16)
        out.append((y, S * decay + jnp.where(head_mask, mm(uv_t, uv_rhs), 0.0)))

    outs = [[] for _ in range(nb)]
    chains = [chain(loaders[i], states[i], decays[i], outs[i]) for i in range(nb)]
    groups = [chains[g:g + WKV_GROUP] for g in range(0, nb, WKV_GROUP)]
    live = [True] * len(groups)
    tick = 0
    while any(live):
        for g, group in enumerate(groups):
            if live[g] and tick >= g * WKV_SKEW:
                for ch in group:
                    if next(ch, "done") == "done":
                        live[g] = False
        tick += 1
    return [o[0][0] for o in outs], [o[0][1] for o in outs]


WKV_PAIRS_PER_STEP = 8


def _wkv_kernel(r_ref, v_ref, kn_ref, ld_ref, kd_ref, b_ref, y_ref, s_ref, cum_s):
    d = pl.program_id(0)
    c = pl.program_id(2)
    sign = 1 - 2 * d

    @pl.when(c == 0)
    def _():
        s_ref[...] = jnp.zeros_like(s_ref)

    tot8, dec8 = [], []
    for pi in range(WKV_PAIRS_PER_STEP):
        ld = ld_ref[0, pi]
        cum_up = ld
        for lvl in range(int(math.log2(CHUNK_T))):
            sh = BATCH << lvl
            cum_up = cum_up + jnp.concatenate([jnp.zeros((sh, PAIR), F32), cum_up[:-sh]], axis=0)
        tot8.append(cum_up[CHUNK_ROWS - BATCH:])
        tot = _bcast_rows(tot8[pi], CHUNK_ROWS)
        cum_s[pi] = jnp.where(d == 0, cum_up, tot - cum_up + ld)
        dec8.append(jnp.exp(tot8[pi]))

    problems = [(pl.ds(bi, CHUNK_T, stride=BATCH), pi, bi)
                for pi in range(WKV_PAIRS_PER_STEP) for bi in range(BATCH)]
    def loader(rw, pi, bi):
        def load():
            cum = cum_s[pi, rw, :]
            ld_b = ld_ref[0, pi, rw, :]
            b = b_ref[0, pi, rw, :]
            kd = kd_ref[0, pi, rw, :]
            e_out = jnp.exp(-cum)
            e_end = jnp.exp(tot8[pi][bi:bi + 1] - cum)
            return (-kn_ref[pi, rw, :] * jnp.exp(cum - ld_b), r_ref[pi, rw, :] * jnp.exp(cum),
                    b * e_out, kd * e_out, b * e_end, kd * e_end, v_ref[pi, rw, :])
        return load

    ys, s_new = _wkv_chunks([loader(*p) for p in problems],
                            [s_ref[pi, bi] for _, pi, bi in problems],
                            [dec8[pi][bi:bi + 1] for _, pi, bi in problems], sign)
    for k, (rw, pi, bi) in enumerate(problems):
        y_ref[0, pi, rw, :] = ys[k]
        s_ref[pi, bi] = s_new[k]


def _wkv(r, v, kn, ld, kd, b):
    def chunk(d, c):
        fwd = c
        bwd = jnp.where(c < N_CTX_CHUNKS, N_CTX_CHUNKS - 1 - c, N_CHUNKS + N_CTX_CHUNKS - 1 - c)
        return jnp.where(d == 0, fwd, bwd)
    lat = lambda d, c: chunk(d, jnp.maximum(c, N_CTX_CHUNKS)) - N_CTX_CHUNKS
    pps = WKV_PAIRS_PER_STEP
    shared = pl.BlockSpec((pps, CHUNK_ROWS, PAIR), lambda d, p, c: (p, chunk(d, c), 0))
    perdir = pl.BlockSpec((1, pps, CHUNK_ROWS, PAIR), lambda d, p, c: (d, p, chunk(d, c), 0))
    return pl.pallas_call(
        _wkv_kernel,
        grid=(2, N_PAIRS // pps, N_CHUNKS),
        in_specs=[shared, shared, shared, perdir, perdir, perdir],
        out_specs=pl.BlockSpec((1, pps, CHUNK_ROWS, PAIR), lambda d, p, c: (d, p, lat(d, c), 0)),
        out_shape=jax.ShapeDtypeStruct((2, N_PAIRS, SEQ * BATCH, PAIR), F32),
        scratch_shapes=[pltpu.VMEM((pps, BATCH, PAIR, PAIR), F32),
                        pltpu.VMEM((pps, CHUNK_ROWS, PAIR), F32)],
        compiler_params=_params(3),
        name="wkv_scan",
    )(r, v, kn, ld, kd, b)


MERGE_T = 32
MERGE_ROWS = MERGE_T * BATCH


def _merge_kernel(lru_ref, y_ref, bon_ref, g_ref, x_ref, gmix_ref, mod_ref,
                  lng_ref, lnb_ref, ones_ref, wgate_ref, wol_ref, wor_ref, wout_ref, o_ref):
    rows = MERGE_ROWS
    W = RWKV_WIDTH
    from_lru = _bdot(lru_ref[...], wol_ref[...])
    x = x_ref[...]
    h = _rms_modulate(x, gmix_ref[...], mod_ref[1], mod_ref[0])
    gates = _bdot(h, wgate_ref[...])
    y2 = (y_ref[0] + y_ref[1]).reshape(N_PAIRS, rows, PAIR)
    y = jnp.concatenate([y2[s] for s in range(N_PAIRS)], axis=1)
    inv_n = 1.0 / RWKV_HEAD

    ones = ones_ref[...]
    y_hi = y.astype(BF16).astype(F32)
    mu = (_head_sums(y_hi, ones) + _head_sums(y - y_hi, ones)) * inv_n
    dy = y - mu
    var = _head_sums(dy * dy, ones) * inv_n
    yn = dy * lax.rsqrt(var + GN_EPS) * lng_ref[...] + lnb_ref[...]
    rw = (yn + bon_ref[...].reshape(rows, W)) * g_ref[...].reshape(rows, W)
    m = (_sigmoid(gates[:, :D_MODEL]) * from_lru
         + _sigmoid(gates[:, D_MODEL:]) * _bdot(rw, wor_ref[...]))
    mix = _bdot(m, wout_ref[...])
    o_ref[...] = x + _bcast_rows(mod_ref[2], rows) * mix


def _merge(lru_l, y, bon, g, x_tb, g_mix, mod_m, ln_g, ln_b, ones, wgate, wol, wor, wout):
    W = RWKV_WIDTH
    n_blocks = SEQ // MERGE_T
    per_row = GRID_W // MERGE_T
    cm4 = lambda i: (i % per_row, i // per_row, 0, 0)
    y6 = y.reshape(2, N_PAIRS, GRID_W, GRID_ROWS, BATCH, PAIR)
    bon4 = bon.reshape(GRID_W, GRID_ROWS, BATCH, W)
    g4 = g.reshape(GRID_W, GRID_ROWS, BATCH, W)
    return pl.pallas_call(
        _merge_kernel,
        grid=(n_blocks,),
        in_specs=[
            pl.BlockSpec((MERGE_ROWS, LRU_WIDTH), lambda i: (i, 0)),
            pl.BlockSpec((2, N_PAIRS, MERGE_T, 1, BATCH, PAIR), lambda i: (0, 0) + cm4(i)),
            pl.BlockSpec((MERGE_T, 1, BATCH, W), cm4),
            pl.BlockSpec((MERGE_T, 1, BATCH, W), cm4),
            pl.BlockSpec((MERGE_ROWS, D_MODEL), lambda i: (i, 0)),
            _const_spec((1, D_MODEL)), _const_spec((3, BATCH, D_MODEL)),
            _const_spec((1, W)), _const_spec((1, W)), _const_spec(ones.shape),
            _const_spec(wgate.shape), _const_spec(wol.shape), _const_spec(wor.shape),
            _const_spec(wout.shape),
        ],
        out_specs=pl.BlockSpec((MERGE_ROWS, D_MODEL), lambda i: (i, 0)),
        out_shape=jax.ShapeDtypeStruct((SEQ * BATCH, D_MODEL), F32),
        compiler_params=_params(1),
        name="merge",
    )(lru_l, y6, bon4, g4, x_tb, g_mix, mod_m, ln_g, ln_b, ones, wgate, wol, wor, wout)


FFN_T = 64
FFN_ROWS = FFN_T * BATCH
FFN_TILE = 256


def _ffn_up_kernel(x_ref, g_ref, sc_ref, sh_ref, w_ref, o_ref):
    h = _rms_modulate(x_ref[...], g_ref[...], sc_ref[...], sh_ref[...]).astype(BF16)
    for j in range(D_FF // FFN_TILE):
        cs = slice(j * FFN_TILE, (j + 1) * FFN_TILE)
        gate = jnp.dot(h, w_ref[:, cs], preferred_element_type=F32)
        up = jnp.dot(h, w_ref[:, D_FF + j * FFN_TILE:D_FF + (j + 1) * FFN_TILE],
                     preferred_element_type=F32)
        o_ref[:, cs] = (_silu(gate) * up).astype(BF16)


def _ffn_up(x1, g, sc8, sh8, w_in):
    return pl.pallas_call(
        _ffn_up_kernel,
        grid=(SEQ * BATCH // FFN_ROWS,),
        in_specs=[pl.BlockSpec((FFN_ROWS, D_MODEL), lambda i: (i, 0)),
                  _const_spec((1, D_MODEL)), _const_spec((BATCH, D_MODEL)),
                  _const_spec((BATCH, D_MODEL)), _const_spec(w_in.shape)],
        out_specs=pl.BlockSpec((FFN_ROWS, D_FF), lambda i: (i, 0)),
        out_shape=jax.ShapeDtypeStruct((SEQ * BATCH, D_FF), BF16),
        compiler_params=_params(1),
        name="ffn_up",
    )(x1, g, sc8, sh8, w_in)


def _ffn_down_kernel(act_ref, x_ref, gf_ref, w_ref, gfin_ref, o_ref, stage_s):
    y = jnp.dot(act_ref[...], w_ref[...], preferred_element_type=F32)
    x2 = x_ref[...] + _bcast_rows(gf_ref[...], FFN_ROWS) * y
    ms = jnp.mean(x2 * x2, axis=-1, keepdims=True)
    out = x2 * lax.rsqrt(ms + RMS_EPS) * gfin_ref[...]
    for s in range(D_TILES):
        stage_s[s] = out[:, s * LANE:(s + 1) * LANE]
    for bi in range(BATCH):
        rows = pl.ds(bi, FFN_T, stride=BATCH)
        for s in range(D_TILES):
            o_ref[bi, :, s * LANE:(s + 1) * LANE] = stage_s[s, rows, :]


def _ffn_down(act, x1, g_f8, w_out, g_final):
    return pl.pallas_call(
        _ffn_down_kernel,
        grid=(SEQ * BATCH // FFN_ROWS,),
        in_specs=[pl.BlockSpec((FFN_ROWS, D_FF), lambda i: (i, 0)),
                  pl.BlockSpec((FFN_ROWS, D_MODEL), lambda i: (i, 0)),
                  _const_spec((BATCH, D_MODEL)), _const_spec(w_out.shape),
                  _const_spec((1, D_MODEL))],
        out_specs=pl.BlockSpec((BATCH, FFN_T, D_MODEL), lambda i: (0, i, 0)),
        out_shape=jax.ShapeDtypeStruct((BATCH, SEQ, D_MODEL), F32),
        scratch_shapes=[pltpu.VMEM((D_TILES, FFN_ROWS, LANE), F32)],
        compiler_params=_params(1),
        name="ffn_down",
    )(act, x1, g_f8, w_out, g_final)


def _block_diag(w):
    n, c, _ = w.shape
    tiled = jnp.tile(w.reshape(n * c, c), (1, n))
    rb = lax.broadcasted_iota(jnp.int32, (n * c, n * c), 0) // c
    cb = lax.broadcasted_iota(jnp.int32, (n * c, n * c), 1) // c
    return jnp.where(rb == cb, tiled, 0.0)


def _lru_gate_tiles(wa, wx):
    da, dx = _block_diag(0.5 * wa), _block_diag(0.5 * wx)
    tiles = []
    for j, k0 in enumerate(LRU_GATE_K0):
        cs = slice(j * LRU_GATE_TILE, (j + 1) * LRU_GATE_TILE)
        tiles.append(jnp.concatenate([da[k0:k0 + LRU_GATE_K, cs], dx[k0:k0 + LRU_GATE_K, cs]], axis=1))
    return jnp.stack(tiles).astype(BF16)


def _two_dir_lora(w):
    z = jnp.zeros_like(w[0])
    return jnp.concatenate([jnp.concatenate([w[0], z], axis=1),
                            jnp.concatenate([z, w[1]], axis=1)], axis=0).astype(BF16)


def kernel(x, c, ctx, c_ctx, norm_mix_g, norm_ffn_g, w_mod, b_mod, w_in, lru_conv_w, lru_conv_b, lru_wa, lru_ba, lru_wx, lru_bx, lru_lambda, w_o_lru, rwkv_mu, rwkv_w0, rwkv_w2, rwkv_a0, rwkv_a2, rwkv_g2, rwkv_k_k, rwkv_k_a, rwkv_r_k, rwkv_ln_g, rwkv_ln_b, w_o_rwkv, w_out, w_ffn_in, w_ffn_out, norm_final_g):
    assert x.shape == (BATCH, SEQ, D_MODEL) and ctx.shape == (BATCH, CTX_LEN, D_MODEL)
    assert w_mod.shape[0] == 1, "single layer only"
    D, W = D_MODEL, RWKV_WIDTH

    c16 = jnp.concatenate([c, c_ctx[None], jnp.zeros((16 - BATCH - 1, D), F32)], axis=0)
    mod = _adaln(c16, w_mod[0], b_mod[0][None])
    mod_lat = mod[:BATCH].reshape(BATCH, 6, D)
    mod_ctx = jnp.broadcast_to(mod[BATCH:BATCH + 1], (BATCH, 6 * D)).reshape(BATCH, 6, D)
    sh_m, sc_m, g_m, sh_f, sc_f, g_f = [mod_lat[:, k] for k in range(6)]
    sh2 = jnp.stack([mod_ctx[:, 0], sh_m])
    sc2 = jnp.stack([mod_ctx[:, 1], sc_m])

    w_in0 = w_in[0]
    n_lru = 2 * LRU_WIDTH
    w_lru = w_in0[:, :n_lru].astype(BF16)
    w_rw = jnp.pad(w_in0[:, n_lru:n_lru + RWKV_IN], ((0, 0), (0, RWKV_IN_PAD - RWKV_IN))).astype(BF16)
    w_gate = w_in0[:, n_lru + RWKV_IN:].astype(BF16)
    g_mix = norm_mix_g[0][None]

    u_all, xc_tb, xl_tb = _proj(ctx, x, g_mix, sc2, sh2, w_lru)

    wt = jnp.stack([_lru_gate_tiles(lru_wa[0, d], lru_wx[0, d]) for d in range(2)])
    lru_p = (lru_conv_w[0], lru_conv_b[0], wt, 0.5 * lru_ba[0], 0.5 * lru_bx[0], lru_lambda[0])
    hf = _lru_scan(u_all, None, lru_p, 0, reverse=False)
    lru_l = _lru_scan(u_all, hf, lru_p, 1, reverse=True)

    mu_pad = jnp.pad(rwkv_mu[0], ((0, 0), (0, RWKV_IN_PAD - RWKV_IN)))
    ones = _block_diag(jnp.ones((PAIR // RWKV_HEAD, RWKV_HEAD, RWKV_HEAD), F32)).astype(BF16)
    rw_p = (mu_pad, _two_dir_lora(rwkv_w2[0]), rwkv_w0[0].reshape(1, 2 * W),
            _two_dir_lora(rwkv_a2[0]), rwkv_a0[0].reshape(1, 2 * W),
            jnp.pad(rwkv_g2[0], ((0, LORA_G_PAD - LORA_G), (0, 0))).astype(BF16),
            rwkv_k_k[0][None], rwkv_k_a[0][None], rwkv_r_k[0].reshape(1, W), ones)
    r, v, kn, ld, kd, b, bon, g = _rwkv_prep(xc_tb, xl_tb, g_mix, sc2, sh2, w_rw, rw_p)
    y = _wkv(r, v, kn, ld, kd, b)

    x1 = _merge(lru_l, y, bon, g, xl_tb, g_mix, jnp.stack([sh_m, sc_m, g_m]),
                rwkv_ln_g[0][None], rwkv_ln_b[0][None], ones, w_gate,
                w_o_lru[0].astype(BF16), w_o_rwkv[0].astype(BF16), w_out[0].astype(BF16))

    act = _ffn_up(x1, norm_ffn_g[0][None], sc_f, sh_f, w_ffn_in[0].astype(BF16))
    return _ffn_down(act, x1, g_f, w_ffn_out[0].astype(BF16), norm_final_g[None])
```

```python
import functools
import math

import jax
import jax.numpy as jnp
from jax import lax
from jax.experimental import pallas as pl
from jax.experimental.pallas import tpu as pltpu

F32 = jnp.float32
BF16 = jnp.bfloat16

D_MODEL = 1024
BATCH = 8
SEQ = 2048
CTX_LEN = 256
GRID_W = 64
GRID_ROWS = SEQ // GRID_W

LRU_WIDTH = 1280
LRU_BLOCKS = 16
LRU_BLOCK = LRU_WIDTH // LRU_BLOCKS
LRU_CONV = 4
LRU_C = 8.0

RWKV_HEAD = 64
RWKV_WIDTH = 1024
LORA_W = 64
LORA_A = 64
LORA_G = 160
RWKV_IN = 3 * RWKV_WIDTH + 2 * LORA_W + 2 * LORA_A + LORA_G
RWKV_IN_PAD = 3584
LORA_G_PAD = RWKV_IN_PAD - (3 * RWKV_WIDTH + 2 * LORA_W + 2 * LORA_A)
D_FF = 2816

RMS_EPS = 1e-6
GN_EPS = 64e-5
L2_EPS = 1e-12

T_ALL = CTX_LEN + SEQ
CHUNK_T = 64
CHUNK_ROWS = CHUNK_T * BATCH
N_CTX_CHUNKS = CTX_LEN // CHUNK_T
N_LAT_CHUNKS = SEQ // CHUNK_T
N_CHUNKS = N_CTX_CHUNKS + N_LAT_CHUNKS

LANE = 128
D_TILES = D_MODEL // LANE
PAIR = 2 * RWKV_HEAD
N_PAIRS = RWKV_WIDTH // PAIR

LRU_GATE_TILE = 256
LRU_GATE_K = 512
LRU_GATE_K0 = (0, 128, 384, 640, 768)

VMEM_LIMIT = 56 * 1024 * 1024


def _params(n_axes):
    return pltpu.CompilerParams(dimension_semantics=("arbitrary",) * n_axes,
                                vmem_limit_bytes=VMEM_LIMIT)


def _const_spec(shape):
    nd = len(shape)
    return pl.BlockSpec(shape, lambda *_: (0,) * nd)


def _bdot(a, b):
    return jnp.dot(a.astype(BF16), b.astype(BF16), preferred_element_type=F32)


def _softplus(x):
    return jnp.maximum(x, 0.0) + jnp.log1p(jnp.exp(-jnp.abs(x)))


def _sigmoid(x):
    return 0.5 * jnp.tanh(0.5 * x) + 0.5


def _head_sums(t, ones_pair):
    tb = t.astype(BF16)
    return jnp.concatenate(
        [jnp.dot(tb[:, s * PAIR:(s + 1) * PAIR], ones_pair, preferred_element_type=F32)
         for s in range(t.shape[1] // PAIR)], axis=1)


def _silu(x):
    return x * _sigmoid(x)


def _gelu_tanh(x):
    c = math.sqrt(2.0 / math.pi)
    return 0.5 * x * (1.0 + jnp.tanh(c * (x + 0.044715 * (x * x * x))))


def _bcast_rows(v8, rows):
    c = v8.shape[-1]
    return jnp.broadcast_to(v8[None], (rows // BATCH, BATCH, c)).reshape(rows, c)


def _rms_modulate(x, g, scale8, shift8):
    rows = x.shape[0]
    ms = jnp.mean(x * x, axis=-1, keepdims=True)
    y = x * lax.rsqrt(ms + RMS_EPS) * g
    return y * (1.0 + _bcast_rows(scale8, rows)) + _bcast_rows(shift8, rows)


def _adaln_kernel(c_ref, w_ref, b_ref, o_ref):
    s = _silu(c_ref[...])
    o_ref[...] = jnp.dot(s, w_ref[...], preferred_element_type=F32,
                         precision=lax.Precision.HIGHEST) + b_ref[...]


def _adaln(c16, w_mod, b_mod):
    n = w_mod.shape[1]
    tn = 1536
    return pl.pallas_call(
        _adaln_kernel,
        grid=(n // tn,),
        in_specs=[_const_spec((16, D_MODEL)),
                  pl.BlockSpec((D_MODEL, tn), lambda j: (0, j)),
                  pl.BlockSpec((1, tn), lambda j: (0, j))],
        out_specs=pl.BlockSpec((16, tn), lambda j: (0, j)),
        out_shape=jax.ShapeDtypeStruct((16, n), F32),
        compiler_params=_params(1),
        name="adaln",
    )(c16, w_mod, b_mod)


LANE = 128
D_TILES = D_MODEL // LANE


PROJ_GROUPS = 4
PROJ_T = CHUNK_T // PROJ_GROUPS
PROJ_ROWS = PROJ_T * BATCH


PROJ_RING = 3


def _proj_kernel(ctx_ref, x_ref, g_ref, sc_ref, sh_ref, w_ref, o_hbm, xc_o, xl_o,
                 stage_s, ring_s, ring_sem):
    i = pl.program_id(0)
    is_ctx = i < N_CTX_CHUNKS
    slot = i % PROJ_RING

    def out_copy(step, slot_):
        rows = pl.ds(pl.multiple_of(step * CHUNK_ROWS, CHUNK_ROWS), CHUNK_ROWS)
        return pltpu.make_async_copy(ring_s.at[slot_], o_hbm.at[rows, :], ring_sem.at[slot_])

    @pl.when(i >= PROJ_RING)
    def _():
        out_copy(i - PROJ_RING, slot).wait()

    def stage(gi):
        t0 = gi * PROJ_T
        for bi in range(BATCH):
            rows = pl.ds(gi * PROJ_ROWS + bi, PROJ_T, stride=BATCH)
            for s in range(D_TILES):
                ln = slice(s * LANE, (s + 1) * LANE)
                stage_s[s, rows, :] = jnp.where(is_ctx, ctx_ref[bi, t0:t0 + PROJ_T, ln],
                                                x_ref[bi, t0:t0 + PROJ_T, ln])

    def staged(rows):
        return jnp.concatenate([stage_s[s, rows, :] for s in range(D_TILES)], axis=1)

    def normalise(gi):
        rows = slice(gi * PROJ_ROWS, (gi + 1) * PROJ_ROWS)
        x = staged(rows)
        xl_o[rows, :] = x
        return _rms_modulate(x, g_ref[...], sc_ref[0], sh_ref[0]).astype(BF16)

    h = None
    for gi in range(PROJ_GROUPS + 1):
        if gi < PROJ_GROUPS:
            stage(gi)
        if gi > 0:
            ring_s[slot, (gi - 1) * PROJ_ROWS:gi * PROJ_ROWS, :] = jnp.dot(
                h, w_ref[...], preferred_element_type=F32)
        if gi < PROJ_GROUPS:
            h = normalise(gi)

    out_copy(i, slot).start()

    @pl.when(is_ctx)
    def _():
        xc_o[...] = staged(slice(0, CHUNK_ROWS))

    @pl.when(i == N_CHUNKS - 1)
    def _():
        for back in range(PROJ_RING):
            step = N_CHUNKS - 1 - back
            out_copy(step, step % PROJ_RING).wait()


def _proj(ctx, x, g, sc2, sh2, w):
    n = w.shape[1]
    seg = lambda i: (i >= N_CTX_CHUNKS).astype(jnp.int32)
    cchunk = lambda i: jnp.minimum(i, N_CTX_CHUNKS - 1)
    lchunk = lambda i: jnp.maximum(i - N_CTX_CHUNKS, 0)
    return pl.pallas_call(
        _proj_kernel,
        grid=(N_CHUNKS,),
        in_specs=[pl.BlockSpec((BATCH, CHUNK_T, D_MODEL), lambda i: (0, cchunk(i), 0)),
                  pl.BlockSpec((BATCH, CHUNK_T, D_MODEL), lambda i: (0, lchunk(i), 0)),
                  _const_spec((1, D_MODEL)),
                  pl.BlockSpec((1, BATCH, D_MODEL), lambda i: (seg(i), 0, 0)),
                  pl.BlockSpec((1, BATCH, D_MODEL), lambda i: (seg(i), 0, 0)),
                  _const_spec((D_MODEL, n))],
        out_specs=[pl.BlockSpec(memory_space=pl.ANY),
                   pl.BlockSpec((CHUNK_ROWS, D_MODEL), lambda i: (cchunk(i), 0)),
                   pl.BlockSpec((CHUNK_ROWS, D_MODEL), lambda i: (lchunk(i), 0))],
        out_shape=[jax.ShapeDtypeStruct((N_CHUNKS * CHUNK_ROWS, n), F32),
                   jax.ShapeDtypeStruct((CTX_LEN * BATCH, D_MODEL), F32),
                   jax.ShapeDtypeStruct((SEQ * BATCH, D_MODEL), F32)],
        scratch_shapes=[pltpu.VMEM((D_TILES, CHUNK_ROWS, LANE), F32),
                        pltpu.VMEM((PROJ_RING, CHUNK_ROWS, n), F32),
                        pltpu.SemaphoreType.DMA((PROJ_RING,))],
        compiler_params=_params(1),
        name="in_proj",
    )(ctx, x, g, sc2, sh2, w)


def _lru_kernel(*refs, reverse):
    if reverse:
        (u_ref, uy_ref, hf_ref, cw_ref, cb_ref, wt_ref, ba_ref, bx_ref, lam_ref,
         o_ref, h_s, halo_s, ext_s, a_s, b_s) = refs
    else:
        (u_ref, cw_ref, cb_ref, wt_ref, ba_ref, bx_ref, lam_ref,
         o_ref, h_s, halo_s, ext_s, a_s, b_s) = refs
    i = pl.program_id(0)
    halo_rows = (LRU_CONV - 1) * BATCH

    @pl.when(i == 0)
    def _():
        h_s[...] = jnp.zeros_like(h_s)

    @pl.when((i == 0) | (i == N_CTX_CHUNKS))
    def _():
        halo_s[...] = jnp.zeros_like(halo_s)

    u = u_ref[...]
    if reverse:
        ext_s[0:CHUNK_ROWS] = u
        ext_s[CHUNK_ROWS:CHUNK_ROWS + halo_rows] = halo_s[...]
        halo_s[...] = u[0:halo_rows]
    else:
        ext_s[0:halo_rows] = halo_s[...]
        ext_s[halo_rows:halo_rows + CHUNK_ROWS] = u
        halo_s[...] = u[CHUNK_ROWS - halo_rows:CHUNK_ROWS]

    xc = jnp.broadcast_to(cb_ref[...], (CHUNK_ROWS, LRU_WIDTH))
    for j in range(LRU_CONV):
        off = (LRU_CONV - 1 - j) if reverse else j
        xc = xc + cw_ref[j:j + 1, :] * ext_s[off * BATCH:off * BATCH + CHUNK_ROWS]

    xcb = xc.astype(BF16)
    c_half = (-0.5 * LRU_C) * _softplus(-lam_ref[...])
    for j in range(LRU_WIDTH // LRU_GATE_TILE):
        k0 = LRU_GATE_K0[j]
        cs = slice(j * LRU_GATE_TILE, (j + 1) * LRU_GATE_TILE)
        g = jnp.dot(xcb[:, k0:k0 + LRU_GATE_K], wt_ref[j], preferred_element_type=F32)
        tanh_r = jnp.tanh(g[:, :LRU_GATE_TILE] + ba_ref[:, cs])
        gate_i = 0.5 * jnp.tanh(g[:, LRU_GATE_TILE:] + bx_ref[:, cs]) + 0.5
        log_a = c_half[:, cs] * tanh_r + c_half[:, cs]
        a = jnp.exp(log_a)
        a_s[:, cs] = a
        om = 1.0 - a * a
        b_s[:, cs] = jnp.where(om > 0.0, om * lax.rsqrt(om), 0.0) * (gate_i * xc[:, cs])

    def step(k, h):
        t = (CHUNK_T - 1 - k) if reverse else k
        rows = pl.ds(pl.multiple_of(t * BATCH, BATCH), BATCH)
        h = a_s[rows, :] * h + b_s[rows, :]
        if reverse:
            o_ref[rows, :] = (hf_ref[rows, :] + h) * _gelu_tanh(uy_ref[rows, :])
        else:
            o_ref[rows, :] = h
        return h

    h_s[...] = lax.fori_loop(0, CHUNK_T, step, h_s[...], unroll=8)


def _lru_scan(u_all, hf, lru_p, d, *, reverse):
    cw, cb, wt, ba, bx, lam = lru_p
    if reverse:
        chunk = lambda i: jnp.where(i < N_CTX_CHUNKS, N_CTX_CHUNKS - 1 - i,
                                    N_CHUNKS + N_CTX_CHUNKS - 1 - i)
    else:
        chunk = lambda i: i
    lat = lambda i: jnp.maximum(chunk(jnp.maximum(i, N_CTX_CHUNKS)) - N_CTX_CHUNKS, 0)
    blk = (CHUNK_ROWS, LRU_WIDTH)
    in_specs = [pl.BlockSpec(blk, lambda i: (chunk(i), 0))]
    args = [u_all]
    if reverse:
        in_specs += [pl.BlockSpec(blk, lambda i: (lat(i) + N_CTX_CHUNKS, 1)),
                     pl.BlockSpec(blk, lambda i: (lat(i), 0))]
        args += [u_all, hf]
    in_specs += [_const_spec((LRU_CONV, LRU_WIDTH)), _const_spec((1, LRU_WIDTH)),
                 _const_spec(wt[d].shape), _const_spec((1, LRU_WIDTH)),
                 _const_spec((1, LRU_WIDTH)), _const_spec((1, LRU_WIDTH))]
    args += [cw[d], cb[d][None], wt[d], ba[d][None], bx[d][None], lam[d][None]]
    halo_rows = (LRU_CONV - 1) * BATCH
    return pl.pallas_call(
        functools.partial(_lru_kernel, reverse=reverse),
        grid=(N_CHUNKS,),
        in_specs=in_specs,
        out_specs=pl.BlockSpec(blk, lambda i: (lat(i), 0)),
        out_shape=jax.ShapeDtypeStruct((SEQ * BATCH, LRU_WIDTH), F32),
        scratch_shapes=[pltpu.VMEM((BATCH, LRU_WIDTH), F32),
                        pltpu.VMEM((halo_rows, LRU_WIDTH), F32),
                        pltpu.VMEM((CHUNK_ROWS + halo_rows, LRU_WIDTH), F32),
                        pltpu.VMEM(blk, F32),
                        pltpu.VMEM(blk, F32)],
        compiler_params=_params(1),
        name="lru_bwd" if reverse else "lru_fwd",
    )(*args)


PREP_T = 32
PREP_ROWS = PREP_T * BATCH
PREP_CTX_BLOCKS = CTX_LEN // PREP_T
PREP_LAT_BLOCKS = SEQ // PREP_T
PREP_BLOCKS = PREP_CTX_BLOCKS + PREP_LAT_BLOCKS


def _prep_kernel(xc_ref, xl_ref, xcp_ref, xcn_ref, xlp_ref, xln_ref, gmix_ref, sc_ref, sh_ref,
                 win_ref, mu_ref,
                 w2_ref, w0_ref, a2_ref, a0_ref, g2_ref, kk_ref, ka_ref, rk_ref, ones_ref,
                 r_o, v_o, kn_o, ld_o, kd_o, b_o, bon_o, g_o, ext_s):
    i = pl.program_id(0)
    is_ctx = i < PREP_CTX_BLOCKS
    first = (i == 0) | (i == PREP_CTX_BLOCKS)
    last = (i == PREP_CTX_BLOCKS - 1) | (i == PREP_BLOCKS - 1)
    lat_rows = lambda ref: ref[...].reshape(-1, D_MODEL)
    x = jnp.concatenate([
        jnp.where(is_ctx, xcp_ref[...], lat_rows(xlp_ref)),
        jnp.where(is_ctx, xc_ref[...], lat_rows(xl_ref)),
        jnp.where(is_ctx, xcn_ref[...], lat_rows(xln_ref))], axis=0)
    seg = jnp.where(is_ctx, 0, 1)
    h = _rms_modulate(x, gmix_ref[...], sc_ref[seg], sh_ref[seg])
    hb = h.astype(BF16)
    W = RWKV_WIDTH
    ones = ones_ref[...]

    def project(c0, c1):
        zt = jnp.dot(hb, win_ref[:, c0:c1], preferred_element_type=F32)
        ext_s[0:BATCH, c0:c1] = jnp.where(first, 0.0, zt[0:BATCH])
        ext_s[BATCH:BATCH + PREP_ROWS, c0:c1] = zt[BATCH:BATCH + PREP_ROWS]
        ext_s[BATCH + PREP_ROWS:2 * BATCH + PREP_ROWS, c0:c1] = jnp.where(
            last, 0.0, zt[BATCH + PREP_ROWS:2 * BATCH + PREP_ROWS])

    def shifted(c0, c1):
        zc = ext_s[BATCH:BATCH + PREP_ROWS, c0:c1]
        zprev = ext_s[0:PREP_ROWS, c0:c1]
        znext = ext_s[2 * BATCH:2 * BATCH + PREP_ROWS, c0:c1]
        return zc + mu_ref[0:1, c0:c1] * (zprev - zc) + mu_ref[1:2, c0:c1] * (znext - zc)

    def project_pairs(g):
        for part in range(3):
            project(part * W + 2 * g * PAIR, part * W + (2 * g + 2) * PAIR)

    def finish_pair(s, w_pre, a_pre):
        ln = slice(s * PAIR, (s + 1) * PAIR)
        r = shifted(s * PAIR, (s + 1) * PAIR)
        k = shifted(W + s * PAIR, W + (s + 1) * PAIR)
        v = shifted(2 * W + s * PAIR, 2 * W + (s + 1) * PAIR)
        kk = k * kk_ref[:, ln]
        ss = jnp.dot((kk * kk).astype(BF16), ones, preferred_element_type=F32)
        kn = kk / jnp.maximum(jnp.sqrt(ss), L2_EPS)
        r_o[s] = r
        v_o[s] = v
        kn_o[s] = kn
        kd_sum = jnp.zeros_like(k)
        for d in range(2):
            cs = slice(d * W + s * PAIR, d * W + (s + 1) * PAIR)
            ld_o[d, s] = -math.exp(-0.5) * _sigmoid(w_pre[:, cs])
            asig = _sigmoid(a_pre[:, cs])
            kd = k * (1.0 + (asig - 1.0) * ka_ref[:, ln])
            kd_o[d, s] = kd
            b_o[d, s] = kn * asig
            kd_sum = kd_sum + kd
        bon = jnp.dot((r * kd_sum * rk_ref[:, ln]).astype(BF16), ones, preferred_element_type=F32)
        bon_o[:, ln] = bon * v

    project(3 * W, RWKV_IN_PAD)
    project_pairs(0)
    wd = shifted(3 * W, 3 * W + 2 * LORA_W)
    ad = shifted(3 * W + 2 * LORA_W, 3 * W + 2 * LORA_W + 2 * LORA_A)
    gd = shifted(3 * W + 2 * LORA_W + 2 * LORA_A, RWKV_IN_PAD)
    w_pre = _bdot(jnp.tanh(wd), w2_ref[...]) + w0_ref[...]
    a_pre = _bdot(ad, a2_ref[...]) + a0_ref[...]
    g_o[...] = _bdot(_sigmoid(gd), g2_ref[...])
    for g in range(N_PAIRS // 2):
        if g + 1 < N_PAIRS // 2:
            project_pairs(g + 1)
        finish_pair(2 * g, w_pre, a_pre)
        finish_pair(2 * g + 1, w_pre, a_pre)


def _rwkv_prep(xc_tb, xl_tb, g_mix, sc2, sh2, w_rw, rw_p):
    mu, w2, w0, a2, a0, g2, k_k, k_a, r_k, ones = rw_p
    W = RWKV_WIDTH
    hb = PREP_T
    cmain = lambda i: jnp.minimum(i, PREP_CTX_BLOCKS - 1)
    lmain = lambda i: jnp.maximum(i - PREP_CTX_BLOCKS, 0)
    assert PREP_T == GRID_ROWS
    xl4 = xl_tb.reshape(GRID_ROWS, GRID_W, BATCH, D_MODEL)
    hblk = (BATCH, D_MODEL)
    n_c8 = CTX_LEN - 1
    in_specs = [
        pl.BlockSpec((PREP_ROWS, D_MODEL), lambda i: (cmain(i), 0)),
        pl.BlockSpec((GRID_ROWS, 1, BATCH, D_MODEL), lambda i: (0, lmain(i), 0, 0)),
        pl.BlockSpec(hblk, lambda i: (jnp.maximum(cmain(i) * hb - 1, 0), 0)),
        pl.BlockSpec(hblk, lambda i: (jnp.minimum((cmain(i) + 1) * hb, n_c8), 0)),
        pl.BlockSpec((1, 1, BATCH, D_MODEL), lambda i: (GRID_ROWS - 1, jnp.maximum(lmain(i) - 1, 0), 0, 0)),
        pl.BlockSpec((1, 1, BATCH, D_MODEL), lambda i: (0, jnp.minimum(lmain(i) + 1, GRID_W - 1), 0, 0)),
        _const_spec((1, D_MODEL)), _const_spec(sc2.shape), _const_spec(sh2.shape),
        _const_spec(w_rw.shape),
        _const_spec(mu.shape), _const_spec(w2.shape), _const_spec(w0.shape),
        _const_spec(a2.shape), _const_spec(a0.shape), _const_spec(g2.shape),
        _const_spec(k_k.shape), _const_spec(k_a.shape), _const_spec(r_k.shape),
        _const_spec(ones.shape),
    ]
    n_rows = T_ALL * BATCH
    n_lat = SEQ * BATCH
    shared = pl.BlockSpec((N_PAIRS, PREP_ROWS, PAIR), lambda i: (0, i, 0))
    perdir = pl.BlockSpec((2, N_PAIRS, PREP_ROWS, PAIR), lambda i: (0, 0, i, 0))
    latonly = pl.BlockSpec((PREP_ROWS, W), lambda i: (lmain(i), 0))
    return pl.pallas_call(
        _prep_kernel,
        grid=(PREP_BLOCKS,),
        in_specs=in_specs,
        out_specs=[shared, shared, shared, perdir, perdir, perdir, latonly, latonly],
        out_shape=[jax.ShapeDtypeStruct((N_PAIRS, n_rows, PAIR), F32)] * 3
        + [jax.ShapeDtypeStruct((2, N_PAIRS, n_rows, PAIR), F32)] * 3
        + [jax.ShapeDtypeStruct((n_lat, W), F32)] * 2,
        scratch_shapes=[pltpu.VMEM((PREP_ROWS + 2 * BATCH, RWKV_IN_PAD), F32)],
        compiler_params=_params(1),
        name="rwkv_prep",
    )(xc_tb, xl4, xc_tb, xc_tb, xl4, xl4, g_mix, sc2, sh2, w_rw,
      mu, w2, w0, a2, a0, g2, k_k, k_a, r_k, ones)


WKV_GROUP = 8
WKV_SKEW = 1


def _wkv_chunks(loaders, states, decays, sign):
    T = CHUNK_T
    R2 = 2 * T
    nb = len(loaders)
    levels = int(math.log2(T))

    lane = lax.broadcasted_iota(jnp.int32, (R2, PAIR), 1)
    row = lax.broadcasted_iota(jnp.int32, (R2, PAIR), 0)
    head_mask = (lane // RWKV_HEAD) == (row // T)

    def stack_masked(x):
        return jnp.where(head_mask, jnp.concatenate([x, x], axis=0), 0.0).astype(BF16)

    tw = lax.broadcasted_iota(jnp.int32, (T, 2 * R2), 0)
    sw = lax.broadcasted_iota(jnp.int32, (T, 2 * R2), 1) % T
    dtw = (tw - sw) * sign
    strict_w = dtw[:, 0:R2] > 0
    incl_w2 = dtw >= 0
    eye_w = jnp.where(dtw[:, 0:R2] == 0, 1.0, 0.0)

    nt = lambda x, y: lax.dot_general(x, y, (((1,), (1,)), ((), ())), preferred_element_type=F32)
    mm = lambda x, y: jnp.dot(x, y, preferred_element_type=F32)

    def chain(load, S, decay, out):
        a_t, r_t, b_t, k_t, b_e, k_e, v = load()
        ar = jnp.concatenate([a_t, r_t], axis=0).astype(BF16)
        bk = jnp.concatenate([stack_masked(b_t), stack_masked(k_t)], axis=0)
        big = nt(ar, bk)
        fs = nt(ar, S.astype(BF16))
        uv_rhs = jnp.concatenate([b_e, k_e], axis=0).astype(BF16)
        yield
        p_w = jnp.where(strict_w, big[0:T, 0:R2], 0.0)
        a_ak = jnp.where(strict_w, big[0:T, R2:2 * R2], 0.0).astype(BF16)
        a_rbk = jnp.where(incl_w2, big[T:R2], 0.0).astype(BF16)
        v_st = stack_masked(v)
        h = mm(a_ak, v_st)
        inv_w = eye_w + p_w
        p_w = mm(p_w.astype(BF16), stack_masked(p_w))
        yield
        for lvl in range(1, levels):
            p_bd = stack_masked(p_w)
            if lvl == levels - 1:
                inv_w = inv_w + mm(inv_w.astype(BF16), p_bd)
            else:
                both = mm(jnp.concatenate([p_w, inv_w], axis=0).astype(BF16), p_bd)
                yield
                p_w = both[0:T]
                inv_w = inv_w + both[T:R2]
        yield
        u = mm(inv_w.astype(BF16), stack_masked(fs[0:T] + h))
        yield
        y = fs[T:R2] + mm(a_rbk, jnp.concatenate([stack_masked(u), v_st], axis=0))
        uv_t = jnp.concatenate([u, v], axis=0).T.astype(BF16)
        out.append((y, S * decay + jnp.where(head_mask, mm(uv_t, uv_rhs), 0.0)))

    outs = [[] for _ in range(nb)]
    chains = [chain(loaders[i], states[i], decays[i], outs[i]) for i in range(nb)]
    groups = [chains[g:g + WKV_GROUP] for g in range(0, nb, WKV_GROUP)]
    live = [True] * len(groups)
    tick = 0
    while any(live):
        for g, group in enumerate(groups):
            if live[g] and tick >= g * WKV_SKEW:
                for ch in group:
                    if next(ch, "done") == "done":
                        live[g] = False
        tick += 1
    return [o[0][0] for o in outs], [o[0][1] for o in outs]


WKV_PAIRS_PER_STEP = 8


def _wkv_kernel(r_ref, v_ref, kn_ref, ld_ref, kd_ref, b_ref, y_ref, s_ref, cum_s):
    d = pl.program_id(0)
    c = pl.program_id(2)
    sign = 1 - 2 * d

    @pl.when(c == 0)
    def _():
        s_ref[...] = jnp.zeros_like(s_ref)

    tot8, dec8 = [], []
    for pi in range(WKV_PAIRS_PER_STEP):
        ld = ld_ref[0, pi]
        cum_up = ld
        for lvl in range(int(math.log2(CHUNK_T))):
            sh = BATCH << lvl
            cum_up = cum_up + jnp.concatenate([jnp.zeros((sh, PAIR), F32), cum_up[:-sh]], axis=0)
        tot8.append(cum_up[CHUNK_ROWS - BATCH:])
        tot = _bcast_rows(tot8[pi], CHUNK_ROWS)
        cum_s[pi] = jnp.where(d == 0, cum_up, tot - cum_up + ld)
        dec8.append(jnp.exp(tot8[pi]))

    problems = [(pl.ds(bi, CHUNK_T, stride=BATCH), pi, bi)
                for pi in range(WKV_PAIRS_PER_STEP) for bi in range(BATCH)]
    def loader(rw, pi, bi):
        def load():
            cum = cum_s[pi, rw, :]
            ld_b = ld_ref[0, pi, rw, :]
            b = b_ref[0, pi, rw, :]
            kd = kd_ref[0, pi, rw, :]
            e_out = jnp.exp(-cum)
            e_end = jnp.exp(tot8[pi][bi:bi + 1] - cum)
            return (-kn_ref[pi, rw, :] * jnp.exp(cum - ld_b), r_ref[pi, rw, :] * jnp.exp(cum),
                    b * e_out, kd * e_out, b * e_end, kd * e_end, v_ref[pi, rw, :])
        return load

    ys, s_new = _wkv_chunks([loader(*p) for p in problems],
                            [s_ref[pi, bi] for _, pi, bi in problems],
                            [dec8[pi][bi:bi + 1] for _, pi, bi in problems], sign)
    for k, (rw, pi, bi) in enumerate(problems):
        y_ref[0, pi, rw, :] = ys[k]
        s_ref[pi, bi] = s_new[k]


def _wkv(r, v, kn, ld, kd, b):
    def chunk(d, c):
        fwd = c
        bwd = jnp.where(c < N_CTX_CHUNKS, N_CTX_CHUNKS - 1 - c, N_CHUNKS + N_CTX_CHUNKS - 1 - c)
        return jnp.where(d == 0, fwd, bwd)
    lat = lambda d, c: chunk(d, jnp.maximum(c, N_CTX_CHUNKS)) - N_CTX_CHUNKS
    pps = WKV_PAIRS_PER_STEP
    shared = pl.BlockSpec((pps, CHUNK_ROWS, PAIR), lambda d, p, c: (p, chunk(d, c), 0))
    perdir = pl.BlockSpec((1, pps, CHUNK_ROWS, PAIR), lambda d, p, c: (d, p, chunk(d, c), 0))
    return pl.pallas_call(
        _wkv_kernel,
        grid=(2, N_PAIRS // pps, N_CHUNKS),
        in_specs=[shared, shared, shared, perdir, perdir, perdir],
        out_specs=pl.BlockSpec((1, pps, CHUNK_ROWS, PAIR), lambda d, p, c: (d, p, lat(d, c), 0)),
        out_shape=jax.ShapeDtypeStruct((2, N_PAIRS, SEQ * BATCH, PAIR), F32),
        scratch_shapes=[pltpu.VMEM((pps, BATCH, PAIR, PAIR), F32),
                        pltpu.VMEM((pps, CHUNK_ROWS, PAIR), F32)],
        compiler_params=_params(3),
        name="wkv_scan",
    )(r, v, kn, ld, kd, b)


MERGE_T = 32
MERGE_ROWS = MERGE_T * BATCH


def _merge_kernel(lru_ref, y_ref, bon_ref, g_ref, x_ref, gmix_ref, mod_ref,
                  lng_ref, lnb_ref, ones_ref, wgate_ref, wol_ref, wor_ref, wout_ref, o_ref):
    rows = MERGE_ROWS
    W = RWKV_WIDTH
    from_lru = _bdot(lru_ref[...], wol_ref[...])
    x = x_ref[...]
    h = _rms_modulate(x, gmix_ref[...], mod_ref[1], mod_ref[0])
    gates = _bdot(h, wgate_ref[...])
    y2 = (y_ref[0] + y_ref[1]).reshape(N_PAIRS, rows, PAIR)
    y = jnp.concatenate([y2[s] for s in range(N_PAIRS)], axis=1)
    inv_n = 1.0 / RWKV_HEAD

    ones = ones_ref[...]
    y_hi = y.astype(BF16).astype(F32)
    mu = (_head_sums(y_hi, ones) + _head_sums(y - y_hi, ones)) * inv_n
    dy = y - mu
    var = _head_sums(dy * dy, ones) * inv_n
    yn = dy * lax.rsqrt(var + GN_EPS) * lng_ref[...] + lnb_ref[...]
    rw = (yn + bon_ref[...].reshape(rows, W)) * g_ref[...].reshape(rows, W)
    m = (_sigmoid(gates[:, :D_MODEL]) * from_lru
         + _sigmoid(gates[:, D_MODEL:]) * _bdot(rw, wor_ref[...]))
    mix = _bdot(m, wout_ref[...])
    o_ref[...] = x + _bcast_rows(mod_ref[2], rows) * mix


def _merge(lru_l, y, bon, g, x_tb, g_mix, mod_m, ln_g, ln_b, ones, wgate, wol, wor, wout):
    W = RWKV_WIDTH
    n_blocks = SEQ // MERGE_T
    per_row = GRID_W // MERGE_T
    cm4 = lambda i: (i % per_row, i // per_row, 0, 0)
    y6 = y.reshape(2, N_PAIRS, GRID_W, GRID_ROWS, BATCH, PAIR)
    bon4 = bon.reshape(GRID_W, GRID_ROWS, BATCH, W)
    g4 = g.reshape(GRID_W, GRID_ROWS, BATCH, W)
    return pl.pallas_call(
        _merge_kernel,
        grid=(n_blocks,),
        in_specs=[
            pl.BlockSpec((MERGE_ROWS, LRU_WIDTH), lambda i: (i, 0)),
            pl.BlockSpec((2, N_PAIRS, MERGE_T, 1, BATCH, PAIR), lambda i: (0, 0) + cm4(i)),
            pl.BlockSpec((MERGE_T, 1, BATCH, W), cm4),
            pl.BlockSpec((MERGE_T, 1, BATCH, W), cm4),
            pl.BlockSpec((MERGE_ROWS, D_MODEL), lambda i: (i, 0)),
            _const_spec((1, D_MODEL)), _const_spec((3, BATCH, D_MODEL)),
            _const_spec((1, W)), _const_spec((1, W)), _const_spec(ones.shape),
            _const_spec(wgate.shape), _const_spec(wol.shape), _const_spec(wor.shape),
            _const_spec(wout.shape),
        ],
        out_specs=pl.BlockSpec((MERGE_ROWS, D_MODEL), lambda i: (i, 0)),
        out_shape=jax.ShapeDtypeStruct((SEQ * BATCH, D_MODEL), F32),
        compiler_params=_params(1),
        name="merge",
    )(lru_l, y6, bon4, g4, x_tb, g_mix, mod_m, ln_g, ln_b, ones, wgate, wol, wor, wout)


FFN_T = 64
FFN_ROWS = FFN_T * BATCH
FFN_TILE = 256


def _ffn_up_kernel(x_ref, g_ref, sc_ref, sh_ref, w_ref, o_ref):
    h = _rms_modulate(x_ref[...], g_ref[...], sc_ref[...], sh_ref[...]).astype(BF16)
    for j in range(D_FF // FFN_TILE):
        cs = slice(j * FFN_TILE, (j + 1) * FFN_TILE)
        gate = jnp.dot(h, w_ref[:, cs], preferred_element_type=F32)
        up = jnp.dot(h, w_ref[:, D_FF + j * FFN_TILE:D_FF + (j + 1) * FFN_TILE],
                     preferred_element_type=F32)
        o_ref[:, cs] = (_silu(gate) * up).astype(BF16)


def _ffn_up(x1, g, sc8, sh8, w_in):
    return pl.pallas_call(
        _ffn_up_kernel,
        grid=(SEQ * BATCH // FFN_ROWS,),
        in_specs=[pl.BlockSpec((FFN_ROWS, D_MODEL), lambda i: (i, 0)),
                  _const_spec((1, D_MODEL)), _const_spec((BATCH, D_MODEL)),
                  _const_spec((BATCH, D_MODEL)), _const_spec(w_in.shape)],
        out_specs=pl.BlockSpec((FFN_ROWS, D_FF), lambda i: (i, 0)),
        out_shape=jax.ShapeDtypeStruct((SEQ * BATCH, D_FF), BF16),
        compiler_params=_params(1),
        name="ffn_up",
    )(x1, g, sc8, sh8, w_in)


def _ffn_down_kernel(act_ref, x_ref, gf_ref, w_ref, gfin_ref, o_ref, stage_s):
    y = jnp.dot(act_ref[...], w_ref[...], preferred_element_type=F32)
    x2 = x_ref[...] + _bcast_rows(gf_ref[...], FFN_ROWS) * y
    ms = jnp.mean(x2 * x2, axis=-1, keepdims=True)
    out = x2 * lax.rsqrt(ms + RMS_EPS) * gfin_ref[...]
    for s in range(D_TILES):
        stage_s[s] = out[:, s * LANE:(s + 1) * LANE]
    for bi in range(BATCH):
        rows = pl.ds(bi, FFN_T, stride=BATCH)
        for s in range(D_TILES):
            o_ref[bi, :, s * LANE:(s + 1) * LANE] = stage_s[s, rows, :]


def _ffn_down(act, x1, g_f8, w_out, g_final):
    return pl.pallas_call(
        _ffn_down_kernel,
        grid=(SEQ * BATCH // FFN_ROWS,),
        in_specs=[pl.BlockSpec((FFN_ROWS, D_FF), lambda i: (i, 0)),
                  pl.BlockSpec((FFN_ROWS, D_MODEL), lambda i: (i, 0)),
                  _const_spec((BATCH, D_MODEL)), _const_spec(w_out.shape),
                  _const_spec((1, D_MODEL))],
        out_specs=pl.BlockSpec((BATCH, FFN_T, D_MODEL), lambda i: (0, i, 0)),
        out_shape=jax.ShapeDtypeStruct((BATCH, SEQ, D_MODEL), F32),
        scratch_shapes=[pltpu.VMEM((D_TILES, FFN_ROWS, LANE), F32)],
        compiler_params=_params(1),
        name="ffn_down",
    )(act, x1, g_f8, w_out, g_final)


def _block_diag(w):
    n, c, _ = w.shape
    tiled = jnp.tile(w.reshape(n * c, c), (1, n))
    rb = lax.broadcasted_iota(jnp.int32, (n * c, n * c), 0) // c
    cb = lax.broadcasted_iota(jnp.int32, (n * c, n * c), 1) // c
    return jnp.where(rb == cb, tiled, 0.0)


def _lru_gate_tiles(wa, wx):
    da, dx = _block_diag(0.5 * wa), _block_diag(0.5 * wx)
    tiles = []
    for j, k0 in enumerate(LRU_GATE_K0):
        cs = slice(j * LRU_GATE_TILE, (j + 1) * LRU_GATE_TILE)
        tiles.append(jnp.concatenate([da[k0:k0 + LRU_GATE_K, cs], dx[k0:k0 + LRU_GATE_K, cs]], axis=1))
    return jnp.stack(tiles).astype(BF16)


def _two_dir_lora(w):
    z = jnp.zeros_like(w[0])
    return jnp.concatenate([jnp.concatenate([w[0], z], axis=1),
                            jnp.concatenate([z, w[1]], axis=1)], axis=0).astype(BF16)


def kernel(x, c, ctx, c_ctx, norm_mix_g, norm_ffn_g, w_mod, b_mod, w_in, lru_conv_w, lru_conv_b, lru_wa, lru_ba, lru_wx, lru_bx, lru_lambda, w_o_lru, rwkv_mu, rwkv_w0, rwkv_w2, rwkv_a0, rwkv_a2, rwkv_g2, rwkv_k_k, rwkv_k_a, rwkv_r_k, rwkv_ln_g, rwkv_ln_b, w_o_rwkv, w_out, w_ffn_in, w_ffn_out, norm_final_g):
    assert x.shape == (BATCH, SEQ, D_MODEL) and ctx.shape == (BATCH, CTX_LEN, D_MODEL)
    assert w_mod.shape[0] == 1, "single layer only"
    D, W = D_MODEL, RWKV_WIDTH

    c16 = jnp.concatenate([c, c_ctx[None], jnp.zeros((16 - BATCH - 1, D), F32)], axis=0)
    mod = _adaln(c16, w_mod[0], b_mod[0][None])
    mod_lat = mod[:BATCH].reshape(BATCH, 6, D)
    mod_ctx = jnp.broadcast_to(mod[BATCH:BATCH + 1], (BATCH, 6 * D)).reshape(BATCH, 6, D)
    sh_m, sc_m, g_m, sh_f, sc_f, g_f = [mod_lat[:, k] for k in range(6)]
    sh2 = jnp.stack([mod_ctx[:, 0], sh_m])
    sc2 = jnp.stack([mod_ctx[:, 1], sc_m])

    w_in0 = w_in[0]
    n_lru = 2 * LRU_WIDTH
    w_lru = w_in0[:, :n_lru].astype(BF16)
    w_rw = jnp.pad(w_in0[:, n_lru:n_lru + RWKV_IN], ((0, 0), (0, RWKV_IN_PAD - RWKV_IN))).astype(BF16)
    w_gate = w_in0[:, n_lru + RWKV_IN:].astype(BF16)
    g_mix = norm_mix_g[0][None]

    u_all, xc_tb, xl_tb = _proj(ctx, x, g_mix, sc2, sh2, w_lru)

    wt = jnp.stack([_lru_gate_tiles(lru_wa[0, d], lru_wx[0, d]) for d in range(2)])
    lru_p = (lru_conv_w[0], lru_conv_b[0], wt, 0.5 * lru_ba[0], 0.5 * lru_bx[0], lru_lambda[0])
    hf = _lru_scan(u_all, None, lru_p, 0, reverse=False)
    lru_l = _lru_scan(u_all, hf, lru_p, 1, reverse=True)

    mu_pad = jnp.pad(rwkv_mu[0], ((0, 0), (0, RWKV_IN_PAD - RWKV_IN)))
    ones = _block_diag(jnp.ones((PAIR // RWKV_HEAD, RWKV_HEAD, RWKV_HEAD), F32)).astype(BF16)
    rw_p = (mu_pad, _two_dir_lora(rwkv_w2[0]), rwkv_w0[0].reshape(1, 2 * W),
            _two_dir_lora(rwkv_a2[0]), rwkv_a0[0].reshape(1, 2 * W),
            jnp.pad(rwkv_g2[0], ((0, LORA_G_PAD - LORA_G), (0, 0))).astype(BF16),
            rwkv_k_k[0][None], rwkv_k_a[0][None], rwkv_r_k[0].reshape(1, W), ones)
    r, v, kn, ld, kd, b, bon, g = _rwkv_prep(xc_tb, xl_tb, g_mix, sc2, sh2, w_rw, rw_p)
    y = _wkv(r, v, kn, ld, kd, b)

    x1 = _merge(lru_l, y, bon, g, xl_tb, g_mix, jnp.stack([sh_m, sc_m, g_m]),
                rwkv_ln_g[0][None], rwkv_ln_b[0][None], ones, w_gate,
                w_o_lru[0].astype(BF16), w_o_rwkv[0].astype(BF16), w_out[0].astype(BF16))

    act = _ffn_up(x1, norm_ffn_g[0][None], sc_f, sh_f, w_ffn_in[0].astype(BF16))
    return _ffn_down(act, x1, g_f, w_ffn_out[0].astype(BF16), norm_final_g[None])
```

```python
import functools
import math

import jax
import jax.numpy as jnp
from jax import lax
from jax.experimental import pallas as pl
from jax.experimental.pallas import tpu as pltpu

F32 = jnp.float32
BF16 = jnp.bfloat16

D_MODEL = 1024
BATCH = 8
SEQ = 2048
CTX_LEN = 256
GRID_W = 64
GRID_ROWS = SEQ // GRID_W

LRU_WIDTH = 1280
LRU_BLOCKS = 16
LRU_BLOCK = LRU_WIDTH // LRU_BLOCKS
LRU_CONV = 4
LRU_C = 8.0

RWKV_HEAD = 64
RWKV_WIDTH = 1024
LORA_W = 64
LORA_A = 64
LORA_G = 160
RWKV_IN = 3 * RWKV_WIDTH + 2 * LORA_W + 2 * LORA_A + LORA_G
RWKV_IN_PAD = 3584
LORA_G_PAD = RWKV_IN_PAD - (3 * RWKV_WIDTH + 2 * LORA_W + 2 * LORA_A)
D_FF = 2816

RMS_EPS = 1e-6
GN_EPS = 64e-5
L2_EPS = 1e-12

T_ALL = CTX_LEN + SEQ
CHUNK_T = 64
CHUNK_ROWS = CHUNK_T * BATCH
N_CTX_CHUNKS = CTX_LEN // CHUNK_T
N_LAT_CHUNKS = SEQ // CHUNK_T
N_CHUNKS = N_CTX_CHUNKS + N_LAT_CHUNKS

LANE = 128
D_TILES = D_MODEL // LANE
PAIR = 2 * RWKV_HEAD
N_PAIRS = RWKV_WIDTH // PAIR

LRU_GATE_TILE = 256
LRU_GATE_K = 512
LRU_GATE_K0 = (0, 128, 384, 640, 768)

VMEM_LIMIT = 56 * 1024 * 1024


def _params(n_axes):
    return pltpu.CompilerParams(dimension_semantics=("arbitrary",) * n_axes,
                                vmem_limit_bytes=VMEM_LIMIT)


def _const_spec(shape):
    nd = len(shape)
    return pl.BlockSpec(shape, lambda *_: (0,) * nd)


def _bdot(a, b):
    return jnp.dot(a.astype(BF16), b.astype(BF16), preferred_element_type=F32)


def _softplus(x):
    return jnp.maximum(x, 0.0) + jnp.log1p(jnp.exp(-jnp.abs(x)))


def _sigmoid(x):
    return 0.5 * jnp.tanh(0.5 * x) + 0.5


def _head_sums(t, ones_pair):
    tb = t.astype(BF16)
    return jnp.concatenate(
        [jnp.dot(tb[:, s * PAIR:(s + 1) * PAIR], ones_pair, preferred_element_type=F32)
         for s in range(t.shape[1] // PAIR)], axis=1)


def _silu(x):
    return x * _sigmoid(x)


def _gelu_tanh(x):
    c = math.sqrt(2.0 / math.pi)
    return 0.5 * x * (1.0 + jnp.tanh(c * (x + 0.044715 * (x * x * x))))


def _bcast_rows(v8, rows):
    c = v8.shape[-1]
    return jnp.broadcast_to(v8[None], (rows // BATCH, BATCH, c)).reshape(rows, c)


def _rms_modulate(x, g, scale8, shift8):
    rows = x.shape[0]
    ms = jnp.mean(x * x, axis=-1, keepdims=True)
    y = x * lax.rsqrt(ms + RMS_EPS) * g
    return y * (1.0 + _bcast_rows(scale8, rows)) + _bcast_rows(shift8, rows)


def _adaln_kernel(c_ref, w_ref, b_ref, o_ref):
    s = _silu(c_ref[...])
    o_ref[...] = jnp.dot(s, w_ref[...], preferred_element_type=F32,
                         precision=lax.Precision.HIGHEST) + b_ref[...]


def _adaln(c16, w_mod, b_mod):
    n = w_mod.shape[1]
    tn = 1536
    return pl.pallas_call(
        _adaln_kernel,
        grid=(n // tn,),
        in_specs=[_const_spec((16, D_MODEL)),
                  pl.BlockSpec((D_MODEL, tn), lambda j: (0, j)),
                  pl.BlockSpec((1, tn), lambda j: (0, j))],
        out_specs=pl.BlockSpec((16, tn), lambda j: (0, j)),
        out_shape=jax.ShapeDtypeStruct((16, n), F32),
        compiler_params=_params(1),
        name="adaln",
    )(c16, w_mod, b_mod)


LANE = 128
D_TILES = D_MODEL // LANE


PROJ_GROUPS = 4
PROJ_T = CHUNK_T // PROJ_GROUPS
PROJ_ROWS = PROJ_T * BATCH


def _proj_kernel(ctx_ref, x_ref, g_ref, sc_ref, sh_ref, w_ref, o_ref, xc_o, xl_o, stage_s):
    is_ctx = pl.program_id(0) < N_CTX_CHUNKS

    def stage(gi):
        t0 = gi * PROJ_T
        for bi in range(BATCH):
            rows = pl.ds(gi * PROJ_ROWS + bi, PROJ_T, stride=BATCH)
            for s in range(D_TILES):
                ln = slice(s * LANE, (s + 1) * LANE)
                stage_s[s, rows, :] = jnp.where(is_ctx, ctx_ref[bi, t0:t0 + PROJ_T, ln],
                                                x_ref[bi, t0:t0 + PROJ_T, ln])

    def staged(rows):
        return jnp.concatenate([stage_s[s, rows, :] for s in range(D_TILES)], axis=1)

    def normalise(gi):
        rows = slice(gi * PROJ_ROWS, (gi + 1) * PROJ_ROWS)
        x = staged(rows)
        xl_o[rows, :] = x
        return _rms_modulate(x, g_ref[...], sc_ref[0], sh_ref[0]).astype(BF16)

    h = None
    for gi in range(PROJ_GROUPS + 1):
        if gi < PROJ_GROUPS:
            stage(gi)
        if gi > 0:
            o_ref[(gi - 1) * PROJ_ROWS:gi * PROJ_ROWS, :] = jnp.dot(
                h, w_ref[...], preferred_element_type=F32)
        if gi < PROJ_GROUPS:
            h = normalise(gi)

    @pl.when(is_ctx)
    def _():
        xc_o[...] = staged(slice(0, CHUNK_ROWS))


def _proj(ctx, x, g, sc2, sh2, w):
    n = w.shape[1]
    seg = lambda i: (i >= N_CTX_CHUNKS).astype(jnp.int32)
    cchunk = lambda i: jnp.minimum(i, N_CTX_CHUNKS - 1)
    lchunk = lambda i: jnp.maximum(i - N_CTX_CHUNKS, 0)
    return pl.pallas_call(
        _proj_kernel,
        grid=(N_CHUNKS,),
        in_specs=[pl.BlockSpec((BATCH, CHUNK_T, D_MODEL), lambda i: (0, cchunk(i), 0)),
                  pl.BlockSpec((BATCH, CHUNK_T, D_MODEL), lambda i: (0, lchunk(i), 0)),
                  _const_spec((1, D_MODEL)),
                  pl.BlockSpec((1, BATCH, D_MODEL), lambda i: (seg(i), 0, 0)),
                  pl.BlockSpec((1, BATCH, D_MODEL), lambda i: (seg(i), 0, 0)),
                  _const_spec((D_MODEL, n))],
        out_specs=[pl.BlockSpec((CHUNK_ROWS, n), lambda i: (i, 0)),
                   pl.BlockSpec((CHUNK_ROWS, D_MODEL), lambda i: (cchunk(i), 0)),
                   pl.BlockSpec((CHUNK_ROWS, D_MODEL), lambda i: (lchunk(i), 0))],
        out_shape=[jax.ShapeDtypeStruct((N_CHUNKS * CHUNK_ROWS, n), F32),
                   jax.ShapeDtypeStruct((CTX_LEN * BATCH, D_MODEL), F32),
                   jax.ShapeDtypeStruct((SEQ * BATCH, D_MODEL), F32)],
        scratch_shapes=[pltpu.VMEM((D_TILES, CHUNK_ROWS, LANE), F32)],
        compiler_params=_params(1),
        name="in_proj",
    )(ctx, x, g, sc2, sh2, w)


def _lru_kernel(*refs, reverse):
    if reverse:
        (u_ref, uy_ref, hf_ref, cw_ref, cb_ref, wt_ref, ba_ref, bx_ref, lam_ref,
         o_ref, h_s, halo_s, ext_s, a_s, b_s) = refs
    else:
        (u_ref, cw_ref, cb_ref, wt_ref, ba_ref, bx_ref, lam_ref,
         o_ref, h_s, halo_s, ext_s, a_s, b_s) = refs
    i = pl.program_id(0)
    halo_rows = (LRU_CONV - 1) * BATCH

    @pl.when(i == 0)
    def _():
        h_s[...] = jnp.zeros_like(h_s)

    @pl.when((i == 0) | (i == N_CTX_CHUNKS))
    def _():
        halo_s[...] = jnp.zeros_like(halo_s)

    u = u_ref[...]
    if reverse:
        ext_s[0:CHUNK_ROWS] = u
        ext_s[CHUNK_ROWS:CHUNK_ROWS + halo_rows] = halo_s[...]
        halo_s[...] = u[0:halo_rows]
    else:
        ext_s[0:halo_rows] = halo_s[...]
        ext_s[halo_rows:halo_rows + CHUNK_ROWS] = u
        halo_s[...] = u[CHUNK_ROWS - halo_rows:CHUNK_ROWS]

    xc = jnp.broadcast_to(cb_ref[...], (CHUNK_ROWS, LRU_WIDTH))
    for j in range(LRU_CONV):
        off = (LRU_CONV - 1 - j) if reverse else j
        xc = xc + cw_ref[j:j + 1, :] * ext_s[off * BATCH:off * BATCH + CHUNK_ROWS]

    xcb = xc.astype(BF16)
    c_half = (-0.5 * LRU_C) * _softplus(-lam_ref[...])
    for j in range(LRU_WIDTH // LRU_GATE_TILE):
        k0 = LRU_GATE_K0[j]
        cs = slice(j * LRU_GATE_TILE, (j + 1) * LRU_GATE_TILE)
        g = jnp.dot(xcb[:, k0:k0 + LRU_GATE_K], wt_ref[j], preferred_element_type=F32)
        tanh_r = jnp.tanh(g[:, :LRU_GATE_TILE] + ba_ref[:, cs])
        gate_i = 0.5 * jnp.tanh(g[:, LRU_GATE_TILE:] + bx_ref[:, cs]) + 0.5
        log_a = c_half[:, cs] * tanh_r + c_half[:, cs]
        a = jnp.exp(log_a)
        a_s[:, cs] = a
        om = 1.0 - a * a
        b_s[:, cs] = jnp.where(om > 0.0, om * lax.rsqrt(om), 0.0) * (gate_i * xc[:, cs])

    def step(k, h):
        t = (CHUNK_T - 1 - k) if reverse else k
        rows = pl.ds(pl.multiple_of(t * BATCH, BATCH), BATCH)
        h = a_s[rows, :] * h + b_s[rows, :]
        if reverse:
            o_ref[rows, :] = (hf_ref[rows, :] + h) * _gelu_tanh(uy_ref[rows, :])
        else:
            o_ref[rows, :] = h
        return h

    h_s[...] = lax.fori_loop(0, CHUNK_T, step, h_s[...], unroll=8)


def _lru_scan(u_all, hf, lru_p, d, *, reverse):
    cw, cb, wt, ba, bx, lam = lru_p
    if reverse:
        chunk = lambda i: jnp.where(i < N_CTX_CHUNKS, N_CTX_CHUNKS - 1 - i,
                                    N_CHUNKS + N_CTX_CHUNKS - 1 - i)
    else:
        chunk = lambda i: i
    lat = lambda i: jnp.maximum(chunk(jnp.maximum(i, N_CTX_CHUNKS)) - N_CTX_CHUNKS, 0)
    blk = (CHUNK_ROWS, LRU_WIDTH)
    in_specs = [pl.BlockSpec(blk, lambda i: (chunk(i), 0))]
    args = [u_all]
    if reverse:
        in_specs += [pl.BlockSpec(blk, lambda i: (lat(i) + N_CTX_CHUNKS, 1)),
                     pl.BlockSpec(blk, lambda i: (lat(i), 0))]
        args += [u_all, hf]
    in_specs += [_const_spec((LRU_CONV, LRU_WIDTH)), _const_spec((1, LRU_WIDTH)),
                 _const_spec(wt[d].shape), _const_spec((1, LRU_WIDTH)),
                 _const_spec((1, LRU_WIDTH)), _const_spec((1, LRU_WIDTH))]
    args += [cw[d], cb[d][None], wt[d], ba[d][None], bx[d][None], lam[d][None]]
    halo_rows = (LRU_CONV - 1) * BATCH
    return pl.pallas_call(
        functools.partial(_lru_kernel, reverse=reverse),
        grid=(N_CHUNKS,),
        in_specs=in_specs,
        out_specs=pl.BlockSpec(blk, lambda i: (lat(i), 0)),
        out_shape=jax.ShapeDtypeStruct((SEQ * BATCH, LRU_WIDTH), F32),
        scratch_shapes=[pltpu.VMEM((BATCH, LRU_WIDTH), F32),
                        pltpu.VMEM((halo_rows, LRU_WIDTH), F32),
                        pltpu.VMEM((CHUNK_ROWS + halo_rows, LRU_WIDTH), F32),
                        pltpu.VMEM(blk, F32),
                        pltpu.VMEM(blk, F32)],
        compiler_params=_params(1),
        name="lru_bwd" if reverse else "lru_fwd",
    )(*args)


PREP_T = 32
PREP_ROWS = PREP_T * BATCH
PREP_CTX_BLOCKS = CTX_LEN // PREP_T
PREP_LAT_BLOCKS = SEQ // PREP_T
PREP_BLOCKS = PREP_CTX_BLOCKS + PREP_LAT_BLOCKS


def _prep_kernel(xc_ref, xl_ref, xcp_ref, xcn_ref, xlp_ref, xln_ref, gmix_ref, sc_ref, sh_ref,
                 win_ref, mu_ref,
                 w2_ref, w0_ref, a2_ref, a0_ref, g2_ref, kk_ref, ka_ref, rk_ref, ones_ref,
                 r_o, v_o, kn_o, ld_o, kd_o, b_o, bon_o, g_o, ext_s):
    i = pl.program_id(0)
    is_ctx = i < PREP_CTX_BLOCKS
    first = (i == 0) | (i == PREP_CTX_BLOCKS)
    last = (i == PREP_CTX_BLOCKS - 1) | (i == PREP_BLOCKS - 1)
    lat_rows = lambda ref: ref[...].reshape(-1, D_MODEL)
    x = jnp.concatenate([
        jnp.where(is_ctx, xcp_ref[...], lat_rows(xlp_ref)),
        jnp.where(is_ctx, xc_ref[...], lat_rows(xl_ref)),
        jnp.where(is_ctx, xcn_ref[...], lat_rows(xln_ref))], axis=0)
    seg = jnp.where(is_ctx, 0, 1)
    h = _rms_modulate(x, gmix_ref[...], sc_ref[seg], sh_ref[seg])
    hb = h.astype(BF16)
    W = RWKV_WIDTH
    ones = ones_ref[...]

    def project(c0, c1):
        zt = jnp.dot(hb, win_ref[:, c0:c1], preferred_element_type=F32)
        ext_s[0:BATCH, c0:c1] = jnp.where(first, 0.0, zt[0:BATCH])
        ext_s[BATCH:BATCH + PREP_ROWS, c0:c1] = zt[BATCH:BATCH + PREP_ROWS]
        ext_s[BATCH + PREP_ROWS:2 * BATCH + PREP_ROWS, c0:c1] = jnp.where(
            last, 0.0, zt[BATCH + PREP_ROWS:2 * BATCH + PREP_ROWS])

    def shifted(c0, c1):
        zc = ext_s[BATCH:BATCH + PREP_ROWS, c0:c1]
        zprev = ext_s[0:PREP_ROWS, c0:c1]
        znext = ext_s[2 * BATCH:2 * BATCH + PREP_ROWS, c0:c1]
        return zc + mu_ref[0:1, c0:c1] * (zprev - zc) + mu_ref[1:2, c0:c1] * (znext - zc)

    def project_pairs(g):
        for part in range(3):
            project(part * W + 2 * g * PAIR, part * W + (2 * g + 2) * PAIR)

    def finish_pair(s, w_pre, a_pre):
        ln = slice(s * PAIR, (s + 1) * PAIR)
        r = shifted(s * PAIR, (s + 1) * PAIR)
        k = shifted(W + s * PAIR, W + (s + 1) * PAIR)
        v = shifted(2 * W + s * PAIR, 2 * W + (s + 1) * PAIR)
        kk = k * kk_ref[:, ln]
        ss = jnp.dot((kk * kk).astype(BF16), ones, preferred_element_type=F32)
        kn = kk / jnp.maximum(jnp.sqrt(ss), L2_EPS)
        r_o[s] = r
        v_o[s] = v
        kn_o[s] = kn
        kd_sum = jnp.zeros_like(k)
        for d in range(2):
            cs = slice(d * W + s * PAIR, d * W + (s + 1) * PAIR)
            ld_o[d, s] = -math.exp(-0.5) * _sigmoid(w_pre[:, cs])
            asig = _sigmoid(a_pre[:, cs])
            kd = k * (1.0 + (asig - 1.0) * ka_ref[:, ln])
            kd_o[d, s] = kd
            b_o[d, s] = kn * asig
            kd_sum = kd_sum + kd
        bon = jnp.dot((r * kd_sum * rk_ref[:, ln]).astype(BF16), ones, preferred_element_type=F32)
        bon_o[:, ln] = bon * v

    project(3 * W, RWKV_IN_PAD)
    project_pairs(0)
    wd = shifted(3 * W, 3 * W + 2 * LORA_W)
    ad = shifted(3 * W + 2 * LORA_W, 3 * W + 2 * LORA_W + 2 * LORA_A)
    gd = shifted(3 * W + 2 * LORA_W + 2 * LORA_A, RWKV_IN_PAD)
    w_pre = _bdot(jnp.tanh(wd), w2_ref[...]) + w0_ref[...]
    a_pre = _bdot(ad, a2_ref[...]) + a0_ref[...]
    g_o[...] = _bdot(_sigmoid(gd), g2_ref[...])
    for g in range(N_PAIRS // 2):
        if g + 1 < N_PAIRS // 2:
            project_pairs(g + 1)
        finish_pair(2 * g, w_pre, a_pre)
        finish_pair(2 * g + 1, w_pre, a_pre)


def _rwkv_prep(xc_tb, xl_tb, g_mix, sc2, sh2, w_rw, rw_p):
    mu, w2, w0, a2, a0, g2, k_k, k_a, r_k, ones = rw_p
    W = RWKV_WIDTH
    hb = PREP_T
    cmain = lambda i: jnp.minimum(i, PREP_CTX_BLOCKS - 1)
    lmain = lambda i: jnp.maximum(i - PREP_CTX_BLOCKS, 0)
    assert PREP_T == GRID_ROWS
    xl4 = xl_tb.reshape(GRID_ROWS, GRID_W, BATCH, D_MODEL)
    hblk = (BATCH, D_MODEL)
    n_c8 = CTX_LEN - 1
    in_specs = [
        pl.BlockSpec((PREP_ROWS, D_MODEL), lambda i: (cmain(i), 0)),
        pl.BlockSpec((GRID_ROWS, 1, BATCH, D_MODEL), lambda i: (0, lmain(i), 0, 0)),
        pl.BlockSpec(hblk, lambda i: (jnp.maximum(cmain(i) * hb - 1, 0), 0)),
        pl.BlockSpec(hblk, lambda i: (jnp.minimum((cmain(i) + 1) * hb, n_c8), 0)),
        pl.BlockSpec((1, 1, BATCH, D_MODEL), lambda i: (GRID_ROWS - 1, jnp.maximum(lmain(i) - 1, 0), 0, 0)),
        pl.BlockSpec((1, 1, BATCH, D_MODEL), lambda i: (0, jnp.minimum(lmain(i) + 1, GRID_W - 1), 0, 0)),
        _const_spec((1, D_MODEL)), _const_spec(sc2.shape), _const_spec(sh2.shape),
        _const_spec(w_rw.shape),
        _const_spec(mu.shape), _const_spec(w2.shape), _const_spec(w0.shape),
        _const_spec(a2.shape), _const_spec(a0.shape), _const_spec(g2.shape),
        _const_spec(k_k.shape), _const_spec(k_a.shape), _const_spec(r_k.shape),
        _const_spec(ones.shape),
    ]
    n_rows = T_ALL * BATCH
    n_lat = SEQ * BATCH
    shared = pl.BlockSpec((N_PAIRS, PREP_ROWS, PAIR), lambda i: (0, i, 0))
    perdir = pl.BlockSpec((2, N_PAIRS, PREP_ROWS, PAIR), lambda i: (0, 0, i, 0))
    latonly = pl.BlockSpec((PREP_ROWS, W), lambda i: (lmain(i), 0))
    return pl.pallas_call(
        _prep_kernel,
        grid=(PREP_BLOCKS,),
        in_specs=in_specs,
        out_specs=[shared, shared, shared, perdir, perdir, perdir, latonly, latonly],
        out_shape=[jax.ShapeDtypeStruct((N_PAIRS, n_rows, PAIR), F32)] * 3
        + [jax.ShapeDtypeStruct((2, N_PAIRS, n_rows, PAIR), F32)] * 3
        + [jax.ShapeDtypeStruct((n_lat, W), F32)] * 2,
        scratch_shapes=[pltpu.VMEM((PREP_ROWS + 2 * BATCH, RWKV_IN_PAD), F32)],
        compiler_params=_params(1),
        name="rwkv_prep",
    )(xc_tb, xl4, xc_tb, xc_tb, xl4, xl4, g_mix, sc2, sh2, w_rw,
      mu, w2, w0, a2, a0, g2, k_k, k_a, r_k, ones)


WKV_GROUP = 8
WKV_SKEW = 1


def _wkv_chunks(loaders, states, decays, sign):
    T = CHUNK_T
    R2 = 2 * T
    nb = len(loaders)
    levels = int(math.log2(T))

    lane = lax.broadcasted_iota(jnp.int32, (R2, PAIR), 1)
    row = lax.broadcasted_iota(jnp.int32, (R2, PAIR), 0)
    head_mask = (lane // RWKV_HEAD) == (row // T)

    def stack_masked(x):
        return jnp.where(head_mask, jnp.concatenate([x, x], axis=0), 0.0).astype(BF16)

    tw = lax.broadcasted_iota(jnp.int32, (T, 2 * R2), 0)
    sw = lax.broadcasted_iota(jnp.int32, (T, 2 * R2), 1) % T
    dtw = (tw - sw) * sign
    strict_w = dtw[:, 0:R2] > 0
    incl_w2 = dtw >= 0
    eye_w = jnp.where(dtw[:, 0:R2] == 0, 1.0, 0.0)

    nt = lambda x, y: lax.dot_general(x, y, (((1,), (1,)), ((), ())), preferred_element_type=F32)
    mm = lambda x, y: jnp.dot(x, y, preferred_element_type=F32)

    def chain(load, S, decay, out):
        a_t, r_t, b_t, k_t, b_e, k_e, v = load()
        ar = jnp.concatenate([a_t, r_t], axis=0).astype(BF16)
        bk = jnp.concatenate([stack_masked(b_t), stack_masked(k_t)], axis=0)
        big = nt(ar, bk)
        fs = nt(ar, S.astype(BF16))
        uv_rhs = jnp.concatenate([b_e, k_e], axis=0).astype(BF16)
        yield
        p_w = jnp.where(strict_w, big[0:T, 0:R2], 0.0)
        a_ak = jnp.where(strict_w, big[0:T, R2:2 * R2], 0.0).astype(BF16)
        a_rbk = jnp.where(incl_w2, big[T:R2], 0.0).astype(BF16)
        v_st = stack_masked(v)
        h = mm(a_ak, v_st)
        inv_w = eye_w + p_w
        p_w = mm(p_w.astype(BF16), stack_masked(p_w))
        yield
        for lvl in range(1, levels):
            p_bd = stack_masked(p_w)
            if lvl == levels - 1:
                inv_w = inv_w + mm(inv_w.astype(BF16), p_bd)
            else:
                both = mm(jnp.concatenate([p_w, inv_w], axis=0).astype(BF16), p_bd)
                yield
                p_w = both[0:T]
                inv_w = inv_w + both[T:R2]
        yield
        u = mm(inv_w.astype(BF16), stack_masked(fs[0:T] + h))
        yield
        y = fs[T:R2] + mm(a_rbk, jnp.concatenate([stack_masked(u), v_st], axis=0))
        uv_t = jnp.concatenate([u, v], axis=0).T.astype(BF16)
        out.append((y, S * decay + jnp.where(head_mask, mm(uv_t, uv_rhs), 0.0)))

    outs = [[] for _ in range(nb)]
    chains = [chain(loaders[i], states[i], decays[i], outs[i]) for i in range(nb)]
    groups = [chains[g:g + WKV_GROUP] for g in range(0, nb, WKV_GROUP)]
    live = [True] * len(groups)
    tick = 0
    while any(live):
        for g, group in enumerate(groups):
            if live[g] and tick >= g * WKV_SKEW:
                for ch in group:
                    if next(ch, "done") == "done":
                        live[g] = False
        tick += 1
    return [o[0][0] for o in outs], [o[0][1] for o in outs]


WKV_PAIRS_PER_STEP = 8


def _wkv_kernel(r_ref, v_ref, kn_ref, ld_ref, kd_ref, b_ref, y_ref, s_ref, cum_s):
    d = pl.program_id(0)
    c = pl.program_id(2)
    sign = 1 - 2 * d

    @pl.when(c == 0)
    def _():
        s_ref[...] = jnp.zeros_like(s_ref)

    tot8, dec8 = [], []
    for pi in range(WKV_PAIRS_PER_STEP):
        ld = ld_ref[0, pi]
        cum_up = ld
        for lvl in range(int(math.log2(CHUNK_T))):
            sh = BATCH << lvl
            cum_up = cum_up + jnp.concatenate([jnp.zeros((sh, PAIR), F32), cum_up[:-sh]], axis=0)
        tot8.append(cum_up[CHUNK_ROWS - BATCH:])
        tot = _bcast_rows(tot8[pi], CHUNK_ROWS)
        cum_s[pi] = jnp.where(d == 0, cum_up, tot - cum_up + ld)
        dec8.append(jnp.exp(tot8[pi]))

    problems = [(pl.ds(bi, CHUNK_T, stride=BATCH), pi, bi)
                for pi in range(WKV_PAIRS_PER_STEP) for bi in range(BATCH)]
    def loader(rw, pi, bi):
        def load():
            cum = cum_s[pi, rw, :]
            ld_b = ld_ref[0, pi, rw, :]
            b = b_ref[0, pi, rw, :]
            kd = kd_ref[0, pi, rw, :]
            e_out = jnp.exp(-cum)
            e_end = jnp.exp(tot8[pi][bi:bi + 1] - cum)
            return (-kn_ref[pi, rw, :] * jnp.exp(cum - ld_b), r_ref[pi, rw, :] * jnp.exp(cum),
                    b * e_out, kd * e_out, b * e_end, kd * e_end, v_ref[pi, rw, :])
        return load

    ys, s_new = _wkv_chunks([loader(*p) for p in problems],
                            [s_ref[pi, bi] for _, pi, bi in problems],
                            [dec8[pi][bi:bi + 1] for _, pi, bi in problems], sign)
    for k, (rw, pi, bi) in enumerate(problems):
        y_ref[0, pi, rw, :] = ys[k]
        s_ref[pi, bi] = s_new[k]


def _wkv(r, v, kn, ld, kd, b):
    def chunk(d, c):
        fwd = c
        bwd = jnp.where(c < N_CTX_CHUNKS, N_CTX_CHUNKS - 1 - c, N_CHUNKS + N_CTX_CHUNKS - 1 - c)
        return jnp.where(d == 0, fwd, bwd)
    lat = lambda d, c: chunk(d, jnp.maximum(c, N_CTX_CHUNKS)) - N_CTX_CHUNKS
    pps = WKV_PAIRS_PER_STEP
    shared = pl.BlockSpec((pps, CHUNK_ROWS, PAIR), lambda d, p, c: (p, chunk(d, c), 0))
    perdir = pl.BlockSpec((1, pps, CHUNK_ROWS, PAIR), lambda d, p, c: (d, p, chunk(d, c), 0))
    return pl.pallas_call(
        _wkv_kernel,
        grid=(2, N_PAIRS // pps, N_CHUNKS),
        in_specs=[shared, shared, shared, perdir, perdir, perdir],
        out_specs=pl.BlockSpec((1, pps, CHUNK_ROWS, PAIR), lambda d, p, c: (d, p, lat(d, c), 0)),
        out_shape=jax.ShapeDtypeStruct((2, N_PAIRS, SEQ * BATCH, PAIR), F32),
        scratch_shapes=[pltpu.VMEM((pps, BATCH, PAIR, PAIR), F32),
                        pltpu.VMEM((pps, CHUNK_ROWS, PAIR), F32)],
        compiler_params=_params(3),
        name="wkv_scan",
    )(r, v, kn, ld, kd, b)


MERGE_T = 32
MERGE_ROWS = MERGE_T * BATCH


def _merge_kernel(lru_ref, y_ref, bon_ref, g_ref, x_ref, gmix_ref, mod_ref,
                  lng_ref, lnb_ref, ones_ref, wgate_ref, wol_ref, wor_ref, wout_ref, o_ref):
    rows = MERGE_ROWS
    W = RWKV_WIDTH
    from_lru = _bdot(lru_ref[...], wol_ref[...])
    x = x_ref[...]
    h = _rms_modulate(x, gmix_ref[...], mod_ref[1], mod_ref[0])
    gates = _bdot(h, wgate_ref[...])
    y2 = (y_ref[0] + y_ref[1]).reshape(N_PAIRS, rows, PAIR)
    y = jnp.concatenate([y2[s] for s in range(N_PAIRS)], axis=1)
    inv_n = 1.0 / RWKV_HEAD

    ones = ones_ref[...]
    y_hi = y.astype(BF16).astype(F32)
    mu = (_head_sums(y_hi, ones) + _head_sums(y - y_hi, ones)) * inv_n
    dy = y - mu
    var = _head_sums(dy * dy, ones) * inv_n
    yn = dy * lax.rsqrt(var + GN_EPS) * lng_ref[...] + lnb_ref[...]
    rw = (yn + bon_ref[...].reshape(rows, W)) * g_ref[...].reshape(rows, W)
    m = (_sigmoid(gates[:, :D_MODEL]) * from_lru
         + _sigmoid(gates[:, D_MODEL:]) * _bdot(rw, wor_ref[...]))
    mix = _bdot(m, wout_ref[...])
    o_ref[...] = x + _bcast_rows(mod_ref[2], rows) * mix


def _merge(lru_l, y, bon, g, x_tb, g_mix, mod_m, ln_g, ln_b, ones, wgate, wol, wor, wout):
    W = RWKV_WIDTH
    n_blocks = SEQ // MERGE_T
    per_row = GRID_W // MERGE_T
    cm4 = lambda i: (i % per_row, i // per_row, 0, 0)
    y6 = y.reshape(2, N_PAIRS, GRID_W, GRID_ROWS, BATCH, PAIR)
    bon4 = bon.reshape(GRID_W, GRID_ROWS, BATCH, W)
    g4 = g.reshape(GRID_W, GRID_ROWS, BATCH, W)
    return pl.pallas_call(
        _merge_kernel,
        grid=(n_blocks,),
        in_specs=[
            pl.BlockSpec((MERGE_ROWS, LRU_WIDTH), lambda i: (i, 0)),
            pl.BlockSpec((2, N_PAIRS, MERGE_T, 1, BATCH, PAIR), lambda i: (0, 0) + cm4(i)),
            pl.BlockSpec((MERGE_T, 1, BATCH, W), cm4),
            pl.BlockSpec((MERGE_T, 1, BATCH, W), cm4),
            pl.BlockSpec((MERGE_ROWS, D_MODEL), lambda i: (i, 0)),
            _const_spec((1, D_MODEL)), _const_spec((3, BATCH, D_MODEL)),
            _const_spec((1, W)), _const_spec((1, W)), _const_spec(ones.shape),
            _const_spec(wgate.shape), _const_spec(wol.shape), _const_spec(wor.shape),
            _const_spec(wout.shape),
        ],
        out_specs=pl.BlockSpec((MERGE_ROWS, D_MODEL), lambda i: (i, 0)),
        out_shape=jax.ShapeDtypeStruct((SEQ * BATCH, D_MODEL), F32),
        compiler_params=_params(1),
        name="merge",
    )(lru_l, y6, bon4, g4, x_tb, g_mix, mod_m, ln_g, ln_b, ones, wgate, wol, wor, wout)


FFN_T = 64
FFN_ROWS = FFN_T * BATCH
FFN_TILE = 256


def _ffn_kernel(x_ref, g_ref, sc_ref, sh_ref, gf_ref, win_ref, wout_ref, gfin_ref, o_ref,
                act_s, stage_s):
    x = x_ref[...]
    h = _rms_modulate(x, g_ref[...], sc_ref[...], sh_ref[...]).astype(BF16)
    for j in range(D_FF // FFN_TILE):
        cs = slice(j * FFN_TILE, (j + 1) * FFN_TILE)
        gate = jnp.dot(h, win_ref[:, cs], preferred_element_type=F32)
        up = jnp.dot(h, win_ref[:, D_FF + j * FFN_TILE:D_FF + (j + 1) * FFN_TILE],
                     preferred_element_type=F32)
        act_s[:, cs] = (_silu(gate) * up).astype(BF16)
    y = jnp.dot(act_s[...], wout_ref[...], preferred_element_type=F32)
    x2 = x + _bcast_rows(gf_ref[...], FFN_ROWS) * y
    ms = jnp.mean(x2 * x2, axis=-1, keepdims=True)
    out = x2 * lax.rsqrt(ms + RMS_EPS) * gfin_ref[...]
    for s in range(D_TILES):
        stage_s[s] = out[:, s * LANE:(s + 1) * LANE]
    for bi in range(BATCH):
        rows = pl.ds(bi, FFN_T, stride=BATCH)
        for s in range(D_TILES):
            o_ref[bi, :, s * LANE:(s + 1) * LANE] = stage_s[s, rows, :]


def _ffn(x1, g, sc8, sh8, g_f8, w_in, w_out, g_final):
    resident = lambda shape: pl.BlockSpec(shape, lambda i: (0,) * len(shape),
                                          pipeline_mode=pl.Buffered(1))
    return pl.pallas_call(
        _ffn_kernel,
        grid=(SEQ * BATCH // FFN_ROWS,),
        in_specs=[pl.BlockSpec((FFN_ROWS, D_MODEL), lambda i: (i, 0)),
                  _const_spec((1, D_MODEL)), _const_spec((BATCH, D_MODEL)),
                  _const_spec((BATCH, D_MODEL)), _const_spec((BATCH, D_MODEL)),
                  resident(w_in.shape), resident(w_out.shape), _const_spec((1, D_MODEL))],
        out_specs=pl.BlockSpec((BATCH, FFN_T, D_MODEL), lambda i: (0, i, 0)),
        out_shape=jax.ShapeDtypeStruct((BATCH, SEQ, D_MODEL), F32),
        scratch_shapes=[pltpu.VMEM((FFN_ROWS, D_FF), BF16),
                        pltpu.VMEM((D_TILES, FFN_ROWS, LANE), F32)],
        compiler_params=_params(1),
        name="ffn",
    )(x1, g, sc8, sh8, g_f8, w_in, w_out, g_final)


def _block_diag(w):
    n, c, _ = w.shape
    tiled = jnp.tile(w.reshape(n * c, c), (1, n))
    rb = lax.broadcasted_iota(jnp.int32, (n * c, n * c), 0) // c
    cb = lax.broadcasted_iota(jnp.int32, (n * c, n * c), 1) // c
    return jnp.where(rb == cb, tiled, 0.0)


def _lru_gate_tiles(wa, wx):
    da, dx = _block_diag(0.5 * wa), _block_diag(0.5 * wx)
    tiles = []
    for j, k0 in enumerate(LRU_GATE_K0):
        cs = slice(j * LRU_GATE_TILE, (j + 1) * LRU_GATE_TILE)
        tiles.append(jnp.concatenate([da[k0:k0 + LRU_GATE_K, cs], dx[k0:k0 + LRU_GATE_K, cs]], axis=1))
    return jnp.stack(tiles).astype(BF16)


def _two_dir_lora(w):
    z = jnp.zeros_like(w[0])
    return jnp.concatenate([jnp.concatenate([w[0], z], axis=1),
                            jnp.concatenate([z, w[1]], axis=1)], axis=0).astype(BF16)


def kernel(x, c, ctx, c_ctx, norm_mix_g, norm_ffn_g, w_mod, b_mod, w_in, lru_conv_w, lru_conv_b, lru_wa, lru_ba, lru_wx, lru_bx, lru_lambda, w_o_lru, rwkv_mu, rwkv_w0, rwkv_w2, rwkv_a0, rwkv_a2, rwkv_g2, rwkv_k_k, rwkv_k_a, rwkv_r_k, rwkv_ln_g, rwkv_ln_b, w_o_rwkv, w_out, w_ffn_in, w_ffn_out, norm_final_g):
    assert x.shape == (BATCH, SEQ, D_MODEL) and ctx.shape == (BATCH, CTX_LEN, D_MODEL)
    assert w_mod.shape[0] == 1, "single layer only"
    D, W = D_MODEL, RWKV_WIDTH

    c16 = jnp.concatenate([c, c_ctx[None], jnp.zeros((16 - BATCH - 1, D), F32)], axis=0)
    mod = _adaln(c16, w_mod[0], b_mod[0][None])
    mod_lat = mod[:BATCH].reshape(BATCH, 6, D)
    mod_ctx = jnp.broadcast_to(mod[BATCH:BATCH + 1], (BATCH, 6 * D)).reshape(BATCH, 6, D)
    sh_m, sc_m, g_m, sh_f, sc_f, g_f = [mod_lat[:, k] for k in range(6)]
    sh2 = jnp.stack([mod_ctx[:, 0], sh_m])
    sc2 = jnp.stack([mod_ctx[:, 1], sc_m])

    w_in0 = w_in[0]
    n_lru = 2 * LRU_WIDTH
    w_lru = w_in0[:, :n_lru].astype(BF16)
    w_rw = jnp.pad(w_in0[:, n_lru:n_lru + RWKV_IN], ((0, 0), (0, RWKV_IN_PAD - RWKV_IN))).astype(BF16)
    w_gate = w_in0[:, n_lru + RWKV_IN:].astype(BF16)
    g_mix = norm_mix_g[0][None]

    u_all, xc_tb, xl_tb = _proj(ctx, x, g_mix, sc2, sh2, w_lru)

    wt = jnp.stack([_lru_gate_tiles(lru_wa[0, d], lru_wx[0, d]) for d in range(2)])
    lru_p = (lru_conv_w[0], lru_conv_b[0], wt, 0.5 * lru_ba[0], 0.5 * lru_bx[0], lru_lambda[0])
    hf = _lru_scan(u_all, None, lru_p, 0, reverse=False)
    lru_l = _lru_scan(u_all, hf, lru_p, 1, reverse=True)

    mu_pad = jnp.pad(rwkv_mu[0], ((0, 0), (0, RWKV_IN_PAD - RWKV_IN)))
    ones = _block_diag(jnp.ones((PAIR // RWKV_HEAD, RWKV_HEAD, RWKV_HEAD), F32)).astype(BF16)
    rw_p = (mu_pad, _two_dir_lora(rwkv_w2[0]), rwkv_w0[0].reshape(1, 2 * W),
            _two_dir_lora(rwkv_a2[0]), rwkv_a0[0].reshape(1, 2 * W),
            jnp.pad(rwkv_g2[0], ((0, LORA_G_PAD - LORA_G), (0, 0))).astype(BF16),
            rwkv_k_k[0][None], rwkv_k_a[0][None], rwkv_r_k[0].reshape(1, W), ones)
    r, v, kn, ld, kd, b, bon, g = _rwkv_prep(xc_tb, xl_tb, g_mix, sc2, sh2, w_rw, rw_p)
    y = _wkv(r, v, kn, ld, kd, b)

    x1 = _merge(lru_l, y, bon, g, xl_tb, g_mix, jnp.stack([sh_m, sc_m, g_m]),
                rwkv_ln_g[0][None], rwkv_ln_b[0][None], ones, w_gate,
                w_o_lru[0].astype(BF16), w_o_rwkv[0].astype(BF16), w_out[0].astype(BF16))

    return _ffn(x1, norm_ffn_g[0][None], sc_f, sh_f, g_f, w_ffn_in[0].astype(BF16),
                w_ffn_out[0].astype(BF16), norm_final_g[None])
```

```python
import functools
import math

import jax
import jax.numpy as jnp
from jax import lax
from jax.experimental import pallas as pl
from jax.experimental.pallas import tpu as pltpu

F32 = jnp.float32
BF16 = jnp.bfloat16

D_MODEL = 1024
BATCH = 8
SEQ = 2048
CTX_LEN = 256
GRID_W = 64
GRID_ROWS = SEQ // GRID_W

LRU_WIDTH = 1280
LRU_BLOCKS = 16
LRU_BLOCK = LRU_WIDTH // LRU_BLOCKS
LRU_CONV = 4
LRU_C = 8.0

RWKV_HEAD = 64
RWKV_WIDTH = 1024
LORA_W = 64
LORA_A = 64
LORA_G = 160
RWKV_IN = 3 * RWKV_WIDTH + 2 * LORA_W + 2 * LORA_A + LORA_G
RWKV_IN_PAD = 3584
LORA_G_PAD = RWKV_IN_PAD - (3 * RWKV_WIDTH + 2 * LORA_W + 2 * LORA_A)
D_FF = 2816

RMS_EPS = 1e-6
GN_EPS = 64e-5
L2_EPS = 1e-12

T_ALL = CTX_LEN + SEQ
CHUNK_T = 64
CHUNK_ROWS = CHUNK_T * BATCH
N_CTX_CHUNKS = CTX_LEN // CHUNK_T
N_LAT_CHUNKS = SEQ // CHUNK_T
N_CHUNKS = N_CTX_CHUNKS + N_LAT_CHUNKS

LANE = 128
D_TILES = D_MODEL // LANE
PAIR = 2 * RWKV_HEAD
N_PAIRS = RWKV_WIDTH // PAIR

LRU_GATE_TILE = 256
LRU_GATE_K = 512
LRU_GATE_K0 = (0, 128, 384, 640, 768)

VMEM_LIMIT = 56 * 1024 * 1024


def _params(n_axes):
    return pltpu.CompilerParams(dimension_semantics=("arbitrary",) * n_axes,
                                vmem_limit_bytes=VMEM_LIMIT)


def _const_spec(shape):
    nd = len(shape)
    return pl.BlockSpec(shape, lambda *_: (0,) * nd)


def _bdot(a, b):
    return jnp.dot(a.astype(BF16), b.astype(BF16), preferred_element_type=F32)


def _softplus(x):
    return jnp.maximum(x, 0.0) + jnp.log1p(jnp.exp(-jnp.abs(x)))


def _sigmoid(x):
    return 0.5 * jnp.tanh(0.5 * x) + 0.5


def _head_sums(t, ones_pair):
    tb = t.astype(BF16)
    return jnp.concatenate(
        [jnp.dot(tb[:, s * PAIR:(s + 1) * PAIR], ones_pair, preferred_element_type=F32)
         for s in range(t.shape[1] // PAIR)], axis=1)


def _silu(x):
    return x * _sigmoid(x)


def _gelu_tanh(x):
    c = math.sqrt(2.0 / math.pi)
    return 0.5 * x * (1.0 + jnp.tanh(c * (x + 0.044715 * (x * x * x))))


def _bcast_rows(v8, rows):
    c = v8.shape[-1]
    return jnp.broadcast_to(v8[None], (rows // BATCH, BATCH, c)).reshape(rows, c)


def _rms_modulate(x, g, scale8, shift8):
    rows = x.shape[0]
    ms = jnp.mean(x * x, axis=-1, keepdims=True)
    y = x * lax.rsqrt(ms + RMS_EPS) * g
    return y * (1.0 + _bcast_rows(scale8, rows)) + _bcast_rows(shift8, rows)


def _adaln_kernel(c_ref, w_ref, b_ref, o_ref):
    s = _silu(c_ref[...])
    o_ref[...] = jnp.dot(s, w_ref[...], preferred_element_type=F32,
                         precision=lax.Precision.HIGHEST) + b_ref[...]


def _adaln(c16, w_mod, b_mod):
    n = w_mod.shape[1]
    tn = 1536
    return pl.pallas_call(
        _adaln_kernel,
        grid=(n // tn,),
        in_specs=[_const_spec((16, D_MODEL)),
                  pl.BlockSpec((D_MODEL, tn), lambda j: (0, j)),
                  pl.BlockSpec((1, tn), lambda j: (0, j))],
        out_specs=pl.BlockSpec((16, tn), lambda j: (0, j)),
        out_shape=jax.ShapeDtypeStruct((16, n), F32),
        compiler_params=_params(1),
        name="adaln",
    )(c16, w_mod, b_mod)


LANE = 128
D_TILES = D_MODEL // LANE


PROJ_GROUPS = 4
PROJ_T = CHUNK_T // PROJ_GROUPS
PROJ_ROWS = PROJ_T * BATCH


def _proj_kernel(ctx_ref, x_ref, g_ref, sc_ref, sh_ref, w_ref, o_ref, xc_o, xl_o, stage_s):
    is_ctx = pl.program_id(0) < N_CTX_CHUNKS

    def stage(gi):
        t0 = gi * PROJ_T
        for bi in range(BATCH):
            rows = pl.ds(gi * PROJ_ROWS + bi, PROJ_T, stride=BATCH)
            for s in range(D_TILES):
                ln = slice(s * LANE, (s + 1) * LANE)
                stage_s[s, rows, :] = jnp.where(is_ctx, ctx_ref[bi, t0:t0 + PROJ_T, ln],
                                                x_ref[bi, t0:t0 + PROJ_T, ln])

    def staged(rows):
        return jnp.concatenate([stage_s[s, rows, :] for s in range(D_TILES)], axis=1)

    def normalise(gi):
        rows = slice(gi * PROJ_ROWS, (gi + 1) * PROJ_ROWS)
        x = staged(rows)
        xl_o[rows, :] = x
        return _rms_modulate(x, g_ref[...], sc_ref[0], sh_ref[0]).astype(BF16)

    h = None
    for gi in range(PROJ_GROUPS + 1):
        if gi < PROJ_GROUPS:
            stage(gi)
        if gi > 0:
            o_ref[(gi - 1) * PROJ_ROWS:gi * PROJ_ROWS, :] = jnp.dot(
                h, w_ref[...], preferred_element_type=F32)
        if gi < PROJ_GROUPS:
            h = normalise(gi)

    @pl.when(is_ctx)
    def _():
        xc_o[...] = staged(slice(0, CHUNK_ROWS))


def _proj(ctx, x, g, sc2, sh2, w):
    n = w.shape[1]
    seg = lambda i: (i >= N_CTX_CHUNKS).astype(jnp.int32)
    cchunk = lambda i: jnp.minimum(i, N_CTX_CHUNKS - 1)
    lchunk = lambda i: jnp.maximum(i - N_CTX_CHUNKS, 0)
    return pl.pallas_call(
        _proj_kernel,
        grid=(N_CHUNKS,),
        in_specs=[pl.BlockSpec((BATCH, CHUNK_T, D_MODEL), lambda i: (0, cchunk(i), 0)),
                  pl.BlockSpec((BATCH, CHUNK_T, D_MODEL), lambda i: (0, lchunk(i), 0)),
                  _const_spec((1, D_MODEL)),
                  pl.BlockSpec((1, BATCH, D_MODEL), lambda i: (seg(i), 0, 0)),
                  pl.BlockSpec((1, BATCH, D_MODEL), lambda i: (seg(i), 0, 0)),
                  _const_spec((D_MODEL, n))],
        out_specs=[pl.BlockSpec((CHUNK_ROWS, n), lambda i: (i, 0)),
                   pl.BlockSpec((CHUNK_ROWS, D_MODEL), lambda i: (cchunk(i), 0)),
                   pl.BlockSpec((CHUNK_ROWS, D_MODEL), lambda i: (lchunk(i), 0))],
        out_shape=[jax.ShapeDtypeStruct((N_CHUNKS * CHUNK_ROWS, n), F32),
                   jax.ShapeDtypeStruct((CTX_LEN * BATCH, D_MODEL), F32),
                   jax.ShapeDtypeStruct((SEQ * BATCH, D_MODEL), F32)],
        scratch_shapes=[pltpu.VMEM((D_TILES, CHUNK_ROWS, LANE), F32)],
        compiler_params=_params(1),
        name="in_proj",
    )(ctx, x, g, sc2, sh2, w)


def _lru_kernel(*refs, reverse):
    if reverse:
        (u_ref, uy_ref, hf_ref, cw_ref, cb_ref, wt_ref, ba_ref, bx_ref, lam_ref,
         o_ref, h_s, halo_s, ext_s, a_s, b_s) = refs
    else:
        (u_ref, cw_ref, cb_ref, wt_ref, ba_ref, bx_ref, lam_ref,
         o_ref, h_s, halo_s, ext_s, a_s, b_s) = refs
    i = pl.program_id(0)
    halo_rows = (LRU_CONV - 1) * BATCH

    @pl.when(i == 0)
    def _():
        h_s[...] = jnp.zeros_like(h_s)

    @pl.when((i == 0) | (i == N_CTX_CHUNKS))
    def _():
        halo_s[...] = jnp.zeros_like(halo_s)

    u = u_ref[...]
    if reverse:
        ext_s[0:CHUNK_ROWS] = u
        ext_s[CHUNK_ROWS:CHUNK_ROWS + halo_rows] = halo_s[...]
        halo_s[...] = u[0:halo_rows]
    else:
        ext_s[0:halo_rows] = halo_s[...]
        ext_s[halo_rows:halo_rows + CHUNK_ROWS] = u
        halo_s[...] = u[CHUNK_ROWS - halo_rows:CHUNK_ROWS]

    xc = jnp.broadcast_to(cb_ref[...], (CHUNK_ROWS, LRU_WIDTH))
    for j in range(LRU_CONV):
        off = (LRU_CONV - 1 - j) if reverse else j
        xc = xc + cw_ref[j:j + 1, :] * ext_s[off * BATCH:off * BATCH + CHUNK_ROWS]

    xcb = xc.astype(BF16)
    c_half = (-0.5 * LRU_C) * _softplus(-lam_ref[...])
    for j in range(LRU_WIDTH // LRU_GATE_TILE):
        k0 = LRU_GATE_K0[j]
        cs = slice(j * LRU_GATE_TILE, (j + 1) * LRU_GATE_TILE)
        g = jnp.dot(xcb[:, k0:k0 + LRU_GATE_K], wt_ref[j], preferred_element_type=F32)
        tanh_r = jnp.tanh(g[:, :LRU_GATE_TILE] + ba_ref[:, cs])
        gate_i = 0.5 * jnp.tanh(g[:, LRU_GATE_TILE:] + bx_ref[:, cs]) + 0.5
        log_a = c_half[:, cs] * tanh_r + c_half[:, cs]
        a = jnp.exp(log_a)
        a_s[:, cs] = a
        om = 1.0 - a * a
        b_s[:, cs] = jnp.where(om > 0.0, om * lax.rsqrt(om), 0.0) * (gate_i * xc[:, cs])

    def step(k, h):
        t = (CHUNK_T - 1 - k) if reverse else k
        rows = pl.ds(pl.multiple_of(t * BATCH, BATCH), BATCH)
        h = a_s[rows, :] * h + b_s[rows, :]
        if reverse:
            o_ref[rows, :] = (hf_ref[rows, :] + h) * _gelu_tanh(uy_ref[rows, :])
        else:
            o_ref[rows, :] = h
        return h

    h_s[...] = lax.fori_loop(0, CHUNK_T, step, h_s[...], unroll=8)


def _lru_scan(u_all, hf, lru_p, d, *, reverse):
    cw, cb, wt, ba, bx, lam = lru_p
    if reverse:
        chunk = lambda i: jnp.where(i < N_CTX_CHUNKS, N_CTX_CHUNKS - 1 - i,
                                    N_CHUNKS + N_CTX_CHUNKS - 1 - i)
    else:
        chunk = lambda i: i
    lat = lambda i: jnp.maximum(chunk(jnp.maximum(i, N_CTX_CHUNKS)) - N_CTX_CHUNKS, 0)
    blk = (CHUNK_ROWS, LRU_WIDTH)
    in_specs = [pl.BlockSpec(blk, lambda i: (chunk(i), 0))]
    args = [u_all]
    if reverse:
        in_specs += [pl.BlockSpec(blk, lambda i: (lat(i) + N_CTX_CHUNKS, 1)),
                     pl.BlockSpec(blk, lambda i: (lat(i), 0))]
        args += [u_all, hf]
    in_specs += [_const_spec((LRU_CONV, LRU_WIDTH)), _const_spec((1, LRU_WIDTH)),
                 _const_spec(wt[d].shape), _const_spec((1, LRU_WIDTH)),
                 _const_spec((1, LRU_WIDTH)), _const_spec((1, LRU_WIDTH))]
    args += [cw[d], cb[d][None], wt[d], ba[d][None], bx[d][None], lam[d][None]]
    halo_rows = (LRU_CONV - 1) * BATCH
    return pl.pallas_call(
        functools.partial(_lru_kernel, reverse=reverse),
        grid=(N_CHUNKS,),
        in_specs=in_specs,
        out_specs=pl.BlockSpec(blk, lambda i: (lat(i), 0)),
        out_shape=jax.ShapeDtypeStruct((SEQ * BATCH, LRU_WIDTH), F32),
        scratch_shapes=[pltpu.VMEM((BATCH, LRU_WIDTH), F32),
                        pltpu.VMEM((halo_rows, LRU_WIDTH), F32),
                        pltpu.VMEM((CHUNK_ROWS + halo_rows, LRU_WIDTH), F32),
                        pltpu.VMEM(blk, F32),
                        pltpu.VMEM(blk, F32)],
        compiler_params=_params(1),
        name="lru_bwd" if reverse else "lru_fwd",
    )(*args)


PREP_T = 32
PREP_ROWS = PREP_T * BATCH
PREP_CTX_BLOCKS = CTX_LEN // PREP_T
PREP_LAT_BLOCKS = SEQ // PREP_T
PREP_BLOCKS = PREP_CTX_BLOCKS + PREP_LAT_BLOCKS


def _prep_kernel(xc_ref, xl_ref, xcp_ref, xcn_ref, xlp_ref, xln_ref, gmix_ref, sc_ref, sh_ref,
                 win_ref, mu_ref,
                 w2_ref, w0_ref, a2_ref, a0_ref, g2_ref, kk_ref, ka_ref, rk_ref, ones_ref,
                 r_o, v_o, kn_o, ld_o, kd_o, b_o, bon_o, g_o, ext_s):
    i = pl.program_id(0)
    is_ctx = i < PREP_CTX_BLOCKS
    first = (i == 0) | (i == PREP_CTX_BLOCKS)
    last = (i == PREP_CTX_BLOCKS - 1) | (i == PREP_BLOCKS - 1)
    lat_rows = lambda ref: ref[...].reshape(-1, D_MODEL)
    x = jnp.concatenate([
        jnp.where(is_ctx, xcp_ref[...], lat_rows(xlp_ref)),
        jnp.where(is_ctx, xc_ref[...], lat_rows(xl_ref)),
        jnp.where(is_ctx, xcn_ref[...], lat_rows(xln_ref))], axis=0)
    seg = jnp.where(is_ctx, 0, 1)
    h = _rms_modulate(x, gmix_ref[...], sc_ref[seg], sh_ref[seg])
    hb = h.astype(BF16)
    W = RWKV_WIDTH
    ones = ones_ref[...]

    def project(c0, c1):
        zt = jnp.dot(hb, win_ref[:, c0:c1], preferred_element_type=F32)
        ext_s[0:BATCH, c0:c1] = jnp.where(first, 0.0, zt[0:BATCH])
        ext_s[BATCH:BATCH + PREP_ROWS, c0:c1] = zt[BATCH:BATCH + PREP_ROWS]
        ext_s[BATCH + PREP_ROWS:2 * BATCH + PREP_ROWS, c0:c1] = jnp.where(
            last, 0.0, zt[BATCH + PREP_ROWS:2 * BATCH + PREP_ROWS])

    def shifted(c0, c1):
        zc = ext_s[BATCH:BATCH + PREP_ROWS, c0:c1]
        zprev = ext_s[0:PREP_ROWS, c0:c1]
        znext = ext_s[2 * BATCH:2 * BATCH + PREP_ROWS, c0:c1]
        return zc + mu_ref[0:1, c0:c1] * (zprev - zc) + mu_ref[1:2, c0:c1] * (znext - zc)

    def project_pairs(g):
        for part in range(3):
            project(part * W + 2 * g * PAIR, part * W + (2 * g + 2) * PAIR)

    def finish_pair(s, w_pre, a_pre):
        ln = slice(s * PAIR, (s + 1) * PAIR)
        r = shifted(s * PAIR, (s + 1) * PAIR)
        k = shifted(W + s * PAIR, W + (s + 1) * PAIR)
        v = shifted(2 * W + s * PAIR, 2 * W + (s + 1) * PAIR)
        kk = k * kk_ref[:, ln]
        ss = jnp.dot((kk * kk).astype(BF16), ones, preferred_element_type=F32)
        kn = kk / jnp.maximum(jnp.sqrt(ss), L2_EPS)
        r_o[s] = r
        v_o[s] = v
        kn_o[s] = kn
        kd_sum = jnp.zeros_like(k)
        for d in range(2):
            cs = slice(d * W + s * PAIR, d * W + (s + 1) * PAIR)
            ld_o[d, s] = -math.exp(-0.5) * _sigmoid(w_pre[:, cs])
            asig = _sigmoid(a_pre[:, cs])
            kd = k * (1.0 + (asig - 1.0) * ka_ref[:, ln])
            kd_o[d, s] = kd
            b_o[d, s] = kn * asig
            kd_sum = kd_sum + kd
        bon = jnp.dot((r * kd_sum * rk_ref[:, ln]).astype(BF16), ones, preferred_element_type=F32)
        bon_o[:, ln] = bon * v

    project(3 * W, RWKV_IN_PAD)
    project_pairs(0)
    wd = shifted(3 * W, 3 * W + 2 * LORA_W)
    ad = shifted(3 * W + 2 * LORA_W, 3 * W + 2 * LORA_W + 2 * LORA_A)
    gd = shifted(3 * W + 2 * LORA_W + 2 * LORA_A, RWKV_IN_PAD)
    w_pre = _bdot(jnp.tanh(wd), w2_ref[...]) + w0_ref[...]
    a_pre = _bdot(ad, a2_ref[...]) + a0_ref[...]
    g_o[...] = _bdot(_sigmoid(gd), g2_ref[...])
    for g in range(N_PAIRS // 2):
        if g + 1 < N_PAIRS // 2:
            project_pairs(g + 1)
        finish_pair(2 * g, w_pre, a_pre)
        finish_pair(2 * g + 1, w_pre, a_pre)


def _rwkv_prep(xc_tb, xl_tb, g_mix, sc2, sh2, w_rw, rw_p):
    mu, w2, w0, a2, a0, g2, k_k, k_a, r_k, ones = rw_p
    W = RWKV_WIDTH
    hb = PREP_T
    cmain = lambda i: jnp.minimum(i, PREP_CTX_BLOCKS - 1)
    lmain = lambda i: jnp.maximum(i - PREP_CTX_BLOCKS, 0)
    assert PREP_T == GRID_ROWS
    xl4 = xl_tb.reshape(GRID_ROWS, GRID_W, BATCH, D_MODEL)
    hblk = (BATCH, D_MODEL)
    n_c8 = CTX_LEN - 1
    in_specs = [
        pl.BlockSpec((PREP_ROWS, D_MODEL), lambda i: (cmain(i), 0)),
        pl.BlockSpec((GRID_ROWS, 1, BATCH, D_MODEL), lambda i: (0, lmain(i), 0, 0)),
        pl.BlockSpec(hblk, lambda i: (jnp.maximum(cmain(i) * hb - 1, 0), 0)),
        pl.BlockSpec(hblk, lambda i: (jnp.minimum((cmain(i) + 1) * hb, n_c8), 0)),
        pl.BlockSpec((1, 1, BATCH, D_MODEL), lambda i: (GRID_ROWS - 1, jnp.maximum(lmain(i) - 1, 0), 0, 0)),
        pl.BlockSpec((1, 1, BATCH, D_MODEL), lambda i: (0, jnp.minimum(lmain(i) + 1, GRID_W - 1), 0, 0)),
        _const_spec((1, D_MODEL)), _const_spec(sc2.shape), _const_spec(sh2.shape),
        _const_spec(w_rw.shape),
        _const_spec(mu.shape), _const_spec(w2.shape), _const_spec(w0.shape),
        _const_spec(a2.shape), _const_spec(a0.shape), _const_spec(g2.shape),
        _const_spec(k_k.shape), _const_spec(k_a.shape), _const_spec(r_k.shape),
        _const_spec(ones.shape),
    ]
    n_rows = T_ALL * BATCH
    n_lat = SEQ * BATCH
    shared = pl.BlockSpec((N_PAIRS, PREP_ROWS, PAIR), lambda i: (0, i, 0))
    perdir = pl.BlockSpec((2, N_PAIRS, PREP_ROWS, PAIR), lambda i: (0, 0, i, 0))
    latonly = pl.BlockSpec((PREP_ROWS, W), lambda i: (lmain(i), 0))
    return pl.pallas_call(
        _prep_kernel,
        grid=(PREP_BLOCKS,),
        in_specs=in_specs,
        out_specs=[shared, shared, shared, perdir, perdir, perdir, latonly, latonly],
        out_shape=[jax.ShapeDtypeStruct((N_PAIRS, n_rows, PAIR), F32)] * 3
        + [jax.ShapeDtypeStruct((2, N_PAIRS, n_rows, PAIR), F32)] * 3
        + [jax.ShapeDtypeStruct((n_lat, W), F32)] * 2,
        scratch_shapes=[pltpu.VMEM((PREP_ROWS + 2 * BATCH, RWKV_IN_PAD), F32)],
        compiler_params=_params(1),
        name="rwkv_prep",
    )(xc_tb, xl4, xc_tb, xc_tb, xl4, xl4, g_mix, sc2, sh2, w_rw,
      mu, w2, w0, a2, a0, g2, k_k, k_a, r_k, ones)


WKV_GROUP = 8
WKV_SKEW = 1


def _wkv_chunks(loaders, states, decays, sign):
    T = CHUNK_T
    R2 = 2 * T
    nb = len(loaders)
    levels = int(math.log2(T))

    lane = lax.broadcasted_iota(jnp.int32, (R2, PAIR), 1)
    row = lax.broadcasted_iota(jnp.int32, (R2, PAIR), 0)
    head_mask = (lane // RWKV_HEAD) == (row // T)

    def stack_masked(x):
        return jnp.where(head_mask, jnp.concatenate([x, x], axis=0), 0.0).astype(BF16)

    tw = lax.broadcasted_iota(jnp.int32, (T, 2 * R2), 0)
    sw = lax.broadcasted_iota(jnp.int32, (T, 2 * R2), 1) % T
    dtw = (tw - sw) * sign
    strict_w = dtw[:, 0:R2] > 0
    incl_w2 = dtw >= 0
    eye_w = jnp.where(dtw[:, 0:R2] == 0, 1.0, 0.0)

    nt = lambda x, y: lax.dot_general(x, y, (((1,), (1,)), ((), ())), preferred_element_type=F32)
    mm = lambda x, y: jnp.dot(x, y, preferred_element_type=F32)

    def chain(load, S, decay, out):
        a_t, r_t, b_t, k_t, b_e, k_e, v = load()
        ar = jnp.concatenate([a_t, r_t], axis=0).astype(BF16)
        bk = jnp.concatenate([stack_masked(b_t), stack_masked(k_t)], axis=0)
        big = nt(ar, bk)
        fs = nt(ar, S.astype(BF16))
        uv_rhs = jnp.concatenate([b_e, k_e], axis=0).astype(BF16)
        yield
        p_w = jnp.where(strict_w, big[0:T, 0:R2], 0.0)
        a_ak = jnp.where(strict_w, big[0:T, R2:2 * R2], 0.0).astype(BF16)
        a_rbk = jnp.where(incl_w2, big[T:R2], 0.0).astype(BF16)
        v_st = stack_masked(v)
        h = mm(a_ak, v_st)
        inv_w = eye_w + p_w
        p_w = mm(p_w.astype(BF16), stack_masked(p_w))
        yield
        for lvl in range(1, levels):
            p_bd = stack_masked(p_w)
            if lvl == levels - 1:
                inv_w = inv_w + mm(inv_w.astype(BF16), p_bd)
            else:
                both = mm(jnp.concatenate([p_w, inv_w], axis=0).astype(BF16), p_bd)
                yield
                p_w = both[0:T]
                inv_w = inv_w + both[T:R2]
        yield
        u = mm(inv_w.astype(BF16), stack_masked(fs[0:T] + h))
        yield
        y = fs[T:R2] + mm(a_rbk, jnp.concatenate([stack_masked(u), v_st], axis=0))
        uv_t = jnp.concatenate([u, v], axis=0).T.astype(BF16)
        out.append((y, S * decay + jnp.where(head_mask, mm(uv_t, uv_rhs), 0.0)))

    outs = [[] for _ in range(nb)]
    chains = [chain(loaders[i], states[i], decays[i], outs[i]) for i in range(nb)]
    groups = [chains[g:g + WKV_GROUP] for g in range(0, nb, WKV_GROUP)]
    live = [True] * len(groups)
    tick = 0
    while any(live):
        for g, group in enumerate(groups):
            if live[g] and tick >= g * WKV_SKEW:
                for ch in group:
                    if next(ch, "done") == "done":
                        live[g] = False
        tick += 1
    return [o[0][0] for o in outs], [o[0][1] for o in outs]


WKV_PAIRS_PER_STEP = 8


def _wkv_kernel(r_ref, v_ref, kn_ref, ld_ref, kd_ref, b_ref, y_ref, s_ref, cum_s):
    d = pl.program_id(0)
    c = pl.program_id(2)
    sign = 1 - 2 * d

    @pl.when(c == 0)
    def _():
        s_ref[...] = jnp.zeros_like(s_ref)

    tot8, dec8 = [], []
    for pi in range(WKV_PAIRS_PER_STEP):
        ld = ld_ref[0, pi]
        cum_up = ld
        for lvl in range(int(math.log2(CHUNK_T))):
            sh = BATCH << lvl
            cum_up = cum_up + jnp.concatenate([jnp.zeros((sh, PAIR), F32), cum_up[:-sh]], axis=0)
        tot8.append(cum_up[CHUNK_ROWS - BATCH:])
        tot = _bcast_rows(tot8[pi], CHUNK_ROWS)
        cum_s[pi] = jnp.where(d == 0, cum_up, tot - cum_up + ld)
        dec8.append(jnp.exp(tot8[pi]))

    problems = [(pl.ds(bi, CHUNK_T, stride=BATCH), pi, bi)
                for pi in range(WKV_PAIRS_PER_STEP) for bi in range(BATCH)]
    def loader(rw, pi, bi):
        def load():
            cum = cum_s[pi, rw, :]
            ld_b = ld_ref[0, pi, rw, :]
            b = b_ref[0, pi, rw, :]
            kd = kd_ref[0, pi, rw, :]
            e_out = jnp.exp(-cum)
            e_end = jnp.exp(tot8[pi][bi:bi + 1] - cum)
            return (-kn_ref[pi, rw, :] * jnp.exp(cum - ld_b), r_ref[pi, rw, :] * jnp.exp(cum),
                    b * e_out, kd * e_out, b * e_end, kd * e_end, v_ref[pi, rw, :])
        return load

    ys, s_new = _wkv_chunks([loader(*p) for p in problems],
                            [s_ref[pi, bi] for _, pi, bi in problems],
                            [dec8[pi][bi:bi + 1] for _, pi, bi in problems], sign)
    for k, (rw, pi, bi) in enumerate(problems):
        y_ref[0, pi, rw, :] = ys[k]
        s_ref[pi, bi] = s_new[k]


def _wkv(r, v, kn, ld, kd, b):
    def chunk(d, c):
        fwd = c
        bwd = jnp.where(c < N_CTX_CHUNKS, N_CTX_CHUNKS - 1 - c, N_CHUNKS + N_CTX_CHUNKS - 1 - c)
        return jnp.where(d == 0, fwd, bwd)
    lat = lambda d, c: chunk(d, jnp.maximum(c, N_CTX_CHUNKS)) - N_CTX_CHUNKS
    pps = WKV_PAIRS_PER_STEP
    shared = pl.BlockSpec((pps, CHUNK_ROWS, PAIR), lambda d, p, c: (p, chunk(d, c), 0))
    perdir = pl.BlockSpec((1, pps, CHUNK_ROWS, PAIR), lambda d, p, c: (d, p, chunk(d, c), 0))
    return pl.pallas_call(
        _wkv_kernel,
        grid=(2, N_PAIRS // pps, N_CHUNKS),
        in_specs=[shared, shared, shared, perdir, perdir, perdir],
        out_specs=pl.BlockSpec((1, pps, CHUNK_ROWS, PAIR), lambda d, p, c: (d, p, lat(d, c), 0)),
        out_shape=jax.ShapeDtypeStruct((2, N_PAIRS, SEQ * BATCH, PAIR), F32),
        scratch_shapes=[pltpu.VMEM((pps, BATCH, PAIR, PAIR), F32),
                        pltpu.VMEM((pps, CHUNK_ROWS, PAIR), F32)],
        compiler_params=_params(3),
        name="wkv_scan",
    )(r, v, kn, ld, kd, b)


MERGE_T = 64
MERGE_ROWS = MERGE_T * BATCH


def _merge_kernel(lru_ref, y_ref, bon_ref, g_ref, x_ref, gmix_ref, mod_ref,
                  lng_ref, lnb_ref, ones_ref, wgate_ref, wol_ref, wor_ref, wout_ref, o_ref):
    rows = MERGE_ROWS
    W = RWKV_WIDTH
    from_lru = _bdot(lru_ref[...], wol_ref[...])
    x = x_ref[...]
    h = _rms_modulate(x, gmix_ref[...], mod_ref[1], mod_ref[0])
    gates = _bdot(h, wgate_ref[...])
    y2 = (y_ref[0] + y_ref[1]).reshape(N_PAIRS, rows, PAIR)
    y = jnp.concatenate([y2[s] for s in range(N_PAIRS)], axis=1)
    inv_n = 1.0 / RWKV_HEAD

    ones = ones_ref[...]
    y_hi = y.astype(BF16).astype(F32)
    mu = (_head_sums(y_hi, ones) + _head_sums(y - y_hi, ones)) * inv_n
    dy = y - mu
    var = _head_sums(dy * dy, ones) * inv_n
    yn = dy * lax.rsqrt(var + GN_EPS) * lng_ref[...] + lnb_ref[...]
    rw = (yn + bon_ref[...].reshape(rows, W)) * g_ref[...].reshape(rows, W)
    m = (_sigmoid(gates[:, :D_MODEL]) * from_lru
         + _sigmoid(gates[:, D_MODEL:]) * _bdot(rw, wor_ref[...]))
    mix = _bdot(m, wout_ref[...])
    o_ref[...] = x + _bcast_rows(mod_ref[2], rows) * mix


def _merge(lru_l, y, bon, g, x_tb, g_mix, mod_m, ln_g, ln_b, ones, wgate, wol, wor, wout):
    W = RWKV_WIDTH
    n_blocks = SEQ // MERGE_T
    per_row = GRID_W // MERGE_T
    cm4 = lambda i: (i % per_row, i // per_row, 0, 0)
    y6 = y.reshape(2, N_PAIRS, GRID_W, GRID_ROWS, BATCH, PAIR)
    resident = lambda shape: pl.BlockSpec(shape, lambda i: (0,) * len(shape),
                                          pipeline_mode=pl.Buffered(1))
    bon4 = bon.reshape(GRID_W, GRID_ROWS, BATCH, W)
    g4 = g.reshape(GRID_W, GRID_ROWS, BATCH, W)
    return pl.pallas_call(
        _merge_kernel,
        grid=(n_blocks,),
        in_specs=[
            pl.BlockSpec((MERGE_ROWS, LRU_WIDTH), lambda i: (i, 0)),
            pl.BlockSpec((2, N_PAIRS, MERGE_T, 1, BATCH, PAIR), lambda i: (0, 0) + cm4(i)),
            pl.BlockSpec((MERGE_T, 1, BATCH, W), cm4),
            pl.BlockSpec((MERGE_T, 1, BATCH, W), cm4),
            pl.BlockSpec((MERGE_ROWS, D_MODEL), lambda i: (i, 0)),
            _const_spec((1, D_MODEL)), _const_spec((3, BATCH, D_MODEL)),
            _const_spec((1, W)), _const_spec((1, W)), _const_spec(ones.shape),
            resident(wgate.shape), resident(wol.shape), resident(wor.shape),
            resident(wout.shape),
        ],
        out_specs=pl.BlockSpec((MERGE_ROWS, D_MODEL), lambda i: (i, 0)),
        out_shape=jax.ShapeDtypeStruct((SEQ * BATCH, D_MODEL), F32),
        compiler_params=_params(1),
        name="merge",
    )(lru_l, y6, bon4, g4, x_tb, g_mix, mod_m, ln_g, ln_b, ones, wgate, wol, wor, wout)


FFN_T = 64
FFN_ROWS = FFN_T * BATCH
FFN_TILE = 256


def _ffn_kernel(x_ref, g_ref, sc_ref, sh_ref, gf_ref, win_ref, wout_ref, gfin_ref, o_ref,
                act_s, stage_s):
    x = x_ref[...]
    h = _rms_modulate(x, g_ref[...], sc_ref[...], sh_ref[...]).astype(BF16)
    for j in range(D_FF // FFN_TILE):
        cs = slice(j * FFN_TILE, (j + 1) * FFN_TILE)
        gate = jnp.dot(h, win_ref[:, cs], preferred_element_type=F32)
        up = jnp.dot(h, win_ref[:, D_FF + j * FFN_TILE:D_FF + (j + 1) * FFN_TILE],
                     preferred_element_type=F32)
        act_s[:, cs] = (_silu(gate) * up).astype(BF16)
    y = jnp.dot(act_s[...], wout_ref[...], preferred_element_type=F32)
    x2 = x + _bcast_rows(gf_ref[...], FFN_ROWS) * y
    ms = jnp.mean(x2 * x2, axis=-1, keepdims=True)
    out = x2 * lax.rsqrt(ms + RMS_EPS) * gfin_ref[...]
    for s in range(D_TILES):
        stage_s[s] = out[:, s * LANE:(s + 1) * LANE]
    for bi in range(BATCH):
        rows = pl.ds(bi, FFN_T, stride=BATCH)
        for s in range(D_TILES):
            o_ref[bi, :, s * LANE:(s + 1) * LANE] = stage_s[s, rows, :]


def _ffn(x1, g, sc8, sh8, g_f8, w_in, w_out, g_final):
    resident = lambda shape: pl.BlockSpec(shape, lambda i: (0,) * len(shape),
                                          pipeline_mode=pl.Buffered(1))
    return pl.pallas_call(
        _ffn_kernel,
        grid=(SEQ * BATCH // FFN_ROWS,),
        in_specs=[pl.BlockSpec((FFN_ROWS, D_MODEL), lambda i: (i, 0)),
                  _const_spec((1, D_MODEL)), _const_spec((BATCH, D_MODEL)),
                  _const_spec((BATCH, D_MODEL)), _const_spec((BATCH, D_MODEL)),
                  resident(w_in.shape), resident(w_out.shape), _const_spec((1, D_MODEL))],
        out_specs=pl.BlockSpec((BATCH, FFN_T, D_MODEL), lambda i: (0, i, 0)),
        out_shape=jax.ShapeDtypeStruct((BATCH, SEQ, D_MODEL), F32),
        scratch_shapes=[pltpu.VMEM((FFN_ROWS, D_FF), BF16),
                        pltpu.VMEM((D_TILES, FFN_ROWS, LANE), F32)],
        compiler_params=_params(1),
        name="ffn",
    )(x1, g, sc8, sh8, g_f8, w_in, w_out, g_final)


def _block_diag(w):
    n, c, _ = w.shape
    tiled = jnp.tile(w.reshape(n * c, c), (1, n))
    rb = lax.broadcasted_iota(jnp.int32, (n * c, n * c), 0) // c
    cb = lax.broadcasted_iota(jnp.int32, (n * c, n * c), 1) // c
    return jnp.where(rb == cb, tiled, 0.0)


def _lru_gate_tiles(wa, wx):
    da, dx = _block_diag(0.5 * wa), _block_diag(0.5 * wx)
    tiles = []
    for j, k0 in enumerate(LRU_GATE_K0):
        cs = slice(j * LRU_GATE_TILE, (j + 1) * LRU_GATE_TILE)
        tiles.append(jnp.concatenate([da[k0:k0 + LRU_GATE_K, cs], dx[k0:k0 + LRU_GATE_K, cs]], axis=1))
    return jnp.stack(tiles).astype(BF16)


def _two_dir_lora(w):
    z = jnp.zeros_like(w[0])
    return jnp.concatenate([jnp.concatenate([w[0], z], axis=1),
                            jnp.concatenate([z, w[1]], axis=1)], axis=0).astype(BF16)


def kernel(x, c, ctx, c_ctx, norm_mix_g, norm_ffn_g, w_mod, b_mod, w_in, lru_conv_w, lru_conv_b, lru_wa, lru_ba, lru_wx, lru_bx, lru_lambda, w_o_lru, rwkv_mu, rwkv_w0, rwkv_w2, rwkv_a0, rwkv_a2, rwkv_g2, rwkv_k_k, rwkv_k_a, rwkv_r_k, rwkv_ln_g, rwkv_ln_b, w_o_rwkv, w_out, w_ffn_in, w_ffn_out, norm_final_g):
    assert x.shape == (BATCH, SEQ, D_MODEL) and ctx.shape == (BATCH, CTX_LEN, D_MODEL)
    assert w_mod.shape[0] == 1, "single layer only"
    D, W = D_MODEL, RWKV_WIDTH

    c16 = jnp.concatenate([c, c_ctx[None], jnp.zeros((16 - BATCH - 1, D), F32)], axis=0)
    mod = _adaln(c16, w_mod[0], b_mod[0][None])
    mod_lat = mod[:BATCH].reshape(BATCH, 6, D)
    mod_ctx = jnp.broadcast_to(mod[BATCH:BATCH + 1], (BATCH, 6 * D)).reshape(BATCH, 6, D)
    sh_m, sc_m, g_m, sh_f, sc_f, g_f = [mod_lat[:, k] for k in range(6)]
    sh2 = jnp.stack([mod_ctx[:, 0], sh_m])
    sc2 = jnp.stack([mod_ctx[:, 1], sc_m])

    w_in0 = w_in[0]
    n_lru = 2 * LRU_WIDTH
    w_lru = w_in0[:, :n_lru].astype(BF16)
    w_rw = jnp.pad(w_in0[:, n_lru:n_lru + RWKV_IN], ((0, 0), (0, RWKV_IN_PAD - RWKV_IN))).astype(BF16)
    w_gate = w_in0[:, n_lru + RWKV_IN:].astype(BF16)
    g_mix = norm_mix_g[0][None]

    u_all, xc_tb, xl_tb = _proj(ctx, x, g_mix, sc2, sh2, w_lru)

    wt = jnp.stack([_lru_gate_tiles(lru_wa[0, d], lru_wx[0, d]) for d in range(2)])
    lru_p = (lru_conv_w[0], lru_conv_b[0], wt, 0.5 * lru_ba[0], 0.5 * lru_bx[0], lru_lambda[0])
    hf = _lru_scan(u_all, None, lru_p, 0, reverse=False)
    lru_l = _lru_scan(u_all, hf, lru_p, 1, reverse=True)

    mu_pad = jnp.pad(rwkv_mu[0], ((0, 0), (0, RWKV_IN_PAD - RWKV_IN)))
    ones = _block_diag(jnp.ones((PAIR // RWKV_HEAD, RWKV_HEAD, RWKV_HEAD), F32)).astype(BF16)
    rw_p = (mu_pad, _two_dir_lora(rwkv_w2[0]), rwkv_w0[0].reshape(1, 2 * W),
            _two_dir_lora(rwkv_a2[0]), rwkv_a0[0].reshape(1, 2 * W),
            jnp.pad(rwkv_g2[0], ((0, LORA_G_PAD - LORA_G), (0, 0))).astype(BF16),
            rwkv_k_k[0][None], rwkv_k_a[0][None], rwkv_r_k[0].reshape(1, W), ones)
    r, v, kn, ld, kd, b, bon, g = _rwkv_prep(xc_tb, xl_tb, g_mix, sc2, sh2, w_rw, rw_p)
    y = _wkv(r, v, kn, ld, kd, b)

    x1 = _merge(lru_l, y, bon, g, xl_tb, g_mix, jnp.stack([sh_m, sc_m, g_m]),
                rwkv_ln_g[0][None], rwkv_ln_b[0][None], ones, w_gate,
                w_o_lru[0].astype(BF16), w_o_rwkv[0].astype(BF16), w_out[0].astype(BF16))

    return _ffn(x1, norm_ffn_g[0][None], sc_f, sh_f, g_f, w_ffn_in[0].astype(BF16),
                w_ffn_out[0].astype(BF16), norm_final_g[None])
```

```python
import functools
import math

import jax
import jax.numpy as jnp
from jax import lax
from jax.experimental import pallas as pl
from jax.experimental.pallas import tpu as pltpu

F32 = jnp.float32
BF16 = jnp.bfloat16

D_MODEL = 1024
BATCH = 8
SEQ = 2048
CTX_LEN = 256
GRID_W = 64
GRID_ROWS = SEQ // GRID_W

LRU_WIDTH = 1280
LRU_BLOCKS = 16
LRU_BLOCK = LRU_WIDTH // LRU_BLOCKS
LRU_CONV = 4
LRU_C = 8.0

RWKV_HEAD = 64
RWKV_WIDTH = 1024
LORA_W = 64
LORA_A = 64
LORA_G = 160
RWKV_IN = 3 * RWKV_WIDTH + 2 * LORA_W + 2 * LORA_A + LORA_G
RWKV_IN_PAD = 3584
LORA_G_PAD = RWKV_IN_PAD - (3 * RWKV_WIDTH + 2 * LORA_W + 2 * LORA_A)
D_FF = 2816

RMS_EPS = 1e-6
GN_EPS = 64e-5
L2_EPS = 1e-12

T_ALL = CTX_LEN + SEQ
CHUNK_T = 64
CHUNK_ROWS = CHUNK_T * BATCH
N_CTX_CHUNKS = CTX_LEN // CHUNK_T
N_LAT_CHUNKS = SEQ // CHUNK_T
N_CHUNKS = N_CTX_CHUNKS + N_LAT_CHUNKS

LANE = 128
D_TILES = D_MODEL // LANE
PAIR = 2 * RWKV_HEAD
N_PAIRS = RWKV_WIDTH // PAIR

LRU_GATE_TILE = 256
LRU_GATE_K = 512
LRU_GATE_K0 = (0, 128, 384, 640, 768)

VMEM_LIMIT = 56 * 1024 * 1024


def _params(n_axes):
    return pltpu.CompilerParams(dimension_semantics=("arbitrary",) * n_axes,
                                vmem_limit_bytes=VMEM_LIMIT)


def _const_spec(shape):
    nd = len(shape)
    return pl.BlockSpec(shape, lambda *_: (0,) * nd)


def _bdot(a, b):
    return jnp.dot(a.astype(BF16), b.astype(BF16), preferred_element_type=F32)


def _softplus(x):
    return jnp.maximum(x, 0.0) + jnp.log1p(jnp.exp(-jnp.abs(x)))


def _sigmoid(x):
    return 0.5 * jnp.tanh(0.5 * x) + 0.5


def _head_sums(t, ones_pair):
    tb = t.astype(BF16)
    return jnp.concatenate(
        [jnp.dot(tb[:, s * PAIR:(s + 1) * PAIR], ones_pair, preferred_element_type=F32)
         for s in range(t.shape[1] // PAIR)], axis=1)


def _silu(x):
    return x * _sigmoid(x)


def _gelu_tanh(x):
    c = math.sqrt(2.0 / math.pi)
    return 0.5 * x * (1.0 + jnp.tanh(c * (x + 0.044715 * (x * x * x))))


def _bcast_rows(v8, rows):
    c = v8.shape[-1]
    return jnp.broadcast_to(v8[None], (rows // BATCH, BATCH, c)).reshape(rows, c)


def _rms_modulate(x, g, scale8, shift8):
    rows = x.shape[0]
    ms = jnp.mean(x * x, axis=-1, keepdims=True)
    y = x * lax.rsqrt(ms + RMS_EPS) * g
    return y * (1.0 + _bcast_rows(scale8, rows)) + _bcast_rows(shift8, rows)


def _adaln_kernel(c_ref, w_ref, b_ref, o_ref):
    s = _silu(c_ref[...])
    o_ref[...] = jnp.dot(s, w_ref[...], preferred_element_type=F32,
                         precision=lax.Precision.HIGHEST) + b_ref[...]


def _adaln(c16, w_mod, b_mod):
    n = w_mod.shape[1]
    tn = 1536
    return pl.pallas_call(
        _adaln_kernel,
        grid=(n // tn,),
        in_specs=[_const_spec((16, D_MODEL)),
                  pl.BlockSpec((D_MODEL, tn), lambda j: (0, j)),
                  pl.BlockSpec((1, tn), lambda j: (0, j))],
        out_specs=pl.BlockSpec((16, tn), lambda j: (0, j)),
        out_shape=jax.ShapeDtypeStruct((16, n), F32),
        compiler_params=_params(1),
        name="adaln",
    )(c16, w_mod, b_mod)


LANE = 128
D_TILES = D_MODEL // LANE


PROJ_GROUPS = 4
PROJ_T = CHUNK_T // PROJ_GROUPS
PROJ_ROWS = PROJ_T * BATCH


def _proj_kernel(ctx_ref, x_ref, g_ref, sc_ref, sh_ref, w_ref, o_ref, xc_o, xl_o, stage_s):
    is_ctx = pl.program_id(0) < N_CTX_CHUNKS

    def stage(gi):
        t0 = gi * PROJ_T
        for bi in range(BATCH):
            rows = pl.ds(gi * PROJ_ROWS + bi, PROJ_T, stride=BATCH)
            for s in range(D_TILES):
                ln = slice(s * LANE, (s + 1) * LANE)
                stage_s[s, rows, :] = jnp.where(is_ctx, ctx_ref[bi, t0:t0 + PROJ_T, ln],
                                                x_ref[bi, t0:t0 + PROJ_T, ln])

    def staged(rows):
        return jnp.concatenate([stage_s[s, rows, :] for s in range(D_TILES)], axis=1)

    def normalise(gi):
        rows = slice(gi * PROJ_ROWS, (gi + 1) * PROJ_ROWS)
        x = staged(rows)
        xl_o[rows, :] = x
        return _rms_modulate(x, g_ref[...], sc_ref[0], sh_ref[0]).astype(BF16)

    h = None
    for gi in range(PROJ_GROUPS + 1):
        if gi < PROJ_GROUPS:
            stage(gi)
        if gi > 0:
            o_ref[(gi - 1) * PROJ_ROWS:gi * PROJ_ROWS, :] = jnp.dot(
                h, w_ref[...], preferred_element_type=F32)
        if gi < PROJ_GROUPS:
            h = normalise(gi)

    @pl.when(is_ctx)
    def _():
        xc_o[...] = staged(slice(0, CHUNK_ROWS))


def _proj(ctx, x, g, sc2, sh2, w):
    n = w.shape[1]
    seg = lambda i: (i >= N_CTX_CHUNKS).astype(jnp.int32)
    cchunk = lambda i: jnp.minimum(i, N_CTX_CHUNKS - 1)
    lchunk = lambda i: jnp.maximum(i - N_CTX_CHUNKS, 0)
    return pl.pallas_call(
        _proj_kernel,
        grid=(N_CHUNKS,),
        in_specs=[pl.BlockSpec((BATCH, CHUNK_T, D_MODEL), lambda i: (0, cchunk(i), 0)),
                  pl.BlockSpec((BATCH, CHUNK_T, D_MODEL), lambda i: (0, lchunk(i), 0)),
                  _const_spec((1, D_MODEL)),
                  pl.BlockSpec((1, BATCH, D_MODEL), lambda i: (seg(i), 0, 0)),
                  pl.BlockSpec((1, BATCH, D_MODEL), lambda i: (seg(i), 0, 0)),
                  _const_spec((D_MODEL, n))],
        out_specs=[pl.BlockSpec((CHUNK_ROWS, n), lambda i: (i, 0)),
                   pl.BlockSpec((CHUNK_ROWS, D_MODEL), lambda i: (cchunk(i), 0)),
                   pl.BlockSpec((CHUNK_ROWS, D_MODEL), lambda i: (lchunk(i), 0))],
        out_shape=[jax.ShapeDtypeStruct((N_CHUNKS * CHUNK_ROWS, n), F32),
                   jax.ShapeDtypeStruct((CTX_LEN * BATCH, D_MODEL), F32),
                   jax.ShapeDtypeStruct((SEQ * BATCH, D_MODEL), F32)],
        scratch_shapes=[pltpu.VMEM((D_TILES, CHUNK_ROWS, LANE), F32)],
        compiler_params=_params(1),
        name="in_proj",
    )(ctx, x, g, sc2, sh2, w)


def _lru_kernel(*refs, reverse):
    if reverse:
        (u_ref, uy_ref, hf_ref, cw_ref, cb_ref, wt_ref, ba_ref, bx_ref, lam_ref,
         o_ref, h_s, halo_s, ext_s, a_s, b_s) = refs
    else:
        (u_ref, cw_ref, cb_ref, wt_ref, ba_ref, bx_ref, lam_ref,
         o_ref, h_s, halo_s, ext_s, a_s, b_s) = refs
    i = pl.program_id(0)
    halo_rows = (LRU_CONV - 1) * BATCH

    @pl.when(i == 0)
    def _():
        h_s[...] = jnp.zeros_like(h_s)

    @pl.when((i == 0) | (i == N_CTX_CHUNKS))
    def _():
        halo_s[...] = jnp.zeros_like(halo_s)

    u = u_ref[...]
    if reverse:
        ext_s[0:CHUNK_ROWS] = u
        ext_s[CHUNK_ROWS:CHUNK_ROWS + halo_rows] = halo_s[...]
        halo_s[...] = u[0:halo_rows]
    else:
        ext_s[0:halo_rows] = halo_s[...]
        ext_s[halo_rows:halo_rows + CHUNK_ROWS] = u
        halo_s[...] = u[CHUNK_ROWS - halo_rows:CHUNK_ROWS]

    xc = jnp.broadcast_to(cb_ref[...], (CHUNK_ROWS, LRU_WIDTH))
    for j in range(LRU_CONV):
        off = (LRU_CONV - 1 - j) if reverse else j
        xc = xc + cw_ref[j:j + 1, :] * ext_s[off * BATCH:off * BATCH + CHUNK_ROWS]

    xcb = xc.astype(BF16)
    c_half = (-0.5 * LRU_C) * _softplus(-lam_ref[...])
    for j in range(LRU_WIDTH // LRU_GATE_TILE):
        k0 = LRU_GATE_K0[j]
        cs = slice(j * LRU_GATE_TILE, (j + 1) * LRU_GATE_TILE)
        g = jnp.dot(xcb[:, k0:k0 + LRU_GATE_K], wt_ref[j], preferred_element_type=F32)
        tanh_r = jnp.tanh(g[:, :LRU_GATE_TILE] + ba_ref[:, cs])
        gate_i = 0.5 * jnp.tanh(g[:, LRU_GATE_TILE:] + bx_ref[:, cs]) + 0.5
        log_a = c_half[:, cs] * tanh_r + c_half[:, cs]
        a = jnp.exp(log_a)
        a_s[:, cs] = a
        om = 1.0 - a * a
        b_s[:, cs] = jnp.where(om > 0.0, om * lax.rsqrt(om), 0.0) * (gate_i * xc[:, cs])

    def step(k, h):
        t = (CHUNK_T - 1 - k) if reverse else k
        rows = pl.ds(pl.multiple_of(t * BATCH, BATCH), BATCH)
        h = a_s[rows, :] * h + b_s[rows, :]
        if reverse:
            o_ref[rows, :] = (hf_ref[rows, :] + h) * _gelu_tanh(uy_ref[rows, :])
        else:
            o_ref[rows, :] = h
        return h

    h_s[...] = lax.fori_loop(0, CHUNK_T, step, h_s[...], unroll=8)


def _lru_scan(u_all, hf, lru_p, d, *, reverse):
    cw, cb, wt, ba, bx, lam = lru_p
    if reverse:
        chunk = lambda i: jnp.where(i < N_CTX_CHUNKS, N_CTX_CHUNKS - 1 - i,
                                    N_CHUNKS + N_CTX_CHUNKS - 1 - i)
    else:
        chunk = lambda i: i
    lat = lambda i: jnp.maximum(chunk(jnp.maximum(i, N_CTX_CHUNKS)) - N_CTX_CHUNKS, 0)
    blk = (CHUNK_ROWS, LRU_WIDTH)
    in_specs = [pl.BlockSpec(blk, lambda i: (chunk(i), 0))]
    args = [u_all]
    if reverse:
        in_specs += [pl.BlockSpec(blk, lambda i: (lat(i) + N_CTX_CHUNKS, 1)),
                     pl.BlockSpec(blk, lambda i: (lat(i), 0))]
        args += [u_all, hf]
    in_specs += [_const_spec((LRU_CONV, LRU_WIDTH)), _const_spec((1, LRU_WIDTH)),
                 _const_spec(wt[d].shape), _const_spec((1, LRU_WIDTH)),
                 _const_spec((1, LRU_WIDTH)), _const_spec((1, LRU_WIDTH))]
    args += [cw[d], cb[d][None], wt[d], ba[d][None], bx[d][None], lam[d][None]]
    halo_rows = (LRU_CONV - 1) * BATCH
    return pl.pallas_call(
        functools.partial(_lru_kernel, reverse=reverse),
        grid=(N_CHUNKS,),
        in_specs=in_specs,
        out_specs=pl.BlockSpec(blk, lambda i: (lat(i), 0)),
        out_shape=jax.ShapeDtypeStruct((SEQ * BATCH, LRU_WIDTH), F32),
        scratch_shapes=[pltpu.VMEM((BATCH, LRU_WIDTH), F32),
                        pltpu.VMEM((halo_rows, LRU_WIDTH), F32),
                        pltpu.VMEM((CHUNK_ROWS + halo_rows, LRU_WIDTH), F32),
                        pltpu.VMEM(blk, F32),
                        pltpu.VMEM(blk, F32)],
        compiler_params=_params(1),
        name="lru_bwd" if reverse else "lru_fwd",
    )(*args)


PREP_T = 32
PREP_ROWS = PREP_T * BATCH
PREP_CTX_BLOCKS = CTX_LEN // PREP_T
PREP_LAT_BLOCKS = SEQ // PREP_T
PREP_BLOCKS = PREP_CTX_BLOCKS + PREP_LAT_BLOCKS


def _prep_kernel(xc_ref, xl_ref, xcp_ref, xcn_ref, xlp_ref, xln_ref, gmix_ref, sc_ref, sh_ref,
                 win_ref, mu_ref,
                 w2_ref, w0_ref, a2_ref, a0_ref, g2_ref, kk_ref, ka_ref, rk_ref, ones_ref,
                 r_o, v_o, kn_o, ld_o, kd_o, b_o, bon_o, g_o, ext_s):
    i = pl.program_id(0)
    is_ctx = i < PREP_CTX_BLOCKS
    first = (i == 0) | (i == PREP_CTX_BLOCKS)
    last = (i == PREP_CTX_BLOCKS - 1) | (i == PREP_BLOCKS - 1)
    lat_rows = lambda ref: ref[...].reshape(-1, D_MODEL)
    x = jnp.concatenate([
        jnp.where(is_ctx, xcp_ref[...], lat_rows(xlp_ref)),
        jnp.where(is_ctx, xc_ref[...], lat_rows(xl_ref)),
        jnp.where(is_ctx, xcn_ref[...], lat_rows(xln_ref))], axis=0)
    seg = jnp.where(is_ctx, 0, 1)
    h = _rms_modulate(x, gmix_ref[...], sc_ref[seg], sh_ref[seg])
    hb = h.astype(BF16)
    W = RWKV_WIDTH
    ones = ones_ref[...]

    def project(c0, c1):
        zt = jnp.dot(hb, win_ref[:, c0:c1], preferred_element_type=F32)
        ext_s[0:BATCH, c0:c1] = jnp.where(first, 0.0, zt[0:BATCH])
        ext_s[BATCH:BATCH + PREP_ROWS, c0:c1] = zt[BATCH:BATCH + PREP_ROWS]
        ext_s[BATCH + PREP_ROWS:2 * BATCH + PREP_ROWS, c0:c1] = jnp.where(
            last, 0.0, zt[BATCH + PREP_ROWS:2 * BATCH + PREP_ROWS])

    def shifted(c0, c1):
        zc = ext_s[BATCH:BATCH + PREP_ROWS, c0:c1]
        zprev = ext_s[0:PREP_ROWS, c0:c1]
        znext = ext_s[2 * BATCH:2 * BATCH + PREP_ROWS, c0:c1]
        return zc + mu_ref[0:1, c0:c1] * (zprev - zc) + mu_ref[1:2, c0:c1] * (znext - zc)

    def project_pairs(g):
        for part in range(3):
            project(part * W + 2 * g * PAIR, part * W + (2 * g + 2) * PAIR)

    def finish_pair(s, w_pre, a_pre):
        ln = slice(s * PAIR, (s + 1) * PAIR)
        r = shifted(s * PAIR, (s + 1) * PAIR)
        k = shifted(W + s * PAIR, W + (s + 1) * PAIR)
        v = shifted(2 * W + s * PAIR, 2 * W + (s + 1) * PAIR)
        kk = k * kk_ref[:, ln]
        ss = jnp.dot((kk * kk).astype(BF16), ones, preferred_element_type=F32)
        kn = kk / jnp.maximum(jnp.sqrt(ss), L2_EPS)
        r_o[s] = r
        v_o[s] = v
        kn_o[s] = kn
        kd_sum = jnp.zeros_like(k)
        for d in range(2):
            cs = slice(d * W + s * PAIR, d * W + (s + 1) * PAIR)
            ld_o[d, s] = -math.exp(-0.5) * _sigmoid(w_pre[:, cs])
            asig = _sigmoid(a_pre[:, cs])
            kd = k * (1.0 + (asig - 1.0) * ka_ref[:, ln])
            kd_o[d, s] = kd
            b_o[d, s] = kn * asig
            kd_sum = kd_sum + kd
        bon = jnp.dot((r * kd_sum * rk_ref[:, ln]).astype(BF16), ones, preferred_element_type=F32)
        bon_o[:, ln] = bon * v

    project(3 * W, RWKV_IN_PAD)
    project_pairs(0)
    wd = shifted(3 * W, 3 * W + 2 * LORA_W)
    ad = shifted(3 * W + 2 * LORA_W, 3 * W + 2 * LORA_W + 2 * LORA_A)
    gd = shifted(3 * W + 2 * LORA_W + 2 * LORA_A, RWKV_IN_PAD)
    w_pre = _bdot(jnp.tanh(wd), w2_ref[...]) + w0_ref[...]
    a_pre = _bdot(ad, a2_ref[...]) + a0_ref[...]
    g_o[...] = _bdot(_sigmoid(gd), g2_ref[...])
    for g in range(N_PAIRS // 2):
        finish_pair(2 * g, w_pre, a_pre)
        if g + 1 < N_PAIRS // 2:
            project_pairs(g + 1)
        finish_pair(2 * g + 1, w_pre, a_pre)


def _rwkv_prep(xc_tb, xl_tb, g_mix, sc2, sh2, w_rw, rw_p):
    mu, w2, w0, a2, a0, g2, k_k, k_a, r_k, ones = rw_p
    W = RWKV_WIDTH
    hb = PREP_T
    cmain = lambda i: jnp.minimum(i, PREP_CTX_BLOCKS - 1)
    lmain = lambda i: jnp.maximum(i - PREP_CTX_BLOCKS, 0)
    assert PREP_T == GRID_ROWS
    xl4 = xl_tb.reshape(GRID_ROWS, GRID_W, BATCH, D_MODEL)
    hblk = (BATCH, D_MODEL)
    n_c8 = CTX_LEN - 1
    in_specs = [
        pl.BlockSpec((PREP_ROWS, D_MODEL), lambda i: (cmain(i), 0)),
        pl.BlockSpec((GRID_ROWS, 1, BATCH, D_MODEL), lambda i: (0, lmain(i), 0, 0)),
        pl.BlockSpec(hblk, lambda i: (jnp.maximum(cmain(i) * hb - 1, 0), 0)),
        pl.BlockSpec(hblk, lambda i: (jnp.minimum((cmain(i) + 1) * hb, n_c8), 0)),
        pl.BlockSpec((1, 1, BATCH, D_MODEL), lambda i: (GRID_ROWS - 1, jnp.maximum(lmain(i) - 1, 0), 0, 0)),
        pl.BlockSpec((1, 1, BATCH, D_MODEL), lambda i: (0, jnp.minimum(lmain(i) + 1, GRID_W - 1), 0, 0)),
        _const_spec((1, D_MODEL)), _const_spec(sc2.shape), _const_spec(sh2.shape),
        _const_spec(w_rw.shape),
        _const_spec(mu.shape), _const_spec(w2.shape), _const_spec(w0.shape),
        _const_spec(a2.shape), _const_spec(a0.shape), _const_spec(g2.shape),
        _const_spec(k_k.shape), _const_spec(k_a.shape), _const_spec(r_k.shape),
        _const_spec(ones.shape),
    ]
    n_rows = T_ALL * BATCH
    n_lat = SEQ * BATCH
    shared = pl.BlockSpec((N_PAIRS, PREP_ROWS, PAIR), lambda i: (0, i, 0))
    perdir = pl.BlockSpec((2, N_PAIRS, PREP_ROWS, PAIR), lambda i: (0, 0, i, 0))
    latonly = pl.BlockSpec((PREP_ROWS, W), lambda i: (lmain(i), 0))
    return pl.pallas_call(
        _prep_kernel,
        grid=(PREP_BLOCKS,),
        in_specs=in_specs,
        out_specs=[shared, shared, shared, perdir, perdir, perdir, latonly, latonly],
        out_shape=[jax.ShapeDtypeStruct((N_PAIRS, n_rows, PAIR), F32)] * 3
        + [jax.ShapeDtypeStruct((2, N_PAIRS, n_rows, PAIR), F32)] * 3
        + [jax.ShapeDtypeStruct((n_lat, W), F32)] * 2,
        scratch_shapes=[pltpu.VMEM((PREP_ROWS + 2 * BATCH, RWKV_IN_PAD), F32)],
        compiler_params=_params(1),
        name="rwkv_prep",
    )(xc_tb, xl4, xc_tb, xc_tb, xl4, xl4, g_mix, sc2, sh2, w_rw,
      mu, w2, w0, a2, a0, g2, k_k, k_a, r_k, ones)


WKV_GROUP = 8
WKV_SKEW = 1


def _wkv_chunks(loaders, states, decays, sign):
    T = CHUNK_T
    R2 = 2 * T
    nb = len(loaders)
    levels = int(math.log2(T))

    lane = lax.broadcasted_iota(jnp.int32, (R2, PAIR), 1)
    row = lax.broadcasted_iota(jnp.int32, (R2, PAIR), 0)
    head_mask = (lane // RWKV_HEAD) == (row // T)

    def stack_masked(x):
        return jnp.where(head_mask, jnp.concatenate([x, x], axis=0), 0.0).astype(BF16)

    tw = lax.broadcasted_iota(jnp.int32, (T, 2 * R2), 0)
    sw = lax.broadcasted_iota(jnp.int32, (T, 2 * R2), 1) % T
    dtw = (tw - sw) * sign
    strict_w = dtw[:, 0:R2] > 0
    incl_w2 = dtw >= 0
    eye_w = jnp.where(dtw[:, 0:R2] == 0, 1.0, 0.0)

    nt = lambda x, y: lax.dot_general(x, y, (((1,), (1,)), ((), ())), preferred_element_type=F32)
    mm = lambda x, y: jnp.dot(x, y, preferred_element_type=F32)

    def chain(load, S, decay, out):
        a_t, r_t, b_t, k_t, b_e, k_e, v = load()
        ar = jnp.concatenate([a_t, r_t], axis=0).astype(BF16)
        bk = jnp.concatenate([stack_masked(b_t), stack_masked(k_t)], axis=0)
        big = nt(ar, bk)
        fs = nt(ar, S.astype(BF16))
        uv_rhs = jnp.concatenate([b_e, k_e], axis=0).astype(BF16)
        yield
        p_w = jnp.where(strict_w, big[0:T, 0:R2], 0.0)
        a_ak = jnp.where(strict_w, big[0:T, R2:2 * R2], 0.0).astype(BF16)
        a_rbk = jnp.where(incl_w2, big[T:R2], 0.0).astype(BF16)
        v_st = stack_masked(v)
        h = mm(a_ak, v_st)
        inv_w = eye_w + p_w
        p_w = mm(p_w.astype(BF16), stack_masked(p_w))
        yield
        for lvl in range(1, levels):
            p_bd = stack_masked(p_w)
            if lvl == levels - 1:
                inv_w = inv_w + mm(inv_w.astype(BF16), p_bd)
            else:
                both = mm(jnp.concatenate([p_w, inv_w], axis=0).astype(BF16), p_bd)
                yield
                p_w = both[0:T]
                inv_w = inv_w + both[T:R2]
        yield
        u = mm(inv_w.astype(BF16), stack_masked(fs[0:T] + h))
        yield
        y = fs[T:R2] + mm(a_rbk, jnp.concatenate([stack_masked(u), v_st], axis=0))
        uv_t = jnp.concatenate([u, v], axis=0).T.astype(BF16)
        out.append((y, S * decay + jnp.where(head_mask, mm(uv_t, uv_rhs), 0.0)))

    outs = [[] for _ in range(nb)]
    chains = [chain(loaders[i], states[i], decays[i], outs[i]) for i in range(nb)]
    groups = [chains[g:g + WKV_GROUP] for g in range(0, nb, WKV_GROUP)]
    live = [True] * len(groups)
    tick = 0
    while any(live):
        for g, group in enumerate(groups):
            if live[g] and tick >= g * WKV_SKEW:
                for ch in group:
                    if next(ch, "done") == "done":
                        live[g] = False
        tick += 1
    return [o[0][0] for o in outs], [o[0][1] for o in outs]


WKV_PAIRS_PER_STEP = 8


def _wkv_kernel(r_ref, v_ref, kn_ref, ld_ref, kd_ref, b_ref, y_ref, s_ref, cum_s):
    d = pl.program_id(0)
    c = pl.program_id(2)
    sign = 1 - 2 * d

    @pl.when(c == 0)
    def _():
        s_ref[...] = jnp.zeros_like(s_ref)

    tot8, dec8 = [], []
    for pi in range(WKV_PAIRS_PER_STEP):
        ld = ld_ref[0, pi]
        cum_up = ld
        for lvl in range(int(math.log2(CHUNK_T))):
            sh = BATCH << lvl
            cum_up = cum_up + jnp.concatenate([jnp.zeros((sh, PAIR), F32), cum_up[:-sh]], axis=0)
        tot8.append(cum_up[CHUNK_ROWS - BATCH:])
        tot = _bcast_rows(tot8[pi], CHUNK_ROWS)
        cum_s[pi] = jnp.where(d == 0, cum_up, tot - cum_up + ld)
        dec8.append(jnp.exp(tot8[pi]))

    problems = [(pl.ds(bi, CHUNK_T, stride=BATCH), pi, bi)
                for pi in range(WKV_PAIRS_PER_STEP) for bi in range(BATCH)]
    def loader(rw, pi, bi):
        def load():
            cum = cum_s[pi, rw, :]
            ld_b = ld_ref[0, pi, rw, :]
            b = b_ref[0, pi, rw, :]
            kd = kd_ref[0, pi, rw, :]
            e_out = jnp.exp(-cum)
            e_end = jnp.exp(tot8[pi][bi:bi + 1] - cum)
            return (-kn_ref[pi, rw, :] * jnp.exp(cum - ld_b), r_ref[pi, rw, :] * jnp.exp(cum),
                    b * e_out, kd * e_out, b * e_end, kd * e_end, v_ref[pi, rw, :])
        return load

    ys, s_new = _wkv_chunks([loader(*p) for p in problems],
                            [s_ref[pi, bi] for _, pi, bi in problems],
                            [dec8[pi][bi:bi + 1] for _, pi, bi in problems], sign)
    for k, (rw, pi, bi) in enumerate(problems):
        y_ref[0, pi, rw, :] = ys[k]
        s_ref[pi, bi] = s_new[k]


def _wkv(r, v, kn, ld, kd, b):
    def chunk(d, c):
        fwd = c
        bwd = jnp.where(c < N_CTX_CHUNKS, N_CTX_CHUNKS - 1 - c, N_CHUNKS + N_CTX_CHUNKS - 1 - c)
        return jnp.where(d == 0, fwd, bwd)
    lat = lambda d, c: chunk(d, jnp.maximum(c, N_CTX_CHUNKS)) - N_CTX_CHUNKS
    pps = WKV_PAIRS_PER_STEP
    shared = pl.BlockSpec((pps, CHUNK_ROWS, PAIR), lambda d, p, c: (p, chunk(d, c), 0))
    perdir = pl.BlockSpec((1, pps, CHUNK_ROWS, PAIR), lambda d, p, c: (d, p, chunk(d, c), 0))
    return pl.pallas_call(
        _wkv_kernel,
        grid=(2, N_PAIRS // pps, N_CHUNKS),
        in_specs=[shared, shared, shared, perdir, perdir, perdir],
        out_specs=pl.BlockSpec((1, pps, CHUNK_ROWS, PAIR), lambda d, p, c: (d, p, lat(d, c), 0)),
        out_shape=jax.ShapeDtypeStruct((2, N_PAIRS, SEQ * BATCH, PAIR), F32),
        scratch_shapes=[pltpu.VMEM((pps, BATCH, PAIR, PAIR), F32),
                        pltpu.VMEM((pps, CHUNK_ROWS, PAIR), F32)],
        compiler_params=_params(3),
        name="wkv_scan",
    )(r, v, kn, ld, kd, b)


MERGE_T = 64
MERGE_ROWS = MERGE_T * BATCH


def _merge_kernel(lru_ref, y_ref, bon_ref, g_ref, x_ref, gmix_ref, mod_ref,
                  lng_ref, lnb_ref, ones_ref, wgate_ref, wol_ref, wor_ref, wout_ref, o_ref):
    rows = MERGE_ROWS
    W = RWKV_WIDTH
    from_lru = _bdot(lru_ref[...], wol_ref[...])
    x = x_ref[...]
    h = _rms_modulate(x, gmix_ref[...], mod_ref[1], mod_ref[0])
    gates = _bdot(h, wgate_ref[...])
    y2 = (y_ref[0] + y_ref[1]).reshape(N_PAIRS, rows, PAIR)
    y = jnp.concatenate([y2[s] for s in range(N_PAIRS)], axis=1)
    inv_n = 1.0 / RWKV_HEAD

    ones = ones_ref[...]
    y_hi = y.astype(BF16).astype(F32)
    mu = (_head_sums(y_hi, ones) + _head_sums(y - y_hi, ones)) * inv_n
    dy = y - mu
    var = _head_sums(dy * dy, ones) * inv_n
    yn = dy * lax.rsqrt(var + GN_EPS) * lng_ref[...] + lnb_ref[...]
    rw = (yn + bon_ref[...].reshape(rows, W)) * g_ref[...].reshape(rows, W)
    m = (_sigmoid(gates[:, :D_MODEL]) * from_lru
         + _sigmoid(gates[:, D_MODEL:]) * _bdot(rw, wor_ref[...]))
    mix = _bdot(m, wout_ref[...])
    o_ref[...] = x + _bcast_rows(mod_ref[2], rows) * mix


def _merge(lru_l, y, bon, g, x_tb, g_mix, mod_m, ln_g, ln_b, ones, wgate, wol, wor, wout):
    W = RWKV_WIDTH
    n_blocks = SEQ // MERGE_T
    per_row = GRID_W // MERGE_T
    cm4 = lambda i: (i % per_row, i // per_row, 0, 0)
    y6 = y.reshape(2, N_PAIRS, GRID_W, GRID_ROWS, BATCH, PAIR)
    resident = lambda shape: pl.BlockSpec(shape, lambda i: (0,) * len(shape),
                                          pipeline_mode=pl.Buffered(1))
    bon4 = bon.reshape(GRID_W, GRID_ROWS, BATCH, W)
    g4 = g.reshape(GRID_W, GRID_ROWS, BATCH, W)
    return pl.pallas_call(
        _merge_kernel,
        grid=(n_blocks,),
        in_specs=[
            pl.BlockSpec((MERGE_ROWS, LRU_WIDTH), lambda i: (i, 0)),
            pl.BlockSpec((2, N_PAIRS, MERGE_T, 1, BATCH, PAIR), lambda i: (0, 0) + cm4(i)),
            pl.BlockSpec((MERGE_T, 1, BATCH, W), cm4),
            pl.BlockSpec((MERGE_T, 1, BATCH, W), cm4),
            pl.BlockSpec((MERGE_ROWS, D_MODEL), lambda i: (i, 0)),
            _const_spec((1, D_MODEL)), _const_spec((3, BATCH, D_MODEL)),
            _const_spec((1, W)), _const_spec((1, W)), _const_spec(ones.shape),
            resident(wgate.shape), resident(wol.shape), resident(wor.shape),
            resident(wout.shape),
        ],
        out_specs=pl.BlockSpec((MERGE_ROWS, D_MODEL), lambda i: (i, 0)),
        out_shape=jax.ShapeDtypeStruct((SEQ * BATCH, D_MODEL), F32),
        compiler_params=_params(1),
        name="merge",
    )(lru_l, y6, bon4, g4, x_tb, g_mix, mod_m, ln_g, ln_b, ones, wgate, wol, wor, wout)


FFN_T = 64
FFN_ROWS = FFN_T * BATCH
FFN_TILE = 256


FFN_GROUPS = 2


def _ffn_kernel(x_ref, g_ref, sc_ref, sh_ref, gf_ref, win_ref, wout_ref, gfin_ref, o_ref,
                act_s, stage_s):
    t_g = FFN_T // FFN_GROUPS
    rows_g = t_g * BATCH

    def up(gi):
        rows = slice(gi * rows_g, (gi + 1) * rows_g)
        h = _rms_modulate(x_ref[rows, :], g_ref[...], sc_ref[...], sh_ref[...]).astype(BF16)
        for j in range(D_FF // FFN_TILE):
            cs = slice(j * FFN_TILE, (j + 1) * FFN_TILE)
            gate = jnp.dot(h, win_ref[:, cs], preferred_element_type=F32)
            lin = jnp.dot(h, win_ref[:, D_FF + j * FFN_TILE:D_FF + (j + 1) * FFN_TILE],
                          preferred_element_type=F32)
            act_s[rows, cs] = (_silu(gate) * lin).astype(BF16)

    def down(gi):
        rows = slice(gi * rows_g, (gi + 1) * rows_g)
        y = jnp.dot(act_s[rows, :], wout_ref[...], preferred_element_type=F32)
        x2 = x_ref[rows, :] + _bcast_rows(gf_ref[...], rows_g) * y
        ms = jnp.mean(x2 * x2, axis=-1, keepdims=True)
        out = x2 * lax.rsqrt(ms + RMS_EPS) * gfin_ref[...]
        for s in range(D_TILES):
            stage_s[s, rows, :] = out[:, s * LANE:(s + 1) * LANE]
        for bi in range(BATCH):
            src = pl.ds(gi * rows_g + bi, t_g, stride=BATCH)
            for s in range(D_TILES):
                o_ref[bi, gi * t_g:(gi + 1) * t_g, s * LANE:(s + 1) * LANE] = stage_s[s, src, :]

    up(0)
    for gi in range(FFN_GROUPS):
        if gi + 1 < FFN_GROUPS:
            up(gi + 1)
        down(gi)


def _ffn(x1, g, sc8, sh8, g_f8, w_in, w_out, g_final):
    resident = lambda shape: pl.BlockSpec(shape, lambda i: (0,) * len(shape),
                                          pipeline_mode=pl.Buffered(1))
    return pl.pallas_call(
        _ffn_kernel,
        grid=(SEQ * BATCH // FFN_ROWS,),
        in_specs=[pl.BlockSpec((FFN_ROWS, D_MODEL), lambda i: (i, 0)),
                  _const_spec((1, D_MODEL)), _const_spec((BATCH, D_MODEL)),
                  _const_spec((BATCH, D_MODEL)), _const_spec((BATCH, D_MODEL)),
                  resident(w_in.shape), resident(w_out.shape), _const_spec((1, D_MODEL))],
        out_specs=pl.BlockSpec((BATCH, FFN_T, D_MODEL), lambda i: (0, i, 0)),
        out_shape=jax.ShapeDtypeStruct((BATCH, SEQ, D_MODEL), F32),
        scratch_shapes=[pltpu.VMEM((FFN_ROWS, D_FF), BF16),
                        pltpu.VMEM((D_TILES, FFN_ROWS, LANE), F32)],
        compiler_params=_params(1),
        name="ffn",
    )(x1, g, sc8, sh8, g_f8, w_in, w_out, g_final)


def _block_diag(w):
    n, c, _ = w.shape
    tiled = jnp.tile(w.reshape(n * c, c), (1, n))
    rb = lax.broadcasted_iota(jnp.int32, (n * c, n * c), 0) // c
    cb = lax.broadcasted_iota(jnp.int32, (n * c, n * c), 1) // c
    return jnp.where(rb == cb, tiled, 0.0)


def _lru_gate_tiles(wa, wx):
    da, dx = _block_diag(0.5 * wa), _block_diag(0.5 * wx)
    tiles = []
    for j, k0 in enumerate(LRU_GATE_K0):
        cs = slice(j * LRU_GATE_TILE, (j + 1) * LRU_GATE_TILE)
        tiles.append(jnp.concatenate([da[k0:k0 + LRU_GATE_K, cs], dx[k0:k0 + LRU_GATE_K, cs]], axis=1))
    return jnp.stack(tiles).astype(BF16)


def _two_dir_lora(w):
    z = jnp.zeros_like(w[0])
    return jnp.concatenate([jnp.concatenate([w[0], z], axis=1),
                            jnp.concatenate([z, w[1]], axis=1)], axis=0).astype(BF16)


def kernel(x, c, ctx, c_ctx, norm_mix_g, norm_ffn_g, w_mod, b_mod, w_in, lru_conv_w, lru_conv_b, lru_wa, lru_ba, lru_wx, lru_bx, lru_lambda, w_o_lru, rwkv_mu, rwkv_w0, rwkv_w2, rwkv_a0, rwkv_a2, rwkv_g2, rwkv_k_k, rwkv_k_a, rwkv_r_k, rwkv_ln_g, rwkv_ln_b, w_o_rwkv, w_out, w_ffn_in, w_ffn_out, norm_final_g):
    assert x.shape == (BATCH, SEQ, D_MODEL) and ctx.shape == (BATCH, CTX_LEN, D_MODEL)
    assert w_mod.shape[0] == 1, "single layer only"
    D, W = D_MODEL, RWKV_WIDTH

    c16 = jnp.concatenate([c, c_ctx[None], jnp.zeros((16 - BATCH - 1, D), F32)], axis=0)
    mod = _adaln(c16, w_mod[0], b_mod[0][None])
    mod_lat = mod[:BATCH].reshape(BATCH, 6, D)
    mod_ctx = jnp.broadcast_to(mod[BATCH:BATCH + 1], (BATCH, 6 * D)).reshape(BATCH, 6, D)
    sh_m, sc_m, g_m, sh_f, sc_f, g_f = [mod_lat[:, k] for k in range(6)]
    sh2 = jnp.stack([mod_ctx[:, 0], sh_m])
    sc2 = jnp.stack([mod_ctx[:, 1], sc_m])

    w_in0 = w_in[0]
    n_lru = 2 * LRU_WIDTH
    w_lru = w_in0[:, :n_lru].astype(BF16)
    w_rw = jnp.pad(w_in0[:, n_lru:n_lru + RWKV_IN], ((0, 0), (0, RWKV_IN_PAD - RWKV_IN))).astype(BF16)
    w_gate = w_in0[:, n_lru + RWKV_IN:].astype(BF16)
    g_mix = norm_mix_g[0][None]

    u_all, xc_tb, xl_tb = _proj(ctx, x, g_mix, sc2, sh2, w_lru)

    wt = jnp.stack([_lru_gate_tiles(lru_wa[0, d], lru_wx[0, d]) for d in range(2)])
    lru_p = (lru_conv_w[0], lru_conv_b[0], wt, 0.5 * lru_ba[0], 0.5 * lru_bx[0], lru_lambda[0])
    hf = _lru_scan(u_all, None, lru_p, 0, reverse=False)
    lru_l = _lru_scan(u_all, hf, lru_p, 1, reverse=True)

    mu_pad = jnp.pad(rwkv_mu[0], ((0, 0), (0, RWKV_IN_PAD - RWKV_IN)))
    ones = _block_diag(jnp.ones((PAIR // RWKV_HEAD, RWKV_HEAD, RWKV_HEAD), F32)).astype(BF16)
    rw_p = (mu_pad, _two_dir_lora(rwkv_w2[0]), rwkv_w0[0].reshape(1, 2 * W),
            _two_dir_lora(rwkv_a2[0]), rwkv_a0[0].reshape(1, 2 * W),
            jnp.pad(rwkv_g2[0], ((0, LORA_G_PAD - LORA_G), (0, 0))).astype(BF16),
            rwkv_k_k[0][None], rwkv_k_a[0][None], rwkv_r_k[0].reshape(1, W), ones)
    r, v, kn, ld, kd, b, bon, g = _rwkv_prep(xc_tb, xl_tb, g_mix, sc2, sh2, w_rw, rw_p)
    y = _wkv(r, v, kn, ld, kd, b)

    x1 = _merge(lru_l, y, bon, g, xl_tb, g_mix, jnp.stack([sh_m, sc_m, g_m]),
                rwkv_ln_g[0][None], rwkv_ln_b[0][None], ones, w_gate,
                w_o_lru[0].astype(BF16), w_o_rwkv[0].astype(BF16), w_out[0].astype(BF16))

    return _ffn(x1, norm_ffn_g[0][None], sc_f, sh_f, g_f, w_ffn_in[0].astype(BF16),
                w_ffn_out[0].astype(BF16), norm_final_g[None])
```

```python
import functools
import math

import jax
import jax.numpy as jnp
from jax import lax
from jax.experimental import pallas as pl
from jax.experimental.pallas import tpu as pltpu

F32 = jnp.float32
BF16 = jnp.bfloat16

D_MODEL = 1024
BATCH = 8
SEQ = 2048
CTX_LEN = 256
GRID_W = 64
GRID_ROWS = SEQ // GRID_W

LRU_WIDTH = 1280
LRU_BLOCKS = 16
LRU_BLOCK = LRU_WIDTH // LRU_BLOCKS
LRU_CONV = 4
LRU_C = 8.0

RWKV_HEAD = 64
RWKV_WIDTH = 1024
LORA_W = 64
LORA_A = 64
LORA_G = 160
RWKV_IN = 3 * RWKV_WIDTH + 2 * LORA_W + 2 * LORA_A + LORA_G
RWKV_IN_PAD = 3584
LORA_G_PAD = RWKV_IN_PAD - (3 * RWKV_WIDTH + 2 * LORA_W + 2 * LORA_A)
D_FF = 2816

RMS_EPS = 1e-6
GN_EPS = 64e-5
L2_EPS = 1e-12

T_ALL = CTX_LEN + SEQ
CHUNK_T = 64
CHUNK_ROWS = CHUNK_T * BATCH
N_CTX_CHUNKS = CTX_LEN // CHUNK_T
N_LAT_CHUNKS = SEQ // CHUNK_T
N_CHUNKS = N_CTX_CHUNKS + N_LAT_CHUNKS

LANE = 128
D_TILES = D_MODEL // LANE
PAIR = 2 * RWKV_HEAD
N_PAIRS = RWKV_WIDTH // PAIR

LRU_GATE_TILE = 256
LRU_GATE_K = 512
LRU_GATE_K0 = (0, 128, 384, 640, 768)

VMEM_LIMIT = 56 * 1024 * 1024


def _params(n_axes):
    return pltpu.CompilerParams(dimension_semantics=("arbitrary",) * n_axes,
                                vmem_limit_bytes=VMEM_LIMIT)


def _const_spec(shape):
    nd = len(shape)
    return pl.BlockSpec(shape, lambda *_: (0,) * nd)


def _bdot(a, b):
    return jnp.dot(a.astype(BF16), b.astype(BF16), preferred_element_type=F32)


def _softplus(x):
    return jnp.maximum(x, 0.0) + jnp.log1p(jnp.exp(-jnp.abs(x)))


def _sigmoid(x):
    return 0.5 * jnp.tanh(0.5 * x) + 0.5


def _head_sums(t, ones_pair):
    tb = t.astype(BF16)
    return jnp.concatenate(
        [jnp.dot(tb[:, s * PAIR:(s + 1) * PAIR], ones_pair, preferred_element_type=F32)
         for s in range(t.shape[1] // PAIR)], axis=1)


def _silu(x):
    return x * _sigmoid(x)


def _gelu_tanh(x):
    c = math.sqrt(2.0 / math.pi)
    return 0.5 * x * (1.0 + jnp.tanh(c * (x + 0.044715 * (x * x * x))))


def _bcast_rows(v8, rows):
    c = v8.shape[-1]
    return jnp.broadcast_to(v8[None], (rows // BATCH, BATCH, c)).reshape(rows, c)


def _rms_modulate(x, g, scale8, shift8):
    rows = x.shape[0]
    ms = jnp.mean(x * x, axis=-1, keepdims=True)
    y = x * lax.rsqrt(ms + RMS_EPS) * g
    return y * (1.0 + _bcast_rows(scale8, rows)) + _bcast_rows(shift8, rows)


def _adaln_kernel(c_ref, w_ref, b_ref, o_ref):
    s = _silu(c_ref[...])
    o_ref[...] = _bdot(s, w_ref[...]) + b_ref[...]


def _adaln(c16, w_mod, b_mod):
    n = w_mod.shape[1]
    tn = 1536
    return pl.pallas_call(
        _adaln_kernel,
        grid=(n // tn,),
        in_specs=[_const_spec((16, D_MODEL)),
                  pl.BlockSpec((D_MODEL, tn), lambda j: (0, j)),
                  pl.BlockSpec((1, tn), lambda j: (0, j))],
        out_specs=pl.BlockSpec((16, tn), lambda j: (0, j)),
        out_shape=jax.ShapeDtypeStruct((16, n), F32),
        compiler_params=_params(1),
        name="adaln",
    )(c16, w_mod, b_mod)


LANE = 128
D_TILES = D_MODEL // LANE


PROJ_GROUPS = 4
PROJ_T = CHUNK_T // PROJ_GROUPS
PROJ_ROWS = PROJ_T * BATCH


def _proj_kernel(ctx_ref, x_ref, g_ref, sc_ref, sh_ref, w_ref, o_ref, xc_o, xl_o, stage_s):
    is_ctx = pl.program_id(0) < N_CTX_CHUNKS

    def stage(gi):
        t0 = gi * PROJ_T
        for bi in range(BATCH):
            rows = pl.ds(gi * PROJ_ROWS + bi, PROJ_T, stride=BATCH)
            for s in range(D_TILES):
                ln = slice(s * LANE, (s + 1) * LANE)
                stage_s[s, rows, :] = jnp.where(is_ctx, ctx_ref[bi, t0:t0 + PROJ_T, ln],
                                                x_ref[bi, t0:t0 + PROJ_T, ln])

    def staged(rows):
        return jnp.concatenate([stage_s[s, rows, :] for s in range(D_TILES)], axis=1)

    def normalise(gi):
        rows = slice(gi * PROJ_ROWS, (gi + 1) * PROJ_ROWS)
        x = staged(rows)
        xl_o[rows, :] = x
        return _rms_modulate(x, g_ref[...], sc_ref[0], sh_ref[0]).astype(BF16)

    h = None
    for gi in range(PROJ_GROUPS + 1):
        if gi < PROJ_GROUPS:
            stage(gi)
        if gi > 0:
            o_ref[(gi - 1) * PROJ_ROWS:gi * PROJ_ROWS, :] = jnp.dot(
                h, w_ref[...], preferred_element_type=F32)
        if gi < PROJ_GROUPS:
            h = normalise(gi)

    @pl.when(is_ctx)
    def _():
        xc_o[...] = staged(slice(0, CHUNK_ROWS))


def _proj(ctx, x, g, sc2, sh2, w):
    n = w.shape[1]
    seg = lambda i: (i >= N_CTX_CHUNKS).astype(jnp.int32)
    cchunk = lambda i: jnp.minimum(i, N_CTX_CHUNKS - 1)
    lchunk = lambda i: jnp.maximum(i - N_CTX_CHUNKS, 0)
    return pl.pallas_call(
        _proj_kernel,
        grid=(N_CHUNKS,),
        in_specs=[pl.BlockSpec((BATCH, CHUNK_T, D_MODEL), lambda i: (0, cchunk(i), 0)),
                  pl.BlockSpec((BATCH, CHUNK_T, D_MODEL), lambda i: (0, lchunk(i), 0)),
                  _const_spec((1, D_MODEL)),
                  pl.BlockSpec((1, BATCH, D_MODEL), lambda i: (seg(i), 0, 0)),
                  pl.BlockSpec((1, BATCH, D_MODEL), lambda i: (seg(i), 0, 0)),
                  _const_spec((D_MODEL, n))],
        out_specs=[pl.BlockSpec((CHUNK_ROWS, n), lambda i: (i, 0)),
                   pl.BlockSpec((CHUNK_ROWS, D_MODEL), lambda i: (cchunk(i), 0)),
                   pl.BlockSpec((CHUNK_ROWS, D_MODEL), lambda i: (lchunk(i), 0))],
        out_shape=[jax.ShapeDtypeStruct((N_CHUNKS * CHUNK_ROWS, n), F32),
                   jax.ShapeDtypeStruct((CTX_LEN * BATCH, D_MODEL), F32),
                   jax.ShapeDtypeStruct((SEQ * BATCH, D_MODEL), F32)],
        scratch_shapes=[pltpu.VMEM((D_TILES, CHUNK_ROWS, LANE), F32)],
        compiler_params=_params(1),
        name="in_proj",
    )(ctx, x, g, sc2, sh2, w)


def _lru_kernel(*refs, reverse):
    if reverse:
        (u_ref, uy_ref, hf_ref, cw_ref, cb_ref, wt_ref, ba_ref, bx_ref, lam_ref,
         o_ref, h_s, halo_s, ext_s, a_s, b_s) = refs
    else:
        (u_ref, cw_ref, cb_ref, wt_ref, ba_ref, bx_ref, lam_ref,
         o_ref, h_s, halo_s, ext_s, a_s, b_s) = refs
    i = pl.program_id(0)
    halo_rows = (LRU_CONV - 1) * BATCH

    @pl.when(i == 0)
    def _():
        h_s[...] = jnp.zeros_like(h_s)

    @pl.when((i == 0) | (i == N_CTX_CHUNKS))
    def _():
        halo_s[...] = jnp.zeros_like(halo_s)

    u = u_ref[...]
    if reverse:
        ext_s[0:CHUNK_ROWS] = u
        ext_s[CHUNK_ROWS:CHUNK_ROWS + halo_rows] = halo_s[...]
        halo_s[...] = u[0:halo_rows]
    else:
        ext_s[0:halo_rows] = halo_s[...]
        ext_s[halo_rows:halo_rows + CHUNK_ROWS] = u
        halo_s[...] = u[CHUNK_ROWS - halo_rows:CHUNK_ROWS]

    xc = jnp.broadcast_to(cb_ref[...], (CHUNK_ROWS, LRU_WIDTH))
    for j in range(LRU_CONV):
        off = (LRU_CONV - 1 - j) if reverse else j
        xc = xc + cw_ref[j:j + 1, :] * ext_s[off * BATCH:off * BATCH + CHUNK_ROWS]

    xcb = xc.astype(BF16)
    c_half = (-0.5 * LRU_C) * _softplus(-lam_ref[...])
    for j in range(LRU_WIDTH // LRU_GATE_TILE):
        k0 = LRU_GATE_K0[j]
        cs = slice(j * LRU_GATE_TILE, (j + 1) * LRU_GATE_TILE)
        g = jnp.dot(xcb[:, k0:k0 + LRU_GATE_K], wt_ref[j], preferred_element_type=F32)
        tanh_r = jnp.tanh(g[:, :LRU_GATE_TILE] + ba_ref[:, cs])
        gate_i = 0.5 * jnp.tanh(g[:, LRU_GATE_TILE:] + bx_ref[:, cs]) + 0.5
        log_a = c_half[:, cs] * tanh_r + c_half[:, cs]
        a = jnp.exp(log_a)
        a_s[:, cs] = a
        om = 1.0 - a * a
        b_s[:, cs] = jnp.where(om > 0.0, om * lax.rsqrt(om), 0.0) * (gate_i * xc[:, cs])

    def step(k, h):
        t = (CHUNK_T - 1 - k) if reverse else k
        rows = pl.ds(pl.multiple_of(t * BATCH, BATCH), BATCH)
        h = a_s[rows, :] * h + b_s[rows, :]
        if reverse:
            o_ref[rows, :] = (hf_ref[rows, :] + h) * _gelu_tanh(uy_ref[rows, :])
        else:
            o_ref[rows, :] = h
        return h

    h_s[...] = lax.fori_loop(0, CHUNK_T, step, h_s[...], unroll=8)


def _lru_scan(u_all, hf, lru_p, d, *, reverse):
    cw, cb, wt, ba, bx, lam = lru_p
    if reverse:
        chunk = lambda i: jnp.where(i < N_CTX_CHUNKS, N_CTX_CHUNKS - 1 - i,
                                    N_CHUNKS + N_CTX_CHUNKS - 1 - i)
    else:
        chunk = lambda i: i
    lat = lambda i: jnp.maximum(chunk(jnp.maximum(i, N_CTX_CHUNKS)) - N_CTX_CHUNKS, 0)
    blk = (CHUNK_ROWS, LRU_WIDTH)
    in_specs = [pl.BlockSpec(blk, lambda i: (chunk(i), 0))]
    args = [u_all]
    if reverse:
        in_specs += [pl.BlockSpec(blk, lambda i: (lat(i) + N_CTX_CHUNKS, 1)),
                     pl.BlockSpec(blk, lambda i: (lat(i), 0))]
        args += [u_all, hf]
    in_specs += [_const_spec((LRU_CONV, LRU_WIDTH)), _const_spec((1, LRU_WIDTH)),
                 _const_spec(wt[d].shape), _const_spec((1, LRU_WIDTH)),
                 _const_spec((1, LRU_WIDTH)), _const_spec((1, LRU_WIDTH))]
    args += [cw[d], cb[d][None], wt[d], ba[d][None], bx[d][None], lam[d][None]]
    halo_rows = (LRU_CONV - 1) * BATCH
    return pl.pallas_call(
        functools.partial(_lru_kernel, reverse=reverse),
        grid=(N_CHUNKS,),
        in_specs=in_specs,
        out_specs=pl.BlockSpec(blk, lambda i: (lat(i), 0)),
        out_shape=jax.ShapeDtypeStruct((SEQ * BATCH, LRU_WIDTH), F32),
        scratch_shapes=[pltpu.VMEM((BATCH, LRU_WIDTH), F32),
                        pltpu.VMEM((halo_rows, LRU_WIDTH), F32),
                        pltpu.VMEM((CHUNK_ROWS + halo_rows, LRU_WIDTH), F32),
                        pltpu.VMEM(blk, F32),
                        pltpu.VMEM(blk, F32)],
        compiler_params=_params(1),
        name="lru_bwd" if reverse else "lru_fwd",
    )(*args)


PREP_T = 32
PREP_ROWS = PREP_T * BATCH
PREP_CTX_BLOCKS = CTX_LEN // PREP_T
PREP_LAT_BLOCKS = SEQ // PREP_T
PREP_BLOCKS = PREP_CTX_BLOCKS + PREP_LAT_BLOCKS


def _prep_kernel(xc_ref, xl_ref, xcp_ref, xcn_ref, xlp_ref, xln_ref, gmix_ref, sc_ref, sh_ref,
                 win_ref, mu_ref,
                 w2_ref, w0_ref, a2_ref, a0_ref, g2_ref, kk_ref, ka_ref, rk_ref, ones_ref,
                 r_o, v_o, kn_o, ld_o, kd_o, b_o, bon_o, g_o, ext_s):
    i = pl.program_id(0)
    is_ctx = i < PREP_CTX_BLOCKS
    first = (i == 0) | (i == PREP_CTX_BLOCKS)
    last = (i == PREP_CTX_BLOCKS - 1) | (i == PREP_BLOCKS - 1)
    lat_rows = lambda ref: ref[...].reshape(-1, D_MODEL)
    x = jnp.concatenate([
        jnp.where(is_ctx, xcp_ref[...], lat_rows(xlp_ref)),
        jnp.where(is_ctx, xc_ref[...], lat_rows(xl_ref)),
        jnp.where(is_ctx, xcn_ref[...], lat_rows(xln_ref))], axis=0)
    seg = jnp.where(is_ctx, 0, 1)
    h = _rms_modulate(x, gmix_ref[...], sc_ref[seg], sh_ref[seg])
    hb = h.astype(BF16)
    W = RWKV_WIDTH
    ones = ones_ref[...]

    def project(c0, c1):
        zt = jnp.dot(hb, win_ref[:, c0:c1], preferred_element_type=F32)
        ext_s[0:BATCH, c0:c1] = jnp.where(first, 0.0, zt[0:BATCH])
        ext_s[BATCH:BATCH + PREP_ROWS, c0:c1] = zt[BATCH:BATCH + PREP_ROWS]
        ext_s[BATCH + PREP_ROWS:2 * BATCH + PREP_ROWS, c0:c1] = jnp.where(
            last, 0.0, zt[BATCH + PREP_ROWS:2 * BATCH + PREP_ROWS])

    def shifted(c0, c1):
        zc = ext_s[BATCH:BATCH + PREP_ROWS, c0:c1]
        zprev = ext_s[0:PREP_ROWS, c0:c1]
        znext = ext_s[2 * BATCH:2 * BATCH + PREP_ROWS, c0:c1]
        return zc + mu_ref[0:1, c0:c1] * (zprev - zc) + mu_ref[1:2, c0:c1] * (znext - zc)

    def project_pairs(g):
        for part in range(3):
            project(part * W + 2 * g * PAIR, part * W + (2 * g + 2) * PAIR)

    def finish_pair(s, w_pre, a_pre):
        ln = slice(s * PAIR, (s + 1) * PAIR)
        r = shifted(s * PAIR, (s + 1) * PAIR)
        k = shifted(W + s * PAIR, W + (s + 1) * PAIR)
        v = shifted(2 * W + s * PAIR, 2 * W + (s + 1) * PAIR)
        kk = k * kk_ref[:, ln]
        ss = jnp.dot((kk * kk).astype(BF16), ones, preferred_element_type=F32)
        kn = kk / jnp.maximum(jnp.sqrt(ss), L2_EPS)
        r_o[s] = r
        v_o[s] = v
        kn_o[s] = kn
        kd_sum = jnp.zeros_like(k)
        for d in range(2):
            cs = slice(d * W + s * PAIR, d * W + (s + 1) * PAIR)
            ld_o[d, s] = -math.exp(-0.5) * _sigmoid(w_pre[:, cs])
            asig = _sigmoid(a_pre[:, cs])
            kd = k * (1.0 + (asig - 1.0) * ka_ref[:, ln])
            kd_o[d, s] = kd
            b_o[d, s] = kn * asig
            kd_sum = kd_sum + kd
        bon = jnp.dot((r * kd_sum * rk_ref[:, ln]).astype(BF16), ones, preferred_element_type=F32)
        bon_o[:, ln] = bon * v

    project(3 * W, RWKV_IN_PAD)
    project_pairs(0)
    wd = shifted(3 * W, 3 * W + 2 * LORA_W)
    ad = shifted(3 * W + 2 * LORA_W, 3 * W + 2 * LORA_W + 2 * LORA_A)
    gd = shifted(3 * W + 2 * LORA_W + 2 * LORA_A, RWKV_IN_PAD)
    w_pre = _bdot(jnp.tanh(wd), w2_ref[...]) + w0_ref[...]
    a_pre = _bdot(ad, a2_ref[...]) + a0_ref[...]
    g_o[...] = _bdot(_sigmoid(gd), g2_ref[...])
    for g in range(N_PAIRS // 2):
        finish_pair(2 * g, w_pre, a_pre)
        if g + 1 < N_PAIRS // 2:
            project_pairs(g + 1)
        finish_pair(2 * g + 1, w_pre, a_pre)


def _rwkv_prep(xc_tb, xl_tb, g_mix, sc2, sh2, w_rw, rw_p):
    mu, w2, w0, a2, a0, g2, k_k, k_a, r_k, ones = rw_p
    W = RWKV_WIDTH
    hb = PREP_T
    cmain = lambda i: jnp.minimum(i, PREP_CTX_BLOCKS - 1)
    lmain = lambda i: jnp.maximum(i - PREP_CTX_BLOCKS, 0)
    assert PREP_T == GRID_ROWS
    xl4 = xl_tb.reshape(GRID_ROWS, GRID_W, BATCH, D_MODEL)
    hblk = (BATCH, D_MODEL)
    n_c8 = CTX_LEN - 1
    in_specs = [
        pl.BlockSpec((PREP_ROWS, D_MODEL), lambda i: (cmain(i), 0)),
        pl.BlockSpec((GRID_ROWS, 1, BATCH, D_MODEL), lambda i: (0, lmain(i), 0, 0)),
        pl.BlockSpec(hblk, lambda i: (jnp.maximum(cmain(i) * hb - 1, 0), 0)),
        pl.BlockSpec(hblk, lambda i: (jnp.minimum((cmain(i) + 1) * hb, n_c8), 0)),
        pl.BlockSpec((1, 1, BATCH, D_MODEL), lambda i: (GRID_ROWS - 1, jnp.maximum(lmain(i) - 1, 0), 0, 0)),
        pl.BlockSpec((1, 1, BATCH, D_MODEL), lambda i: (0, jnp.minimum(lmain(i) + 1, GRID_W - 1), 0, 0)),
        _const_spec((1, D_MODEL)), _const_spec(sc2.shape), _const_spec(sh2.shape),
        _const_spec(w_rw.shape),
        _const_spec(mu.shape), _const_spec(w2.shape), _const_spec(w0.shape),
        _const_spec(a2.shape), _const_spec(a0.shape), _const_spec(g2.shape),
        _const_spec(k_k.shape), _const_spec(k_a.shape), _const_spec(r_k.shape),
        _const_spec(ones.shape),
    ]
    n_rows = T_ALL * BATCH
    n_lat = SEQ * BATCH
    shared = pl.BlockSpec((N_PAIRS, PREP_ROWS, PAIR), lambda i: (0, i, 0))
    perdir = pl.BlockSpec((2, N_PAIRS, PREP_ROWS, PAIR), lambda i: (0, 0, i, 0))
    latonly = pl.BlockSpec((PREP_ROWS, W), lambda i: (lmain(i), 0))
    return pl.pallas_call(
        _prep_kernel,
        grid=(PREP_BLOCKS,),
        in_specs=in_specs,
        out_specs=[shared, shared, shared, perdir, perdir, perdir, latonly, latonly],
        out_shape=[jax.ShapeDtypeStruct((N_PAIRS, n_rows, PAIR), F32)] * 3
        + [jax.ShapeDtypeStruct((2, N_PAIRS, n_rows, PAIR), F32)] * 3
        + [jax.ShapeDtypeStruct((n_lat, W), F32)] * 2,
        scratch_shapes=[pltpu.VMEM((PREP_ROWS + 2 * BATCH, RWKV_IN_PAD), F32)],
        compiler_params=_params(1),
        name="rwkv_prep",
    )(xc_tb, xl4, xc_tb, xc_tb, xl4, xl4, g_mix, sc2, sh2, w_rw,
      mu, w2, w0, a2, a0, g2, k_k, k_a, r_k, ones)


WKV_GROUP = 8
WKV_SKEW = 1


def _wkv_chunks(loaders, states, decays, sign):
    T = CHUNK_T
    R2 = 2 * T
    nb = len(loaders)
    levels = int(math.log2(T))

    lane = lax.broadcasted_iota(jnp.int32, (R2, PAIR), 1)
    row = lax.broadcasted_iota(jnp.int32, (R2, PAIR), 0)
    head_mask = (lane // RWKV_HEAD) == (row // T)

    def stack_masked(x):
        return jnp.where(head_mask, jnp.concatenate([x, x], axis=0), 0.0).astype(BF16)

    tw = lax.broadcasted_iota(jnp.int32, (T, 2 * R2), 0)
    sw = lax.broadcasted_iota(jnp.int32, (T, 2 * R2), 1) % T
    dtw = (tw - sw) * sign
    strict_w = dtw[:, 0:R2] > 0
    incl_w2 = dtw >= 0
    eye_w = jnp.where(dtw[:, 0:R2] == 0, 1.0, 0.0)

    nt = lambda x, y: lax.dot_general(x, y, (((1,), (1,)), ((), ())), preferred_element_type=F32)
    mm = lambda x, y: jnp.dot(x, y, preferred_element_type=F32)

    def chain(load, S, decay, out):
        a_t, r_t, b_t, k_t, b_e, k_e, v = load()
        ar = jnp.concatenate([a_t, r_t], axis=0).astype(BF16)
        bk = jnp.concatenate([stack_masked(b_t), stack_masked(k_t)], axis=0)
        big = nt(ar, bk)
        fs = nt(ar, S.astype(BF16))
        uv_rhs = jnp.concatenate([b_e, k_e], axis=0).astype(BF16)
        yield
        p_w = jnp.where(strict_w, big[0:T, 0:R2], 0.0)
        a_ak = jnp.where(strict_w, big[0:T, R2:2 * R2], 0.0).astype(BF16)
        a_rbk = jnp.where(incl_w2, big[T:R2], 0.0).astype(BF16)
        v_st = stack_masked(v)
        h = mm(a_ak, v_st)
        inv_w = eye_w + p_w
        p_w = mm(p_w.astype(BF16), stack_masked(p_w))
        yield
        for lvl in range(1, levels):
            p_bd = stack_masked(p_w)
            if lvl == levels - 1:
                inv_w = inv_w + mm(inv_w.astype(BF16), p_bd)
            else:
                both = mm(jnp.concatenate([p_w, inv_w], axis=0).astype(BF16), p_bd)
                yield
                p_w = both[0:T]
                inv_w = inv_w + both[T:R2]
        yield
        u = mm(inv_w.astype(BF16), stack_masked(fs[0:T] + h))
        yield
        y = fs[T:R2] + mm(a_rbk, jnp.concatenate([stack_masked(u), v_st], axis=0))
        uv_t = jnp.concatenate([u, v], axis=0).T.astype(BF16)
        out.append((y, S * decay + jnp.where(head_mask, mm(uv_t, uv_rhs), 0.0)))

    outs = [[] for _ in range(nb)]
    chains = [chain(loaders[i], states[i], decays[i], outs[i]) for i in range(nb)]
    groups = [chains[g:g + WKV_GROUP] for g in range(0, nb, WKV_GROUP)]
    live = [True] * len(groups)
    tick = 0
    while any(live):
        for g, group in enumerate(groups):
            if live[g] and tick >= g * WKV_SKEW:
                for ch in group:
                    if next(ch, "done") == "done":
                        live[g] = False
        tick += 1
    return [o[0][0] for o in outs], [o[0][1] for o in outs]


WKV_PAIRS_PER_STEP = 8


def _wkv_kernel(r_ref, v_ref, kn_ref, ld_ref, kd_ref, b_ref, y_ref, s_ref, cum_s):
    d = pl.program_id(0)
    c = pl.program_id(2)
    sign = 1 - 2 * d

    @pl.when(c == 0)
    def _():
        s_ref[...] = jnp.zeros_like(s_ref)

    tot8, dec8 = [], []
    for pi in range(WKV_PAIRS_PER_STEP):
        ld = ld_ref[0, pi]
        cum_up = ld
        for lvl in range(int(math.log2(CHUNK_T))):
            sh = BATCH << lvl
            cum_up = cum_up + jnp.concatenate([jnp.zeros((sh, PAIR), F32), cum_up[:-sh]], axis=0)
        tot8.append(cum_up[CHUNK_ROWS - BATCH:])
        tot = _bcast_rows(tot8[pi], CHUNK_ROWS)
        cum_s[pi] = jnp.where(d == 0, cum_up, tot - cum_up + ld)
        dec8.append(jnp.exp(tot8[pi]))

    problems = [(pl.ds(bi, CHUNK_T, stride=BATCH), pi, bi)
                for pi in range(WKV_PAIRS_PER_STEP) for bi in range(BATCH)]
    def loader(rw, pi, bi):
        def load():
            cum = cum_s[pi, rw, :]
            ld_b = ld_ref[0, pi, rw, :]
            b = b_ref[0, pi, rw, :]
            kd = kd_ref[0, pi, rw, :]
            e_out = jnp.exp(-cum)
            e_end = jnp.exp(tot8[pi][bi:bi + 1] - cum)
            return (-kn_ref[pi, rw, :] * jnp.exp(cum - ld_b), r_ref[pi, rw, :] * jnp.exp(cum),
                    b * e_out, kd * e_out, b * e_end, kd * e_end, v_ref[pi, rw, :])
        return load

    ys, s_new = _wkv_chunks([loader(*p) for p in problems],
                            [s_ref[pi, bi] for _, pi, bi in problems],
                            [dec8[pi][bi:bi + 1] for _, pi, bi in problems], sign)
    for k, (rw, pi, bi) in enumerate(problems):
        y_ref[0, pi, rw, :] = ys[k]
        s_ref[pi, bi] = s_new[k]


def _wkv(r, v, kn, ld, kd, b):
    def chunk(d, c):
        fwd = c
        bwd = jnp.where(c < N_CTX_CHUNKS, N_CTX_CHUNKS - 1 - c, N_CHUNKS + N_CTX_CHUNKS - 1 - c)
        return jnp.where(d == 0, fwd, bwd)
    lat = lambda d, c: chunk(d, jnp.maximum(c, N_CTX_CHUNKS)) - N_CTX_CHUNKS
    pps = WKV_PAIRS_PER_STEP
    shared = pl.BlockSpec((pps, CHUNK_ROWS, PAIR), lambda d, p, c: (p, chunk(d, c), 0))
    perdir = pl.BlockSpec((1, pps, CHUNK_ROWS, PAIR), lambda d, p, c: (d, p, chunk(d, c), 0))
    return pl.pallas_call(
        _wkv_kernel,
        grid=(2, N_PAIRS // pps, N_CHUNKS),
        in_specs=[shared, shared, shared, perdir, perdir, perdir],
        out_specs=pl.BlockSpec((1, pps, CHUNK_ROWS, PAIR), lambda d, p, c: (d, p, lat(d, c), 0)),
        out_shape=jax.ShapeDtypeStruct((2, N_PAIRS, SEQ * BATCH, PAIR), F32),
        scratch_shapes=[pltpu.VMEM((pps, BATCH, PAIR, PAIR), F32),
                        pltpu.VMEM((pps, CHUNK_ROWS, PAIR), F32)],
        compiler_params=_params(3),
        name="wkv_scan",
    )(r, v, kn, ld, kd, b)


MERGE_T = 64
MERGE_ROWS = MERGE_T * BATCH


def _merge_kernel(lru_ref, y_ref, bon_ref, g_ref, x_ref, gmix_ref, mod_ref,
                  lng_ref, lnb_ref, ones_ref, wgate_ref, wol_ref, wor_ref, wout_ref, o_ref):
    rows = MERGE_ROWS
    W = RWKV_WIDTH
    from_lru = _bdot(lru_ref[...], wol_ref[...])
    x = x_ref[...]
    h = _rms_modulate(x, gmix_ref[...], mod_ref[1], mod_ref[0])
    gates = _bdot(h, wgate_ref[...])
    y2 = (y_ref[0] + y_ref[1]).reshape(N_PAIRS, rows, PAIR)
    y = jnp.concatenate([y2[s] for s in range(N_PAIRS)], axis=1)
    inv_n = 1.0 / RWKV_HEAD

    ones = ones_ref[...]
    y_hi = y.astype(BF16).astype(F32)
    mu = (_head_sums(y_hi, ones) + _head_sums(y - y_hi, ones)) * inv_n
    dy = y - mu
    var = _head_sums(dy * dy, ones) * inv_n
    yn = dy * lax.rsqrt(var + GN_EPS) * lng_ref[...] + lnb_ref[...]
    rw = (yn + bon_ref[...].reshape(rows, W)) * g_ref[...].reshape(rows, W)
    m = (_sigmoid(gates[:, :D_MODEL]) * from_lru
         + _sigmoid(gates[:, D_MODEL:]) * _bdot(rw, wor_ref[...]))
    mix = _bdot(m, wout_ref[...])
    o_ref[...] = x + _bcast_rows(mod_ref[2], rows) * mix


def _merge(lru_l, y, bon, g, x_tb, g_mix, mod_m, ln_g, ln_b, ones, wgate, wol, wor, wout):
    W = RWKV_WIDTH
    n_blocks = SEQ // MERGE_T
    per_row = GRID_W // MERGE_T
    cm4 = lambda i: (i % per_row, i // per_row, 0, 0)
    y6 = y.reshape(2, N_PAIRS, GRID_W, GRID_ROWS, BATCH, PAIR)
    resident = lambda shape: pl.BlockSpec(shape, lambda i: (0,) * len(shape),
                                          pipeline_mode=pl.Buffered(1))
    bon4 = bon.reshape(GRID_W, GRID_ROWS, BATCH, W)
    g4 = g.reshape(GRID_W, GRID_ROWS, BATCH, W)
    return pl.pallas_call(
        _merge_kernel,
        grid=(n_blocks,),
        in_specs=[
            pl.BlockSpec((MERGE_ROWS, LRU_WIDTH), lambda i: (i, 0)),
            pl.BlockSpec((2, N_PAIRS, MERGE_T, 1, BATCH, PAIR), lambda i: (0, 0) + cm4(i)),
            pl.BlockSpec((MERGE_T, 1, BATCH, W), cm4),
            pl.BlockSpec((MERGE_T, 1, BATCH, W), cm4),
            pl.BlockSpec((MERGE_ROWS, D_MODEL), lambda i: (i, 0)),
            _const_spec((1, D_MODEL)), _const_spec((3, BATCH, D_MODEL)),
            _const_spec((1, W)), _const_spec((1, W)), _const_spec(ones.shape),
            resident(wgate.shape), resident(wol.shape), resident(wor.shape),
            resident(wout.shape),
        ],
        out_specs=pl.BlockSpec((MERGE_ROWS, D_MODEL), lambda i: (i, 0)),
        out_shape=jax.ShapeDtypeStruct((SEQ * BATCH, D_MODEL), F32),
        compiler_params=_params(1),
        name="merge",
    )(lru_l, y6, bon4, g4, x_tb, g_mix, mod_m, ln_g, ln_b, ones, wgate, wol, wor, wout)


FFN_T = 64
FFN_ROWS = FFN_T * BATCH
FFN_TILE = 256


FFN_GROUPS = 2


def _ffn_kernel(x_ref, g_ref, sc_ref, sh_ref, gf_ref, win_ref, wout_ref, gfin_ref, o_ref,
                act_s, stage_s):
    t_g = FFN_T // FFN_GROUPS
    rows_g = t_g * BATCH

    def up(gi):
        rows = slice(gi * rows_g, (gi + 1) * rows_g)
        h = _rms_modulate(x_ref[rows, :], g_ref[...], sc_ref[...], sh_ref[...]).astype(BF16)
        for j in range(D_FF // FFN_TILE):
            cs = slice(j * FFN_TILE, (j + 1) * FFN_TILE)
            gate = jnp.dot(h, win_ref[:, cs], preferred_element_type=F32)
            lin = jnp.dot(h, win_ref[:, D_FF + j * FFN_TILE:D_FF + (j + 1) * FFN_TILE],
                          preferred_element_type=F32)
            act_s[rows, cs] = (_silu(gate) * lin).astype(BF16)

    def down(gi):
        rows = slice(gi * rows_g, (gi + 1) * rows_g)
        y = jnp.dot(act_s[rows, :], wout_ref[...], preferred_element_type=F32)
        x2 = x_ref[rows, :] + _bcast_rows(gf_ref[...], rows_g) * y
        ms = jnp.mean(x2 * x2, axis=-1, keepdims=True)
        out = x2 * lax.rsqrt(ms + RMS_EPS) * gfin_ref[...]
        for s in range(D_TILES):
            stage_s[s, rows, :] = out[:, s * LANE:(s + 1) * LANE]
        for bi in range(BATCH):
            src = pl.ds(gi * rows_g + bi, t_g, stride=BATCH)
            for s in range(D_TILES):
                o_ref[bi, gi * t_g:(gi + 1) * t_g, s * LANE:(s + 1) * LANE] = stage_s[s, src, :]

    up(0)
    for gi in range(FFN_GROUPS):
        if gi + 1 < FFN_GROUPS:
            up(gi + 1)
        down(gi)


def _ffn(x1, g, sc8, sh8, g_f8, w_in, w_out, g_final):
    resident = lambda shape: pl.BlockSpec(shape, lambda i: (0,) * len(shape),
                                          pipeline_mode=pl.Buffered(1))
    return pl.pallas_call(
        _ffn_kernel,
        grid=(SEQ * BATCH // FFN_ROWS,),
        in_specs=[pl.BlockSpec((FFN_ROWS, D_MODEL), lambda i: (i, 0)),
                  _const_spec((1, D_MODEL)), _const_spec((BATCH, D_MODEL)),
                  _const_spec((BATCH, D_MODEL)), _const_spec((BATCH, D_MODEL)),
                  resident(w_in.shape), resident(w_out.shape), _const_spec((1, D_MODEL))],
        out_specs=pl.BlockSpec((BATCH, FFN_T, D_MODEL), lambda i: (0, i, 0)),
        out_shape=jax.ShapeDtypeStruct((BATCH, SEQ, D_MODEL), F32),
        scratch_shapes=[pltpu.VMEM((FFN_ROWS, D_FF), BF16),
                        pltpu.VMEM((D_TILES, FFN_ROWS, LANE), F32)],
        compiler_params=_params(1),
        name="ffn",
    )(x1, g, sc8, sh8, g_f8, w_in, w_out, g_final)


def _block_diag(w):
    n, c, _ = w.shape
    tiled = jnp.tile(w.reshape(n * c, c), (1, n))
    rb = lax.broadcasted_iota(jnp.int32, (n * c, n * c), 0) // c
    cb = lax.broadcasted_iota(jnp.int32, (n * c, n * c), 1) // c
    return jnp.where(rb == cb, tiled, 0.0)


def _lru_gate_tiles(wa, wx):
    da, dx = _block_diag(0.5 * wa), _block_diag(0.5 * wx)
    tiles = []
    for j, k0 in enumerate(LRU_GATE_K0):
        cs = slice(j * LRU_GATE_TILE, (j + 1) * LRU_GATE_TILE)
        tiles.append(jnp.concatenate([da[k0:k0 + LRU_GATE_K, cs], dx[k0:k0 + LRU_GATE_K, cs]], axis=1))
    return jnp.stack(tiles).astype(BF16)


def _two_dir_lora(w):
    z = jnp.zeros_like(w[0])
    return jnp.concatenate([jnp.concatenate([w[0], z], axis=1),
                            jnp.concatenate([z, w[1]], axis=1)], axis=0).astype(BF16)


def kernel(x, c, ctx, c_ctx, norm_mix_g, norm_ffn_g, w_mod, b_mod, w_in, lru_conv_w, lru_conv_b, lru_wa, lru_ba, lru_wx, lru_bx, lru_lambda, w_o_lru, rwkv_mu, rwkv_w0, rwkv_w2, rwkv_a0, rwkv_a2, rwkv_g2, rwkv_k_k, rwkv_k_a, rwkv_r_k, rwkv_ln_g, rwkv_ln_b, w_o_rwkv, w_out, w_ffn_in, w_ffn_out, norm_final_g):
    assert x.shape == (BATCH, SEQ, D_MODEL) and ctx.shape == (BATCH, CTX_LEN, D_MODEL)
    assert w_mod.shape[0] == 1, "single layer only"
    D, W = D_MODEL, RWKV_WIDTH

    c16 = jnp.concatenate([c, c_ctx[None], jnp.zeros((16 - BATCH - 1, D), F32)], axis=0)
    mod = _adaln(c16, w_mod[0], b_mod[0][None])
    mod_lat = mod[:BATCH].reshape(BATCH, 6, D)
    mod_ctx = jnp.broadcast_to(mod[BATCH:BATCH + 1], (BATCH, 6 * D)).reshape(BATCH, 6, D)
    sh_m, sc_m, g_m, sh_f, sc_f, g_f = [mod_lat[:, k] for k in range(6)]
    sh2 = jnp.stack([mod_ctx[:, 0], sh_m])
    sc2 = jnp.stack([mod_ctx[:, 1], sc_m])

    w_in0 = w_in[0]
    n_lru = 2 * LRU_WIDTH
    w_lru = w_in0[:, :n_lru].astype(BF16)
    w_rw = jnp.pad(w_in0[:, n_lru:n_lru + RWKV_IN], ((0, 0), (0, RWKV_IN_PAD - RWKV_IN))).astype(BF16)
    w_gate = w_in0[:, n_lru + RWKV_IN:].astype(BF16)
    g_mix = norm_mix_g[0][None]

    u_all, xc_tb, xl_tb = _proj(ctx, x, g_mix, sc2, sh2, w_lru)

    wt = jnp.stack([_lru_gate_tiles(lru_wa[0, d], lru_wx[0, d]) for d in range(2)])
    lru_p = (lru_conv_w[0], lru_conv_b[0], wt, 0.5 * lru_ba[0], 0.5 * lru_bx[0], lru_lambda[0])
    hf = _lru_scan(u_all, None, lru_p, 0, reverse=False)
    lru_l = _lru_scan(u_all, hf, lru_p, 1, reverse=True)

    mu_pad = jnp.pad(rwkv_mu[0], ((0, 0), (0, RWKV_IN_PAD - RWKV_IN)))
    ones = _block_diag(jnp.ones((PAIR // RWKV_HEAD, RWKV_HEAD, RWKV_HEAD), F32)).astype(BF16)
    rw_p = (mu_pad, _two_dir_lora(rwkv_w2[0]), rwkv_w0[0].reshape(1, 2 * W),
            _two_dir_lora(rwkv_a2[0]), rwkv_a0[0].reshape(1, 2 * W),
            jnp.pad(rwkv_g2[0], ((0, LORA_G_PAD - LORA_G), (0, 0))).astype(BF16),
            rwkv_k_k[0][None], rwkv_k_a[0][None], rwkv_r_k[0].reshape(1, W), ones)
    r, v, kn, ld, kd, b, bon, g = _rwkv_prep(xc_tb, xl_tb, g_mix, sc2, sh2, w_rw, rw_p)
    y = _wkv(r, v, kn, ld, kd, b)

    x1 = _merge(lru_l, y, bon, g, xl_tb, g_mix, jnp.stack([sh_m, sc_m, g_m]),
                rwkv_ln_g[0][None], rwkv_ln_b[0][None], ones, w_gate,
                w_o_lru[0].astype(BF16), w_o_rwkv[0].astype(BF16), w_out[0].astype(BF16))

    return _ffn(x1, norm_ffn_g[0][None], sc_f, sh_f, g_f, w_ffn_in[0].astype(BF16),
                w_ffn_out[0].astype(BF16), norm_final_g[None])
```
